```python
import math
import jax, jax.numpy as jnp
from jax import lax
import numpy as np

D_MODEL = 1024
BATCH = 4
SEQ = 4096
DEPTH = 2

N_MIXERS = 2
N_A_LAYERS = (DEPTH + 1) // 2
N_B_LAYERS = DEPTH // 2
HEAD_DIM = 64
ATTN_SCALE = HEAD_DIM ** -0.5
A_DILATED = ((128, 1), (512, 4), (2048, 16))
A_N_GROUPS = 3
A_HEADS_PER_GROUP = D_MODEL // HEAD_DIM
A_HEADS = A_N_GROUPS * A_HEADS_PER_GROUP
A_GROUP_WIDTH = A_HEADS_PER_GROUP * HEAD_DIM
A_IN_COLS = A_N_GROUPS * 3 * A_GROUP_WIDTH
B_HEADS = D_MODEL // HEAD_DIM
B_WIDTH = B_HEADS * HEAD_DIM
B_IN_COLS = 3 * B_WIDTH + B_HEADS
Q_BLOCK = 128
REL_BUCKETS = 32
REL_MAX_DIST = 2048
N_GROUPS = 4
EXPERTS_PER_GROUP = 4
N_EXPERTS = N_GROUPS * EXPERTS_PER_GROUP
EXPERT_FF = D_MODEL // 4
TOP_K_IN_GROUP = 2
EPS = 1e-6
NEG_INF = -1e30

kernel_name = "hybrid_dilated_fox_hmoe_adaln"


def rmsnorm(x, g):
    xf = x.astype(jnp.float32)
    y = xf * lax.rsqrt(jnp.mean(xf * xf, axis=-1, keepdims=True) + EPS)
    return (y * g.astype(jnp.float32)).astype(x.dtype)


def t5_bucket(dist):
    max_exact = REL_BUCKETS // 2
    d = jnp.maximum(dist, 0)
    large = max_exact + (jnp.log(jnp.maximum(d, 1).astype(jnp.float32) / max_exact)
                         / math.log(REL_MAX_DIST / max_exact)
                         * (REL_BUCKETS - max_exact)).astype(jnp.int32)
    large = jnp.minimum(large, REL_BUCKETS - 1)
    return jnp.where(d < max_exact, d, large)


def dilated_group_attention(q, k, v, bias_table, window, dilation):
    b, s, h, hd = q.shape
    r = dilation
    n = s // r
    n_pad = -(-n // Q_BLOCK) * Q_BLOCK
    nb = n_pad // Q_BLOCK
    w_sub = window // dilation

    def strided(t):
        t = t.reshape(b, n, r, h, hd).transpose(0, 2, 1, 3, 4).reshape(b * r, n, h, hd)
        t = jnp.pad(t, ((0, 0), (0, n_pad - n), (0, 0), (0, 0)))
        return t.reshape(b * r, nb, Q_BLOCK, h, hd)

    def with_prev(t):
        prev = jnp.pad(t[:, :-1], ((0, 0), (1, 0), (0, 0), (0, 0), (0, 0)))
        return jnp.concatenate([prev, t], axis=2)

    qs = strided(q)
    kc = with_prev(strided(k))
    vc = with_prev(strided(v))
    qi = jnp.arange(Q_BLOCK)[:, None]
    kj = jnp.arange(2 * Q_BLOCK)[None, :]
    dist = qi + Q_BLOCK - kj
    bias = bias_table[t5_bucket(dist * r)]
    blk = jnp.arange(nb)[:, None, None]
    valid = (dist >= 0) & (dist <= w_sub) & (blk * Q_BLOCK + qi - dist >= 0)
    logits = jnp.einsum('znqhd,znkhd->znhqk', qs, kc).astype(jnp.float32)
    logits = logits + bias.transpose(2, 0, 1).astype(jnp.float32)[None, None]
    logits = jnp.where(valid[None, :, None], logits, NEG_INF)
    m = jnp.max(logits, axis=-1, keepdims=True)
    p = jnp.exp(logits - m)
    denom = jnp.sum(p, axis=-1)
    out = jnp.einsum('znhqk,znkhd->znqhd', p, vc.astype(jnp.float32))
    out = out / denom.transpose(0, 1, 3, 2)[..., None]
    lse = m[..., 0] + jnp.log(denom)
    out = out.reshape(b, r, n_pad, h, hd)[:, :, :n].transpose(0, 2, 1, 3, 4).reshape(b, s, h, hd)
    lse = lse.transpose(0, 1, 3, 2).reshape(b, r, n_pad, h)[:, :, :n].transpose(0, 2, 1, 3).reshape(b, s, h)
    return out, lse


def dilated_mixer(h, w_in, w_out, q_gain, k_gain, rel_bias):
    b, s, _ = h.shape
    proj = (h @ w_in).reshape(b, s, A_N_GROUPS, 3, A_HEADS_PER_GROUP, HEAD_DIM)
    outs, lses = [], []
    for g, (window, dil) in enumerate(A_DILATED):
        q = rmsnorm(proj[:, :, g, 0], q_gain) * ATTN_SCALE
        k = rmsnorm(proj[:, :, g, 1], k_gain)
        v = proj[:, :, g, 2]
        tab = rel_bias[:, g * A_HEADS_PER_GROUP:(g + 1) * A_HEADS_PER_GROUP]
        o, l = dilated_group_attention(q, k, v, tab, window, dil)
        outs.append(o)
        lses.append(l)
    outs = jnp.stack(outs, axis=0)
    wts = jax.nn.softmax(jnp.stack(lses, axis=0), axis=0)
    o = jnp.sum(wts[..., None] * outs, axis=0).astype(h.dtype)
    return o.reshape(b, s, A_GROUP_WIDTH) @ w_out


def forgetting_mixer(h, w_in, f_bias, w_out, q_gain, k_gain):
    b, s, _ = h.shape
    proj = h @ w_in
    qkv = proj[..., :3 * B_WIDTH].reshape(b, s, 3, B_HEADS, HEAD_DIM)
    log_f = jax.nn.log_sigmoid((proj[..., 3 * B_WIDTH:] + f_bias).astype(jnp.float32))
    cum = jnp.cumsum(log_f, axis=1).transpose(0, 2, 1)
    q = rmsnorm(qkv[:, :, 0], q_gain) * ATTN_SCALE
    k = rmsnorm(qkv[:, :, 1], k_gain)
    v = qkv[:, :, 2].astype(jnp.float32)
    nb = s // Q_BLOCK
    qb = q.reshape(b, nb, Q_BLOCK, B_HEADS, HEAD_DIM).transpose(1, 0, 2, 3, 4)
    cb = cum.reshape(b, B_HEADS, nb, Q_BLOCK).transpose(2, 0, 1, 3)
    key_pos = jnp.arange(s)

    def block(args):
        i, q_i, c_i = args
        logits = jnp.einsum('bqhd,bkhd->bhqk', q_i, k).astype(jnp.float32)
        logits = logits + (c_i[..., None] - cum[:, :, None, :])
        q_pos = i * Q_BLOCK + jnp.arange(Q_BLOCK)
        mask = key_pos[None, :] <= q_pos[:, None]
        p = jax.nn.softmax(jnp.where(mask, logits, NEG_INF), axis=-1)
        return jnp.einsum('bhqk,bkhd->bqhd', p, v)

    o = lax.map(block, (jnp.arange(nb), qb, cb))
    o = o.transpose(1, 0, 2, 3, 4).reshape(b, s, B_WIDTH).astype(h.dtype)
    return o @ w_out


def hierarchical_moe(h, wg_r, bg_r, we_r, be_r, w_gate, w_up, w_down):
    b, s, d = h.shape
    t = h.reshape(-1, d)
    g_prob = jax.nn.softmax((t @ wg_r + bg_r).astype(jnp.float32), axis=-1)
    g_idx = jnp.argmax(g_prob, axis=-1)
    g_onehot = jax.nn.one_hot(g_idx, N_GROUPS, dtype=jnp.float32)
    g_w = jnp.max(g_prob, axis=-1, keepdims=True)
    e_logits = (t @ we_r + be_r).astype(jnp.float32).reshape(-1, N_GROUPS, EXPERTS_PER_GROUP)
    e_sel = jnp.einsum('nge,ng->ne', e_logits, g_onehot)
    top_v, top_i = lax.top_k(e_sel, TOP_K_IN_GROUP)
    top_w = jax.nn.softmax(top_v, axis=-1) * g_w
    in_group = jnp.sum(jax.nn.one_hot(top_i, EXPERTS_PER_GROUP, dtype=jnp.float32) * top_w[..., None], axis=1)
    gates = (g_onehot[:, :, None] * in_group[:, None, :]).reshape(-1, N_EXPERTS)
    hid = jax.nn.silu(jnp.einsum('nd,edf->nef', t, w_gate)) * jnp.einsum('nd,edf->nef', t, w_up)
    y = jnp.einsum('nef,efd->nd', hid * gates[..., None].astype(hid.dtype), w_down)
    return y.reshape(b, s, d)


def setup_inputs(seed: int = 0) -> dict:
    key = jax.random.key(seed)
    ks = jax.random.split(key, 24)
    nrm = jax.random.normal
    D = D_MODEL
    inp = {}
    inp["x"] = nrm(ks[0], (BATCH, SEQ, D), jnp.float32)
    inp["c"] = nrm(ks[1], (BATCH, D), jnp.float32)
    inp["w_ada"] = nrm(ks[2], (DEPTH, D, 6 * D), jnp.float32) * (0.5 * D ** -0.5)
    inp["b_ada"] = nrm(ks[3], (DEPTH, 6 * D), jnp.float32) * 0.02
    inp["norm_mix"] = 1.0 + 0.02 * nrm(ks[4], (DEPTH, D), jnp.float32)
    inp["norm_ffn"] = 1.0 + 0.02 * nrm(ks[5], (DEPTH, D), jnp.float32)
    inp["rel_bias"] = 0.2 * nrm(ks[6], (REL_BUCKETS, A_HEADS), jnp.float32)
    inp["a_w_in"] = nrm(ks[7], (N_A_LAYERS, D, A_IN_COLS), jnp.float32) * D ** -0.5
    inp["a_w_out"] = nrm(ks[8], (N_A_LAYERS, A_GROUP_WIDTH, D), jnp.float32) * A_GROUP_WIDTH ** -0.5
    inp["a_q_norm"] = 1.0 + 0.02 * nrm(ks[9], (N_A_LAYERS, HEAD_DIM), jnp.float32)
    inp["a_k_norm"] = 1.0 + 0.02 * nrm(ks[10], (N_A_LAYERS, HEAD_DIM), jnp.float32)
    inp["b_w_in"] = nrm(ks[11], (N_B_LAYERS, D, B_IN_COLS), jnp.float32) * D ** -0.5
    inp["b_f_bias"] = 3.0 + 0.1 * nrm(ks[12], (N_B_LAYERS, B_HEADS), jnp.float32)
    inp["b_w_out"] = nrm(ks[13], (N_B_LAYERS, B_WIDTH, D), jnp.float32) * B_WIDTH ** -0.5
    inp["b_q_norm"] = 1.0 + 0.02 * nrm(ks[14], (N_B_LAYERS, HEAD_DIM), jnp.float32)
    inp["b_k_norm"] = 1.0 + 0.02 * nrm(ks[15], (N_B_LAYERS, HEAD_DIM), jnp.float32)
    inp["router_group_w"] = nrm(ks[16], (DEPTH, D, N_GROUPS), jnp.float32) * D ** -0.5
    inp["router_group_b"] = 0.01 * nrm(ks[17], (DEPTH, N_GROUPS), jnp.float32)
    inp["router_expert_w"] = nrm(ks[18], (DEPTH, D, N_EXPERTS), jnp.float32) * D ** -0.5
    inp["router_expert_b"] = 0.01 * nrm(ks[19], (DEPTH, N_EXPERTS), jnp.float32)
    inp["w_gate"] = nrm(ks[20], (DEPTH, N_EXPERTS, D, EXPERT_FF), jnp.float32) * D ** -0.5
    inp["w_up"] = nrm(ks[21], (DEPTH, N_EXPERTS, D, EXPERT_FF), jnp.float32) * D ** -0.5
    inp["w_down"] = nrm(ks[22], (DEPTH, N_EXPERTS, EXPERT_FF, D), jnp.float32) * EXPERT_FF ** -0.5
    return inp


def reference(x, c, w_ada, b_ada, norm_mix, norm_ffn, rel_bias,
              a_w_in, a_w_out, a_q_norm, a_k_norm,
              b_w_in, b_f_bias, b_w_out, b_q_norm, b_k_norm,
              router_group_w, router_group_b, router_expert_w, router_expert_b,
              w_gate, w_up, w_down):
    b, s, d = x.shape
    ada_in = jax.nn.silu(c)
    for i in range(DEPTH):
        mod = (ada_in @ w_ada[i] + b_ada[i]).reshape(b, 6, d)
        sh_m, sc_m, g_m = mod[:, 0, None, :], mod[:, 1, None, :], mod[:, 2, None, :]
        sh_f, sc_f, g_f = mod[:, 3, None, :], mod[:, 4, None, :], mod[:, 5, None, :]
        hm = rmsnorm(x, norm_mix[i]) * (1 + sc_m) + sh_m
        j = i // N_MIXERS
        if i % N_MIXERS == 0:
            y = dilated_mixer(hm, a_w_in[j], a_w_out[j], a_q_norm[j], a_k_norm[j], rel_bias)
        else:
            y = forgetting_mixer(hm, b_w_in[j], b_f_bias[j], b_w_out[j], b_q_norm[j], b_k_norm[j])
        x = x + g_m * y
        hf = rmsnorm(x, norm_ffn[i]) * (1 + sc_f) + sh_f
        x = x + g_f * hierarchical_moe(hf, router_group_w[i], router_group_b[i],
                                       router_expert_w[i], router_expert_b[i],
                                       w_gate[i], w_up[i], w_down[i])
    return x
```

```python
import functools
import math

import jax
import jax.numpy as jnp
from jax import lax
from jax.experimental import pallas as pl
from jax.experimental.pallas import tpu as pltpu

F32 = jnp.float32
BF16 = jnp.bfloat16

D_MODEL = 1024
HEAD_DIM = 64
N_HEADS = 16
ATTN_SCALE = HEAD_DIM ** -0.5
A_DILATED = ((128, 1), (512, 4), (2048, 16))
Q_BLOCK = 128
REL_BUCKETS = 32
REL_MAX_DIST = 2048
N_GROUPS = 4
EXPERTS_PER_GROUP = 4
N_EXPERTS = 16
EXPERT_FF = 256
EPS = 1e-6
NEG_INF = -1e30

LANES = 128
VMEM_LIMIT = 48 * 1024 * 1024


def _params(sem):
    return pltpu.CompilerParams(dimension_semantics=sem, vmem_limit_bytes=VMEM_LIMIT)


def _split3(x):
    hi = x.astype(BF16)
    r1 = x - hi.astype(F32)
    mid = r1.astype(BF16)
    lo = (r1 - mid.astype(F32)).astype(BF16)
    return hi, mid, lo


def _dot(a, b):
    return jnp.dot(a, b, preferred_element_type=F32)


def _dot_nt(a, b):
    return lax.dot_general(a, b, (((1,), (1,)), ((), ())), preferred_element_type=F32)


def _ada_kernel(c_ref, w_ref, b_ref, o_ref):
    c = c_ref[...]
    a = c * (1.0 / (1.0 + jnp.exp(-c)))
    o_ref[0] = _dot(a.astype(BF16), w_ref[0].astype(BF16)) + b_ref[0]


def _ada(c, w_ada, b_ada):
    depth, d, n6 = w_ada.shape
    b = c.shape[0]
    rows = 8
    c_pad = jnp.pad(c, ((0, rows - b), (0, 0)))
    tn = 1536
    out = pl.pallas_call(
        _ada_kernel,
        out_shape=jax.ShapeDtypeStruct((depth, rows, n6), F32),
        grid=(depth, n6 // tn),
        in_specs=[
            pl.BlockSpec((rows, d), lambda i, j: (0, 0)),
            pl.BlockSpec((1, d, tn), lambda i, j: (i, 0, j)),
            pl.BlockSpec((1, 1, tn), lambda i, j: (i, 0, j)),
        ],
        out_specs=pl.BlockSpec((1, rows, tn), lambda i, j: (i, 0, j)),
        compiler_params=_params(("arbitrary", "arbitrary")),
        name="ada_mod",
    )(c_pad, w_ada, b_ada.reshape(depth, 1, n6))
    return out[:, :b]


def _modnorm(x, gamma, sc, sh):
    ms = jnp.mean(x * x, axis=-1, keepdims=True)
    y = x * lax.rsqrt(ms + EPS) * gamma
    return y * (1.0 + sc) + sh


def _normproj_kernel(x_ref, g_ref, sc_ref, sh_ref, w_ref, o_ref, h_ref):
    @pl.when(pl.program_id(1) == 0)
    def _():
        h_ref[...] = _modnorm(x_ref[0], g_ref[...], sc_ref[0], sh_ref[0]).astype(BF16)

    o_ref[0] = _dot(h_ref[...], w_ref[...]).astype(BF16)


def _normproj_gate_kernel(x_ref, g_ref, sc_ref, sh_ref, w_ref, wf_ref, o_ref, f_ref, h_ref):
    @pl.when(pl.program_id(1) == 0)
    def _():
        h = _modnorm(x_ref[0], g_ref[...], sc_ref[0], sh_ref[0]).astype(BF16)
        h_ref[...] = h
        f_ref[0] = _dot(h, wf_ref[...])

    o_ref[0] = _dot(h_ref[...], w_ref[...]).astype(BF16)


def _normproj(x, gamma, sc, sh, w, w_f=None, tm=1024, tn=1024):
    b, s, d = x.shape
    ncols = w.shape[1]
    spt = s // tm
    grid = (b * spt, ncols // tn)
    xmap = lambda i, j: (i // spt, i % spt, 0)
    bmap = lambda i, j: (i // spt, 0, 0)
    in_specs = [
        pl.BlockSpec((1, tm, d), xmap),
        pl.BlockSpec((1, d), lambda i, j: (0, 0)),
        pl.BlockSpec((1, 1, d), bmap),
        pl.BlockSpec((1, 1, d), bmap),
        pl.BlockSpec((d, tn), lambda i, j: (0, j)),
    ]
    o_spec = pl.BlockSpec((1, tm, tn), lambda i, j: (i // spt, i % spt, j))
    scratch = [pltpu.VMEM((tm, d), BF16)]
    if w_f is None:
        return pl.pallas_call(
            _normproj_kernel,
            out_shape=jax.ShapeDtypeStruct((b, s, ncols), BF16),
            grid=grid, in_specs=in_specs, out_specs=o_spec, scratch_shapes=scratch,
            compiler_params=_params(("arbitrary", "arbitrary")),
            name="normproj",
        )(x, gamma.reshape(1, d), sc, sh, w)
    in_specs.append(pl.BlockSpec((d, LANES), lambda i, j: (0, 0)))
    f_spec = pl.BlockSpec((1, tm, LANES), xmap)
    return pl.pallas_call(
        _normproj_gate_kernel,
        out_shape=(jax.ShapeDtypeStruct((b, s, ncols), BF16),
                   jax.ShapeDtypeStruct((b, s, LANES), F32)),
        grid=grid, in_specs=in_specs, out_specs=(o_spec, f_spec), scratch_shapes=scratch,
        compiler_params=_params(("arbitrary", "arbitrary")),
        name="normproj_gate",
    )(x, gamma.reshape(1, d), sc, sh, w, w_f)


def _headnorm(t, gain):
    t = t.astype(F32)
    ms = jnp.mean(t * t, axis=-1, keepdims=True)
    return t * lax.rsqrt(ms + EPS) * gain


def _attn_a_kernel(q_ref, kc_ref, kp_ref, vc_ref, vp_ref, bias_ref, gq_ref, gk_ref,
                   o_ref, lse_ref):
    gq = gq_ref[...] * ATTN_SCALE
    gk = gk_ref[...]
    for h in range(N_HEADS):
        sl = slice(h * HEAD_DIM, (h + 1) * HEAD_DIM)
        qn = _headnorm(q_ref[0, :, sl], gq).astype(BF16)
        kc = _headnorm(kc_ref[0, :, sl], gk).astype(BF16)
        kp = _headnorm(kp_ref[0, :, sl], gk).astype(BF16)
        s_p = _dot_nt(qn, kp) + bias_ref[0, h, :, :Q_BLOCK]
        s_c = _dot_nt(qn, kc) + bias_ref[0, h, :, Q_BLOCK:]
        m = jnp.maximum(jnp.max(s_p, axis=-1, keepdims=True),
                        jnp.max(s_c, axis=-1, keepdims=True))
        p_p = jnp.exp(s_p - m)
        p_c = jnp.exp(s_c - m)
        l = jnp.sum(p_p, axis=-1, keepdims=True) + jnp.sum(p_c, axis=-1, keepdims=True)
        o = _dot(p_p.astype(BF16), vp_ref[0, :, sl]) + _dot(p_c.astype(BF16), vc_ref[0, :, sl])
        o_ref[0, :, sl] = (o / l).astype(BF16)
        lse_ref[0, 0, :, h:h + 1] = m + jnp.log(l)


def _t5_bucket(dist):
    max_exact = REL_BUCKETS // 2
    d = jnp.maximum(dist, 0)
    large = max_exact + (jnp.log(jnp.maximum(d, 1).astype(F32) / max_exact)
                         / math.log(REL_MAX_DIST / max_exact)
                         * (REL_BUCKETS - max_exact)).astype(jnp.int32)
    large = jnp.minimum(large, REL_BUCKETS - 1)
    return jnp.where(d < max_exact, d, large)


def _bias_tables(rel_bias_g, window, dilation):
    qi = jnp.arange(Q_BLOCK)[:, None]
    kj = jnp.arange(2 * Q_BLOCK)[None, :]
    dist = qi + Q_BLOCK - kj
    bias = rel_bias_g[_t5_bucket(dist * dilation)].transpose(2, 0, 1).astype(F32)
    band = (dist >= 0) & (dist <= window // dilation)
    first = band & (qi - dist >= 0)
    return jnp.stack([jnp.where(first[None], bias, NEG_INF),
                      jnp.where(band[None], bias, NEG_INF)], axis=0)


def _attn_a_group(proj, g, rel_bias_g, gq, gk, window, dilation):
    b, s, cols = proj.shape
    r = dilation
    n = s // r
    assert n % Q_BLOCK == 0
    nb = n // Q_BLOCK
    gw = N_HEADS * HEAD_DIM
    cpr = cols // gw
    pv = proj.reshape(b, n, r * cols)
    tables = _bias_tables(rel_bias_g, window, dilation)
    base = g * 3

    def cur(which):
        return pl.BlockSpec((1, Q_BLOCK, gw),
                            lambda qb, bi, j: (bi, qb, j * cpr + base + which))

    def prev(which):
        return pl.BlockSpec((1, Q_BLOCK, gw),
                            lambda qb, bi, j: (bi, jnp.maximum(qb - 1, 0), j * cpr + base + which))

    o, lse = pl.pallas_call(
        _attn_a_kernel,
        out_shape=(jax.ShapeDtypeStruct((b, n, r * gw), BF16),
                   jax.ShapeDtypeStruct((b, r, n, N_HEADS), F32)),
        grid=(nb, b, r),
        in_specs=[
            cur(0), cur(1), prev(1), cur(2), prev(2),
            pl.BlockSpec((1, N_HEADS, Q_BLOCK, 2 * Q_BLOCK),
                         lambda qb, bi, j: (jnp.minimum(qb, 1), 0, 0, 0)),
            pl.BlockSpec((1, HEAD_DIM), lambda qb, bi, j: (0, 0)),
            pl.BlockSpec((1, HEAD_DIM), lambda qb, bi, j: (0, 0)),
        ],
        out_specs=(
            pl.BlockSpec((1, Q_BLOCK, gw), lambda qb, bi, j: (bi, qb, j)),
            pl.BlockSpec((1, 1, Q_BLOCK, N_HEADS), lambda qb, bi, j: (bi, j, qb, 0)),
        ),
        compiler_params=_params(("arbitrary", "arbitrary", "arbitrary")),
        name=f"attn_a_g{g}",
    )(pv, pv, pv, pv, pv, tables, gq.reshape(1, HEAD_DIM), gk.reshape(1, HEAD_DIM))
    o = o.reshape(b, s, gw)
    lse = lse.transpose(0, 2, 1, 3).reshape(b, s, N_HEADS)
    return o, lse


def _merge_out_kernel(o0_ref, o1_ref, o2_ref, l0_ref, l1_ref, l2_ref, x_ref, g_ref, w_ref,
                      e_ref, out_ref):
    l0, l1, l2 = l0_ref[0], l1_ref[0], l2_ref[0]
    m = jnp.maximum(jnp.maximum(l0, l1), l2)
    e0, e1, e2 = jnp.exp(l0 - m), jnp.exp(l1 - m), jnp.exp(l2 - m)
    inv = 1.0 / (e0 + e1 + e2)
    expand = e_ref[...]

    def widen(w):
        hi, mid, lo = _split3(w)
        return _dot(hi, expand) + _dot(mid, expand) + _dot(lo, expand)

    o = (widen(e0 * inv) * o0_ref[0].astype(F32)
         + widen(e1 * inv) * o1_ref[0].astype(F32)
         + widen(e2 * inv) * o2_ref[0].astype(F32))
    y = _dot(o.astype(BF16), w_ref[...])
    out_ref[0] = x_ref[0] + g_ref[0] * y


def _merge_out(outs, lses, x, gate, w_out, tm=512):
    b, s, d = x.shape
    spt = s // tm
    xmap = lambda i: (i // spt, i % spt, 0)
    expand = (jnp.arange(d)[None, :] // HEAD_DIM == jnp.arange(N_HEADS)[:, None]).astype(BF16)
    o_spec = pl.BlockSpec((1, tm, d), xmap)
    l_spec = pl.BlockSpec((1, tm, N_HEADS), xmap)
    return pl.pallas_call(
        _merge_out_kernel,
        out_shape=jax.ShapeDtypeStruct((b, s, d), F32),
        grid=(b * spt,),
        in_specs=[o_spec, o_spec, o_spec, l_spec, l_spec, l_spec,
                  pl.BlockSpec((1, tm, d), xmap),
                  pl.BlockSpec((1, 1, d), lambda i: (i // spt, 0, 0)),
                  pl.BlockSpec((d, d), lambda i: (0, 0)),
                  pl.BlockSpec((N_HEADS, d), lambda i: (0, 0))],
        out_specs=pl.BlockSpec((1, tm, d), xmap),
        compiler_params=_params(("arbitrary",)),
        name="merge_out",
    )(*outs, *lses, x, gate, w_out, expand)


def _out_kernel(o_ref, x_ref, g_ref, w_ref, out_ref):
    out_ref[0] = x_ref[0] + g_ref[0] * _dot(o_ref[0], w_ref[...])


def _out_proj(o, x, gate, w_out, tm=512):
    b, s, d = x.shape
    spt = s // tm
    xmap = lambda i: (i // spt, i % spt, 0)
    return pl.pallas_call(
        _out_kernel,
        out_shape=jax.ShapeDtypeStruct((b, s, d), F32),
        grid=(b * spt,),
        in_specs=[pl.BlockSpec((1, tm, d), xmap),
                  pl.BlockSpec((1, tm, d), xmap),
                  pl.BlockSpec((1, 1, d), lambda i: (i // spt, 0, 0)),
                  pl.BlockSpec((d, d), lambda i: (0, 0))],
        out_specs=pl.BlockSpec((1, tm, d), xmap),
        compiler_params=_params(("arbitrary",)),
        name="out_proj",
    )(o, x, gate, w_out)


def _cum_kernel(f_ref, fb_ref, tri_ref, o_ref, carry_ref):
    @pl.when(pl.program_id(1) == 0)
    def _():
        carry_ref[...] = jnp.zeros_like(carry_ref)

    z = f_ref[0] + fb_ref[...]
    logf = jnp.minimum(z, 0.0) - jnp.log(1.0 + jnp.exp(-jnp.abs(z)))
    hi, mid, lo = _split3(logf)
    tri = tri_ref[...]
    cum = _dot(tri, hi) + _dot(tri, mid) + _dot(tri, lo) + carry_ref[...]
    o_ref[0] = cum
    carry_ref[...] = cum[-1:, :]


def _cum_forget(fg, f_bias, tc=512):
    b, s, w = fg.shape
    fb = jnp.pad(f_bias, (0, w - f_bias.shape[0])).reshape(1, w)
    tri = (jnp.arange(tc)[:, None] >= jnp.arange(tc)[None, :]).astype(BF16)
    return pl.pallas_call(
        _cum_kernel,
        out_shape=jax.ShapeDtypeStruct((b, s, w), F32),
        grid=(b, s // tc),
        in_specs=[pl.BlockSpec((1, tc, w), lambda i, j: (i, j, 0)),
                  pl.BlockSpec((1, w), lambda i, j: (0, 0)),
                  pl.BlockSpec((tc, tc), lambda i, j: (0, 0))],
        out_specs=pl.BlockSpec((1, tc, w), lambda i, j: (i, j, 0)),
        scratch_shapes=[pltpu.VMEM((1, w), F32)],
        compiler_params=_params(("arbitrary", "arbitrary")),
        name="cum_forget",
    )(fg, fb, tri)


def _gate_columns(cum_col, lead):
    rows = cum_col.shape[0]
    lane = lax.broadcasted_iota(jnp.int32, (rows, HEAD_DIM), 1)
    hi, mid, lo = (p.astype(F32) for p in _split3(cum_col if lead else -cum_col))
    off = 0 if lead else 3
    ext = jnp.where(lane < 6, 1.0, 0.0)
    ext = jnp.where(lane == off, hi, ext)
    ext = jnp.where(lane == off + 1, mid, ext)
    ext = jnp.where(lane == off + 2, lo, ext)
    return ext.astype(BF16)


def _pick_lane(tile, idx):
    lane = lax.broadcasted_iota(jnp.int32, tile.shape, 1)
    return jnp.sum(jnp.where(lane == idx, tile, 0.0), axis=-1, keepdims=True)


def _fox_kernel(q_ref, k_ref, v_ref, cq_ref, ck_ref, gq_ref, gk_ref, o_ref, ka_ref,
                *, tq, tk):
    hp = pl.program_id(1)
    qi = pl.program_id(2)
    s_len = k_ref.shape[1]

    @pl.when(qi == 0)
    def _():
        gk = gk_ref[...]
        for hh in range(2):
            sl = slice(hh * HEAD_DIM, (hh + 1) * HEAD_DIM)
            ka_ref[hh, :, :HEAD_DIM] = _headnorm(k_ref[0, :, sl], gk).astype(BF16)
            ck = _pick_lane(ck_ref[0], 2 * hp + hh)
            ka_ref[hh, :, HEAD_DIM:] = _gate_columns(ck, lead=False)

    gq = gq_ref[...] * ATTN_SCALE
    row = lax.broadcasted_iota(jnp.int32, (tq, tk), 0)
    col = lax.broadcasted_iota(jnp.int32, (tq, tk), 1)
    causal = col <= row
    for hh in range(2):
        sl = slice(hh * HEAD_DIM, (hh + 1) * HEAD_DIM)
        qn = _headnorm(q_ref[0, :, sl], gq).astype(BF16)
        cq = _pick_lane(cq_ref[0], 2 * hp + hh)
        qa = jnp.concatenate([qn, _gate_columns(cq, lead=True)], axis=1)

        def step(j, carry, masked):
            m, l, acc = carry
            start = pl.multiple_of(j * tk, tk)
            s = _dot_nt(qa, ka_ref[hh, pl.ds(start, tk), :])
            if masked:
                s = jnp.where(causal, s, NEG_INF)
            m_new = jnp.maximum(m, jnp.max(s, axis=-1, keepdims=True))
            alpha = jnp.exp(m - m_new)
            p = jnp.exp(s - m_new)
            l = alpha * l + jnp.sum(p, axis=-1, keepdims=True)
            acc = alpha * acc + _dot(p.astype(BF16), v_ref[0, pl.ds(start, tk), sl])
            return m_new, l, acc

        init = (jnp.full((tq, 1), NEG_INF, F32), jnp.zeros((tq, 1), F32),
                jnp.zeros((tq, HEAD_DIM), F32))
        carry = lax.fori_loop(0, qi, functools.partial(step, masked=False), init)
        m, l, acc = step(qi, carry, masked=True)
        o_ref[0, :, sl] = (acc / l).astype(BF16)


def _fox_attention(qkv, cum, gq, gk, tq=256):
    b, s, _ = qkv.shape
    tk = tq
    nq = s // tq
    pairs = N_HEADS // 2
    pw = 2 * HEAD_DIM
    return pl.pallas_call(
        functools.partial(_fox_kernel, tq=tq, tk=tk),
        out_shape=jax.ShapeDtypeStruct((b, s, N_HEADS * HEAD_DIM), BF16),
        grid=(b, pairs, nq),
        in_specs=[
            pl.BlockSpec((1, tq, pw), lambda bi, hp, qi: (bi, qi, hp)),
            pl.BlockSpec((1, s, pw), lambda bi, hp, qi: (bi, 0, pairs + hp)),
            pl.BlockSpec((1, s, pw), lambda bi, hp, qi: (bi, 0, 2 * pairs + hp)),
            pl.BlockSpec((1, tq, LANES), lambda bi, hp, qi: (bi, qi, 0)),
            pl.BlockSpec((1, s, LANES), lambda bi, hp, qi: (bi, 0, 0)),
            pl.BlockSpec((1, HEAD_DIM), lambda bi, hp, qi: (0, 0)),
            pl.BlockSpec((1, HEAD_DIM), lambda bi, hp, qi: (0, 0)),
        ],
        out_specs=pl.BlockSpec((1, tq, pw), lambda bi, hp, qi: (bi, qi, hp)),
        scratch_shapes=[pltpu.VMEM((2, s, pw), BF16)],
        compiler_params=_params(("arbitrary", "arbitrary", "arbitrary")),
        name="fox_attn",
    )(qkv, qkv, qkv, cum, cum, gq.reshape(1, HEAD_DIM), gk.reshape(1, HEAD_DIM))


ROUTER_LANES = LANES


def _route(logits):
    lane = lax.broadcasted_iota(jnp.int32, logits.shape, 1)
    lane_f = lane.astype(F32)
    big = float(ROUTER_LANES)
    is_g = lane < N_GROUPS
    gl = jnp.where(is_g, logits, -jnp.inf)
    gmax = jnp.max(gl, axis=-1, keepdims=True)
    gsum = jnp.sum(jnp.where(is_g, jnp.exp(logits - gmax), 0.0), axis=-1, keepdims=True)
    g_w = 1.0 / gsum
    gidx = jnp.min(jnp.where(gl == gmax, lane_f, big), axis=-1, keepdims=True)
    lo = N_GROUPS + EXPERTS_PER_GROUP * gidx
    in_grp = (lane_f >= lo) & (lane_f < lo + EXPERTS_PER_GROUP)
    el = jnp.where(in_grp, logits, -jnp.inf)
    t1 = jnp.max(el, axis=-1, keepdims=True)
    i1 = jnp.min(jnp.where(el == t1, lane_f, big), axis=-1, keepdims=True)
    el2 = jnp.where(lane_f == i1, -jnp.inf, el)
    t2 = jnp.max(el2, axis=-1, keepdims=True)
    i2 = jnp.min(jnp.where(el2 == t2, lane_f, big), axis=-1, keepdims=True)
    e2 = jnp.exp(t2 - t1)
    w1 = g_w / (1.0 + e2)
    w2 = w1 * e2
    return jnp.where(lane_f == i1, w1, jnp.where(lane_f == i2, w2, 0.0))


def _moe_kernel(x_ref, gam_ref, sc_ref, sh_ref, gate_ref, wr_ref, br_ref,
                wg_ref, wu_ref, wd_ref, out_ref, h_ref, gates_ref, acc_ref):
    e = pl.program_id(1)

    @pl.when(e == 0)
    def _():
        h = _modnorm(x_ref[0], gam_ref[...], sc_ref[0], sh_ref[0])
        h_hi = h.astype(BF16)
        h_lo = (h - h_hi.astype(F32)).astype(BF16)
        wr = wr_ref[...]
        w_hi = wr.astype(BF16)
        w_lo = (wr - w_hi.astype(F32)).astype(BF16)
        logits = _dot(h_hi, w_hi) + _dot(h_hi, w_lo) + _dot(h_lo, w_hi) + br_ref[...]
        gates_ref[...] = _route(logits)
        h_ref[...] = h_hi
        acc_ref[...] = jnp.zeros_like(acc_ref)

    h = h_ref[...]
    a = _dot(h, wg_ref[0])
    u = _dot(h, wu_ref[0])
    gcol = _pick_lane(gates_ref[...], N_GROUPS + e)
    hid = a * (1.0 / (1.0 + jnp.exp(-a))) * u * gcol
    acc_ref[...] += _dot(hid.astype(BF16), wd_ref[0])

    @pl.when(e == N_EXPERTS - 1)
    def _():
        out_ref[0] = x_ref[0] + gate_ref[0] * acc_ref[...]


def _moe(x, gamma, sc, sh, gate, w_router, b_router, w_gate, w_up, w_down, tm=512):
    b, s, d = x.shape
    spt = s // tm
    xmap = lambda i, e: (i // spt, i % spt, 0)
    bmap = lambda i, e: (i // spt, 0, 0)
    return pl.pallas_call(
        _moe_kernel,
        out_shape=jax.ShapeDtypeStruct((b, s, d), F32),
        grid=(b * spt, N_EXPERTS),
        in_specs=[
            pl.BlockSpec((1, tm, d), xmap),
            pl.BlockSpec((1, d), lambda i, e: (0, 0)),
            pl.BlockSpec((1, 1, d), bmap),
            pl.BlockSpec((1, 1, d), bmap),
            pl.BlockSpec((1, 1, d), bmap),
            pl.BlockSpec((d, ROUTER_LANES), lambda i, e: (0, 0)),
            pl.BlockSpec((1, ROUTER_LANES), lambda i, e: (0, 0)),
            pl.BlockSpec((1, d, EXPERT_FF), lambda i, e: (e, 0, 0)),
            pl.BlockSpec((1, d, EXPERT_FF), lambda i, e: (e, 0, 0)),
            pl.BlockSpec((1, EXPERT_FF, d), lambda i, e: (e, 0, 0)),
        ],
        out_specs=pl.BlockSpec((1, tm, d), xmap),
        scratch_shapes=[pltpu.VMEM((tm, d), BF16),
                        pltpu.VMEM((tm, ROUTER_LANES), F32),
                        pltpu.VMEM((tm, d), F32)],
        compiler_params=_params(("arbitrary", "arbitrary")),
        name="moe",
    )(x, gamma.reshape(1, d), sc, sh, gate, w_router, b_router, w_gate, w_up, w_down)


def kernel(x, c, w_ada, b_ada, norm_mix, norm_ffn, rel_bias, a_w_in, a_w_out, a_q_norm,
           a_k_norm, b_w_in, b_f_bias, b_w_out, b_q_norm, b_k_norm, router_group_w,
           router_group_b, router_expert_w, router_expert_b, w_gate, w_up, w_down):
    b, s, d = x.shape
    depth = w_ada.shape[0]
    mod = _ada(c, w_ada, b_ada).reshape(depth, b, 6, 1, d)
    pad_r = ROUTER_LANES - N_GROUPS - N_EXPERTS
    for i in range(depth):
        sh_m, sc_m, g_m, sh_f, sc_f, g_f = (mod[i, :, k] for k in range(6))
        j = i // 2
        if i % 2 == 0:
            proj = _normproj(x, norm_mix[i], sc_m, sh_m, a_w_in[j].astype(BF16))
            outs, lses = [], []
            for g, (window, dil) in enumerate(A_DILATED):
                tab = rel_bias[:, g * N_HEADS:(g + 1) * N_HEADS]
                o, l = _attn_a_group(proj, g, tab, a_q_norm[j], a_k_norm[j], window, dil)
                outs.append(o)
                lses.append(l)
            x = _merge_out(outs, lses, x, g_m, a_w_out[j].astype(BF16))
        else:
            n_qkv = 3 * N_HEADS * HEAD_DIM
            w_in = b_w_in[j]
            w_f = jnp.pad(w_in[:, n_qkv:], ((0, 0), (0, LANES - N_HEADS))).astype(BF16)
            qkv, fg = _normproj(x, norm_mix[i], sc_m, sh_m, w_in[:, :n_qkv].astype(BF16), w_f)
            cum = _cum_forget(fg, b_f_bias[j])
            o = _fox_attention(qkv, cum, b_q_norm[j], b_k_norm[j])
            x = _out_proj(o, x, g_m, b_w_out[j].astype(BF16))
        w_router = jnp.pad(jnp.concatenate([router_group_w[i], router_expert_w[i]], axis=1),
                           ((0, 0), (0, pad_r)))
        b_router = jnp.pad(jnp.concatenate([router_group_b[i], router_expert_b[i]]),
                           (0, pad_r)).reshape(1, ROUTER_LANES)
        x = _moe(x, norm_ffn[i], sc_f, sh_f, g_f, w_router, b_router,
                 w_gate[i].astype(BF16), w_up[i].astype(BF16), w_down[i].astype(BF16))
    return x
```

```python
import functools
import math

import jax
import jax.numpy as jnp
from jax import lax
from jax.experimental import pallas as pl
from jax.experimental.pallas import tpu as pltpu

F32 = jnp.float32
BF16 = jnp.bfloat16

D_MODEL = 1024
HEAD_DIM = 64
N_HEADS = 16
ATTN_SCALE = HEAD_DIM ** -0.5
A_DILATED = ((128, 1), (512, 4), (2048, 16))
Q_BLOCK = 128
REL_BUCKETS = 32
REL_MAX_DIST = 2048
N_GROUPS = 4
EXPERTS_PER_GROUP = 4
N_EXPERTS = 16
EXPERT_FF = 256
EPS = 1e-6
NEG_INF = -1e30

LANES = 128
VMEM_LIMIT = 48 * 1024 * 1024


def _params(sem):
    return pltpu.CompilerParams(dimension_semantics=sem, vmem_limit_bytes=VMEM_LIMIT)


def _split3(x):
    hi = x.astype(BF16)
    r1 = x - hi.astype(F32)
    mid = r1.astype(BF16)
    lo = (r1 - mid.astype(F32)).astype(BF16)
    return hi, mid, lo


def _dot(a, b):
    return jnp.dot(a, b, preferred_element_type=F32)


def _dot_nt(a, b):
    return lax.dot_general(a, b, (((1,), (1,)), ((), ())), preferred_element_type=F32)


def _ada_kernel(c_ref, w_ref, b_ref, o_ref):
    c = c_ref[...]
    a = c * (1.0 / (1.0 + jnp.exp(-c)))
    o_ref[0] = _dot(a.astype(BF16), w_ref[0].astype(BF16)) + b_ref[0]


def _ada(c, w_ada, b_ada):
    depth, d, n6 = w_ada.shape
    b = c.shape[0]
    rows = 8
    c_pad = jnp.pad(c, ((0, rows - b), (0, 0)))
    tn = 1536
    out = pl.pallas_call(
        _ada_kernel,
        out_shape=jax.ShapeDtypeStruct((depth, rows, n6), F32),
        grid=(depth, n6 // tn),
        in_specs=[
            pl.BlockSpec((rows, d), lambda i, j: (0, 0)),
            pl.BlockSpec((1, d, tn), lambda i, j: (i, 0, j)),
            pl.BlockSpec((1, 1, tn), lambda i, j: (i, 0, j)),
        ],
        out_specs=pl.BlockSpec((1, rows, tn), lambda i, j: (i, 0, j)),
        compiler_params=_params(("arbitrary", "arbitrary")),
        name="ada_mod",
    )(c_pad, w_ada, b_ada.reshape(depth, 1, n6))
    return out[:, :b]


def _modnorm(x, gamma, sc, sh):
    ms = jnp.mean(x * x, axis=-1, keepdims=True)
    y = x * lax.rsqrt(ms + EPS) * gamma
    return y * (1.0 + sc) + sh


def _normproj_kernel(x_ref, g_ref, sc_ref, sh_ref, w_ref, o_ref, h_ref):
    @pl.when(pl.program_id(1) == 0)
    def _():
        h_ref[...] = _modnorm(x_ref[0], g_ref[...], sc_ref[0], sh_ref[0]).astype(BF16)

    o_ref[0] = _dot(h_ref[...], w_ref[...]).astype(BF16)


def _normproj_gate_kernel(x_ref, g_ref, sc_ref, sh_ref, w_ref, wf_ref, o_ref, f_ref, h_ref):
    @pl.when(pl.program_id(1) == 0)
    def _():
        h = _modnorm(x_ref[0], g_ref[...], sc_ref[0], sh_ref[0]).astype(BF16)
        h_ref[...] = h
        f_ref[0] = _dot(h, wf_ref[...])

    o_ref[0] = _dot(h_ref[...], w_ref[...]).astype(BF16)


def _normproj(x, gamma, sc, sh, w, w_f=None, tm=1024, tn=1024):
    b, s, d = x.shape
    ncols = w.shape[1]
    spt = s // tm
    grid = (b * spt, ncols // tn)
    xmap = lambda i, j: (i // spt, i % spt, 0)
    bmap = lambda i, j: (i // spt, 0, 0)
    in_specs = [
        pl.BlockSpec((1, tm, d), xmap),
        pl.BlockSpec((1, d), lambda i, j: (0, 0)),
        pl.BlockSpec((1, 1, d), bmap),
        pl.BlockSpec((1, 1, d), bmap),
        pl.BlockSpec((d, tn), lambda i, j: (0, j)),
    ]
    o_spec = pl.BlockSpec((1, tm, tn), lambda i, j: (i // spt, i % spt, j))
    scratch = [pltpu.VMEM((tm, d), BF16)]
    if w_f is None:
        return pl.pallas_call(
            _normproj_kernel,
            out_shape=jax.ShapeDtypeStruct((b, s, ncols), BF16),
            grid=grid, in_specs=in_specs, out_specs=o_spec, scratch_shapes=scratch,
            compiler_params=_params(("arbitrary", "arbitrary")),
            name="normproj",
        )(x, gamma.reshape(1, d), sc, sh, w)
    in_specs.append(pl.BlockSpec((d, LANES), lambda i, j: (0, 0)))
    f_spec = pl.BlockSpec((1, tm, LANES), xmap)
    return pl.pallas_call(
        _normproj_gate_kernel,
        out_shape=(jax.ShapeDtypeStruct((b, s, ncols), BF16),
                   jax.ShapeDtypeStruct((b, s, LANES), F32)),
        grid=grid, in_specs=in_specs, out_specs=(o_spec, f_spec), scratch_shapes=scratch,
        compiler_params=_params(("arbitrary", "arbitrary")),
        name="normproj_gate",
    )(x, gamma.reshape(1, d), sc, sh, w, w_f)


def _headnorm(t, gain):
    t = t.astype(F32)
    ms = jnp.mean(t * t, axis=-1, keepdims=True)
    return t * lax.rsqrt(ms + EPS) * gain


def _attn_a_kernel(q_ref, kc_ref, kp_ref, vc_ref, vp_ref, bias_ref, gq_ref, gk_ref,
                   o_ref, lse_ref):
    gq = gq_ref[...] * ATTN_SCALE
    gk = gk_ref[...]
    for h in range(N_HEADS):
        sl = slice(h * HEAD_DIM, (h + 1) * HEAD_DIM)
        qn = _headnorm(q_ref[0, :, sl], gq).astype(BF16)
        kc = _headnorm(kc_ref[0, :, sl], gk).astype(BF16)
        kp = _headnorm(kp_ref[0, :, sl], gk).astype(BF16)
        s_p = _dot_nt(qn, kp) + bias_ref[0, h, :, :Q_BLOCK]
        s_c = _dot_nt(qn, kc) + bias_ref[0, h, :, Q_BLOCK:]
        m = jnp.maximum(jnp.max(s_p, axis=-1, keepdims=True),
                        jnp.max(s_c, axis=-1, keepdims=True))
        p_p = jnp.exp(s_p - m)
        p_c = jnp.exp(s_c - m)
        l = jnp.sum(p_p, axis=-1, keepdims=True) + jnp.sum(p_c, axis=-1, keepdims=True)
        o = _dot(p_p.astype(BF16), vp_ref[0, :, sl]) + _dot(p_c.astype(BF16), vc_ref[0, :, sl])
        o_ref[0, :, sl] = (o / l).astype(BF16)
        lse_ref[0, 0, :, h:h + 1] = m + jnp.log(l)


def _t5_bucket(dist):
    max_exact = REL_BUCKETS // 2
    d = jnp.maximum(dist, 0)
    large = max_exact + (jnp.log(jnp.maximum(d, 1).astype(F32) / max_exact)
                         / math.log(REL_MAX_DIST / max_exact)
                         * (REL_BUCKETS - max_exact)).astype(jnp.int32)
    large = jnp.minimum(large, REL_BUCKETS - 1)
    return jnp.where(d < max_exact, d, large)


def _bias_tables(rel_bias_g, window, dilation):
    qi = jnp.arange(Q_BLOCK)[:, None]
    kj = jnp.arange(2 * Q_BLOCK)[None, :]
    dist = qi + Q_BLOCK - kj
    bias = rel_bias_g[_t5_bucket(dist * dilation)].transpose(2, 0, 1).astype(F32)
    band = (dist >= 0) & (dist <= window // dilation)
    first = band & (qi - dist >= 0)
    return jnp.stack([jnp.where(first[None], bias, NEG_INF),
                      jnp.where(band[None], bias, NEG_INF)], axis=0)


def _attn_a_group(proj, g, rel_bias_g, gq, gk, window, dilation):
    b, s, cols = proj.shape
    r = dilation
    n = s // r
    assert n % Q_BLOCK == 0
    nb = n // Q_BLOCK
    gw = N_HEADS * HEAD_DIM
    cpr = cols // gw
    pv = proj.reshape(b, n, r * cols)
    tables = _bias_tables(rel_bias_g, window, dilation)
    base = g * 3

    def cur(which):
        return pl.BlockSpec((1, Q_BLOCK, gw),
                            lambda qb, bi, j: (bi, qb, j * cpr + base + which))

    def prev(which):
        return pl.BlockSpec((1, Q_BLOCK, gw),
                            lambda qb, bi, j: (bi, jnp.maximum(qb - 1, 0), j * cpr + base + which))

    o, lse = pl.pallas_call(
        _attn_a_kernel,
        out_shape=(jax.ShapeDtypeStruct((b, n, r * gw), BF16),
                   jax.ShapeDtypeStruct((b, r, n, N_HEADS), F32)),
        grid=(nb, b, r),
        in_specs=[
            cur(0), cur(1), prev(1), cur(2), prev(2),
            pl.BlockSpec((1, N_HEADS, Q_BLOCK, 2 * Q_BLOCK),
                         lambda qb, bi, j: (jnp.minimum(qb, 1), 0, 0, 0)),
            pl.BlockSpec((1, HEAD_DIM), lambda qb, bi, j: (0, 0)),
            pl.BlockSpec((1, HEAD_DIM), lambda qb, bi, j: (0, 0)),
        ],
        out_specs=(
            pl.BlockSpec((1, Q_BLOCK, gw), lambda qb, bi, j: (bi, qb, j)),
            pl.BlockSpec((1, 1, Q_BLOCK, N_HEADS), lambda qb, bi, j: (bi, j, qb, 0)),
        ),
        compiler_params=_params(("arbitrary", "arbitrary", "arbitrary")),
        name=f"attn_a_g{g}",
    )(pv, pv, pv, pv, pv, tables, gq.reshape(1, HEAD_DIM), gk.reshape(1, HEAD_DIM))
    o = o.reshape(b, s, gw)
    lse = lse.transpose(0, 2, 1, 3).reshape(b, s, N_HEADS)
    return o, lse


def _merge_out_kernel(o0_ref, o1_ref, o2_ref, l0_ref, l1_ref, l2_ref, x_ref, g_ref, w_ref,
                      e_ref, out_ref):
    l0, l1, l2 = l0_ref[0], l1_ref[0], l2_ref[0]
    m = jnp.maximum(jnp.maximum(l0, l1), l2)
    e0, e1, e2 = jnp.exp(l0 - m), jnp.exp(l1 - m), jnp.exp(l2 - m)
    inv = 1.0 / (e0 + e1 + e2)
    expand = e_ref[...]

    def widen(w):
        hi, mid, lo = _split3(w)
        return _dot(hi, expand) + _dot(mid, expand) + _dot(lo, expand)

    o = (widen(e0 * inv) * o0_ref[0].astype(F32)
         + widen(e1 * inv) * o1_ref[0].astype(F32)
         + widen(e2 * inv) * o2_ref[0].astype(F32))
    y = _dot(o.astype(BF16), w_ref[...])
    out_ref[0] = x_ref[0] + g_ref[0] * y


def _merge_out(outs, lses, x, gate, w_out, tm=512):
    b, s, d = x.shape
    spt = s // tm
    xmap = lambda i: (i // spt, i % spt, 0)
    expand = (jnp.arange(d)[None, :] // HEAD_DIM == jnp.arange(N_HEADS)[:, None]).astype(BF16)
    o_spec = pl.BlockSpec((1, tm, d), xmap)
    l_spec = pl.BlockSpec((1, tm, N_HEADS), xmap)
    return pl.pallas_call(
        _merge_out_kernel,
        out_shape=jax.ShapeDtypeStruct((b, s, d), F32),
        grid=(b * spt,),
        in_specs=[o_spec, o_spec, o_spec, l_spec, l_spec, l_spec,
                  pl.BlockSpec((1, tm, d), xmap),
                  pl.BlockSpec((1, 1, d), lambda i: (i // spt, 0, 0)),
                  pl.BlockSpec((d, d), lambda i: (0, 0)),
                  pl.BlockSpec((N_HEADS, d), lambda i: (0, 0))],
        out_specs=pl.BlockSpec((1, tm, d), xmap),
        compiler_params=_params(("arbitrary",)),
        name="merge_out",
    )(*outs, *lses, x, gate, w_out, expand)


def _out_kernel(o_ref, x_ref, g_ref, w_ref, out_ref):
    out_ref[0] = x_ref[0] + g_ref[0] * _dot(o_ref[0], w_ref[...])


def _out_proj(o, x, gate, w_out, tm=512):
    b, s, d = x.shape
    spt = s // tm
    xmap = lambda i: (i // spt, i % spt, 0)
    return pl.pallas_call(
        _out_kernel,
        out_shape=jax.ShapeDtypeStruct((b, s, d), F32),
        grid=(b * spt,),
        in_specs=[pl.BlockSpec((1, tm, d), xmap),
                  pl.BlockSpec((1, tm, d), xmap),
                  pl.BlockSpec((1, 1, d), lambda i: (i // spt, 0, 0)),
                  pl.BlockSpec((d, d), lambda i: (0, 0))],
        out_specs=pl.BlockSpec((1, tm, d), xmap),
        compiler_params=_params(("arbitrary",)),
        name="out_proj",
    )(o, x, gate, w_out)


def _cum_kernel(f_ref, fb_ref, tri_ref, o_ref, carry_ref):
    @pl.when(pl.program_id(1) == 0)
    def _():
        carry_ref[...] = jnp.zeros_like(carry_ref)

    z = f_ref[0] + fb_ref[...]
    logf = jnp.minimum(z, 0.0) - jnp.log(1.0 + jnp.exp(-jnp.abs(z)))
    hi, mid, lo = _split3(logf)
    tri = tri_ref[...]
    cum = _dot(tri, hi) + _dot(tri, mid) + _dot(tri, lo) + carry_ref[...]
    o_ref[0] = cum
    carry_ref[...] = cum[-1:, :]


def _cum_forget(fg, f_bias, tc=512):
    b, s, w = fg.shape
    fb = jnp.pad(f_bias, (0, w - f_bias.shape[0])).reshape(1, w)
    tri = (jnp.arange(tc)[:, None] >= jnp.arange(tc)[None, :]).astype(BF16)
    return pl.pallas_call(
        _cum_kernel,
        out_shape=jax.ShapeDtypeStruct((b, s, w), F32),
        grid=(b, s // tc),
        in_specs=[pl.BlockSpec((1, tc, w), lambda i, j: (i, j, 0)),
                  pl.BlockSpec((1, w), lambda i, j: (0, 0)),
                  pl.BlockSpec((tc, tc), lambda i, j: (0, 0))],
        out_specs=pl.BlockSpec((1, tc, w), lambda i, j: (i, j, 0)),
        scratch_shapes=[pltpu.VMEM((1, w), F32)],
        compiler_params=_params(("arbitrary", "arbitrary")),
        name="cum_forget",
    )(fg, fb, tri)


def _gate_columns(cum_col, lead):
    rows = cum_col.shape[0]
    lane = lax.broadcasted_iota(jnp.int32, (rows, HEAD_DIM), 1)
    hi, mid, lo = (p.astype(F32) for p in _split3(cum_col if lead else -cum_col))
    off = 0 if lead else 3
    ext = jnp.where(lane < 6, 1.0, 0.0)
    ext = jnp.where(lane == off, hi, ext)
    ext = jnp.where(lane == off + 1, mid, ext)
    ext = jnp.where(lane == off + 2, lo, ext)
    return ext.astype(BF16)


def _pick_lane(tile, idx):
    lane = lax.broadcasted_iota(jnp.int32, tile.shape, 1)
    return jnp.sum(jnp.where(lane == idx, tile, 0.0), axis=-1, keepdims=True)


LOG2E = 1.4426950408889634


def _fox_kernel(q_ref, k_ref, v_ref, cq_ref, ck_ref, gq_ref, gk_ref, o_ref, ka_ref, vt_ref,
                s_ref, p_ref, *, tq, tk):
    hp = pl.program_id(1)
    qi = pl.program_id(2)
    n_kt = vt_ref.shape[1]

    @pl.when(qi == 0)
    def _():
        gk = gk_ref[...]
        vt = v_ref[0].astype(F32).T.astype(BF16)
        for hh in range(2):
            sl = slice(hh * HEAD_DIM, (hh + 1) * HEAD_DIM)
            ka_ref[hh, :, :HEAD_DIM] = _headnorm(k_ref[0, :, sl], gk).astype(BF16)
            ck = _pick_lane(ck_ref[0], 2 * hp + hh) * LOG2E
            ka_ref[hh, :, HEAD_DIM:] = _gate_columns(ck, lead=False)
            for jj in range(n_kt):
                vt_ref[hh, jj] = vt[sl, jj * tk:(jj + 1) * tk]

    gq = gq_ref[...] * (ATTN_SCALE * LOG2E)
    qat = []
    for hh in range(2):
        sl = slice(hh * HEAD_DIM, (hh + 1) * HEAD_DIM)
        qn = _headnorm(q_ref[0, :, sl], gq)
        cq = _pick_lane(cq_ref[0], 2 * hp + hh) * LOG2E
        qa = jnp.concatenate([qn, _gate_columns(cq, lead=True).astype(F32)], axis=1)
        qat.append(qa.T.astype(BF16))
    key = lax.broadcasted_iota(jnp.int32, (tk, tq), 0)
    qry = lax.broadcasted_iota(jnp.int32, (tk, tq), 1)
    ahead = key - qry

    def scores(t, hh):
        start = pl.multiple_of(t * tk, tk)
        st = _dot(ka_ref[hh, pl.ds(start, tk), :], qat[hh])
        limit = jnp.where(t == qi, 0, tk)
        st = jnp.where(ahead > limit, NEG_INF, st)
        s_ref[hh] = st
        return jnp.max(st, axis=0, keepdims=True)

    def trip(i, carry):
        out = []
        for hh in range(2):
            m, l, alpha_prev, acc, mx = carry[hh]
            acc = alpha_prev * acc + _dot(vt_ref[hh, jnp.maximum(i - 1, 0)], p_ref[hh])
            m_new = jnp.maximum(m, mx)
            alpha = jnp.exp2(m - m_new)
            p = jnp.exp2(s_ref[hh] - m_new)
            l = alpha * l + jnp.sum(p, axis=0, keepdims=True)
            p_ref[hh] = p.astype(BF16)
            mx = scores(jnp.minimum(i + 1, qi), hh)
            out.append((m_new, l, alpha, acc, mx))
        return tuple(out)

    p_ref[...] = jnp.zeros_like(p_ref)
    init = tuple((jnp.full((1, tq), NEG_INF, F32), jnp.zeros((1, tq), F32),
                  jnp.ones((1, tq), F32), jnp.zeros((HEAD_DIM, tq), F32), scores(0, hh))
                 for hh in range(2))
    carry = lax.fori_loop(0, qi + 1, trip, init)
    o_t = []
    for hh in range(2):
        _, l, alpha, acc, _ = carry[hh]
        acc = alpha * acc + _dot(vt_ref[hh, qi], p_ref[hh])
        o_t.append(acc / l)
    o_ref[0] = jnp.concatenate(o_t, axis=0).T.astype(BF16)


def _fox_attention(qkv, cum, gq, gk, tq=256):
    b, s, _ = qkv.shape
    tk = tq
    nq = s // tq
    pairs = N_HEADS // 2
    pw = 2 * HEAD_DIM
    return pl.pallas_call(
        functools.partial(_fox_kernel, tq=tq, tk=tk),
        out_shape=jax.ShapeDtypeStruct((b, s, N_HEADS * HEAD_DIM), BF16),
        grid=(b, pairs, nq),
        in_specs=[
            pl.BlockSpec((1, tq, pw), lambda bi, hp, qi: (bi, qi, hp)),
            pl.BlockSpec((1, s, pw), lambda bi, hp, qi: (bi, 0, pairs + hp)),
            pl.BlockSpec((1, s, pw), lambda bi, hp, qi: (bi, 0, 2 * pairs + hp)),
            pl.BlockSpec((1, tq, LANES), lambda bi, hp, qi: (bi, qi, 0)),
            pl.BlockSpec((1, s, LANES), lambda bi, hp, qi: (bi, 0, 0)),
            pl.BlockSpec((1, HEAD_DIM), lambda bi, hp, qi: (0, 0)),
            pl.BlockSpec((1, HEAD_DIM), lambda bi, hp, qi: (0, 0)),
        ],
        out_specs=pl.BlockSpec((1, tq, pw), lambda bi, hp, qi: (bi, qi, hp)),
        scratch_shapes=[pltpu.VMEM((2, s, pw), BF16),
                        pltpu.VMEM((2, s // tk, HEAD_DIM, tk), BF16),
                        pltpu.VMEM((2, tk, tq), F32),
                        pltpu.VMEM((2, tk, tq), BF16)],
        compiler_params=_params(("arbitrary", "arbitrary", "arbitrary")),
        name="fox_attn",
    )(qkv, qkv, qkv, cum, cum, gq.reshape(1, HEAD_DIM), gk.reshape(1, HEAD_DIM))


ROUTER_LANES = LANES


def _route(logits):
    lane = lax.broadcasted_iota(jnp.int32, logits.shape, 1)
    lane_f = lane.astype(F32)
    big = float(ROUTER_LANES)
    is_g = lane < N_GROUPS
    gl = jnp.where(is_g, logits, -jnp.inf)
    gmax = jnp.max(gl, axis=-1, keepdims=True)
    gsum = jnp.sum(jnp.where(is_g, jnp.exp(logits - gmax), 0.0), axis=-1, keepdims=True)
    g_w = 1.0 / gsum
    gidx = jnp.min(jnp.where(gl == gmax, lane_f, big), axis=-1, keepdims=True)
    lo = N_GROUPS + EXPERTS_PER_GROUP * gidx
    in_grp = (lane_f >= lo) & (lane_f < lo + EXPERTS_PER_GROUP)
    el = jnp.where(in_grp, logits, -jnp.inf)
    t1 = jnp.max(el, axis=-1, keepdims=True)
    i1 = jnp.min(jnp.where(el == t1, lane_f, big), axis=-1, keepdims=True)
    el2 = jnp.where(lane_f == i1, -jnp.inf, el)
    t2 = jnp.max(el2, axis=-1, keepdims=True)
    i2 = jnp.min(jnp.where(el2 == t2, lane_f, big), axis=-1, keepdims=True)
    e2 = jnp.exp(t2 - t1)
    w1 = g_w / (1.0 + e2)
    w2 = w1 * e2
    return jnp.where(lane_f == i1, w1, jnp.where(lane_f == i2, w2, 0.0))


def _moe_kernel(x_ref, gam_ref, sc_ref, sh_ref, gate_ref, wr_ref, br_ref,
                wg_ref, wu_ref, wd_ref, out_ref, h_ref, gates_ref, acc_ref):
    e = pl.program_id(1)

    @pl.when(e == 0)
    def _():
        h = _modnorm(x_ref[0], gam_ref[...], sc_ref[0], sh_ref[0])
        h_hi = h.astype(BF16)
        h_lo = (h - h_hi.astype(F32)).astype(BF16)
        wr = wr_ref[...]
        w_hi = wr.astype(BF16)
        w_lo = (wr - w_hi.astype(F32)).astype(BF16)
        logits = _dot(h_hi, w_hi) + _dot(h_hi, w_lo) + _dot(h_lo, w_hi) + br_ref[...]
        gates_ref[...] = _route(logits)
        h_ref[...] = h_hi
        acc_ref[...] = jnp.zeros_like(acc_ref)

    h = h_ref[...]
    a = _dot(h, wg_ref[0])
    u = _dot(h, wu_ref[0])
    gcol = _pick_lane(gates_ref[...], N_GROUPS + e)
    hid = a * (1.0 / (1.0 + jnp.exp(-a))) * u * gcol
    acc_ref[...] += _dot(hid.astype(BF16), wd_ref[0])

    @pl.when(e == N_EXPERTS - 1)
    def _():
        out_ref[0] = x_ref[0] + gate_ref[0] * acc_ref[...]


def _moe(x, gamma, sc, sh, gate, w_router, b_router, w_gate, w_up, w_down, tm=512):
    b, s, d = x.shape
    spt = s // tm
    xmap = lambda i, e: (i // spt, i % spt, 0)
    bmap = lambda i, e: (i // spt, 0, 0)
    return pl.pallas_call(
        _moe_kernel,
        out_shape=jax.ShapeDtypeStruct((b, s, d), F32),
        grid=(b * spt, N_EXPERTS),
        in_specs=[
            pl.BlockSpec((1, tm, d), xmap),
            pl.BlockSpec((1, d), lambda i, e: (0, 0)),
            pl.BlockSpec((1, 1, d), bmap),
            pl.BlockSpec((1, 1, d), bmap),
            pl.BlockSpec((1, 1, d), bmap),
            pl.BlockSpec((d, ROUTER_LANES), lambda i, e: (0, 0)),
            pl.BlockSpec((1, ROUTER_LANES), lambda i, e: (0, 0)),
            pl.BlockSpec((1, d, EXPERT_FF), lambda i, e: (e, 0, 0)),
            pl.BlockSpec((1, d, EXPERT_FF), lambda i, e: (e, 0, 0)),
            pl.BlockSpec((1, EXPERT_FF, d), lambda i, e: (e, 0, 0)),
        ],
        out_specs=pl.BlockSpec((1, tm, d), xmap),
        scratch_shapes=[pltpu.VMEM((tm, d), BF16),
                        pltpu.VMEM((tm, ROUTER_LANES), F32),
                        pltpu.VMEM((tm, d), F32)],
        compiler_params=_params(("arbitrary", "arbitrary")),
        name="moe",
    )(x, gamma.reshape(1, d), sc, sh, gate, w_router, b_router, w_gate, w_up, w_down)


def kernel(x, c, w_ada, b_ada, norm_mix, norm_ffn, rel_bias, a_w_in, a_w_out, a_q_norm,
           a_k_norm, b_w_in, b_f_bias, b_w_out, b_q_norm, b_k_norm, router_group_w,
           router_group_b, router_expert_w, router_expert_b, w_gate, w_up, w_down):
    b, s, d = x.shape
    depth = w_ada.shape[0]
    mod = _ada(c, w_ada, b_ada).reshape(depth, b, 6, 1, d)
    pad_r = ROUTER_LANES - N_GROUPS - N_EXPERTS
    for i in range(depth):
        sh_m, sc_m, g_m, sh_f, sc_f, g_f = (mod[i, :, k] for k in range(6))
        j = i // 2
        if i % 2 == 0:
            proj = _normproj(x, norm_mix[i], sc_m, sh_m, a_w_in[j].astype(BF16))
            outs, lses = [], []
            for g, (window, dil) in enumerate(A_DILATED):
                tab = rel_bias[:, g * N_HEADS:(g + 1) * N_HEADS]
                o, l = _attn_a_group(proj, g, tab, a_q_norm[j], a_k_norm[j], window, dil)
                outs.append(o)
                lses.append(l)
            x = _merge_out(outs, lses, x, g_m, a_w_out[j].astype(BF16))
        else:
            n_qkv = 3 * N_HEADS * HEAD_DIM
            w_in = b_w_in[j]
            w_f = jnp.pad(w_in[:, n_qkv:], ((0, 0), (0, LANES - N_HEADS))).astype(BF16)
            qkv, fg = _normproj(x, norm_mix[i], sc_m, sh_m, w_in[:, :n_qkv].astype(BF16), w_f)
            cum = _cum_forget(fg, b_f_bias[j])
            o = _fox_attention(qkv, cum, b_q_norm[j], b_k_norm[j])
            x = _out_proj(o, x, g_m, b_w_out[j].astype(BF16))
        w_router = jnp.pad(jnp.concatenate([router_group_w[i], router_expert_w[i]], axis=1),
                           ((0, 0), (0, pad_r)))
        b_router = jnp.pad(jnp.concatenate([router_group_b[i], router_expert_b[i]]),
                           (0, pad_r)).reshape(1, ROUTER_LANES)
        x = _moe(x, norm_ffn[i], sc_f, sh_f, g_f, w_router, b_router,
                 w_gate[i].astype(BF16), w_up[i].astype(BF16), w_down[i].astype(BF16))
    return x
```

```python
import functools
import math

import jax
import jax.numpy as jnp
from jax import lax
from jax.experimental import pallas as pl
from jax.experimental.pallas import tpu as pltpu

F32 = jnp.float32
BF16 = jnp.bfloat16

D_MODEL = 1024
HEAD_DIM = 64
N_HEADS = 16
ATTN_SCALE = HEAD_DIM ** -0.5
A_DILATED = ((128, 1), (512, 4), (2048, 16))
Q_BLOCK = 128
REL_BUCKETS = 32
REL_MAX_DIST = 2048
N_GROUPS = 4
EXPERTS_PER_GROUP = 4
N_EXPERTS = 16
EXPERT_FF = 256
EPS = 1e-6
NEG_INF = -1e30

LANES = 128
VMEM_LIMIT = 48 * 1024 * 1024
LOG2E = 1.4426950408889634
LN2 = 0.6931471805599453
PERM_TILE = 1024
PAIR = 2 * HEAD_DIM
N_PAIRS = N_HEADS // 2


def _params(sem):
    return pltpu.CompilerParams(dimension_semantics=sem, vmem_limit_bytes=VMEM_LIMIT)


def _split3(x):
    hi = x.astype(BF16)
    r1 = x - hi.astype(F32)
    mid = r1.astype(BF16)
    lo = (r1 - mid.astype(F32)).astype(BF16)
    return hi, mid, lo


def _dot(a, b):
    return jnp.dot(a, b, preferred_element_type=F32)


def _dot_nt(a, b):
    return lax.dot_general(a, b, (((1,), (1,)), ((), ())), preferred_element_type=F32)


def _ada_kernel(c_ref, w_ref, b_ref, o_ref):
    c = c_ref[...]
    a = c * (1.0 / (1.0 + jnp.exp(-c)))
    o_ref[0] = _dot(a.astype(BF16), w_ref[0].astype(BF16)) + b_ref[0]


def _ada(c, w_ada, b_ada):
    depth, d, n6 = w_ada.shape
    b = c.shape[0]
    rows = 8
    c_pad = jnp.pad(c, ((0, rows - b), (0, 0)))
    tn = 1536
    out = pl.pallas_call(
        _ada_kernel,
        out_shape=jax.ShapeDtypeStruct((depth, rows, n6), F32),
        grid=(depth, n6 // tn),
        in_specs=[
            pl.BlockSpec((rows, d), lambda i, j: (0, 0)),
            pl.BlockSpec((1, d, tn), lambda i, j: (i, 0, j)),
            pl.BlockSpec((1, 1, tn), lambda i, j: (i, 0, j)),
        ],
        out_specs=pl.BlockSpec((1, rows, tn), lambda i, j: (i, 0, j)),
        compiler_params=_params(("arbitrary", "arbitrary")),
        name="ada_mod",
    )(c_pad, w_ada, b_ada.reshape(depth, 1, n6))
    return out[:, :b]


def _modnorm(x, gamma, sc, sh):
    ms = jnp.mean(x * x, axis=-1, keepdims=True)
    y = x * lax.rsqrt(ms + EPS) * gamma
    return y * (1.0 + sc) + sh


def _normproj_kernel(x_ref, g_ref, sc_ref, sh_ref, w_ref, o_ref, h_ref):
    @pl.when(pl.program_id(1) == 0)
    def _():
        h_ref[...] = _modnorm(x_ref[0], g_ref[...], sc_ref[0], sh_ref[0]).astype(BF16)

    o_ref[0] = _dot(h_ref[...], w_ref[...]).astype(BF16)


def _normproj_gate_kernel(x_ref, g_ref, sc_ref, sh_ref, w_ref, wf_ref, o_ref, f_ref, h_ref):
    @pl.when(pl.program_id(1) == 0)
    def _():
        h = _modnorm(x_ref[0], g_ref[...], sc_ref[0], sh_ref[0]).astype(BF16)
        h_ref[...] = h
        f_ref[0] = _dot(h, wf_ref[...])

    o_ref[0] = _dot(h_ref[...], w_ref[...]).astype(BF16)


def _normproj(x, gamma, sc, sh, w, w_f=None, tm=1024, tn=1024):
    b, s, d = x.shape
    ncols = w.shape[1]
    spt = s // tm
    grid = (b * spt, ncols // tn)
    xmap = lambda i, j: (i // spt, i % spt, 0)
    bmap = lambda i, j: (i // spt, 0, 0)
    in_specs = [
        pl.BlockSpec((1, tm, d), xmap),
        pl.BlockSpec((1, d), lambda i, j: (0, 0)),
        pl.BlockSpec((1, 1, d), bmap),
        pl.BlockSpec((1, 1, d), bmap),
        pl.BlockSpec((d, tn), lambda i, j: (0, j)),
    ]
    o_spec = pl.BlockSpec((1, tm, tn), lambda i, j: (i // spt, i % spt, j))
    scratch = [pltpu.VMEM((tm, d), BF16)]
    if w_f is None:
        return pl.pallas_call(
            _normproj_kernel,
            out_shape=jax.ShapeDtypeStruct((b, s, ncols), BF16),
            grid=grid, in_specs=in_specs, out_specs=o_spec, scratch_shapes=scratch,
            compiler_params=_params(("arbitrary", "arbitrary")),
            name="normproj",
        )(x, gamma.reshape(1, d), sc, sh, w)
    in_specs.append(pl.BlockSpec((d, LANES), lambda i, j: (0, 0)))
    f_spec = pl.BlockSpec((1, tm, LANES), xmap)
    return pl.pallas_call(
        _normproj_gate_kernel,
        out_shape=(jax.ShapeDtypeStruct((b, s, ncols), BF16),
                   jax.ShapeDtypeStruct((b, s, LANES), F32)),
        grid=grid, in_specs=in_specs, out_specs=(o_spec, f_spec), scratch_shapes=scratch,
        compiler_params=_params(("arbitrary", "arbitrary")),
        name="normproj_gate",
    )(x, gamma.reshape(1, d), sc, sh, w, w_f)


def _normproj_a_kernel(x_ref, g_ref, sc_ref, sh_ref, w_ref, o_ref, hf_ref, hp_ref):
    tm = x_ref.shape[1]
    n_slab = x_ref.shape[2] // LANES

    @pl.when(pl.program_id(1) == 0)
    def _():
        h = _modnorm(x_ref[0], g_ref[...], sc_ref[0], sh_ref[0])
        hp_ref[0] = h.astype(BF16)
        for c in range(n_slab):
            hf_ref[c] = h[:, c * LANES:(c + 1) * LANES]
        for g, (_, r) in enumerate(A_DILATED):
            if r == 1:
                continue
            n = tm // r
            for jr in range(r):
                for c in range(n_slab):
                    hp_ref[g, jr * n:(jr + 1) * n, c * LANES:(c + 1) * LANES] = (
                        hf_ref[c, pl.ds(jr, n, stride=r), :].astype(BF16))

    o_ref[0] = _dot(hp_ref[pl.program_id(1) // 3], w_ref[...]).astype(BF16)


def _normproj_a(x, gamma, sc, sh, w, tn=1024):
    b, s, d = x.shape
    tm = PERM_TILE
    ncols = w.shape[1]
    spt = s // tm
    bmap = lambda i, j: (i // spt, 0, 0)
    return pl.pallas_call(
        _normproj_a_kernel,
        out_shape=jax.ShapeDtypeStruct((b, s, ncols), BF16),
        grid=(b * spt, ncols // tn),
        in_specs=[
            pl.BlockSpec((1, tm, d), lambda i, j: (i // spt, i % spt, 0)),
            pl.BlockSpec((1, d), lambda i, j: (0, 0)),
            pl.BlockSpec((1, 1, d), bmap),
            pl.BlockSpec((1, 1, d), bmap),
            pl.BlockSpec((d, tn), lambda i, j: (0, j)),
        ],
        out_specs=pl.BlockSpec((1, tm, tn), lambda i, j: (i // spt, i % spt, j)),
        scratch_shapes=[pltpu.VMEM((d // LANES, tm, LANES), F32),
                        pltpu.VMEM((len(A_DILATED), tm, d), BF16)],
        compiler_params=_params(("arbitrary", "arbitrary")),
        name="normproj_a",
    )(x, gamma.reshape(1, d), sc, sh, w)


def _headnorm(t, gain):
    t = t.astype(F32)
    ms = jnp.mean(t * t, axis=-1, keepdims=True)
    return t * lax.rsqrt(ms + EPS) * gain


def _pairnorm(t, gain2, even):
    t = t.astype(F32)
    sq = t * t
    tot = jnp.sum(sq, axis=-1, keepdims=True)
    s_even = jnp.sum(jnp.where(even, sq, 0.0), axis=-1, keepdims=True)
    inv = jnp.where(even, lax.rsqrt(s_even * (1.0 / HEAD_DIM) + EPS),
                    lax.rsqrt((tot - s_even) * (1.0 / HEAD_DIM) + EPS))
    return t * inv * gain2


def _rows(ref, cols=slice(None)):
    if len(ref.shape) == 2:
        return ref[:, cols]
    return jnp.concatenate([ref[i, :, cols] for i in range(ref.shape[0])], axis=0)


def _store_rows(ref, cols, val):
    if len(ref.shape) == 2:
        ref[:, cols] = val
    else:
        n = ref.shape[1]
        for i in range(ref.shape[0]):
            ref[i, :, cols] = val[i * n:(i + 1) * n]


def _attn_a_kernel(q_ref, k_ref, v_ref, bias_ref, gq_ref, gk_ref, o_ref, lse_ref,
                   kn_ref, vv_ref):
    ub = pl.program_id(2)
    slot = ub % 2
    prev = 1 - slot

    @pl.when(ub == 0)
    def _():
        kn_ref[1] = jnp.zeros(kn_ref.shape[1:], BF16)
        vv_ref[1] = jnp.zeros(vv_ref.shape[1:], BF16)

    first = jnp.minimum(ub, 1)
    lane = lax.broadcasted_iota(jnp.int32, (Q_BLOCK, PAIR), 1)
    even = lane < HEAD_DIM
    lane_row = lax.broadcasted_iota(jnp.int32, (1, PAIR), 1)
    keep_even = jnp.where(lane_row < HEAD_DIM, 1.0, 0.0).astype(BF16)
    keep_odd = jnp.where(lane_row < HEAD_DIM, 0.0, 1.0).astype(BF16)
    gq2 = gq_ref[...] * (ATTN_SCALE * LOG2E)
    gk2 = gk_ref[...]
    for hp in range(N_PAIRS):
        cs = slice(hp * PAIR, (hp + 1) * PAIR)
        kn_ref[slot, :, cs] = _pairnorm(_rows(k_ref, cs), gk2, even).astype(BF16)
    vv_ref[slot] = _rows(v_ref)
    lse_tile = jnp.zeros((Q_BLOCK, LANES), F32)
    for hp in range(N_PAIRS):
        cs = slice(hp * PAIR, (hp + 1) * PAIR)
        qn = _pairnorm(_rows(q_ref, cs), gq2, even)
        qq = jnp.concatenate([jnp.where(even, qn, 0.0), jnp.where(even, 0.0, qn)],
                             axis=0).astype(BF16)
        kcat = jnp.concatenate([kn_ref[prev, :, cs], kn_ref[slot, :, cs]], axis=0)
        s = _dot_nt(qq, kcat) + bias_ref[first, hp]
        m = jnp.max(s, axis=-1, keepdims=True)
        p = jnp.exp2(s - m)
        l = jnp.sum(p, axis=-1, keepdims=True)
        pb = p.astype(BF16)
        vcat = jnp.concatenate([vv_ref[prev, :, cs], vv_ref[slot, :, cs]], axis=0)
        o = (_dot(pb[:Q_BLOCK], vcat * keep_even)
             + _dot(pb[Q_BLOCK:], vcat * keep_odd))
        inv = 1.0 / l
        o = o * jnp.where(even, inv[:Q_BLOCK], inv[Q_BLOCK:])
        _store_rows(o_ref, cs, o.astype(BF16))
        lse = m * LN2 + jnp.log(l)
        lse_tile = jnp.where(lane == 2 * hp, lse[:Q_BLOCK],
                             jnp.where(lane == 2 * hp + 1, lse[Q_BLOCK:], lse_tile))
    _store_rows(lse_ref, slice(None), lse_tile)


def _t5_bucket(dist):
    max_exact = REL_BUCKETS // 2
    d = jnp.maximum(dist, 0)
    large = max_exact + (jnp.log(jnp.maximum(d, 1).astype(F32) / max_exact)
                         / math.log(REL_MAX_DIST / max_exact)
                         * (REL_BUCKETS - max_exact)).astype(jnp.int32)
    large = jnp.minimum(large, REL_BUCKETS - 1)
    return jnp.where(d < max_exact, d, large)


def _bias_kernel(tab_ref, onehot_ref, o_ref):
    hi, mid, lo = _split3(tab_ref[...])
    oh = onehot_ref[...]
    o_ref[...] = (_dot(hi, oh) + _dot(mid, oh) + _dot(lo, oh)) * LOG2E


def _bias_tables(rel_bias_g, window, dilation):
    qi = jnp.arange(Q_BLOCK)[:, None]
    kj = jnp.arange(2 * Q_BLOCK)[None, :]
    dist = qi + Q_BLOCK - kj
    bucket = _t5_bucket(dist * dilation).reshape(1, -1)
    onehot = (bucket == jnp.arange(REL_BUCKETS)[:, None]).astype(BF16)
    npos = onehot.shape[1]
    chunk = 4096
    bias = pl.pallas_call(
        _bias_kernel,
        out_shape=jax.ShapeDtypeStruct((N_HEADS, npos), F32),
        grid=(npos // chunk,),
        in_specs=[pl.BlockSpec((N_HEADS, REL_BUCKETS), lambda i: (0, 0)),
                  pl.BlockSpec((REL_BUCKETS, chunk), lambda i: (0, i))],
        out_specs=pl.BlockSpec((N_HEADS, chunk), lambda i: (0, i)),
        compiler_params=_params(("arbitrary",)),
        name="rel_bias_table",
    )(rel_bias_g.T.astype(F32), onehot)
    bias = bias.reshape(N_PAIRS, 2 * Q_BLOCK, 2 * Q_BLOCK)
    band = (dist >= 0) & (dist <= window // dilation)
    first = band & (qi - dist >= 0)
    band = jnp.concatenate([band, band], axis=0)[None]
    first = jnp.concatenate([first, first], axis=0)[None]
    return jnp.stack([jnp.where(first, bias, NEG_INF), jnp.where(band, bias, NEG_INF)], axis=0)


def _attn_a_group(proj, g, rel_bias_g, gq, gk, window, dilation):
    b, s, cols = proj.shape
    r = dilation
    nt = s // PERM_TILE
    n_t = PERM_TILE // r
    nb = s // r // Q_BLOCK
    gw = N_HEADS * HEAD_DIM
    tables = _bias_tables(rel_bias_g, window, dilation)
    base = g * 3
    if n_t >= Q_BLOCK:
        bpt = n_t // Q_BLOCK
        blk = lambda w: (None, None, None, Q_BLOCK, w)
        imap = lambda col: (lambda bi, jr, ub: (bi, ub // bpt, jr, ub % bpt, col))
    else:
        blk = lambda w: (None, Q_BLOCK // n_t, None, n_t, w)
        imap = lambda col: (lambda bi, jr, ub: (bi, ub, jr, 0, col))
    pv = proj.reshape(b, nt, r, n_t, cols)
    gains = lambda t: jnp.concatenate([t, t]).reshape(1, PAIR)
    o, lse = pl.pallas_call(
        _attn_a_kernel,
        out_shape=(jax.ShapeDtypeStruct((b, nt, r, n_t, gw), BF16),
                   jax.ShapeDtypeStruct((b, nt, r, n_t, LANES), F32)),
        grid=(b, r, nb),
        in_specs=[
            pl.BlockSpec(blk(gw), imap(base)),
            pl.BlockSpec(blk(gw), imap(base + 1)),
            pl.BlockSpec(blk(gw), imap(base + 2)),
            pl.BlockSpec(tables.shape, lambda bi, jr, ub: (0, 0, 0, 0)),
            pl.BlockSpec((1, PAIR), lambda bi, jr, ub: (0, 0)),
            pl.BlockSpec((1, PAIR), lambda bi, jr, ub: (0, 0)),
        ],
        out_specs=(pl.BlockSpec(blk(gw), imap(0)), pl.BlockSpec(blk(LANES), imap(0))),
        scratch_shapes=[pltpu.VMEM((2, Q_BLOCK, gw), BF16), pltpu.VMEM((2, Q_BLOCK, gw), BF16)],
        compiler_params=_params(("arbitrary", "arbitrary", "arbitrary")),
        name=f"attn_a_g{g}",
    )(pv, pv, pv, tables, gains(gq), gains(gk))
    return o.reshape(b, s, gw), lse.reshape(b, s, LANES)


MERGE_CHUNK = 128


def _merge_out_kernel(o0_ref, o1_ref, o2_ref, l0_ref, l1_ref, l2_ref, x_ref, g_ref, w_ref,
                      e_ref, out_ref, lun_ref, oun_ref, om_ref):
    tm = o0_ref.shape[1]
    n_slab = o0_ref.shape[2] // LANES

    @pl.when(pl.program_id(1) == 0)
    def _():
        for gi, (o_ref, l_ref) in enumerate(((o1_ref, l1_ref), (o2_ref, l2_ref))):
            r = A_DILATED[gi + 1][1]
            n = tm // r
            for jr in range(r):
                rows = slice(jr * n, (jr + 1) * n)
                lun_ref[gi, pl.ds(jr, n, stride=r), :] = l_ref[0, rows, :]
                for c in range(n_slab):
                    oun_ref[gi, c, pl.ds(jr, n, stride=r), :] = (
                        o_ref[0, rows, c * LANES:(c + 1) * LANES].astype(F32))
        expand = e_ref[...]

        def widen(w):
            hi = w.astype(BF16)
            lo = (w - hi.astype(F32)).astype(BF16)
            return _dot(hi, expand) + _dot(lo, expand)

        for k in range(tm // MERGE_CHUNK):
            rs = slice(k * MERGE_CHUNK, (k + 1) * MERGE_CHUNK)
            l0, l1, l2 = l0_ref[0, rs, :], lun_ref[0, rs, :], lun_ref[1, rs, :]
            m = jnp.maximum(jnp.maximum(l0, l1), l2)
            e0, e1, e2 = jnp.exp(l0 - m), jnp.exp(l1 - m), jnp.exp(l2 - m)
            inv = 1.0 / (e0 + e1 + e2)
            o1 = jnp.concatenate([oun_ref[0, c, rs, :] for c in range(n_slab)], axis=1)
            o2 = jnp.concatenate([oun_ref[1, c, rs, :] for c in range(n_slab)], axis=1)
            o = (widen(e0 * inv) * o0_ref[0, rs, :].astype(F32)
                 + widen(e1 * inv) * o1 + widen(e2 * inv) * o2)
            om_ref[rs, :] = o.astype(BF16)

    out_ref[0] = x_ref[0] + g_ref[0] * _dot(om_ref[...], w_ref[...])


def _merge_out(outs, lses, x, gate, w_out, tn=512):
    b, s, d = x.shape
    tm = PERM_TILE
    spt = s // tm
    rmap = lambda i, j: (i // spt, i % spt, 0)
    cmap = lambda i, j: (i // spt, i % spt, j)
    expand = ((jnp.arange(d)[None, :] // HEAD_DIM == jnp.arange(LANES)[:, None])
              & (jnp.arange(LANES)[:, None] < N_HEADS)).astype(BF16)
    o_spec = pl.BlockSpec((1, tm, d), rmap)
    l_spec = pl.BlockSpec((1, tm, LANES), rmap)
    return pl.pallas_call(
        _merge_out_kernel,
        out_shape=jax.ShapeDtypeStruct((b, s, d), F32),
        grid=(b * spt, d // tn),
        in_specs=[o_spec, o_spec, o_spec, l_spec, l_spec, l_spec,
                  pl.BlockSpec((1, tm, tn), cmap),
                  pl.BlockSpec((1, 1, tn), lambda i, j: (i // spt, 0, j)),
                  pl.BlockSpec((d, tn), lambda i, j: (0, j)),
                  pl.BlockSpec((LANES, d), lambda i, j: (0, 0))],
        out_specs=pl.BlockSpec((1, tm, tn), cmap),
        scratch_shapes=[pltpu.VMEM((2, tm, LANES), F32),
                        pltpu.VMEM((2, d // LANES, tm, LANES), F32),
                        pltpu.VMEM((tm, d), BF16)],
        compiler_params=_params(("arbitrary", "arbitrary")),
        name="merge_out",
    )(*outs, *lses, x, gate, w_out, expand)


def _out_kernel(o_ref, x_ref, g_ref, w_ref, out_ref):
    out_ref[0] = x_ref[0] + g_ref[0] * _dot(o_ref[0], w_ref[...])


def _out_proj(o, x, gate, w_out, tm=512):
    b, s, d = x.shape
    spt = s // tm
    xmap = lambda i: (i // spt, i % spt, 0)
    return pl.pallas_call(
        _out_kernel,
        out_shape=jax.ShapeDtypeStruct((b, s, d), F32),
        grid=(b * spt,),
        in_specs=[pl.BlockSpec((1, tm, d), xmap),
                  pl.BlockSpec((1, tm, d), xmap),
                  pl.BlockSpec((1, 1, d), lambda i: (i // spt, 0, 0)),
                  pl.BlockSpec((d, d), lambda i: (0, 0))],
        out_specs=pl.BlockSpec((1, tm, d), xmap),
        compiler_params=_params(("arbitrary",)),
        name="out_proj",
    )(o, x, gate, w_out)


def _cum_kernel(f_ref, fb_ref, tri_ref, o_ref, carry_ref):
    @pl.when(pl.program_id(1) == 0)
    def _():
        carry_ref[...] = jnp.zeros_like(carry_ref)

    z = f_ref[0] + fb_ref[...]
    logf = jnp.minimum(z, 0.0) - jnp.log(1.0 + jnp.exp(-jnp.abs(z)))
    hi, mid, lo = _split3(logf)
    tri = tri_ref[...]
    cum = _dot(tri, hi) + _dot(tri, mid) + _dot(tri, lo) + carry_ref[...]
    o_ref[0] = cum
    carry_ref[...] = cum[-1:, :]


def _cum_forget(fg, f_bias, tc=512):
    b, s, w = fg.shape
    fb = jnp.pad(f_bias, (0, w - f_bias.shape[0])).reshape(1, w)
    tri = (jnp.arange(tc)[:, None] >= jnp.arange(tc)[None, :]).astype(BF16)
    return pl.pallas_call(
        _cum_kernel,
        out_shape=jax.ShapeDtypeStruct((b, s, w), F32),
        grid=(b, s // tc),
        in_specs=[pl.BlockSpec((1, tc, w), lambda i, j: (i, j, 0)),
                  pl.BlockSpec((1, w), lambda i, j: (0, 0)),
                  pl.BlockSpec((tc, tc), lambda i, j: (0, 0))],
        out_specs=pl.BlockSpec((1, tc, w), lambda i, j: (i, j, 0)),
        scratch_shapes=[pltpu.VMEM((1, w), F32)],
        compiler_params=_params(("arbitrary", "arbitrary")),
        name="cum_forget",
    )(fg, fb, tri)


def _gate_columns(cum_col, lead):
    rows = cum_col.shape[0]
    lane = lax.broadcasted_iota(jnp.int32, (rows, HEAD_DIM), 1)
    hi, mid, lo = (p.astype(F32) for p in _split3(cum_col if lead else -cum_col))
    off = 0 if lead else 3
    ext = jnp.where(lane < 6, 1.0, 0.0)
    ext = jnp.where(lane == off, hi, ext)
    ext = jnp.where(lane == off + 1, mid, ext)
    ext = jnp.where(lane == off + 2, lo, ext)
    return ext.astype(BF16)


def _pick_lane(tile, idx):
    lane = lax.broadcasted_iota(jnp.int32, tile.shape, 1)
    return jnp.sum(jnp.where(lane == idx, tile, 0.0), axis=-1, keepdims=True)


def _fox_kernel(q_ref, k_ref, v_ref, cq_ref, ck_ref, gq_ref, gk_ref, o_ref, ka_ref, vt_ref,
                s_ref, p_ref, *, tq, tk):
    hp = pl.program_id(1)
    qi = pl.program_id(2)
    n_kt = vt_ref.shape[1]

    @pl.when(qi == 0)
    def _():
        gk = gk_ref[...]
        vt = v_ref[0].astype(F32).T.astype(BF16)
        for hh in range(2):
            sl = slice(hh * HEAD_DIM, (hh + 1) * HEAD_DIM)
            ka_ref[hh, :, :HEAD_DIM] = _headnorm(k_ref[0, :, sl], gk).astype(BF16)
            ck = _pick_lane(ck_ref[0], 2 * hp + hh) * LOG2E
            ka_ref[hh, :, HEAD_DIM:] = _gate_columns(ck, lead=False)
            for jj in range(n_kt):
                vt_ref[hh, jj] = vt[sl, jj * tk:(jj + 1) * tk]

    gq = gq_ref[...] * (ATTN_SCALE * LOG2E)
    qat = []
    for hh in range(2):
        sl = slice(hh * HEAD_DIM, (hh + 1) * HEAD_DIM)
        qn = _headnorm(q_ref[0, :, sl], gq)
        cq = _pick_lane(cq_ref[0], 2 * hp + hh) * LOG2E
        qa = jnp.concatenate([qn, _gate_columns(cq, lead=True).astype(F32)], axis=1)
        qat.append(qa.T.astype(BF16))
    key = lax.broadcasted_iota(jnp.int32, (tk, tq), 0)
    qry = lax.broadcasted_iota(jnp.int32, (tk, tq), 1)
    ahead = key - qry

    def scores(t, hh):
        start = pl.multiple_of(t * tk, tk)
        st = _dot(ka_ref[hh, pl.ds(start, tk), :], qat[hh])
        limit = jnp.where(t == qi, 0, tk)
        st = jnp.where(ahead > limit, NEG_INF, st)
        s_ref[hh] = st
        return jnp.max(st, axis=0, keepdims=True)

    def trip(i, carry):
        out = []
        for hh in range(2):
            m, l, alpha_prev, acc, mx = carry[hh]
            acc = alpha_prev * acc + _dot(vt_ref[hh, jnp.maximum(i - 1, 0)], p_ref[hh])
            m_new = jnp.maximum(m, mx)
            alpha = jnp.exp2(m - m_new)
            p = jnp.exp2(s_ref[hh] - m_new)
            l = alpha * l + jnp.sum(p, axis=0, keepdims=True)
            p_ref[hh] = p.astype(BF16)
            mx = scores(jnp.minimum(i + 1, qi), hh)
            out.append((m_new, l, alpha, acc, mx))
        return tuple(out)

    p_ref[...] = jnp.zeros_like(p_ref)
    init = tuple((jnp.full((1, tq), NEG_INF, F32), jnp.zeros((1, tq), F32),
                  jnp.ones((1, tq), F32), jnp.zeros((HEAD_DIM, tq), F32), scores(0, hh))
                 for hh in range(2))
    carry = lax.fori_loop(0, qi + 1, trip, init)
    o_t = []
    for hh in range(2):
        _, l, alpha, acc, _ = carry[hh]
        acc = alpha * acc + _dot(vt_ref[hh, qi], p_ref[hh])
        o_t.append(acc / l)
    o_ref[0] = jnp.concatenate(o_t, axis=0).T.astype(BF16)


def _fox_attention(qkv, cum, gq, gk, tq=256):
    b, s, _ = qkv.shape
    tk = tq
    nq = s // tq
    pairs = N_HEADS // 2
    pw = 2 * HEAD_DIM
    return pl.pallas_call(
        functools.partial(_fox_kernel, tq=tq, tk=tk),
        out_shape=jax.ShapeDtypeStruct((b, s, N_HEADS * HEAD_DIM), BF16),
        grid=(b, pairs, nq),
        in_specs=[
            pl.BlockSpec((1, tq, pw), lambda bi, hp, qi: (bi, qi, hp)),
            pl.BlockSpec((1, s, pw), lambda bi, hp, qi: (bi, 0, pairs + hp)),
            pl.BlockSpec((1, s, pw), lambda bi, hp, qi: (bi, 0, 2 * pairs + hp)),
            pl.BlockSpec((1, tq, LANES), lambda bi, hp, qi: (bi, qi, 0)),
            pl.BlockSpec((1, s, LANES), lambda bi, hp, qi: (bi, 0, 0)),
            pl.BlockSpec((1, HEAD_DIM), lambda bi, hp, qi: (0, 0)),
            pl.BlockSpec((1, HEAD_DIM), lambda bi, hp, qi: (0, 0)),
        ],
        out_specs=pl.BlockSpec((1, tq, pw), lambda bi, hp, qi: (bi, qi, hp)),
        scratch_shapes=[pltpu.VMEM((2, s, pw), BF16),
                        pltpu.VMEM((2, s // tk, HEAD_DIM, tk), BF16),
                        pltpu.VMEM((2, tk, tq), F32),
                        pltpu.VMEM((2, tk, tq), BF16)],
        compiler_params=_params(("arbitrary", "arbitrary", "arbitrary")),
        name="fox_attn",
    )(qkv, qkv, qkv, cum, cum, gq.reshape(1, HEAD_DIM), gk.reshape(1, HEAD_DIM))


ROUTER_LANES = LANES


def _route(logits):
    lane = lax.broadcasted_iota(jnp.int32, logits.shape, 1)
    lane_f = lane.astype(F32)
    big = float(ROUTER_LANES)
    is_g = lane < N_GROUPS
    gl = jnp.where(is_g, logits, -jnp.inf)
    gmax = jnp.max(gl, axis=-1, keepdims=True)
    gsum = jnp.sum(jnp.where(is_g, jnp.exp(logits - gmax), 0.0), axis=-1, keepdims=True)
    g_w = 1.0 / gsum
    gidx = jnp.min(jnp.where(gl == gmax, lane_f, big), axis=-1, keepdims=True)
    lo = N_GROUPS + EXPERTS_PER_GROUP * gidx
    in_grp = (lane_f >= lo) & (lane_f < lo + EXPERTS_PER_GROUP)
    el = jnp.where(in_grp, logits, -jnp.inf)
    t1 = jnp.max(el, axis=-1, keepdims=True)
    i1 = jnp.min(jnp.where(el == t1, lane_f, big), axis=-1, keepdims=True)
    el2 = jnp.where(lane_f == i1, -jnp.inf, el)
    t2 = jnp.max(el2, axis=-1, keepdims=True)
    i2 = jnp.min(jnp.where(el2 == t2, lane_f, big), axis=-1, keepdims=True)
    e2 = jnp.exp(t2 - t1)
    w1 = g_w / (1.0 + e2)
    w2 = w1 * e2
    return jnp.where(lane_f == i1, w1, jnp.where(lane_f == i2, w2, 0.0))


def _moe_kernel(x_ref, gam_ref, sc_ref, sh_ref, gate_ref, wr_ref, br_ref,
                wg_ref, wu_ref, wd_ref, out_ref, h_ref, gates_ref, acc_ref):
    e = pl.program_id(1)

    @pl.when(e == 0)
    def _():
        h = _modnorm(x_ref[0], gam_ref[...], sc_ref[0], sh_ref[0])
        h_hi = h.astype(BF16)
        h_lo = (h - h_hi.astype(F32)).astype(BF16)
        wr = wr_ref[...]
        w_hi = wr.astype(BF16)
        w_lo = (wr - w_hi.astype(F32)).astype(BF16)
        logits = _dot(h_hi, w_hi) + _dot(h_hi, w_lo) + _dot(h_lo, w_hi) + br_ref[...]
        gates_ref[...] = _route(logits)
        h_ref[...] = h_hi
        acc_ref[...] = jnp.zeros_like(acc_ref)

    h = h_ref[...]
    a = _dot(h, wg_ref[0])
    u = _dot(h, wu_ref[0])
    gcol = _pick_lane(gates_ref[...], N_GROUPS + e)
    hid = a * (1.0 / (1.0 + jnp.exp(-a))) * u * gcol
    acc_ref[...] += _dot(hid.astype(BF16), wd_ref[0])

    @pl.when(e == N_EXPERTS - 1)
    def _():
        out_ref[0] = x_ref[0] + gate_ref[0] * acc_ref[...]


def _moe(x, gamma, sc, sh, gate, w_router, b_router, w_gate, w_up, w_down, tm=512):
    b, s, d = x.shape
    spt = s // tm
    xmap = lambda i, e: (i // spt, i % spt, 0)
    bmap = lambda i, e: (i // spt, 0, 0)
    return pl.pallas_call(
        _moe_kernel,
        out_shape=jax.ShapeDtypeStruct((b, s, d), F32),
        grid=(b * spt, N_EXPERTS),
        in_specs=[
            pl.BlockSpec((1, tm, d), xmap),
            pl.BlockSpec((1, d), lambda i, e: (0, 0)),
            pl.BlockSpec((1, 1, d), bmap),
            pl.BlockSpec((1, 1, d), bmap),
            pl.BlockSpec((1, 1, d), bmap),
            pl.BlockSpec((d, ROUTER_LANES), lambda i, e: (0, 0)),
            pl.BlockSpec((1, ROUTER_LANES), lambda i, e: (0, 0)),
            pl.BlockSpec((1, d, EXPERT_FF), lambda i, e: (e, 0, 0)),
            pl.BlockSpec((1, d, EXPERT_FF), lambda i, e: (e, 0, 0)),
            pl.BlockSpec((1, EXPERT_FF, d), lambda i, e: (e, 0, 0)),
        ],
        out_specs=pl.BlockSpec((1, tm, d), xmap),
        scratch_shapes=[pltpu.VMEM((tm, d), BF16),
                        pltpu.VMEM((tm, ROUTER_LANES), F32),
                        pltpu.VMEM((tm, d), F32)],
        compiler_params=_params(("arbitrary", "arbitrary")),
        name="moe",
    )(x, gamma.reshape(1, d), sc, sh, gate, w_router, b_router, w_gate, w_up, w_down)


def kernel(x, c, w_ada, b_ada, norm_mix, norm_ffn, rel_bias, a_w_in, a_w_out, a_q_norm,
           a_k_norm, b_w_in, b_f_bias, b_w_out, b_q_norm, b_k_norm, router_group_w,
           router_group_b, router_expert_w, router_expert_b, w_gate, w_up, w_down):
    b, s, d = x.shape
    depth = w_ada.shape[0]
    mod = _ada(c, w_ada, b_ada).reshape(depth, b, 6, 1, d)
    pad_r = ROUTER_LANES - N_GROUPS - N_EXPERTS
    for i in range(depth):
        sh_m, sc_m, g_m, sh_f, sc_f, g_f = (mod[i, :, k] for k in range(6))
        j = i // 2
        if i % 2 == 0:
            proj = _normproj_a(x, norm_mix[i], sc_m, sh_m, a_w_in[j].astype(BF16))
            outs, lses = [], []
            for g, (window, dil) in enumerate(A_DILATED):
                tab = rel_bias[:, g * N_HEADS:(g + 1) * N_HEADS]
                o, l = _attn_a_group(proj, g, tab, a_q_norm[j], a_k_norm[j], window, dil)
                outs.append(o)
                lses.append(l)
            x = _merge_out(outs, lses, x, g_m, a_w_out[j].astype(BF16))
        else:
            n_qkv = 3 * N_HEADS * HEAD_DIM
            w_in = b_w_in[j]
            w_f = jnp.pad(w_in[:, n_qkv:], ((0, 0), (0, LANES - N_HEADS))).astype(BF16)
            qkv, fg = _normproj(x, norm_mix[i], sc_m, sh_m, w_in[:, :n_qkv].astype(BF16), w_f)
            cum = _cum_forget(fg, b_f_bias[j])
            o = _fox_attention(qkv, cum, b_q_norm[j], b_k_norm[j])
            x = _out_proj(o, x, g_m, b_w_out[j].astype(BF16))
        w_router = jnp.pad(jnp.concatenate([router_group_w[i], router_expert_w[i]], axis=1),
                           ((0, 0), (0, pad_r)))
        b_router = jnp.pad(jnp.concatenate([router_group_b[i], router_expert_b[i]]),
                           (0, pad_r)).reshape(1, ROUTER_LANES)
        x = _moe(x, norm_ffn[i], sc_f, sh_f, g_f, w_router, b_router,
                 w_gate[i].astype(BF16), w_up[i].astype(BF16), w_down[i].astype(BF16))
    return x
```

```python
import functools
import math

import jax
import jax.numpy as jnp
from jax import lax
from jax.experimental import pallas as pl
from jax.experimental.pallas import tpu as pltpu

F32 = jnp.float32
BF16 = jnp.bfloat16

D_MODEL = 1024
HEAD_DIM = 64
N_HEADS = 16
ATTN_SCALE = HEAD_DIM ** -0.5
A_DILATED = ((128, 1), (512, 4), (2048, 16))
Q_BLOCK = 128
REL_BUCKETS = 32
REL_MAX_DIST = 2048
N_GROUPS = 4
EXPERTS_PER_GROUP = 4
N_EXPERTS = 16
EXPERT_FF = 256
EPS = 1e-6
NEG_INF = -1e30

LANES = 128
VMEM_LIMIT = 48 * 1024 * 1024
LOG2E = 1.4426950408889634
LN2 = 0.6931471805599453
PERM_TILE = 1024
PAIR = 2 * HEAD_DIM
N_PAIRS = N_HEADS // 2


def _params(sem):
    return pltpu.CompilerParams(dimension_semantics=sem, vmem_limit_bytes=VMEM_LIMIT)


def _split3(x):
    hi = x.astype(BF16)
    r1 = x - hi.astype(F32)
    mid = r1.astype(BF16)
    lo = (r1 - mid.astype(F32)).astype(BF16)
    return hi, mid, lo


def _dot(a, b):
    return jnp.dot(a, b, preferred_element_type=F32)


def _dot_nt(a, b):
    return lax.dot_general(a, b, (((1,), (1,)), ((), ())), preferred_element_type=F32)


def _ada_kernel(c_ref, w_ref, b_ref, o_ref):
    c = c_ref[...]
    a = c * (1.0 / (1.0 + jnp.exp(-c)))
    o_ref[0] = _dot(a.astype(BF16), w_ref[0].astype(BF16)) + b_ref[0]


def _ada(c, w_ada, b_ada):
    depth, d, n6 = w_ada.shape
    b = c.shape[0]
    rows = 8
    c_pad = jnp.pad(c, ((0, rows - b), (0, 0)))
    tn = 1536
    out = pl.pallas_call(
        _ada_kernel,
        out_shape=jax.ShapeDtypeStruct((depth, rows, n6), F32),
        grid=(depth, n6 // tn),
        in_specs=[
            pl.BlockSpec((rows, d), lambda i, j: (0, 0)),
            pl.BlockSpec((1, d, tn), lambda i, j: (i, 0, j)),
            pl.BlockSpec((1, 1, tn), lambda i, j: (i, 0, j)),
        ],
        out_specs=pl.BlockSpec((1, rows, tn), lambda i, j: (i, 0, j)),
        compiler_params=_params(("arbitrary", "arbitrary")),
        name="ada_mod",
    )(c_pad, w_ada, b_ada.reshape(depth, 1, n6))
    return out[:, :b]


def _modnorm(x, gamma, sc, sh):
    ms = jnp.mean(x * x, axis=-1, keepdims=True)
    y = x * lax.rsqrt(ms + EPS) * gamma
    return y * (1.0 + sc) + sh


def _normproj_kernel(x_ref, g_ref, sc_ref, sh_ref, w_ref, o_ref, h_ref):
    @pl.when(pl.program_id(1) == 0)
    def _():
        h_ref[...] = _modnorm(x_ref[0], g_ref[...], sc_ref[0], sh_ref[0]).astype(BF16)

    o_ref[0] = _dot(h_ref[...], w_ref[...]).astype(BF16)


def _normproj_gate_kernel(x_ref, g_ref, sc_ref, sh_ref, w_ref, wf_ref, o_ref, f_ref, h_ref):
    @pl.when(pl.program_id(1) == 0)
    def _():
        h = _modnorm(x_ref[0], g_ref[...], sc_ref[0], sh_ref[0]).astype(BF16)
        h_ref[...] = h
        f_ref[0] = _dot(h, wf_ref[...])

    o_ref[0] = _dot(h_ref[...], w_ref[...]).astype(BF16)


def _normproj(x, gamma, sc, sh, w, w_f=None, tm=1024, tn=1024):
    b, s, d = x.shape
    ncols = w.shape[1]
    spt = s // tm
    grid = (b * spt, ncols // tn)
    xmap = lambda i, j: (i // spt, i % spt, 0)
    bmap = lambda i, j: (i // spt, 0, 0)
    in_specs = [
        pl.BlockSpec((1, tm, d), xmap),
        pl.BlockSpec((1, d), lambda i, j: (0, 0)),
        pl.BlockSpec((1, 1, d), bmap),
        pl.BlockSpec((1, 1, d), bmap),
        pl.BlockSpec((d, tn), lambda i, j: (0, j)),
    ]
    o_spec = pl.BlockSpec((1, tm, tn), lambda i, j: (i // spt, i % spt, j))
    scratch = [pltpu.VMEM((tm, d), BF16)]
    if w_f is None:
        return pl.pallas_call(
            _normproj_kernel,
            out_shape=jax.ShapeDtypeStruct((b, s, ncols), BF16),
            grid=grid, in_specs=in_specs, out_specs=o_spec, scratch_shapes=scratch,
            compiler_params=_params(("arbitrary", "arbitrary")),
            name="normproj",
        )(x, gamma.reshape(1, d), sc, sh, w)
    in_specs.append(pl.BlockSpec((d, LANES), lambda i, j: (0, 0)))
    f_spec = pl.BlockSpec((1, tm, LANES), xmap)
    return pl.pallas_call(
        _normproj_gate_kernel,
        out_shape=(jax.ShapeDtypeStruct((b, s, ncols), BF16),
                   jax.ShapeDtypeStruct((b, s, LANES), F32)),
        grid=grid, in_specs=in_specs, out_specs=(o_spec, f_spec), scratch_shapes=scratch,
        compiler_params=_params(("arbitrary", "arbitrary")),
        name="normproj_gate",
    )(x, gamma.reshape(1, d), sc, sh, w, w_f)


def _normproj_a_kernel(x_ref, g_ref, sc_ref, sh_ref, w_ref, o_ref, hf_ref, hp_ref):
    tm = x_ref.shape[1]
    n_slab = x_ref.shape[2] // LANES

    @pl.when(pl.program_id(1) == 0)
    def _():
        h = _modnorm(x_ref[0], g_ref[...], sc_ref[0], sh_ref[0])
        hp_ref[0] = h.astype(BF16)
        for c in range(n_slab):
            hf_ref[c] = h[:, c * LANES:(c + 1) * LANES]
        for g, (_, r) in enumerate(A_DILATED):
            if r == 1:
                continue
            n = tm // r
            for jr in range(r):
                for c in range(n_slab):
                    hp_ref[g, jr * n:(jr + 1) * n, c * LANES:(c + 1) * LANES] = (
                        hf_ref[c, pl.ds(jr, n, stride=r), :].astype(BF16))

    o_ref[0] = _dot(hp_ref[pl.program_id(1) // 3], w_ref[...]).astype(BF16)


def _normproj_a(x, gamma, sc, sh, w, tn=1024):
    b, s, d = x.shape
    tm = PERM_TILE
    ncols = w.shape[1]
    spt = s // tm
    bmap = lambda i, j: (i // spt, 0, 0)
    return pl.pallas_call(
        _normproj_a_kernel,
        out_shape=jax.ShapeDtypeStruct((b, s, ncols), BF16),
        grid=(b * spt, ncols // tn),
        in_specs=[
            pl.BlockSpec((1, tm, d), lambda i, j: (i // spt, i % spt, 0)),
            pl.BlockSpec((1, d), lambda i, j: (0, 0)),
            pl.BlockSpec((1, 1, d), bmap),
            pl.BlockSpec((1, 1, d), bmap),
            pl.BlockSpec((d, tn), lambda i, j: (0, j)),
        ],
        out_specs=pl.BlockSpec((1, tm, tn), lambda i, j: (i // spt, i % spt, j)),
        scratch_shapes=[pltpu.VMEM((d // LANES, tm, LANES), F32),
                        pltpu.VMEM((len(A_DILATED), tm, d), BF16)],
        compiler_params=_params(("arbitrary", "arbitrary")),
        name="normproj_a",
    )(x, gamma.reshape(1, d), sc, sh, w)


def _headnorm(t, gain):
    t = t.astype(F32)
    ms = jnp.mean(t * t, axis=-1, keepdims=True)
    return t * lax.rsqrt(ms + EPS) * gain


def _pairnorm(t, gain2, even):
    t = t.astype(F32)
    sq = t * t
    tot = jnp.sum(sq, axis=-1, keepdims=True)
    s_even = jnp.sum(jnp.where(even, sq, 0.0), axis=-1, keepdims=True)
    inv = jnp.where(even, lax.rsqrt(s_even * (1.0 / HEAD_DIM) + EPS),
                    lax.rsqrt((tot - s_even) * (1.0 / HEAD_DIM) + EPS))
    return t * inv * gain2


def _rows(ref, cols=slice(None)):
    if len(ref.shape) == 2:
        return ref[:, cols]
    return jnp.concatenate([ref[i, :, cols] for i in range(ref.shape[0])], axis=0)


def _store_rows(ref, cols, val):
    if len(ref.shape) == 2:
        ref[:, cols] = val
    else:
        n = ref.shape[1]
        for i in range(ref.shape[0]):
            ref[i, :, cols] = val[i * n:(i + 1) * n]


def _attn_a_kernel(q_ref, k_ref, v_ref, bias_ref, gq_ref, gk_ref, o_ref, lse_ref,
                   kn_ref, vv_ref):
    ub = pl.program_id(2)
    slot = ub % 2
    prev = 1 - slot

    @pl.when(ub == 0)
    def _():
        kn_ref[1] = jnp.zeros(kn_ref.shape[1:], BF16)
        vv_ref[1] = jnp.zeros(vv_ref.shape[1:], BF16)

    first = jnp.minimum(ub, 1)
    lane = lax.broadcasted_iota(jnp.int32, (Q_BLOCK, PAIR), 1)
    even = lane < HEAD_DIM
    lane_row = lax.broadcasted_iota(jnp.int32, (1, PAIR), 1)
    keep_even = jnp.where(lane_row < HEAD_DIM, 1.0, 0.0).astype(BF16)
    keep_odd = jnp.where(lane_row < HEAD_DIM, 0.0, 1.0).astype(BF16)
    gq2 = gq_ref[...] * (ATTN_SCALE * LOG2E)
    gk2 = gk_ref[...]
    for hp in range(N_PAIRS):
        cs = slice(hp * PAIR, (hp + 1) * PAIR)
        kn_ref[slot, :, cs] = _pairnorm(_rows(k_ref, cs), gk2, even).astype(BF16)
    vv_ref[slot] = _rows(v_ref)
    lse_tile = jnp.zeros((Q_BLOCK, LANES), F32)
    for hp in range(N_PAIRS):
        cs = slice(hp * PAIR, (hp + 1) * PAIR)
        qn = _pairnorm(_rows(q_ref, cs), gq2, even)
        qq = jnp.concatenate([jnp.where(even, qn, 0.0), jnp.where(even, 0.0, qn)],
                             axis=0).astype(BF16)
        kcat = jnp.concatenate([kn_ref[prev, :, cs], kn_ref[slot, :, cs]], axis=0)
        s = _dot_nt(qq, kcat) + bias_ref[first, hp]
        m = jnp.max(s, axis=-1, keepdims=True)
        p = jnp.exp2(s - m)
        l = jnp.sum(p, axis=-1, keepdims=True)
        pb = p.astype(BF16)
        vcat = jnp.concatenate([vv_ref[prev, :, cs], vv_ref[slot, :, cs]], axis=0)
        o = (_dot(pb[:Q_BLOCK], vcat * keep_even)
             + _dot(pb[Q_BLOCK:], vcat * keep_odd))
        inv = 1.0 / l
        o = o * jnp.where(even, inv[:Q_BLOCK], inv[Q_BLOCK:])
        _store_rows(o_ref, cs, o.astype(BF16))
        lse = m * LN2 + jnp.log(l)
        lse_tile = jnp.where(lane == 2 * hp, lse[:Q_BLOCK],
                             jnp.where(lane == 2 * hp + 1, lse[Q_BLOCK:], lse_tile))
    _store_rows(lse_ref, slice(None), lse_tile)


def _t5_bucket(dist):
    max_exact = REL_BUCKETS // 2
    d = jnp.maximum(dist, 0)
    large = max_exact + (jnp.log(jnp.maximum(d, 1).astype(F32) / max_exact)
                         / math.log(REL_MAX_DIST / max_exact)
                         * (REL_BUCKETS - max_exact)).astype(jnp.int32)
    large = jnp.minimum(large, REL_BUCKETS - 1)
    return jnp.where(d < max_exact, d, large)


def _bias_kernel(tab_ref, onehot_ref, o_ref):
    hi, mid, lo = _split3(tab_ref[...])
    oh = onehot_ref[...]
    o_ref[...] = (_dot(hi, oh) + _dot(mid, oh) + _dot(lo, oh)) * LOG2E


def _bias_tables(rel_bias_g, window, dilation):
    qi = jnp.arange(Q_BLOCK)[:, None]
    kj = jnp.arange(2 * Q_BLOCK)[None, :]
    dist = qi + Q_BLOCK - kj
    bucket = _t5_bucket(dist * dilation).reshape(1, -1)
    onehot = (bucket == jnp.arange(REL_BUCKETS)[:, None]).astype(BF16)
    npos = onehot.shape[1]
    chunk = 4096
    bias = pl.pallas_call(
        _bias_kernel,
        out_shape=jax.ShapeDtypeStruct((N_HEADS, npos), F32),
        grid=(npos // chunk,),
        in_specs=[pl.BlockSpec((N_HEADS, REL_BUCKETS), lambda i: (0, 0)),
                  pl.BlockSpec((REL_BUCKETS, chunk), lambda i: (0, i))],
        out_specs=pl.BlockSpec((N_HEADS, chunk), lambda i: (0, i)),
        compiler_params=_params(("arbitrary",)),
        name="rel_bias_table",
    )(rel_bias_g.T.astype(F32), onehot)
    bias = bias.reshape(N_PAIRS, 2 * Q_BLOCK, 2 * Q_BLOCK)
    band = (dist >= 0) & (dist <= window // dilation)
    first = band & (qi - dist >= 0)
    band = jnp.concatenate([band, band], axis=0)[None]
    first = jnp.concatenate([first, first], axis=0)[None]
    return jnp.stack([jnp.where(first, bias, NEG_INF), jnp.where(band, bias, NEG_INF)], axis=0)


def _attn_a_group(proj, g, rel_bias_g, gq, gk, window, dilation):
    b, s, cols = proj.shape
    r = dilation
    nt = s // PERM_TILE
    n_t = PERM_TILE // r
    nb = s // r // Q_BLOCK
    gw = N_HEADS * HEAD_DIM
    tables = _bias_tables(rel_bias_g, window, dilation)
    base = g * 3
    if n_t >= Q_BLOCK:
        bpt = n_t // Q_BLOCK
        blk = lambda w: (None, None, None, Q_BLOCK, w)
        imap = lambda col: (lambda bi, jr, ub: (bi, ub // bpt, jr, ub % bpt, col))
    else:
        blk = lambda w: (None, Q_BLOCK // n_t, None, n_t, w)
        imap = lambda col: (lambda bi, jr, ub: (bi, ub, jr, 0, col))
    pv = proj.reshape(b, nt, r, n_t, cols)
    gains = lambda t: jnp.concatenate([t, t]).reshape(1, PAIR)
    o, lse = pl.pallas_call(
        _attn_a_kernel,
        out_shape=(jax.ShapeDtypeStruct((b, nt, r, n_t, gw), BF16),
                   jax.ShapeDtypeStruct((b, nt, r, n_t, LANES), F32)),
        grid=(b, r, nb),
        in_specs=[
            pl.BlockSpec(blk(gw), imap(base)),
            pl.BlockSpec(blk(gw), imap(base + 1)),
            pl.BlockSpec(blk(gw), imap(base + 2)),
            pl.BlockSpec(tables.shape, lambda bi, jr, ub: (0, 0, 0, 0)),
            pl.BlockSpec((1, PAIR), lambda bi, jr, ub: (0, 0)),
            pl.BlockSpec((1, PAIR), lambda bi, jr, ub: (0, 0)),
        ],
        out_specs=(pl.BlockSpec(blk(gw), imap(0)), pl.BlockSpec(blk(LANES), imap(0))),
        scratch_shapes=[pltpu.VMEM((2, Q_BLOCK, gw), BF16), pltpu.VMEM((2, Q_BLOCK, gw), BF16)],
        compiler_params=_params(("arbitrary", "arbitrary", "arbitrary")),
        name=f"attn_a_g{g}",
    )(pv, pv, pv, tables, gains(gq), gains(gk))
    return o.reshape(b, s, gw), lse.reshape(b, s, LANES)


MERGE_CHUNK = 128


def _merge_out_kernel(o0_ref, o1_ref, o2_ref, l0_ref, l1_ref, l2_ref, x_ref, g_ref, w_ref,
                      e_ref, out_ref, lun_ref, oun_ref, om_ref):
    tm = o0_ref.shape[1]
    n_slab = o0_ref.shape[2] // LANES

    @pl.when(pl.program_id(1) == 0)
    def _():
        for gi, (o_ref, l_ref) in enumerate(((o1_ref, l1_ref), (o2_ref, l2_ref))):
            r = A_DILATED[gi + 1][1]
            n = tm // r
            for jr in range(r):
                rows = slice(jr * n, (jr + 1) * n)
                lun_ref[gi, pl.ds(jr, n, stride=r), :] = l_ref[0, rows, :]
                for c in range(n_slab):
                    oun_ref[gi, c, pl.ds(jr, n, stride=r), :] = (
                        o_ref[0, rows, c * LANES:(c + 1) * LANES].astype(F32))
        expand = e_ref[...]

        def widen(w):
            hi = w.astype(BF16)
            lo = (w - hi.astype(F32)).astype(BF16)
            return _dot(hi, expand) + _dot(lo, expand)

        for k in range(tm // MERGE_CHUNK):
            rs = slice(k * MERGE_CHUNK, (k + 1) * MERGE_CHUNK)
            l0, l1, l2 = l0_ref[0, rs, :], lun_ref[0, rs, :], lun_ref[1, rs, :]
            m = jnp.maximum(jnp.maximum(l0, l1), l2)
            e0, e1, e2 = jnp.exp(l0 - m), jnp.exp(l1 - m), jnp.exp(l2 - m)
            inv = 1.0 / (e0 + e1 + e2)
            o1 = jnp.concatenate([oun_ref[0, c, rs, :] for c in range(n_slab)], axis=1)
            o2 = jnp.concatenate([oun_ref[1, c, rs, :] for c in range(n_slab)], axis=1)
            o = (widen(e0 * inv) * o0_ref[0, rs, :].astype(F32)
                 + widen(e1 * inv) * o1 + widen(e2 * inv) * o2)
            om_ref[rs, :] = o.astype(BF16)

    out_ref[0] = x_ref[0] + g_ref[0] * _dot(om_ref[...], w_ref[...])


def _merge_out(outs, lses, x, gate, w_out, tn=512):
    b, s, d = x.shape
    tm = PERM_TILE
    spt = s // tm
    rmap = lambda i, j: (i // spt, i % spt, 0)
    cmap = lambda i, j: (i // spt, i % spt, j)
    expand = ((jnp.arange(d)[None, :] // HEAD_DIM == jnp.arange(LANES)[:, None])
              & (jnp.arange(LANES)[:, None] < N_HEADS)).astype(BF16)
    o_spec = pl.BlockSpec((1, tm, d), rmap)
    l_spec = pl.BlockSpec((1, tm, LANES), rmap)
    return pl.pallas_call(
        _merge_out_kernel,
        out_shape=jax.ShapeDtypeStruct((b, s, d), F32),
        grid=(b * spt, d // tn),
        in_specs=[o_spec, o_spec, o_spec, l_spec, l_spec, l_spec,
                  pl.BlockSpec((1, tm, tn), cmap),
                  pl.BlockSpec((1, 1, tn), lambda i, j: (i // spt, 0, j)),
                  pl.BlockSpec((d, tn), lambda i, j: (0, j)),
                  pl.BlockSpec((LANES, d), lambda i, j: (0, 0))],
        out_specs=pl.BlockSpec((1, tm, tn), cmap),
        scratch_shapes=[pltpu.VMEM((2, tm, LANES), F32),
                        pltpu.VMEM((2, d // LANES, tm, LANES), F32),
                        pltpu.VMEM((tm, d), BF16)],
        compiler_params=_params(("arbitrary", "arbitrary")),
        name="merge_out",
    )(*outs, *lses, x, gate, w_out, expand)


def _out_kernel(o_ref, x_ref, g_ref, w_ref, out_ref):
    out_ref[0] = x_ref[0] + g_ref[0] * _dot(o_ref[0], w_ref[...])


def _out_proj(o, x, gate, w_out, tm=512):
    b, s, d = x.shape
    spt = s // tm
    xmap = lambda i: (i // spt, i % spt, 0)
    return pl.pallas_call(
        _out_kernel,
        out_shape=jax.ShapeDtypeStruct((b, s, d), F32),
        grid=(b * spt,),
        in_specs=[pl.BlockSpec((1, tm, d), xmap),
                  pl.BlockSpec((1, tm, d), xmap),
                  pl.BlockSpec((1, 1, d), lambda i: (i // spt, 0, 0)),
                  pl.BlockSpec((d, d), lambda i: (0, 0))],
        out_specs=pl.BlockSpec((1, tm, d), xmap),
        compiler_params=_params(("arbitrary",)),
        name="out_proj",
    )(o, x, gate, w_out)


def _cum_kernel(f_ref, fb_ref, tri_ref, o_ref, carry_ref):
    @pl.when(pl.program_id(1) == 0)
    def _():
        carry_ref[...] = jnp.zeros_like(carry_ref)

    z = f_ref[0] + fb_ref[...]
    logf = jnp.minimum(z, 0.0) - jnp.log(1.0 + jnp.exp(-jnp.abs(z)))
    hi, mid, lo = _split3(logf)
    tri = tri_ref[...]
    cum = _dot(tri, hi) + _dot(tri, mid) + _dot(tri, lo) + carry_ref[...]
    o_ref[0] = cum
    carry_ref[...] = cum[-1:, :]


def _cum_forget(fg, f_bias, tc=512):
    b, s, w = fg.shape
    fb = jnp.pad(f_bias, (0, w - f_bias.shape[0])).reshape(1, w)
    tri = (jnp.arange(tc)[:, None] >= jnp.arange(tc)[None, :]).astype(BF16)
    return pl.pallas_call(
        _cum_kernel,
        out_shape=jax.ShapeDtypeStruct((b, s, w), F32),
        grid=(b, s // tc),
        in_specs=[pl.BlockSpec((1, tc, w), lambda i, j: (i, j, 0)),
                  pl.BlockSpec((1, w), lambda i, j: (0, 0)),
                  pl.BlockSpec((tc, tc), lambda i, j: (0, 0))],
        out_specs=pl.BlockSpec((1, tc, w), lambda i, j: (i, j, 0)),
        scratch_shapes=[pltpu.VMEM((1, w), F32)],
        compiler_params=_params(("arbitrary", "arbitrary")),
        name="cum_forget",
    )(fg, fb, tri)


def _gate_columns(cum_col, lead):
    rows = cum_col.shape[0]
    lane = lax.broadcasted_iota(jnp.int32, (rows, HEAD_DIM), 1)
    hi, mid, lo = (p.astype(F32) for p in _split3(cum_col if lead else -cum_col))
    off = 0 if lead else 3
    ext = jnp.where(lane < 6, 1.0, 0.0)
    ext = jnp.where(lane == off, hi, ext)
    ext = jnp.where(lane == off + 1, mid, ext)
    ext = jnp.where(lane == off + 2, lo, ext)
    return ext.astype(BF16)


def _pick_lane(tile, idx):
    lane = lax.broadcasted_iota(jnp.int32, tile.shape, 1)
    return jnp.sum(jnp.where(lane == idx, tile, 0.0), axis=-1, keepdims=True)


ONES_ROWS = 16
FOX_HEADS_PER_STEP = 8


def _fox_kernel(q_ref, k_ref, v_ref, cq_ref, ck_ref, gq_ref, gk_ref, o_ref, ka_ref, vt_ref,
                qat_ref, s_ref, p_ref, acc_ref, *, tq, tk):
    hg = pl.program_id(1)
    qi = pl.program_id(2)
    n_kt = vt_ref.shape[1]
    nh = ka_ref.shape[0]
    heads = range(nh)

    @pl.when(qi == 0)
    def _():
        gk = gk_ref[...]
        for hh in heads:
            sl = slice(hh * HEAD_DIM, (hh + 1) * HEAD_DIM)
            ka_ref[hh, :, :HEAD_DIM] = _headnorm(k_ref[0, :, sl], gk).astype(BF16)
            ck = _pick_lane(ck_ref[0], nh * hg + hh) * LOG2E
            ka_ref[hh, :, HEAD_DIM:] = _gate_columns(ck, lead=False)
        for pr in range(nh // 2):
            vt = v_ref[0, :, pr * PAIR:(pr + 1) * PAIR].astype(F32).T.astype(BF16)
            for hh in (2 * pr, 2 * pr + 1):
                rows = slice((hh % 2) * HEAD_DIM, (hh % 2 + 1) * HEAD_DIM)
                for jj in range(n_kt):
                    vt_ref[hh, jj, :HEAD_DIM] = vt[rows, jj * tk:(jj + 1) * tk]
                    vt_ref[hh, jj, HEAD_DIM:] = jnp.ones((ONES_ROWS, tk), BF16)

    gq = gq_ref[...] * (ATTN_SCALE * LOG2E)
    for hh in heads:
        sl = slice(hh * HEAD_DIM, (hh + 1) * HEAD_DIM)
        qn = _headnorm(q_ref[0, :, sl], gq)
        cq = _pick_lane(cq_ref[0], nh * hg + hh) * LOG2E
        qa = jnp.concatenate([qn, _gate_columns(cq, lead=True).astype(F32)], axis=1)
        qat_ref[hh] = qa.T.astype(BF16)
    key = lax.broadcasted_iota(jnp.int32, (tk, tq), 0)
    qry = lax.broadcasted_iota(jnp.int32, (tk, tq), 1)
    ahead = key - qry

    def scores(t, hh, diagonal):
        start = pl.multiple_of(t * tk, tk)
        st = _dot(ka_ref[hh, pl.ds(start, tk), :], qat_ref[hh])
        if diagonal is None:
            st = jnp.where(ahead > jnp.where(t == qi, 0, tk), NEG_INF, st)
        elif diagonal:
            st = jnp.where(ahead > 0, NEG_INF, st)
        s_ref[hh] = st
        return jnp.max(st, axis=0, keepdims=True)

    def pv(t, hh, alpha_prev):
        acc_ref[hh] = alpha_prev * acc_ref[hh] + _dot(vt_ref[hh, t], p_ref[hh])

    def trip(i, carry, next_diagonal):
        out = []
        for hh in heads:
            m, alpha_prev, mx = carry[hh]
            pv(jnp.maximum(i - 1, 0), hh, alpha_prev)
            m_new = jnp.maximum(m, mx)
            alpha = jnp.exp2(m - m_new)
            p_ref[hh] = jnp.exp2(s_ref[hh] - m_new).astype(BF16)
            if next_diagonal is not None:
                mx = scores(i + 1, hh, next_diagonal)
            out.append((m_new, alpha, mx))
        return tuple(out)

    p_ref[...] = jnp.zeros_like(p_ref)
    acc_ref[...] = jnp.zeros_like(acc_ref)
    carry = tuple((jnp.full((1, tq), NEG_INF, F32), jnp.ones((1, tq), F32),
                   scores(0, hh, None)) for hh in heads)
    carry = lax.fori_loop(0, jnp.maximum(qi - 1, 0),
                          functools.partial(trip, next_diagonal=False), carry)
    carry = lax.fori_loop(jnp.maximum(qi - 1, 0), qi,
                          functools.partial(trip, next_diagonal=True), carry)
    carry = trip(qi, carry, None)
    for pr in range(nh // 2):
        o_t = []
        for hh in (2 * pr, 2 * pr + 1):
            pv(qi, hh, carry[hh][1])
            acc = acc_ref[hh]
            o_t.append(acc[:HEAD_DIM] / acc[HEAD_DIM:HEAD_DIM + 1])
        o_ref[0, :, pr * PAIR:(pr + 1) * PAIR] = (
            jnp.concatenate(o_t, axis=0).T.astype(BF16))


def _fox_attention(qkv, cum, gq, gk, tq=256):
    b, s, _ = qkv.shape
    tk = tq
    nq = s // tq
    nh = FOX_HEADS_PER_STEP
    groups = N_HEADS // nh
    gw = nh * HEAD_DIM
    vrows = HEAD_DIM + ONES_ROWS
    return pl.pallas_call(
        functools.partial(_fox_kernel, tq=tq, tk=tk),
        out_shape=jax.ShapeDtypeStruct((b, s, N_HEADS * HEAD_DIM), BF16),
        grid=(b, groups, nq),
        in_specs=[
            pl.BlockSpec((1, tq, gw), lambda bi, hg, qi: (bi, qi, hg)),
            pl.BlockSpec((1, s, gw), lambda bi, hg, qi: (bi, 0, groups + hg)),
            pl.BlockSpec((1, s, gw), lambda bi, hg, qi: (bi, 0, 2 * groups + hg)),
            pl.BlockSpec((1, tq, LANES), lambda bi, hg, qi: (bi, qi, 0)),
            pl.BlockSpec((1, s, LANES), lambda bi, hg, qi: (bi, 0, 0)),
            pl.BlockSpec((1, HEAD_DIM), lambda bi, hg, qi: (0, 0)),
            pl.BlockSpec((1, HEAD_DIM), lambda bi, hg, qi: (0, 0)),
        ],
        out_specs=pl.BlockSpec((1, tq, gw), lambda bi, hg, qi: (bi, qi, hg)),
        scratch_shapes=[pltpu.VMEM((nh, s, 2 * HEAD_DIM), BF16),
                        pltpu.VMEM((nh, s // tk, vrows, tk), BF16),
                        pltpu.VMEM((nh, 2 * HEAD_DIM, tq), BF16),
                        pltpu.VMEM((nh, tk, tq), F32),
                        pltpu.VMEM((nh, tk, tq), BF16),
                        pltpu.VMEM((nh, vrows, tq), F32)],
        compiler_params=_params(("arbitrary", "arbitrary", "arbitrary")),
        name="fox_attn",
    )(qkv, qkv, qkv, cum, cum, gq.reshape(1, HEAD_DIM), gk.reshape(1, HEAD_DIM))


ROUTER_LANES = LANES


def _route(logits):
    lane = lax.broadcasted_iota(jnp.int32, logits.shape, 1)
    lane_f = lane.astype(F32)
    big = float(ROUTER_LANES)
    is_g = lane < N_GROUPS
    gl = jnp.where(is_g, logits, -jnp.inf)
    gmax = jnp.max(gl, axis=-1, keepdims=True)
    gsum = jnp.sum(jnp.where(is_g, jnp.exp(logits - gmax), 0.0), axis=-1, keepdims=True)
    g_w = 1.0 / gsum
    gidx = jnp.min(jnp.where(gl == gmax, lane_f, big), axis=-1, keepdims=True)
    lo = N_GROUPS + EXPERTS_PER_GROUP * gidx
    in_grp = (lane_f >= lo) & (lane_f < lo + EXPERTS_PER_GROUP)
    el = jnp.where(in_grp, logits, -jnp.inf)
    t1 = jnp.max(el, axis=-1, keepdims=True)
    i1 = jnp.min(jnp.where(el == t1, lane_f, big), axis=-1, keepdims=True)
    el2 = jnp.where(lane_f == i1, -jnp.inf, el)
    t2 = jnp.max(el2, axis=-1, keepdims=True)
    i2 = jnp.min(jnp.where(el2 == t2, lane_f, big), axis=-1, keepdims=True)
    e2 = jnp.exp(t2 - t1)
    w1 = g_w / (1.0 + e2)
    w2 = w1 * e2
    return jnp.where(lane_f == i1, w1, jnp.where(lane_f == i2, w2, 0.0))


def _moe_kernel(x_ref, gam_ref, sc_ref, sh_ref, gate_ref, wr_ref, br_ref,
                wg_ref, wu_ref, wd_ref, out_ref, h_ref, gates_ref, acc_ref):
    e = pl.program_id(1)

    @pl.when(e == 0)
    def _():
        h = _modnorm(x_ref[0], gam_ref[...], sc_ref[0], sh_ref[0])
        h_hi = h.astype(BF16)
        h_lo = (h - h_hi.astype(F32)).astype(BF16)
        wr = wr_ref[...]
        w_hi = wr.astype(BF16)
        w_lo = (wr - w_hi.astype(F32)).astype(BF16)
        logits = _dot(h_hi, w_hi) + _dot(h_hi, w_lo) + _dot(h_lo, w_hi) + br_ref[...]
        gates_ref[...] = _route(logits)
        h_ref[...] = h_hi
        acc_ref[...] = jnp.zeros_like(acc_ref)

    h = h_ref[...]
    a = _dot(h, wg_ref[0])
    u = _dot(h, wu_ref[0])
    gcol = _pick_lane(gates_ref[...], N_GROUPS + e)
    hid = a * (1.0 / (1.0 + jnp.exp(-a))) * u * gcol
    acc_ref[...] += _dot(hid.astype(BF16), wd_ref[0])

    @pl.when(e == N_EXPERTS - 1)
    def _():
        out_ref[0] = x_ref[0] + gate_ref[0] * acc_ref[...]


def _moe(x, gamma, sc, sh, gate, w_router, b_router, w_gate, w_up, w_down, tm=512):
    b, s, d = x.shape
    spt = s // tm
    xmap = lambda i, e: (i // spt, i % spt, 0)
    bmap = lambda i, e: (i // spt, 0, 0)
    return pl.pallas_call(
        _moe_kernel,
        out_shape=jax.ShapeDtypeStruct((b, s, d), F32),
        grid=(b * spt, N_EXPERTS),
        in_specs=[
            pl.BlockSpec((1, tm, d), xmap),
            pl.BlockSpec((1, d), lambda i, e: (0, 0)),
            pl.BlockSpec((1, 1, d), bmap),
            pl.BlockSpec((1, 1, d), bmap),
            pl.BlockSpec((1, 1, d), bmap),
            pl.BlockSpec((d, ROUTER_LANES), lambda i, e: (0, 0)),
            pl.BlockSpec((1, ROUTER_LANES), lambda i, e: (0, 0)),
            pl.BlockSpec((1, d, EXPERT_FF), lambda i, e: (e, 0, 0)),
            pl.BlockSpec((1, d, EXPERT_FF), lambda i, e: (e, 0, 0)),
            pl.BlockSpec((1, EXPERT_FF, d), lambda i, e: (e, 0, 0)),
        ],
        out_specs=pl.BlockSpec((1, tm, d), xmap),
        scratch_shapes=[pltpu.VMEM((tm, d), BF16),
                        pltpu.VMEM((tm, ROUTER_LANES), F32),
                        pltpu.VMEM((tm, d), F32)],
        compiler_params=_params(("arbitrary", "arbitrary")),
        name="moe",
    )(x, gamma.reshape(1, d), sc, sh, gate, w_router, b_router, w_gate, w_up, w_down)


def kernel(x, c, w_ada, b_ada, norm_mix, norm_ffn, rel_bias, a_w_in, a_w_out, a_q_norm,
           a_k_norm, b_w_in, b_f_bias, b_w_out, b_q_norm, b_k_norm, router_group_w,
           router_group_b, router_expert_w, router_expert_b, w_gate, w_up, w_down):
    b, s, d = x.shape
    depth = w_ada.shape[0]
    mod = _ada(c, w_ada, b_ada).reshape(depth, b, 6, 1, d)
    pad_r = ROUTER_LANES - N_GROUPS - N_EXPERTS
    for i in range(depth):
        sh_m, sc_m, g_m, sh_f, sc_f, g_f = (mod[i, :, k] for k in range(6))
        j = i // 2
        if i % 2 == 0:
            proj = _normproj_a(x, norm_mix[i], sc_m, sh_m, a_w_in[j].astype(BF16))
            outs, lses = [], []
            for g, (window, dil) in enumerate(A_DILATED):
                tab = rel_bias[:, g * N_HEADS:(g + 1) * N_HEADS]
                o, l = _attn_a_group(proj, g, tab, a_q_norm[j], a_k_norm[j], window, dil)
                outs.append(o)
                lses.append(l)
            x = _merge_out(outs, lses, x, g_m, a_w_out[j].astype(BF16))
        else:
            n_qkv = 3 * N_HEADS * HEAD_DIM
            w_in = b_w_in[j]
            w_f = jnp.pad(w_in[:, n_qkv:], ((0, 0), (0, LANES - N_HEADS))).astype(BF16)
            qkv, fg = _normproj(x, norm_mix[i], sc_m, sh_m, w_in[:, :n_qkv].astype(BF16), w_f)
            cum = _cum_forget(fg, b_f_bias[j])
            o = _fox_attention(qkv, cum, b_q_norm[j], b_k_norm[j])
            x = _out_proj(o, x, g_m, b_w_out[j].astype(BF16))
        w_router = jnp.pad(jnp.concatenate([router_group_w[i], router_expert_w[i]], axis=1),
                           ((0, 0), (0, pad_r)))
        b_router = jnp.pad(jnp.concatenate([router_group_b[i], router_expert_b[i]]),
                           (0, pad_r)).reshape(1, ROUTER_LANES)
        x = _moe(x, norm_ffn[i], sc_f, sh_f, g_f, w_router, b_router,
                 w_gate[i].astype(BF16), w_up[i].astype(BF16), w_down[i].astype(BF16))
    return x
```

```python
import functools
import math

import jax
import jax.numpy as jnp
from jax import lax
from jax.experimental import pallas as pl
from jax.experimental.pallas import tpu as pltpu

F32 = jnp.float32
BF16 = jnp.bfloat16

D_MODEL = 1024
HEAD_DIM = 64
N_HEADS = 16
ATTN_SCALE = HEAD_DIM ** -0.5
A_DILATED = ((128, 1), (512, 4), (2048, 16))
Q_BLOCK = 128
REL_BUCKETS = 32
REL_MAX_DIST = 2048
N_GROUPS = 4
EXPERTS_PER_GROUP = 4
N_EXPERTS = 16
EXPERT_FF = 256
EPS = 1e-6
NEG_INF = -1e30

LANES = 128
VMEM_LIMIT = 48 * 1024 * 1024
LOG2E = 1.4426950408889634
LN2 = 0.6931471805599453
PERM_TILE = 1024
PAIR = 2 * HEAD_DIM
N_PAIRS = N_HEADS // 2


def _params(sem):
    return pltpu.CompilerParams(dimension_semantics=sem, vmem_limit_bytes=VMEM_LIMIT)


def _split3(x):
    hi = x.astype(BF16)
    r1 = x - hi.astype(F32)
    mid = r1.astype(BF16)
    lo = (r1 - mid.astype(F32)).astype(BF16)
    return hi, mid, lo


def _dot(a, b):
    return jnp.dot(a, b, preferred_element_type=F32)


def _dot_nt(a, b):
    return lax.dot_general(a, b, (((1,), (1,)), ((), ())), preferred_element_type=F32)


def _ada_kernel(c_ref, w_ref, b_ref, o_ref):
    c = c_ref[...]
    a = c * (1.0 / (1.0 + jnp.exp(-c)))
    o_ref[0] = _dot(a.astype(BF16), w_ref[0].astype(BF16)) + b_ref[0]


def _ada(c, w_ada, b_ada):
    depth, d, n6 = w_ada.shape
    b = c.shape[0]
    rows = 8
    c_pad = jnp.pad(c, ((0, rows - b), (0, 0)))
    tn = 1536
    out = pl.pallas_call(
        _ada_kernel,
        out_shape=jax.ShapeDtypeStruct((depth, rows, n6), F32),
        grid=(depth, n6 // tn),
        in_specs=[
            pl.BlockSpec((rows, d), lambda i, j: (0, 0)),
            pl.BlockSpec((1, d, tn), lambda i, j: (i, 0, j)),
            pl.BlockSpec((1, 1, tn), lambda i, j: (i, 0, j)),
        ],
        out_specs=pl.BlockSpec((1, rows, tn), lambda i, j: (i, 0, j)),
        compiler_params=_params(("arbitrary", "arbitrary")),
        name="ada_mod",
    )(c_pad, w_ada, b_ada.reshape(depth, 1, n6))
    return out[:, :b]


def _modnorm(x, gamma, sc, sh):
    ms = jnp.mean(x * x, axis=-1, keepdims=True)
    y = x * lax.rsqrt(ms + EPS) * gamma
    return y * (1.0 + sc) + sh


def _normproj_kernel(x_ref, g_ref, sc_ref, sh_ref, w_ref, o_ref, h_ref):
    @pl.when(pl.program_id(1) == 0)
    def _():
        h_ref[...] = _modnorm(x_ref[0], g_ref[...], sc_ref[0], sh_ref[0]).astype(BF16)

    o_ref[0] = _dot(h_ref[...], w_ref[...]).astype(BF16)


def _normproj_gate_kernel(x_ref, g_ref, sc_ref, sh_ref, w_ref, wf_ref, o_ref, f_ref, h_ref):
    @pl.when(pl.program_id(1) == 0)
    def _():
        h = _modnorm(x_ref[0], g_ref[...], sc_ref[0], sh_ref[0]).astype(BF16)
        h_ref[...] = h
        f_ref[0] = _dot(h, wf_ref[...])

    o_ref[0] = _dot(h_ref[...], w_ref[...]).astype(BF16)


def _normproj(x, gamma, sc, sh, w, w_f=None, tm=1024, tn=1024):
    b, s, d = x.shape
    ncols = w.shape[1]
    spt = s // tm
    grid = (b * spt, ncols // tn)
    xmap = lambda i, j: (i // spt, i % spt, 0)
    bmap = lambda i, j: (i // spt, 0, 0)
    in_specs = [
        pl.BlockSpec((1, tm, d), xmap),
        pl.BlockSpec((1, d), lambda i, j: (0, 0)),
        pl.BlockSpec((1, 1, d), bmap),
        pl.BlockSpec((1, 1, d), bmap),
        pl.BlockSpec((d, tn), lambda i, j: (0, j)),
    ]
    o_spec = pl.BlockSpec((1, tm, tn), lambda i, j: (i // spt, i % spt, j))
    scratch = [pltpu.VMEM((tm, d), BF16)]
    if w_f is None:
        return pl.pallas_call(
            _normproj_kernel,
            out_shape=jax.ShapeDtypeStruct((b, s, ncols), BF16),
            grid=grid, in_specs=in_specs, out_specs=o_spec, scratch_shapes=scratch,
            compiler_params=_params(("arbitrary", "arbitrary")),
            name="normproj",
        )(x, gamma.reshape(1, d), sc, sh, w)
    in_specs.append(pl.BlockSpec((d, LANES), lambda i, j: (0, 0)))
    f_spec = pl.BlockSpec((1, tm, LANES), xmap)
    return pl.pallas_call(
        _normproj_gate_kernel,
        out_shape=(jax.ShapeDtypeStruct((b, s, ncols), BF16),
                   jax.ShapeDtypeStruct((b, s, LANES), F32)),
        grid=grid, in_specs=in_specs, out_specs=(o_spec, f_spec), scratch_shapes=scratch,
        compiler_params=_params(("arbitrary", "arbitrary")),
        name="normproj_gate",
    )(x, gamma.reshape(1, d), sc, sh, w, w_f)


def _normproj_a_kernel(x_ref, g_ref, sc_ref, sh_ref, w_ref, o_ref, hf_ref, hp_ref):
    tm = x_ref.shape[1]
    n_slab = x_ref.shape[2] // LANES

    @pl.when(pl.program_id(1) == 0)
    def _():
        h = _modnorm(x_ref[0], g_ref[...], sc_ref[0], sh_ref[0])
        hp_ref[0] = h.astype(BF16)
        for c in range(n_slab):
            hf_ref[c] = h[:, c * LANES:(c + 1) * LANES]
        for g, (_, r) in enumerate(A_DILATED):
            if r == 1:
                continue
            n = tm // r
            for jr in range(r):
                for c in range(n_slab):
                    hp_ref[g, jr * n:(jr + 1) * n, c * LANES:(c + 1) * LANES] = (
                        hf_ref[c, pl.ds(jr, n, stride=r), :].astype(BF16))

    o_ref[0] = _dot(hp_ref[pl.program_id(1) // 3], w_ref[...]).astype(BF16)


def _normproj_a(x, gamma, sc, sh, w, tn=1024):
    b, s, d = x.shape
    tm = PERM_TILE
    ncols = w.shape[1]
    spt = s // tm
    bmap = lambda i, j: (i // spt, 0, 0)
    return pl.pallas_call(
        _normproj_a_kernel,
        out_shape=jax.ShapeDtypeStruct((b, s, ncols), BF16),
        grid=(b * spt, ncols // tn),
        in_specs=[
            pl.BlockSpec((1, tm, d), lambda i, j: (i // spt, i % spt, 0)),
            pl.BlockSpec((1, d), lambda i, j: (0, 0)),
            pl.BlockSpec((1, 1, d), bmap),
            pl.BlockSpec((1, 1, d), bmap),
            pl.BlockSpec((d, tn), lambda i, j: (0, j)),
        ],
        out_specs=pl.BlockSpec((1, tm, tn), lambda i, j: (i // spt, i % spt, j)),
        scratch_shapes=[pltpu.VMEM((d // LANES, tm, LANES), F32),
                        pltpu.VMEM((len(A_DILATED), tm, d), BF16)],
        compiler_params=_params(("arbitrary", "arbitrary")),
        name="normproj_a",
    )(x, gamma.reshape(1, d), sc, sh, w)


def _headnorm(t, gain):
    t = t.astype(F32)
    ms = jnp.mean(t * t, axis=-1, keepdims=True)
    return t * lax.rsqrt(ms + EPS) * gain


def _pairnorm(t, gain2, even):
    t = t.astype(F32)
    sq = t * t
    tot = jnp.sum(sq, axis=-1, keepdims=True)
    s_even = jnp.sum(jnp.where(even, sq, 0.0), axis=-1, keepdims=True)
    inv = jnp.where(even, lax.rsqrt(s_even * (1.0 / HEAD_DIM) + EPS),
                    lax.rsqrt((tot - s_even) * (1.0 / HEAD_DIM) + EPS))
    return t * inv * gain2


def _rows(ref, cols=slice(None)):
    if len(ref.shape) == 2:
        return ref[:, cols]
    return jnp.concatenate([ref[i, :, cols] for i in range(ref.shape[0])], axis=0)


def _store_rows(ref, cols, val):
    if len(ref.shape) == 2:
        ref[:, cols] = val
    else:
        n = ref.shape[1]
        for i in range(ref.shape[0]):
            ref[i, :, cols] = val[i * n:(i + 1) * n]


def _attn_a_kernel(q_ref, k_ref, v_ref, bias_ref, gq_ref, gk_ref, o_ref, lse_ref,
                   kn_ref, vv_ref):
    ub = pl.program_id(2)
    slot = ub % 2
    prev = 1 - slot

    @pl.when(ub == 0)
    def _():
        kn_ref[1] = jnp.zeros(kn_ref.shape[1:], BF16)
        vv_ref[1] = jnp.zeros(vv_ref.shape[1:], BF16)

    first = jnp.minimum(ub, 1)
    lane = lax.broadcasted_iota(jnp.int32, (Q_BLOCK, PAIR), 1)
    even = lane < HEAD_DIM
    lane_row = lax.broadcasted_iota(jnp.int32, (1, PAIR), 1)
    keep_even = jnp.where(lane_row < HEAD_DIM, 1.0, 0.0).astype(BF16)
    keep_odd = jnp.where(lane_row < HEAD_DIM, 0.0, 1.0).astype(BF16)
    gq2 = gq_ref[...] * (ATTN_SCALE * LOG2E)
    gk2 = gk_ref[...]
    for hp in range(N_PAIRS):
        cs = slice(hp * PAIR, (hp + 1) * PAIR)
        kn_ref[slot, :, cs] = _pairnorm(_rows(k_ref, cs), gk2, even).astype(BF16)
    vv_ref[slot] = _rows(v_ref)
    lse_tile = jnp.zeros((Q_BLOCK, LANES), F32)
    for hp in range(N_PAIRS):
        cs = slice(hp * PAIR, (hp + 1) * PAIR)
        qn = _pairnorm(_rows(q_ref, cs), gq2, even)
        qq = jnp.concatenate([jnp.where(even, qn, 0.0), jnp.where(even, 0.0, qn)],
                             axis=0).astype(BF16)
        kcat = jnp.concatenate([kn_ref[prev, :, cs], kn_ref[slot, :, cs]], axis=0)
        s = _dot_nt(qq, kcat) + bias_ref[first, hp]
        m = jnp.max(s, axis=-1, keepdims=True)
        p = jnp.exp2(s - m)
        l = jnp.sum(p, axis=-1, keepdims=True)
        pb = p.astype(BF16)
        vcat = jnp.concatenate([vv_ref[prev, :, cs], vv_ref[slot, :, cs]], axis=0)
        o = (_dot(pb[:Q_BLOCK], vcat * keep_even)
             + _dot(pb[Q_BLOCK:], vcat * keep_odd))
        inv = 1.0 / l
        o = o * jnp.where(even, inv[:Q_BLOCK], inv[Q_BLOCK:])
        _store_rows(o_ref, cs, o.astype(BF16))
        lse = m * LN2 + jnp.log(l)
        lse_tile = jnp.where(lane == 2 * hp, lse[:Q_BLOCK],
                             jnp.where(lane == 2 * hp + 1, lse[Q_BLOCK:], lse_tile))
    _store_rows(lse_ref, slice(None), lse_tile)


def _t5_bucket(dist):
    max_exact = REL_BUCKETS // 2
    d = jnp.maximum(dist, 0)
    large = max_exact + (jnp.log(jnp.maximum(d, 1).astype(F32) / max_exact)
                         / math.log(REL_MAX_DIST / max_exact)
                         * (REL_BUCKETS - max_exact)).astype(jnp.int32)
    large = jnp.minimum(large, REL_BUCKETS - 1)
    return jnp.where(d < max_exact, d, large)


def _bias_kernel(tab_ref, onehot_ref, o_ref):
    hi, mid, lo = _split3(tab_ref[...])
    oh = onehot_ref[...]
    o_ref[...] = (_dot(hi, oh) + _dot(mid, oh) + _dot(lo, oh)) * LOG2E


def _bias_tables(rel_bias_g, window, dilation):
    qi = jnp.arange(Q_BLOCK)[:, None]
    kj = jnp.arange(2 * Q_BLOCK)[None, :]
    dist = qi + Q_BLOCK - kj
    bucket = _t5_bucket(dist * dilation).reshape(1, -1)
    onehot = (bucket == jnp.arange(REL_BUCKETS)[:, None]).astype(BF16)
    npos = onehot.shape[1]
    chunk = 4096
    bias = pl.pallas_call(
        _bias_kernel,
        out_shape=jax.ShapeDtypeStruct((N_HEADS, npos), F32),
        grid=(npos // chunk,),
        in_specs=[pl.BlockSpec((N_HEADS, REL_BUCKETS), lambda i: (0, 0)),
                  pl.BlockSpec((REL_BUCKETS, chunk), lambda i: (0, i))],
        out_specs=pl.BlockSpec((N_HEADS, chunk), lambda i: (0, i)),
        compiler_params=_params(("arbitrary",)),
        name="rel_bias_table",
    )(rel_bias_g.T.astype(F32), onehot)
    bias = bias.reshape(N_PAIRS, 2 * Q_BLOCK, 2 * Q_BLOCK)
    band = (dist >= 0) & (dist <= window // dilation)
    first = band & (qi - dist >= 0)
    band = jnp.concatenate([band, band], axis=0)[None]
    first = jnp.concatenate([first, first], axis=0)[None]
    return jnp.stack([jnp.where(first, bias, NEG_INF), jnp.where(band, bias, NEG_INF)], axis=0)


def _attn_a_group(proj, g, rel_bias_g, gq, gk, window, dilation):
    b, s, cols = proj.shape
    r = dilation
    nt = s // PERM_TILE
    n_t = PERM_TILE // r
    nb = s // r // Q_BLOCK
    gw = N_HEADS * HEAD_DIM
    tables = _bias_tables(rel_bias_g, window, dilation)
    base = g * 3
    if n_t >= Q_BLOCK:
        bpt = n_t // Q_BLOCK
        blk = lambda w: (None, None, None, Q_BLOCK, w)
        imap = lambda col: (lambda bi, jr, ub: (bi, ub // bpt, jr, ub % bpt, col))
    else:
        blk = lambda w: (None, Q_BLOCK // n_t, None, n_t, w)
        imap = lambda col: (lambda bi, jr, ub: (bi, ub, jr, 0, col))
    pv = proj.reshape(b, nt, r, n_t, cols)
    gains = lambda t: jnp.concatenate([t, t]).reshape(1, PAIR)
    o, lse = pl.pallas_call(
        _attn_a_kernel,
        out_shape=(jax.ShapeDtypeStruct((b, nt, r, n_t, gw), BF16),
                   jax.ShapeDtypeStruct((b, nt, r, n_t, LANES), F32)),
        grid=(b, r, nb),
        in_specs=[
            pl.BlockSpec(blk(gw), imap(base)),
            pl.BlockSpec(blk(gw), imap(base + 1)),
            pl.BlockSpec(blk(gw), imap(base + 2)),
            pl.BlockSpec(tables.shape, lambda bi, jr, ub: (0, 0, 0, 0)),
            pl.BlockSpec((1, PAIR), lambda bi, jr, ub: (0, 0)),
            pl.BlockSpec((1, PAIR), lambda bi, jr, ub: (0, 0)),
        ],
        out_specs=(pl.BlockSpec(blk(gw), imap(0)), pl.BlockSpec(blk(LANES), imap(0))),
        scratch_shapes=[pltpu.VMEM((2, Q_BLOCK, gw), BF16), pltpu.VMEM((2, Q_BLOCK, gw), BF16)],
        compiler_params=_params(("arbitrary", "arbitrary", "arbitrary")),
        name=f"attn_a_g{g}",
    )(pv, pv, pv, tables, gains(gq), gains(gk))
    return o.reshape(b, s, gw), lse.reshape(b, s, LANES)


MERGE_CHUNK = 128


def _merge_out_kernel(o0_ref, o1_ref, o2_ref, l0_ref, l1_ref, l2_ref, x_ref, g_ref, w_ref,
                      e_ref, out_ref, lun_ref, oun_ref, om_ref):
    tm = o0_ref.shape[1]
    n_slab = o0_ref.shape[2] // LANES

    @pl.when(pl.program_id(1) == 0)
    def _():
        for gi, (o_ref, l_ref) in enumerate(((o1_ref, l1_ref), (o2_ref, l2_ref))):
            r = A_DILATED[gi + 1][1]
            n = tm // r
            for jr in range(r):
                rows = slice(jr * n, (jr + 1) * n)
                lun_ref[gi, pl.ds(jr, n, stride=r), :] = l_ref[0, rows, :]
                for c in range(n_slab):
                    oun_ref[gi, c, pl.ds(jr, n, stride=r), :] = (
                        o_ref[0, rows, c * LANES:(c + 1) * LANES].astype(F32))
        expand = e_ref[...]

        def widen(w):
            hi = w.astype(BF16)
            lo = (w - hi.astype(F32)).astype(BF16)
            return _dot(hi, expand) + _dot(lo, expand)

        for k in range(tm // MERGE_CHUNK):
            rs = slice(k * MERGE_CHUNK, (k + 1) * MERGE_CHUNK)
            l0, l1, l2 = l0_ref[0, rs, :], lun_ref[0, rs, :], lun_ref[1, rs, :]
            m = jnp.maximum(jnp.maximum(l0, l1), l2)
            e0, e1, e2 = jnp.exp(l0 - m), jnp.exp(l1 - m), jnp.exp(l2 - m)
            inv = 1.0 / (e0 + e1 + e2)
            o1 = jnp.concatenate([oun_ref[0, c, rs, :] for c in range(n_slab)], axis=1)
            o2 = jnp.concatenate([oun_ref[1, c, rs, :] for c in range(n_slab)], axis=1)
            o = (widen(e0 * inv) * o0_ref[0, rs, :].astype(F32)
                 + widen(e1 * inv) * o1 + widen(e2 * inv) * o2)
            om_ref[rs, :] = o.astype(BF16)

    out_ref[0] = x_ref[0] + g_ref[0] * _dot(om_ref[...], w_ref[...])


def _merge_out(outs, lses, x, gate, w_out, tn=512):
    b, s, d = x.shape
    tm = PERM_TILE
    spt = s // tm
    rmap = lambda i, j: (i // spt, i % spt, 0)
    cmap = lambda i, j: (i // spt, i % spt, j)
    expand = ((jnp.arange(d)[None, :] // HEAD_DIM == jnp.arange(LANES)[:, None])
              & (jnp.arange(LANES)[:, None] < N_HEADS)).astype(BF16)
    o_spec = pl.BlockSpec((1, tm, d), rmap)
    l_spec = pl.BlockSpec((1, tm, LANES), rmap)
    return pl.pallas_call(
        _merge_out_kernel,
        out_shape=jax.ShapeDtypeStruct((b, s, d), F32),
        grid=(b * spt, d // tn),
        in_specs=[o_spec, o_spec, o_spec, l_spec, l_spec, l_spec,
                  pl.BlockSpec((1, tm, tn), cmap),
                  pl.BlockSpec((1, 1, tn), lambda i, j: (i // spt, 0, j)),
                  pl.BlockSpec((d, tn), lambda i, j: (0, j)),
                  pl.BlockSpec((LANES, d), lambda i, j: (0, 0))],
        out_specs=pl.BlockSpec((1, tm, tn), cmap),
        scratch_shapes=[pltpu.VMEM((2, tm, LANES), F32),
                        pltpu.VMEM((2, d // LANES, tm, LANES), F32),
                        pltpu.VMEM((tm, d), BF16)],
        compiler_params=_params(("arbitrary", "arbitrary")),
        name="merge_out",
    )(*outs, *lses, x, gate, w_out, expand)


def _out_kernel(o_ref, x_ref, g_ref, w_ref, out_ref):
    out_ref[0] = x_ref[0] + g_ref[0] * _dot(o_ref[0], w_ref[...])


def _out_proj(o, x, gate, w_out, tm=512):
    b, s, d = x.shape
    spt = s // tm
    xmap = lambda i: (i // spt, i % spt, 0)
    return pl.pallas_call(
        _out_kernel,
        out_shape=jax.ShapeDtypeStruct((b, s, d), F32),
        grid=(b * spt,),
        in_specs=[pl.BlockSpec((1, tm, d), xmap),
                  pl.BlockSpec((1, tm, d), xmap),
                  pl.BlockSpec((1, 1, d), lambda i: (i // spt, 0, 0)),
                  pl.BlockSpec((d, d), lambda i: (0, 0))],
        out_specs=pl.BlockSpec((1, tm, d), xmap),
        compiler_params=_params(("arbitrary",)),
        name="out_proj",
    )(o, x, gate, w_out)


def _cum_kernel(f_ref, fb_ref, tri_ref, o_ref, carry_ref):
    @pl.when(pl.program_id(1) == 0)
    def _():
        carry_ref[...] = jnp.zeros_like(carry_ref)

    z = f_ref[0] + fb_ref[...]
    logf = jnp.minimum(z, 0.0) - jnp.log(1.0 + jnp.exp(-jnp.abs(z)))
    hi, mid, lo = _split3(logf)
    tri = tri_ref[...]
    cum = _dot(tri, hi) + _dot(tri, mid) + _dot(tri, lo) + carry_ref[...]
    o_ref[0] = cum
    carry_ref[...] = cum[-1:, :]


def _cum_forget(fg, f_bias, tc=512):
    b, s, w = fg.shape
    fb = jnp.pad(f_bias, (0, w - f_bias.shape[0])).reshape(1, w)
    tri = (jnp.arange(tc)[:, None] >= jnp.arange(tc)[None, :]).astype(BF16)
    return pl.pallas_call(
        _cum_kernel,
        out_shape=jax.ShapeDtypeStruct((b, s, w), F32),
        grid=(b, s // tc),
        in_specs=[pl.BlockSpec((1, tc, w), lambda i, j: (i, j, 0)),
                  pl.BlockSpec((1, w), lambda i, j: (0, 0)),
                  pl.BlockSpec((tc, tc), lambda i, j: (0, 0))],
        out_specs=pl.BlockSpec((1, tc, w), lambda i, j: (i, j, 0)),
        scratch_shapes=[pltpu.VMEM((1, w), F32)],
        compiler_params=_params(("arbitrary", "arbitrary")),
        name="cum_forget",
    )(fg, fb, tri)


def _gate_columns(cum_col, lead):
    rows = cum_col.shape[0]
    lane = lax.broadcasted_iota(jnp.int32, (rows, HEAD_DIM), 1)
    hi, mid, lo = (p.astype(F32) for p in _split3(cum_col if lead else -cum_col))
    off = 0 if lead else 3
    ext = jnp.where(lane < 6, 1.0, 0.0)
    ext = jnp.where(lane == off, hi, ext)
    ext = jnp.where(lane == off + 1, mid, ext)
    ext = jnp.where(lane == off + 2, lo, ext)
    return ext.astype(BF16)


def _pick_lane(tile, idx):
    lane = lax.broadcasted_iota(jnp.int32, tile.shape, 1)
    return jnp.sum(jnp.where(lane == idx, tile, 0.0), axis=-1, keepdims=True)


ONES_ROWS = 16
FOX_HEADS_PER_STEP = 8


def _fox_kernel(q_ref, k_ref, v_ref, cq_ref, ck_ref, gq_ref, gk_ref, o_ref, ka_ref, vt_ref,
                qat_ref, s_ref, p_ref, acc_ref, *, tq, tk):
    hg = pl.program_id(1)
    qi = pl.program_id(2)
    n_kt = vt_ref.shape[1]
    nh = ka_ref.shape[0]
    heads = range(nh)

    @pl.when(qi == 0)
    def _():
        gk = gk_ref[...]
        for hh in heads:
            sl = slice(hh * HEAD_DIM, (hh + 1) * HEAD_DIM)
            ka_ref[hh, :, :HEAD_DIM] = _headnorm(k_ref[0, :, sl], gk).astype(BF16)
            ck = _pick_lane(ck_ref[0], nh * hg + hh) * LOG2E
            ka_ref[hh, :, HEAD_DIM:] = _gate_columns(ck, lead=False)
        for pr in range(nh // 2):
            vt = v_ref[0, :, pr * PAIR:(pr + 1) * PAIR].astype(F32).T.astype(BF16)
            for hh in (2 * pr, 2 * pr + 1):
                rows = slice((hh % 2) * HEAD_DIM, (hh % 2 + 1) * HEAD_DIM)
                for jj in range(n_kt):
                    vt_ref[hh, jj, :HEAD_DIM] = vt[rows, jj * tk:(jj + 1) * tk]
                    vt_ref[hh, jj, HEAD_DIM:] = jnp.ones((ONES_ROWS, tk), BF16)

    gq = gq_ref[...] * (ATTN_SCALE * LOG2E)
    for hh in heads:
        sl = slice(hh * HEAD_DIM, (hh + 1) * HEAD_DIM)
        qn = _headnorm(q_ref[0, :, sl], gq)
        cq = _pick_lane(cq_ref[0], nh * hg + hh) * LOG2E
        qa = jnp.concatenate([qn, _gate_columns(cq, lead=True).astype(F32)], axis=1)
        qat_ref[hh] = qa.T.astype(BF16)
    key = lax.broadcasted_iota(jnp.int32, (tk, tq), 0)
    qry = lax.broadcasted_iota(jnp.int32, (tk, tq), 1)
    ahead = key - qry

    def scores(t, hh, diagonal):
        start = pl.multiple_of(t * tk, tk)
        st = _dot(ka_ref[hh, pl.ds(start, tk), :], qat_ref[hh])
        if diagonal is None:
            st = jnp.where(ahead > jnp.where(t == qi, 0, tk), NEG_INF, st)
        elif diagonal:
            st = jnp.where(ahead > 0, NEG_INF, st)
        s_ref[hh] = st
        return jnp.max(st, axis=0, keepdims=True)

    def pv(t, hh, alpha_prev):
        acc_ref[hh] = alpha_prev * acc_ref[hh] + _dot(vt_ref[hh, t], p_ref[hh])

    def trip(i, carry, next_diagonal):
        out = []
        for hh in heads:
            m, alpha_prev, mx = carry[hh]
            pv(jnp.maximum(i - 1, 0), hh, alpha_prev)
            m_new = jnp.maximum(m, mx)
            alpha = jnp.exp2(m - m_new)
            p_ref[hh] = jnp.exp2(s_ref[hh] - m_new).astype(BF16)
            if next_diagonal is not None:
                mx = scores(i + 1, hh, next_diagonal)
            out.append((m_new, alpha, mx))
        return tuple(out)

    p_ref[...] = jnp.zeros_like(p_ref)
    acc_ref[...] = jnp.zeros_like(acc_ref)
    carry = tuple((jnp.full((1, tq), NEG_INF, F32), jnp.ones((1, tq), F32),
                   scores(0, hh, None)) for hh in heads)
    carry = lax.fori_loop(0, jnp.maximum(qi - 1, 0),
                          functools.partial(trip, next_diagonal=False), carry)
    carry = lax.fori_loop(jnp.maximum(qi - 1, 0), qi,
                          functools.partial(trip, next_diagonal=True), carry)
    carry = trip(qi, carry, None)
    for pr in range(nh // 2):
        o_t = []
        for hh in (2 * pr, 2 * pr + 1):
            pv(qi, hh, carry[hh][1])
            acc = acc_ref[hh]
            o_t.append(acc[:HEAD_DIM] / acc[HEAD_DIM:HEAD_DIM + 1])
        o_ref[0, :, pr * PAIR:(pr + 1) * PAIR] = (
            jnp.concatenate(o_t, axis=0).T.astype(BF16))


def _fox_attention(qkv, cum, gq, gk, tq=256):
    b, s, _ = qkv.shape
    tk = tq
    nq = s // tq
    nh = FOX_HEADS_PER_STEP
    groups = N_HEADS // nh
    gw = nh * HEAD_DIM
    vrows = HEAD_DIM + ONES_ROWS
    return pl.pallas_call(
        functools.partial(_fox_kernel, tq=tq, tk=tk),
        out_shape=jax.ShapeDtypeStruct((b, s, N_HEADS * HEAD_DIM), BF16),
        grid=(b, groups, nq),
        in_specs=[
            pl.BlockSpec((1, tq, gw), lambda bi, hg, qi: (bi, qi, hg)),
            pl.BlockSpec((1, s, gw), lambda bi, hg, qi: (bi, 0, groups + hg)),
            pl.BlockSpec((1, s, gw), lambda bi, hg, qi: (bi, 0, 2 * groups + hg)),
            pl.BlockSpec((1, tq, LANES), lambda bi, hg, qi: (bi, qi, 0)),
            pl.BlockSpec((1, s, LANES), lambda bi, hg, qi: (bi, 0, 0)),
            pl.BlockSpec((1, HEAD_DIM), lambda bi, hg, qi: (0, 0)),
            pl.BlockSpec((1, HEAD_DIM), lambda bi, hg, qi: (0, 0)),
        ],
        out_specs=pl.BlockSpec((1, tq, gw), lambda bi, hg, qi: (bi, qi, hg)),
        scratch_shapes=[pltpu.VMEM((nh, s, 2 * HEAD_DIM), BF16),
                        pltpu.VMEM((nh, s // tk, vrows, tk), BF16),
                        pltpu.VMEM((nh, 2 * HEAD_DIM, tq), BF16),
                        pltpu.VMEM((nh, tk, tq), F32),
                        pltpu.VMEM((nh, tk, tq), BF16),
                        pltpu.VMEM((nh, vrows, tq), F32)],
        compiler_params=_params(("arbitrary", "arbitrary", "arbitrary")),
        name="fox_attn",
    )(qkv, qkv, qkv, cum, cum, gq.reshape(1, HEAD_DIM), gk.reshape(1, HEAD_DIM))


ROUTER_LANES = LANES


def _route(logits):
    lane = lax.broadcasted_iota(jnp.int32, logits.shape, 1)
    lane_f = lane.astype(F32)
    big = float(ROUTER_LANES)
    is_g = lane < N_GROUPS
    gl = jnp.where(is_g, logits, -jnp.inf)
    gmax = jnp.max(gl, axis=-1, keepdims=True)
    gsum = jnp.sum(jnp.where(is_g, jnp.exp(logits - gmax), 0.0), axis=-1, keepdims=True)
    g_w = 1.0 / gsum
    gidx = jnp.min(jnp.where(gl == gmax, lane_f, big), axis=-1, keepdims=True)
    lo = N_GROUPS + EXPERTS_PER_GROUP * gidx
    in_grp = (lane_f >= lo) & (lane_f < lo + EXPERTS_PER_GROUP)
    el = jnp.where(in_grp, logits, -jnp.inf)
    t1 = jnp.max(el, axis=-1, keepdims=True)
    i1 = jnp.min(jnp.where(el == t1, lane_f, big), axis=-1, keepdims=True)
    el2 = jnp.where(lane_f == i1, -jnp.inf, el)
    t2 = jnp.max(el2, axis=-1, keepdims=True)
    i2 = jnp.min(jnp.where(el2 == t2, lane_f, big), axis=-1, keepdims=True)
    e2 = jnp.exp(t2 - t1)
    w1 = g_w / (1.0 + e2)
    w2 = w1 * e2
    gates = jnp.where(lane_f == i1, w1, jnp.where(lane_f == i2, w2, 0.0))
    return jnp.where(lane == 0, gidx, gates)


MOE_TILE = 512
HALF_D = D_MODEL // 2
ROW_WORDS = HALF_D + ROUTER_LANES
U32 = jnp.uint32


def _pack_pairs(x):
    n = x.shape[1] // 2
    hi = pltpu.bitcast(x[:, :n].astype(BF16).astype(F32), U32)
    lo = pltpu.bitcast(x[:, n:].astype(BF16).astype(F32), U32)
    return hi | (lo >> 16)


def _unpack_pairs(w):
    hi = pltpu.bitcast(w & U32(0xFFFF0000), F32)
    lo = pltpu.bitcast(w << 16, F32)
    return jnp.concatenate([hi, lo], axis=1)


def _moe_route_kernel(x_ref, gam_ref, sc_ref, sh_ref, wr_ref, br_ref, o_ref):
    h = _modnorm(x_ref[0], gam_ref[...], sc_ref[0], sh_ref[0])
    h_hi = h.astype(BF16)
    h_lo = (h - h_hi.astype(F32)).astype(BF16)
    wr = wr_ref[...]
    w_hi = wr.astype(BF16)
    w_lo = (wr - w_hi.astype(F32)).astype(BF16)
    logits = _dot(h_hi, w_hi) + _dot(h_hi, w_lo) + _dot(h_lo, w_hi) + br_ref[...]
    o_ref[:, :HALF_D] = _pack_pairs(h)
    o_ref[:, HALF_D:] = pltpu.bitcast(_route(logits), U32)


def _row_copy(src, src_row, dst, dst_row, sem):
    return pltpu.make_async_copy(src.at[pl.ds(src_row, 1), :], dst.at[pl.ds(dst_row, 1), :], sem)


def _moe_expert_kernel(tg_ref, nt_ref, sg_ref, nv_ref, hx_hbm, wg_ref, wu_ref, wd_ref, e_ref,
                       y_hbm, buf, ybuf, gsem, ssem):
    t = pl.program_id(0)
    n_used = nt_ref[0]
    slot = t % 2
    rows = buf.shape[1]
    unroll = 8

    def gather_start(tile, sl):
        def body(r, c):
            _row_copy(hx_hbm, sg_ref[tile * rows + r], buf.at[sl], r, gsem.at[sl]).start()
            return c
        lax.fori_loop(0, rows, body, 0, unroll=unroll)

    def gather_wait(sl):
        def body(r, c):
            _row_copy(hx_hbm, 0, buf.at[sl], r, gsem.at[sl]).wait()
            return c
        lax.fori_loop(0, rows, body, 0, unroll=unroll)

    def valid_rows_loop(tile, body):
        n_valid = nv_ref[tile]
        chunks = n_valid // unroll

        def chunk(k, c):
            for i in range(unroll):
                body(k * unroll + i)
            return c
        lax.fori_loop(0, chunks, chunk, 0)

        def single(r, c):
            body(r)
            return c
        lax.fori_loop(chunks * unroll, n_valid, single, 0)

    def scatter_start(tile, sl):
        valid_rows_loop(tile, lambda r: _row_copy(
            ybuf.at[sl], r, y_hbm, sg_ref[tile * rows + r], ssem.at[sl]).start())

    def scatter_wait(tile, sl):
        valid_rows_loop(tile, lambda r: _row_copy(ybuf.at[sl], r, y_hbm, 0, ssem.at[sl]).wait())

    @pl.when(t == 0)
    def _():
        gather_start(0, 0)

    @pl.when(t + 1 < n_used)
    def _():
        gather_start(t + 1, 1 - slot)

    @pl.when(t < n_used)
    def _():
        gather_wait(slot)
        w = buf[slot]
        h = _unpack_pairs(w[:, :HALF_D]).astype(BF16)
        gates = pltpu.bitcast(w[:, HALF_D:], F32)
        a = _dot(h, wg_ref[0])
        u = _dot(h, wu_ref[0])
        g_hi = gates.astype(BF16)
        g_lo = (gates - g_hi.astype(F32)).astype(BF16)
        gexp = _dot(g_hi, e_ref[0]) + _dot(g_lo, e_ref[0])
        hid = a * (1.0 / (1.0 + jnp.exp(-a))) * u * gexp
        y = _dot(hid.astype(BF16), wd_ref[0])

        @pl.when(t >= 2)
        def _():
            scatter_wait(t - 2, slot)

        ybuf[slot] = _pack_pairs(y)
        scatter_start(t, slot)

    @pl.when(t == pl.num_programs(0) - 1)
    def _():
        for back in (2, 1):
            scatter_wait(n_used - back, (n_used - back) % 2)


def _moe_combine_kernel(x_ref, g_ref, y_ref, o_ref):
    o_ref[0] = x_ref[0] + g_ref[0] * _unpack_pairs(y_ref[...])


def _moe(x, gamma, sc, sh, gate, w_router, b_router, w_gate, w_up, w_down):
    b, s, d = x.shape
    n = b * s
    tm = PERM_TILE
    spt = s // tm
    rows = MOE_TILE
    n_tiles = n // rows + N_GROUPS
    p_rows = n_tiles * rows
    bmap = lambda i: (i // spt, 0, 0)
    hx = pl.pallas_call(
        _moe_route_kernel,
        out_shape=jax.ShapeDtypeStruct((n, ROW_WORDS), U32),
        grid=(b * spt,),
        in_specs=[
            pl.BlockSpec((1, tm, d), lambda i: (i // spt, i % spt, 0)),
            pl.BlockSpec((1, d), lambda i: (0, 0)),
            pl.BlockSpec((1, 1, d), bmap),
            pl.BlockSpec((1, 1, d), bmap),
            pl.BlockSpec((d, ROUTER_LANES), lambda i: (0, 0)),
            pl.BlockSpec((1, ROUTER_LANES), lambda i: (0, 0)),
        ],
        out_specs=pl.BlockSpec((tm, ROW_WORDS), lambda i: (i, 0)),
        compiler_params=_params(("arbitrary",)),
        name="moe_route",
    )(x, gamma.reshape(1, d), sc, sh, w_router, b_router)

    gidx = lax.bitcast_convert_type(hx[:, HALF_D], F32).astype(jnp.int32)
    onehot = (gidx[:, None] == jnp.arange(N_GROUPS)[None, :]).astype(jnp.int32)
    csum = jnp.cumsum(onehot, axis=0)
    rank = jnp.sum((csum - onehot) * onehot, axis=1)
    padded = (csum[-1] + rows - 1) // rows * rows
    ends = jnp.cumsum(padded)
    dest = jnp.sum(onehot * (ends - padded)[None, :], axis=1) + rank
    n_used = (ends[-1] // rows).reshape(1).astype(jnp.int32)
    tile_start = jnp.arange(n_tiles, dtype=jnp.int32) * rows
    tile_group = jnp.minimum(jnp.sum(ends[None, :] <= tile_start[:, None], axis=1),
                             N_GROUPS - 1).astype(jnp.int32)
    seg_valid_end = ends - padded + csum[-1]
    n_valid = jnp.clip(jnp.sum(jnp.where(jnp.arange(N_GROUPS)[None, :] == tile_group[:, None],
                                         seg_valid_end[None, :], 0), axis=1) - tile_start,
                       0, rows).astype(jnp.int32)
    src_token = jnp.zeros((p_rows,), jnp.int32).at[dest].set(
        jnp.arange(n, dtype=jnp.int32))

    ff = w_gate.shape[2]
    lanes = jnp.arange(ROUTER_LANES)[None, :, None]
    expert = N_GROUPS + EXPERTS_PER_GROUP * jnp.arange(N_GROUPS)[:, None, None] \
        + jnp.arange(ff)[None, None, :] // EXPERT_FF
    expand = (lanes == expert).astype(BF16)
    wmap = lambda t, tg, nt, sg, ss: (tg[t], 0, 0)
    y_tok = pl.pallas_call(
        _moe_expert_kernel,
        out_shape=jax.ShapeDtypeStruct((n, HALF_D), U32),
        grid_spec=pltpu.PrefetchScalarGridSpec(
            num_scalar_prefetch=4,
            grid=(n_tiles,),
            in_specs=[
                pl.BlockSpec(memory_space=pl.ANY),
                pl.BlockSpec((1, d, ff), wmap),
                pl.BlockSpec((1, d, ff), wmap),
                pl.BlockSpec((1, ff, d), wmap),
                pl.BlockSpec((1, ROUTER_LANES, ff), wmap),
            ],
            out_specs=pl.BlockSpec(memory_space=pl.ANY),
            scratch_shapes=[pltpu.VMEM((2, rows, ROW_WORDS), U32),
                            pltpu.VMEM((2, rows, HALF_D), U32),
                            pltpu.SemaphoreType.DMA((2,)),
                            pltpu.SemaphoreType.DMA((2,))],
        ),
        compiler_params=_params(("arbitrary",)),
        name="moe_experts",
    )(tile_group, n_used, src_token, n_valid, hx, w_gate, w_up, w_down, expand)

    return pl.pallas_call(
        _moe_combine_kernel,
        out_shape=jax.ShapeDtypeStruct((b, s, d), F32),
        grid=(b * spt,),
        in_specs=[pl.BlockSpec((1, tm, d), lambda i: (i // spt, i % spt, 0)),
                  pl.BlockSpec((1, 1, d), bmap),
                  pl.BlockSpec((tm, HALF_D), lambda i: (i, 0))],
        out_specs=pl.BlockSpec((1, tm, d), lambda i: (i // spt, i % spt, 0)),
        compiler_params=_params(("arbitrary",)),
        name="moe_combine",
    )(x, gate, y_tok)


def kernel(x, c, w_ada, b_ada, norm_mix, norm_ffn, rel_bias, a_w_in, a_w_out, a_q_norm,
           a_k_norm, b_w_in, b_f_bias, b_w_out, b_q_norm, b_k_norm, router_group_w,
           router_group_b, router_expert_w, router_expert_b, w_gate, w_up, w_down):
    b, s, d = x.shape
    depth = w_ada.shape[0]
    mod = _ada(c, w_ada, b_ada).reshape(depth, b, 6, 1, d)
    pad_r = ROUTER_LANES - N_GROUPS - N_EXPERTS
    for i in range(depth):
        sh_m, sc_m, g_m, sh_f, sc_f, g_f = (mod[i, :, k] for k in range(6))
        j = i // 2
        if i % 2 == 0:
            proj = _normproj_a(x, norm_mix[i], sc_m, sh_m, a_w_in[j].astype(BF16))
            outs, lses = [], []
            for g, (window, dil) in enumerate(A_DILATED):
                tab = rel_bias[:, g * N_HEADS:(g + 1) * N_HEADS]
                o, l = _attn_a_group(proj, g, tab, a_q_norm[j], a_k_norm[j], window, dil)
                outs.append(o)
                lses.append(l)
            x = _merge_out(outs, lses, x, g_m, a_w_out[j].astype(BF16))
        else:
            n_qkv = 3 * N_HEADS * HEAD_DIM
            w_in = b_w_in[j]
            w_f = jnp.pad(w_in[:, n_qkv:], ((0, 0), (0, LANES - N_HEADS))).astype(BF16)
            qkv, fg = _normproj(x, norm_mix[i], sc_m, sh_m, w_in[:, :n_qkv].astype(BF16), w_f)
            cum = _cum_forget(fg, b_f_bias[j])
            o = _fox_attention(qkv, cum, b_q_norm[j], b_k_norm[j])
            x = _out_proj(o, x, g_m, b_w_out[j].astype(BF16))
        w_router = jnp.pad(jnp.concatenate([router_group_w[i], router_expert_w[i]], axis=1),
                           ((0, 0), (0, pad_r)))
        b_router = jnp.pad(jnp.concatenate([router_group_b[i], router_expert_b[i]]),
                           (0, pad_r)).reshape(1, ROUTER_LANES)
        side_by_side = lambda w: (w.reshape(N_GROUPS, EXPERTS_PER_GROUP, d, EXPERT_FF)
                                  .transpose(0, 2, 1, 3)
                                  .reshape(N_GROUPS, d, EXPERTS_PER_GROUP * EXPERT_FF).astype(BF16))
        x = _moe(x, norm_ffn[i], sc_f, sh_f, g_f, w_router, b_router,
                 side_by_side(w_gate[i]), side_by_side(w_up[i]),
                 w_down[i].reshape(N_GROUPS, EXPERTS_PER_GROUP * EXPERT_FF, d).astype(BF16))
    return x
```

```python
import functools
import math

import jax
import jax.numpy as jnp
from jax import lax
from jax.experimental import pallas as pl
from jax.experimental.pallas import tpu as pltpu

F32 = jnp.float32
BF16 = jnp.bfloat16

D_MODEL = 1024
HEAD_DIM = 64
N_HEADS = 16
ATTN_SCALE = HEAD_DIM ** -0.5
A_DILATED = ((128, 1), (512, 4), (2048, 16))
Q_BLOCK = 128
REL_BUCKETS = 32
REL_MAX_DIST = 2048
N_GROUPS = 4
EXPERTS_PER_GROUP = 4
N_EXPERTS = 16
EXPERT_FF = 256
EPS = 1e-6
NEG_INF = -1e30

LANES = 128
VMEM_LIMIT = 48 * 1024 * 1024
LOG2E = 1.4426950408889634
LN2 = 0.6931471805599453
PERM_TILE = 1024
PAIR = 2 * HEAD_DIM
N_PAIRS = N_HEADS // 2


def _params(sem):
    return pltpu.CompilerParams(dimension_semantics=sem, vmem_limit_bytes=VMEM_LIMIT)


def _split3(x):
    hi = x.astype(BF16)
    r1 = x - hi.astype(F32)
    mid = r1.astype(BF16)
    lo = (r1 - mid.astype(F32)).astype(BF16)
    return hi, mid, lo


def _dot(a, b):
    return jnp.dot(a, b, preferred_element_type=F32)


def _dot_nt(a, b):
    return lax.dot_general(a, b, (((1,), (1,)), ((), ())), preferred_element_type=F32)


def _ada_kernel(c_ref, w_ref, b_ref, o_ref):
    c = c_ref[...]
    a = c * (1.0 / (1.0 + jnp.exp(-c)))
    o_ref[0] = _dot(a.astype(BF16), w_ref[0].astype(BF16)) + b_ref[0]


def _ada(c, w_ada, b_ada):
    depth, d, n6 = w_ada.shape
    b = c.shape[0]
    rows = 8
    c_pad = jnp.pad(c, ((0, rows - b), (0, 0)))
    tn = 1536
    out = pl.pallas_call(
        _ada_kernel,
        out_shape=jax.ShapeDtypeStruct((depth, rows, n6), F32),
        grid=(depth, n6 // tn),
        in_specs=[
            pl.BlockSpec((rows, d), lambda i, j: (0, 0)),
            pl.BlockSpec((1, d, tn), lambda i, j: (i, 0, j)),
            pl.BlockSpec((1, 1, tn), lambda i, j: (i, 0, j)),
        ],
        out_specs=pl.BlockSpec((1, rows, tn), lambda i, j: (i, 0, j)),
        compiler_params=_params(("arbitrary", "arbitrary")),
        name="ada_mod",
    )(c_pad, w_ada, b_ada.reshape(depth, 1, n6))
    return out[:, :b]


def _modnorm(x, gamma, sc, sh):
    ms = jnp.mean(x * x, axis=-1, keepdims=True)
    y = x * lax.rsqrt(ms + EPS) * gamma
    return y * (1.0 + sc) + sh


def _normproj_kernel(x_ref, g_ref, sc_ref, sh_ref, w_ref, o_ref, h_ref):
    @pl.when(pl.program_id(1) == 0)
    def _():
        h_ref[...] = _modnorm(x_ref[0], g_ref[...], sc_ref[0], sh_ref[0]).astype(BF16)

    o_ref[0] = _dot(h_ref[...], w_ref[...]).astype(BF16)


def _normproj_gate_kernel(x_ref, g_ref, sc_ref, sh_ref, w_ref, wf_ref, o_ref, f_ref, h_ref):
    @pl.when(pl.program_id(1) == 0)
    def _():
        h = _modnorm(x_ref[0], g_ref[...], sc_ref[0], sh_ref[0]).astype(BF16)
        h_ref[...] = h
        f_ref[0] = _dot(h, wf_ref[...])

    o_ref[0] = _dot(h_ref[...], w_ref[...]).astype(BF16)


def _normproj(x, gamma, sc, sh, w, w_f=None, tm=1024, tn=1024):
    b, s, d = x.shape
    ncols = w.shape[1]
    spt = s // tm
    grid = (b * spt, ncols // tn)
    xmap = lambda i, j: (i // spt, i % spt, 0)
    bmap = lambda i, j: (i // spt, 0, 0)
    in_specs = [
        pl.BlockSpec((1, tm, d), xmap),
        pl.BlockSpec((1, d), lambda i, j: (0, 0)),
        pl.BlockSpec((1, 1, d), bmap),
        pl.BlockSpec((1, 1, d), bmap),
        pl.BlockSpec((d, tn), lambda i, j: (0, j)),
    ]
    o_spec = pl.BlockSpec((1, tm, tn), lambda i, j: (i // spt, i % spt, j))
    scratch = [pltpu.VMEM((tm, d), BF16)]
    if w_f is None:
        return pl.pallas_call(
            _normproj_kernel,
            out_shape=jax.ShapeDtypeStruct((b, s, ncols), BF16),
            grid=grid, in_specs=in_specs, out_specs=o_spec, scratch_shapes=scratch,
            compiler_params=_params(("arbitrary", "arbitrary")),
            name="normproj",
        )(x, gamma.reshape(1, d), sc, sh, w)
    in_specs.append(pl.BlockSpec((d, LANES), lambda i, j: (0, 0)))
    f_spec = pl.BlockSpec((1, tm, LANES), xmap)
    return pl.pallas_call(
        _normproj_gate_kernel,
        out_shape=(jax.ShapeDtypeStruct((b, s, ncols), BF16),
                   jax.ShapeDtypeStruct((b, s, LANES), F32)),
        grid=grid, in_specs=in_specs, out_specs=(o_spec, f_spec), scratch_shapes=scratch,
        compiler_params=_params(("arbitrary", "arbitrary")),
        name="normproj_gate",
    )(x, gamma.reshape(1, d), sc, sh, w, w_f)


def _normproj_a_kernel(x_ref, g_ref, sc_ref, sh_ref, w_ref, o_ref, hf_ref, hp_ref):
    tm = x_ref.shape[1]
    n_slab = x_ref.shape[2] // LANES

    @pl.when(pl.program_id(1) == 0)
    def _():
        h = _modnorm(x_ref[0], g_ref[...], sc_ref[0], sh_ref[0])
        hp_ref[0] = h.astype(BF16)
        for c in range(n_slab):
            hf_ref[c] = h[:, c * LANES:(c + 1) * LANES]
        for g, (_, r) in enumerate(A_DILATED):
            if r == 1:
                continue
            n = tm // r
            for jr in range(r):
                for c in range(n_slab):
                    hp_ref[g, jr * n:(jr + 1) * n, c * LANES:(c + 1) * LANES] = (
                        hf_ref[c, pl.ds(jr, n, stride=r), :].astype(BF16))

    o_ref[0] = _dot(hp_ref[pl.program_id(1) // 3], w_ref[...]).astype(BF16)


def _normproj_a(x, gamma, sc, sh, w, tn=1024):
    b, s, d = x.shape
    tm = PERM_TILE
    ncols = w.shape[1]
    spt = s // tm
    bmap = lambda i, j: (i // spt, 0, 0)
    return pl.pallas_call(
        _normproj_a_kernel,
        out_shape=jax.ShapeDtypeStruct((b, s, ncols), BF16),
        grid=(b * spt, ncols // tn),
        in_specs=[
            pl.BlockSpec((1, tm, d), lambda i, j: (i // spt, i % spt, 0)),
            pl.BlockSpec((1, d), lambda i, j: (0, 0)),
            pl.BlockSpec((1, 1, d), bmap),
            pl.BlockSpec((1, 1, d), bmap),
            pl.BlockSpec((d, tn), lambda i, j: (0, j)),
        ],
        out_specs=pl.BlockSpec((1, tm, tn), lambda i, j: (i // spt, i % spt, j)),
        scratch_shapes=[pltpu.VMEM((d // LANES, tm, LANES), F32),
                        pltpu.VMEM((len(A_DILATED), tm, d), BF16)],
        compiler_params=_params(("arbitrary", "arbitrary")),
        name="normproj_a",
    )(x, gamma.reshape(1, d), sc, sh, w)


def _headnorm(t, gain):
    t = t.astype(F32)
    ms = jnp.mean(t * t, axis=-1, keepdims=True)
    return t * lax.rsqrt(ms + EPS) * gain


def _pairnorm(t, gain2, even):
    t = t.astype(F32)
    sq = t * t
    tot = jnp.sum(sq, axis=-1, keepdims=True)
    s_even = jnp.sum(jnp.where(even, sq, 0.0), axis=-1, keepdims=True)
    inv = jnp.where(even, lax.rsqrt(s_even * (1.0 / HEAD_DIM) + EPS),
                    lax.rsqrt((tot - s_even) * (1.0 / HEAD_DIM) + EPS))
    return t * inv * gain2


def _rows(ref, cols=slice(None)):
    if len(ref.shape) == 2:
        return ref[:, cols]
    return jnp.concatenate([ref[i, :, cols] for i in range(ref.shape[0])], axis=0)


def _store_rows(ref, cols, val):
    if len(ref.shape) == 2:
        ref[:, cols] = val
    else:
        n = ref.shape[1]
        for i in range(ref.shape[0]):
            ref[i, :, cols] = val[i * n:(i + 1) * n]


def _attn_a_kernel(q_ref, k_ref, v_ref, bias_ref, gq_ref, gk_ref, o_ref, lse_ref,
                   kn_ref, vv_ref):
    ub = pl.program_id(2)
    slot = ub % 2
    prev = 1 - slot

    @pl.when(ub == 0)
    def _():
        kn_ref[1] = jnp.zeros(kn_ref.shape[1:], BF16)
        vv_ref[1] = jnp.zeros(vv_ref.shape[1:], BF16)

    first = jnp.minimum(ub, 1)
    lane = lax.broadcasted_iota(jnp.int32, (Q_BLOCK, PAIR), 1)
    even = lane < HEAD_DIM
    lane_row = lax.broadcasted_iota(jnp.int32, (1, PAIR), 1)
    keep_even = jnp.where(lane_row < HEAD_DIM, 1.0, 0.0).astype(BF16)
    keep_odd = jnp.where(lane_row < HEAD_DIM, 0.0, 1.0).astype(BF16)
    gq2 = gq_ref[...] * (ATTN_SCALE * LOG2E)
    gk2 = gk_ref[...]
    for hp in range(N_PAIRS):
        cs = slice(hp * PAIR, (hp + 1) * PAIR)
        kn_ref[slot, :, cs] = _pairnorm(_rows(k_ref, cs), gk2, even).astype(BF16)
    vv_ref[slot] = _rows(v_ref)
    lse_tile = jnp.zeros((Q_BLOCK, LANES), F32)
    for hp in range(N_PAIRS):
        cs = slice(hp * PAIR, (hp + 1) * PAIR)
        qn = _pairnorm(_rows(q_ref, cs), gq2, even)
        qq = jnp.concatenate([jnp.where(even, qn, 0.0), jnp.where(even, 0.0, qn)],
                             axis=0).astype(BF16)
        kcat = jnp.concatenate([kn_ref[prev, :, cs], kn_ref[slot, :, cs]], axis=0)
        s = _dot_nt(qq, kcat) + bias_ref[first, hp]
        m = jnp.max(s, axis=-1, keepdims=True)
        p = jnp.exp2(s - m)
        l = jnp.sum(p, axis=-1, keepdims=True)
        pb = p.astype(BF16)
        vcat = jnp.concatenate([vv_ref[prev, :, cs], vv_ref[slot, :, cs]], axis=0)
        o = (_dot(pb[:Q_BLOCK], vcat * keep_even)
             + _dot(pb[Q_BLOCK:], vcat * keep_odd))
        inv = 1.0 / l
        o = o * jnp.where(even, inv[:Q_BLOCK], inv[Q_BLOCK:])
        _store_rows(o_ref, cs, o.astype(BF16))
        lse = m * LN2 + jnp.log(l)
        lse_tile = jnp.where(lane == 2 * hp, lse[:Q_BLOCK],
                             jnp.where(lane == 2 * hp + 1, lse[Q_BLOCK:], lse_tile))
    _store_rows(lse_ref, slice(None), lse_tile)


def _t5_bucket(dist):
    max_exact = REL_BUCKETS // 2
    d = jnp.maximum(dist, 0)
    large = max_exact + (jnp.log(jnp.maximum(d, 1).astype(F32) / max_exact)
                         / math.log(REL_MAX_DIST / max_exact)
                         * (REL_BUCKETS - max_exact)).astype(jnp.int32)
    large = jnp.minimum(large, REL_BUCKETS - 1)
    return jnp.where(d < max_exact, d, large)


def _bias_kernel(tab_ref, onehot_ref, o_ref):
    hi, mid, lo = _split3(tab_ref[...])
    oh = onehot_ref[...]
    o_ref[...] = (_dot(hi, oh) + _dot(mid, oh) + _dot(lo, oh)) * LOG2E


def _bias_tables(rel_bias_g, window, dilation):
    qi = jnp.arange(Q_BLOCK)[:, None]
    kj = jnp.arange(2 * Q_BLOCK)[None, :]
    dist = qi + Q_BLOCK - kj
    bucket = _t5_bucket(dist * dilation).reshape(1, -1)
    onehot = (bucket == jnp.arange(REL_BUCKETS)[:, None]).astype(BF16)
    npos = onehot.shape[1]
    chunk = 4096
    bias = pl.pallas_call(
        _bias_kernel,
        out_shape=jax.ShapeDtypeStruct((N_HEADS, npos), F32),
        grid=(npos // chunk,),
        in_specs=[pl.BlockSpec((N_HEADS, REL_BUCKETS), lambda i: (0, 0)),
                  pl.BlockSpec((REL_BUCKETS, chunk), lambda i: (0, i))],
        out_specs=pl.BlockSpec((N_HEADS, chunk), lambda i: (0, i)),
        compiler_params=_params(("arbitrary",)),
        name="rel_bias_table",
    )(rel_bias_g.T.astype(F32), onehot)
    bias = bias.reshape(N_PAIRS, 2 * Q_BLOCK, 2 * Q_BLOCK)
    band = (dist >= 0) & (dist <= window // dilation)
    first = band & (qi - dist >= 0)
    band = jnp.concatenate([band, band], axis=0)[None]
    first = jnp.concatenate([first, first], axis=0)[None]
    return jnp.stack([jnp.where(first, bias, NEG_INF), jnp.where(band, bias, NEG_INF)], axis=0)


def _attn_a_group(proj, g, rel_bias_g, gq, gk, window, dilation):
    b, s, cols = proj.shape
    r = dilation
    nt = s // PERM_TILE
    n_t = PERM_TILE // r
    nb = s // r // Q_BLOCK
    gw = N_HEADS * HEAD_DIM
    tables = _bias_tables(rel_bias_g, window, dilation)
    base = g * 3
    if n_t >= Q_BLOCK:
        bpt = n_t // Q_BLOCK
        blk = lambda w: (None, None, None, Q_BLOCK, w)
        imap = lambda col: (lambda bi, jr, ub: (bi, ub // bpt, jr, ub % bpt, col))
    else:
        blk = lambda w: (None, Q_BLOCK // n_t, None, n_t, w)
        imap = lambda col: (lambda bi, jr, ub: (bi, ub, jr, 0, col))
    pv = proj.reshape(b, nt, r, n_t, cols)
    gains = lambda t: jnp.concatenate([t, t]).reshape(1, PAIR)
    o, lse = pl.pallas_call(
        _attn_a_kernel,
        out_shape=(jax.ShapeDtypeStruct((b, nt, r, n_t, gw), BF16),
                   jax.ShapeDtypeStruct((b, nt, r, n_t, LANES), F32)),
        grid=(b, r, nb),
        in_specs=[
            pl.BlockSpec(blk(gw), imap(base)),
            pl.BlockSpec(blk(gw), imap(base + 1)),
            pl.BlockSpec(blk(gw), imap(base + 2)),
            pl.BlockSpec(tables.shape, lambda bi, jr, ub: (0, 0, 0, 0)),
            pl.BlockSpec((1, PAIR), lambda bi, jr, ub: (0, 0)),
            pl.BlockSpec((1, PAIR), lambda bi, jr, ub: (0, 0)),
        ],
        out_specs=(pl.BlockSpec(blk(gw), imap(0)), pl.BlockSpec(blk(LANES), imap(0))),
        scratch_shapes=[pltpu.VMEM((2, Q_BLOCK, gw), BF16), pltpu.VMEM((2, Q_BLOCK, gw), BF16)],
        compiler_params=_params(("arbitrary", "arbitrary", "arbitrary")),
        name=f"attn_a_g{g}",
    )(pv, pv, pv, tables, gains(gq), gains(gk))
    return o.reshape(b, s, gw), lse.reshape(b, s, LANES)


MERGE_CHUNK = 128


def _merge_out_kernel(o0_ref, o1_ref, o2_ref, l0_ref, l1_ref, l2_ref, x_ref, g_ref, w_ref,
                      e_ref, out_ref, lun_ref, oun_ref, om_ref):
    tm = o0_ref.shape[1]
    n_slab = o0_ref.shape[2] // LANES

    @pl.when(pl.program_id(1) == 0)
    def _():
        for gi, (o_ref, l_ref) in enumerate(((o1_ref, l1_ref), (o2_ref, l2_ref))):
            r = A_DILATED[gi + 1][1]
            n = tm // r
            for jr in range(r):
                rows = slice(jr * n, (jr + 1) * n)
                lun_ref[gi, pl.ds(jr, n, stride=r), :] = l_ref[0, rows, :]
                for c in range(n_slab):
                    oun_ref[gi, c, pl.ds(jr, n, stride=r), :] = (
                        o_ref[0, rows, c * LANES:(c + 1) * LANES].astype(F32))
        expand = e_ref[...]

        def widen(w):
            hi = w.astype(BF16)
            lo = (w - hi.astype(F32)).astype(BF16)
            return _dot(hi, expand) + _dot(lo, expand)

        for k in range(tm // MERGE_CHUNK):
            rs = slice(k * MERGE_CHUNK, (k + 1) * MERGE_CHUNK)
            l0, l1, l2 = l0_ref[0, rs, :], lun_ref[0, rs, :], lun_ref[1, rs, :]
            m = jnp.maximum(jnp.maximum(l0, l1), l2)
            e0, e1, e2 = jnp.exp(l0 - m), jnp.exp(l1 - m), jnp.exp(l2 - m)
            inv = 1.0 / (e0 + e1 + e2)
            o1 = jnp.concatenate([oun_ref[0, c, rs, :] for c in range(n_slab)], axis=1)
            o2 = jnp.concatenate([oun_ref[1, c, rs, :] for c in range(n_slab)], axis=1)
            o = (widen(e0 * inv) * o0_ref[0, rs, :].astype(F32)
                 + widen(e1 * inv) * o1 + widen(e2 * inv) * o2)
            om_ref[rs, :] = o.astype(BF16)

    out_ref[0] = x_ref[0] + g_ref[0] * _dot(om_ref[...], w_ref[...])


def _merge_out(outs, lses, x, gate, w_out, tn=512):
    b, s, d = x.shape
    tm = PERM_TILE
    spt = s // tm
    rmap = lambda i, j: (i // spt, i % spt, 0)
    cmap = lambda i, j: (i // spt, i % spt, j)
    expand = ((jnp.arange(d)[None, :] // HEAD_DIM == jnp.arange(LANES)[:, None])
              & (jnp.arange(LANES)[:, None] < N_HEADS)).astype(BF16)
    o_spec = pl.BlockSpec((1, tm, d), rmap)
    l_spec = pl.BlockSpec((1, tm, LANES), rmap)
    return pl.pallas_call(
        _merge_out_kernel,
        out_shape=jax.ShapeDtypeStruct((b, s, d), F32),
        grid=(b * spt, d // tn),
        in_specs=[o_spec, o_spec, o_spec, l_spec, l_spec, l_spec,
                  pl.BlockSpec((1, tm, tn), cmap),
                  pl.BlockSpec((1, 1, tn), lambda i, j: (i // spt, 0, j)),
                  pl.BlockSpec((d, tn), lambda i, j: (0, j)),
                  pl.BlockSpec((LANES, d), lambda i, j: (0, 0))],
        out_specs=pl.BlockSpec((1, tm, tn), cmap),
        scratch_shapes=[pltpu.VMEM((2, tm, LANES), F32),
                        pltpu.VMEM((2, d // LANES, tm, LANES), F32),
                        pltpu.VMEM((tm, d), BF16)],
        compiler_params=_params(("arbitrary", "arbitrary")),
        name="merge_out",
    )(*outs, *lses, x, gate, w_out, expand)


def _out_kernel(o_ref, x_ref, g_ref, w_ref, out_ref):
    out_ref[0] = x_ref[0] + g_ref[0] * _dot(o_ref[0], w_ref[...])


def _out_proj(o, x, gate, w_out, tm=512):
    b, s, d = x.shape
    spt = s // tm
    xmap = lambda i: (i // spt, i % spt, 0)
    return pl.pallas_call(
        _out_kernel,
        out_shape=jax.ShapeDtypeStruct((b, s, d), F32),
        grid=(b * spt,),
        in_specs=[pl.BlockSpec((1, tm, d), xmap),
                  pl.BlockSpec((1, tm, d), xmap),
                  pl.BlockSpec((1, 1, d), lambda i: (i // spt, 0, 0)),
                  pl.BlockSpec((d, d), lambda i: (0, 0))],
        out_specs=pl.BlockSpec((1, tm, d), xmap),
        compiler_params=_params(("arbitrary",)),
        name="out_proj",
    )(o, x, gate, w_out)


def _cum_kernel(f_ref, fb_ref, tri_ref, o_ref, carry_ref):
    @pl.when(pl.program_id(1) == 0)
    def _():
        carry_ref[...] = jnp.zeros_like(carry_ref)

    z = f_ref[0] + fb_ref[...]
    logf = jnp.minimum(z, 0.0) - jnp.log(1.0 + jnp.exp(-jnp.abs(z)))
    hi, mid, lo = _split3(logf)
    tri = tri_ref[...]
    cum = _dot(tri, hi) + _dot(tri, mid) + _dot(tri, lo) + carry_ref[...]
    o_ref[0] = cum
    carry_ref[...] = cum[-1:, :]


def _cum_forget(fg, f_bias, tc=512):
    b, s, w = fg.shape
    fb = jnp.pad(f_bias, (0, w - f_bias.shape[0])).reshape(1, w)
    tri = (jnp.arange(tc)[:, None] >= jnp.arange(tc)[None, :]).astype(BF16)
    return pl.pallas_call(
        _cum_kernel,
        out_shape=jax.ShapeDtypeStruct((b, s, w), F32),
        grid=(b, s // tc),
        in_specs=[pl.BlockSpec((1, tc, w), lambda i, j: (i, j, 0)),
                  pl.BlockSpec((1, w), lambda i, j: (0, 0)),
                  pl.BlockSpec((tc, tc), lambda i, j: (0, 0))],
        out_specs=pl.BlockSpec((1, tc, w), lambda i, j: (i, j, 0)),
        scratch_shapes=[pltpu.VMEM((1, w), F32)],
        compiler_params=_params(("arbitrary", "arbitrary")),
        name="cum_forget",
    )(fg, fb, tri)


def _gate_columns(cum_col, lead):
    rows = cum_col.shape[0]
    lane = lax.broadcasted_iota(jnp.int32, (rows, HEAD_DIM), 1)
    hi, mid, lo = (p.astype(F32) for p in _split3(cum_col if lead else -cum_col))
    off = 0 if lead else 3
    ext = jnp.where(lane < 6, 1.0, 0.0)
    ext = jnp.where(lane == off, hi, ext)
    ext = jnp.where(lane == off + 1, mid, ext)
    ext = jnp.where(lane == off + 2, lo, ext)
    return ext.astype(BF16)


def _pick_lane(tile, idx):
    lane = lax.broadcasted_iota(jnp.int32, tile.shape, 1)
    return jnp.sum(jnp.where(lane == idx, tile, 0.0), axis=-1, keepdims=True)


ONES_ROWS = 16
FOX_HEADS_PER_STEP = 8


def _fox_kernel(q_ref, k_ref, v_ref, cq_ref, ck_ref, gq_ref, gk_ref, o_ref, ka_ref, vt_ref,
                qat_ref, s_ref, p_ref, acc_ref, *, tq, tk):
    hg = pl.program_id(1)
    qi = pl.program_id(2)
    n_kt = vt_ref.shape[1]
    nh = ka_ref.shape[0]
    heads = range(nh)

    @pl.when(qi == 0)
    def _():
        gk = gk_ref[...]
        for hh in heads:
            sl = slice(hh * HEAD_DIM, (hh + 1) * HEAD_DIM)
            ka_ref[hh, :, :HEAD_DIM] = _headnorm(k_ref[0, :, sl], gk).astype(BF16)
            ck = _pick_lane(ck_ref[0], nh * hg + hh) * LOG2E
            ka_ref[hh, :, HEAD_DIM:] = _gate_columns(ck, lead=False)
        for pr in range(nh // 2):
            vt = v_ref[0, :, pr * PAIR:(pr + 1) * PAIR].astype(F32).T.astype(BF16)
            for hh in (2 * pr, 2 * pr + 1):
                rows = slice((hh % 2) * HEAD_DIM, (hh % 2 + 1) * HEAD_DIM)
                for jj in range(n_kt):
                    vt_ref[hh, jj, :HEAD_DIM] = vt[rows, jj * tk:(jj + 1) * tk]
                    vt_ref[hh, jj, HEAD_DIM:] = jnp.ones((ONES_ROWS, tk), BF16)

    gq = gq_ref[...] * (ATTN_SCALE * LOG2E)
    for hh in heads:
        sl = slice(hh * HEAD_DIM, (hh + 1) * HEAD_DIM)
        qn = _headnorm(q_ref[0, :, sl], gq)
        cq = _pick_lane(cq_ref[0], nh * hg + hh) * LOG2E
        qa = jnp.concatenate([qn, _gate_columns(cq, lead=True).astype(F32)], axis=1)
        qat_ref[hh] = qa.T.astype(BF16)
    key = lax.broadcasted_iota(jnp.int32, (tk, tq), 0)
    qry = lax.broadcasted_iota(jnp.int32, (tk, tq), 1)
    ahead = key - qry

    def scores(t, hh, diagonal):
        start = pl.multiple_of(t * tk, tk)
        st = _dot(ka_ref[hh, pl.ds(start, tk), :], qat_ref[hh])
        if diagonal is None:
            st = jnp.where(ahead > jnp.where(t == qi, 0, tk), NEG_INF, st)
        elif diagonal:
            st = jnp.where(ahead > 0, NEG_INF, st)
        s_ref[hh] = st
        return jnp.max(st, axis=0, keepdims=True)

    def pv(t, hh, alpha_prev):
        acc_ref[hh] = alpha_prev * acc_ref[hh] + _dot(vt_ref[hh, t], p_ref[hh])

    def trip(i, carry, next_diagonal):
        out = []
        for hh in heads:
            m, alpha_prev, mx = carry[hh]
            pv(jnp.maximum(i - 1, 0), hh, alpha_prev)
            m_new = jnp.maximum(m, mx)
            alpha = jnp.exp2(m - m_new)
            p_ref[hh] = jnp.exp2(s_ref[hh] - m_new).astype(BF16)
            if next_diagonal is not None:
                mx = scores(i + 1, hh, next_diagonal)
            out.append((m_new, alpha, mx))
        return tuple(out)

    p_ref[...] = jnp.zeros_like(p_ref)
    acc_ref[...] = jnp.zeros_like(acc_ref)
    carry = tuple((jnp.full((1, tq), NEG_INF, F32), jnp.ones((1, tq), F32),
                   scores(0, hh, None)) for hh in heads)
    carry = lax.fori_loop(0, jnp.maximum(qi - 1, 0),
                          functools.partial(trip, next_diagonal=False), carry)
    carry = lax.fori_loop(jnp.maximum(qi - 1, 0), qi,
                          functools.partial(trip, next_diagonal=True), carry)
    carry = trip(qi, carry, None)
    for pr in range(nh // 2):
        o_t = []
        for hh in (2 * pr, 2 * pr + 1):
            pv(qi, hh, carry[hh][1])
            acc = acc_ref[hh]
            o_t.append(acc[:HEAD_DIM] / acc[HEAD_DIM:HEAD_DIM + 1])
        o_ref[0, :, pr * PAIR:(pr + 1) * PAIR] = (
            jnp.concatenate(o_t, axis=0).T.astype(BF16))


def _fox_attention(qkv, cum, gq, gk, tq=256):
    b, s, _ = qkv.shape
    tk = tq
    nq = s // tq
    nh = FOX_HEADS_PER_STEP
    groups = N_HEADS // nh
    gw = nh * HEAD_DIM
    vrows = HEAD_DIM + ONES_ROWS
    return pl.pallas_call(
        functools.partial(_fox_kernel, tq=tq, tk=tk),
        out_shape=jax.ShapeDtypeStruct((b, s, N_HEADS * HEAD_DIM), BF16),
        grid=(b, groups, nq),
        in_specs=[
            pl.BlockSpec((1, tq, gw), lambda bi, hg, qi: (bi, qi, hg)),
            pl.BlockSpec((1, s, gw), lambda bi, hg, qi: (bi, 0, groups + hg)),
            pl.BlockSpec((1, s, gw), lambda bi, hg, qi: (bi, 0, 2 * groups + hg)),
            pl.BlockSpec((1, tq, LANES), lambda bi, hg, qi: (bi, qi, 0)),
            pl.BlockSpec((1, s, LANES), lambda bi, hg, qi: (bi, 0, 0)),
            pl.BlockSpec((1, HEAD_DIM), lambda bi, hg, qi: (0, 0)),
            pl.BlockSpec((1, HEAD_DIM), lambda bi, hg, qi: (0, 0)),
        ],
        out_specs=pl.BlockSpec((1, tq, gw), lambda bi, hg, qi: (bi, qi, hg)),
        scratch_shapes=[pltpu.VMEM((nh, s, 2 * HEAD_DIM), BF16),
                        pltpu.VMEM((nh, s // tk, vrows, tk), BF16),
                        pltpu.VMEM((nh, 2 * HEAD_DIM, tq), BF16),
                        pltpu.VMEM((nh, tk, tq), F32),
                        pltpu.VMEM((nh, tk, tq), BF16),
                        pltpu.VMEM((nh, vrows, tq), F32)],
        compiler_params=_params(("arbitrary", "arbitrary", "arbitrary")),
        name="fox_attn",
    )(qkv, qkv, qkv, cum, cum, gq.reshape(1, HEAD_DIM), gk.reshape(1, HEAD_DIM))


ROUTER_LANES = LANES


def _route(logits):
    lane = lax.broadcasted_iota(jnp.int32, logits.shape, 1)
    lane_f = lane.astype(F32)
    big = float(ROUTER_LANES)
    is_g = lane < N_GROUPS
    gl = jnp.where(is_g, logits, -jnp.inf)
    gmax = jnp.max(gl, axis=-1, keepdims=True)
    gsum = jnp.sum(jnp.where(is_g, jnp.exp(logits - gmax), 0.0), axis=-1, keepdims=True)
    g_w = 1.0 / gsum
    gidx = jnp.min(jnp.where(gl == gmax, lane_f, big), axis=-1, keepdims=True)
    lo = N_GROUPS + EXPERTS_PER_GROUP * gidx
    in_grp = (lane_f >= lo) & (lane_f < lo + EXPERTS_PER_GROUP)
    el = jnp.where(in_grp, logits, -jnp.inf)
    t1 = jnp.max(el, axis=-1, keepdims=True)
    i1 = jnp.min(jnp.where(el == t1, lane_f, big), axis=-1, keepdims=True)
    el2 = jnp.where(lane_f == i1, -jnp.inf, el)
    t2 = jnp.max(el2, axis=-1, keepdims=True)
    i2 = jnp.min(jnp.where(el2 == t2, lane_f, big), axis=-1, keepdims=True)
    e2 = jnp.exp(t2 - t1)
    w1 = g_w / (1.0 + e2)
    w2 = w1 * e2
    gates = jnp.where(lane_f == i1, w1, jnp.where(lane_f == i2, w2, 0.0))
    return jnp.where(lane == 0, gidx, gates)


MOE_TILE = 512
HALF_D = D_MODEL // 2
ROW_WORDS = HALF_D + ROUTER_LANES
U32 = jnp.uint32


def _pack_pairs(x):
    n = x.shape[1] // 2
    hi = pltpu.bitcast(x[:, :n].astype(BF16).astype(F32), U32)
    lo = pltpu.bitcast(x[:, n:].astype(BF16).astype(F32), U32)
    return hi | (lo >> 16)


def _unpack_pairs(w):
    hi = pltpu.bitcast(w & U32(0xFFFF0000), F32)
    lo = pltpu.bitcast(w << 16, F32)
    return jnp.concatenate([hi, lo], axis=1)


SUBLANES = 8


def _moe_route_kernel(x_ref, gam_ref, sc_ref, sh_ref, wr_ref, br_ref, tri_ref, o_ref, meta_ref,
                      cnt_ref):
    h = _modnorm(x_ref[0], gam_ref[...], sc_ref[0], sh_ref[0])
    h_hi = h.astype(BF16)
    h_lo = (h - h_hi.astype(F32)).astype(BF16)
    wr = wr_ref[...]
    w_hi = wr.astype(BF16)
    w_lo = (wr - w_hi.astype(F32)).astype(BF16)
    logits = _dot(h_hi, w_hi) + _dot(h_hi, w_lo) + _dot(h_lo, w_hi) + br_ref[...]
    route = _route(logits)
    tm = route.shape[0]

    @pl.when(pl.program_id(0) == 0)
    def _():
        cnt_ref[...] = jnp.zeros_like(cnt_ref)

    lane = lax.broadcasted_iota(jnp.int32, route.shape, 1)
    member = jnp.where((lane.astype(F32) == route[:, 0:1]) & (lane < N_GROUPS), 1.0, 0.0)
    incl = _dot(tri_ref[...], member.astype(BF16))
    rank = jnp.sum(member * (incl - member + cnt_ref[...]), axis=-1, keepdims=True)
    cnt_ref[...] = cnt_ref[...] + incl[tm - 1:tm, :]
    meta = jnp.where(lane == 0, route[:, 0:1], jnp.where(lane == 1, rank, 0.0))
    meta_ref[...] = meta.T[:SUBLANES, :]

    packed = _pack_pairs(h)
    for c in range(HALF_D // LANES):
        o_ref[pl.ds(c, tm, stride=SUBLANES), :] = packed[:, c * LANES:(c + 1) * LANES]
    o_ref[pl.ds(HALF_D // LANES, tm, stride=SUBLANES), :] = pltpu.bitcast(route, U32)
    for c in range(HALF_D // LANES + 1, SUBLANES):
        o_ref[pl.ds(c, tm, stride=SUBLANES), :] = jnp.zeros((tm, LANES), U32)


def _token_copy(src, src_tok, dst, dst_tok, sem):
    s0 = pl.multiple_of(src_tok * SUBLANES, SUBLANES)
    d0 = pl.multiple_of(dst_tok * SUBLANES, SUBLANES)
    return pltpu.make_async_copy(src.at[pl.ds(s0, SUBLANES), :], dst.at[pl.ds(d0, SUBLANES), :], sem)


def _moe_expert_kernel(tg_ref, nt_ref, nv_ref, dest_ref, hx_hbm, wg_ref, wu_ref, wd_ref, e_ref,
                       y_hbm, buf, ybuf, src_ref, gsem, ssem):
    t = pl.program_id(0)
    n_used = nt_ref[0]
    slot = t % 2
    rows = buf.shape[1] // SUBLANES
    unroll = 8

    @pl.when(t == 0)
    def _():
        def clear(p, c):
            src_ref[p] = 0
            return c
        lax.fori_loop(0, src_ref.shape[0], clear, 0, unroll=unroll)

        def invert(tok, c):
            src_ref[dest_ref[tok]] = tok
            return c
        lax.fori_loop(0, dest_ref.shape[0], invert, 0, unroll=unroll)

    def gather_start(tile, sl):
        def body(r, c):
            _token_copy(hx_hbm, src_ref[tile * rows + r], buf.at[sl], r, gsem.at[sl]).start()
            return c
        lax.fori_loop(0, rows, body, 0, unroll=unroll)

    def gather_wait(sl):
        def body(r, c):
            _token_copy(hx_hbm, 0, buf.at[sl], r, gsem.at[sl]).wait()
            return c
        lax.fori_loop(0, rows, body, 0, unroll=unroll)

    def valid_rows_loop(tile, body):
        n_valid = nv_ref[tile]
        chunks = n_valid // unroll

        def chunk(k, c):
            for i in range(unroll):
                body(k * unroll + i)
            return c
        lax.fori_loop(0, chunks, chunk, 0)

        def single(r, c):
            body(r)
            return c
        lax.fori_loop(chunks * unroll, n_valid, single, 0)

    def scatter_start(tile, sl):
        valid_rows_loop(tile, lambda r: _token_copy(
            ybuf.at[sl], r, y_hbm, src_ref[tile * rows + r], ssem.at[sl]).start())

    def scatter_wait(tile, sl):
        valid_rows_loop(tile, lambda r: _token_copy(ybuf.at[sl], r, y_hbm, 0, ssem.at[sl]).wait())

    @pl.when(t == 0)
    def _():
        gather_start(0, 0)

    @pl.when(t + 1 < n_used)
    def _():
        gather_start(t + 1, 1 - slot)

    @pl.when(t < n_used)
    def _():
        gather_wait(slot)
        slab = lambda c: buf[slot, pl.ds(c, rows, stride=SUBLANES), :]
        n_pair_slabs = HALF_D // LANES
        w = jnp.concatenate([slab(c) for c in range(n_pair_slabs)], axis=1)
        h = _unpack_pairs(w).astype(BF16)
        gates = pltpu.bitcast(slab(n_pair_slabs), F32)
        a = _dot(h, wg_ref[0])
        u = _dot(h, wu_ref[0])
        g_hi = gates.astype(BF16)
        g_lo = (gates - g_hi.astype(F32)).astype(BF16)
        gexp = _dot(g_hi, e_ref[0]) + _dot(g_lo, e_ref[0])
        hid = a * (1.0 / (1.0 + jnp.exp(-a))) * u * gexp
        y = _dot(hid.astype(BF16), wd_ref[0])

        @pl.when(t >= 2)
        def _():
            scatter_wait(t - 2, slot)

        for c in range(y.shape[1] // LANES):
            ybuf[slot, pl.ds(c, rows, stride=SUBLANES), :] = y[:, c * LANES:(c + 1) * LANES]
        scatter_start(t, slot)

    @pl.when(t == pl.num_programs(0) - 1)
    def _():
        for back in (2, 1):
            scatter_wait(n_used - back, (n_used - back) % 2)


def _moe_combine_kernel(x_ref, g_ref, y_ref, o_ref):
    tm = x_ref.shape[1]
    y = jnp.concatenate([y_ref[pl.ds(c, tm, stride=SUBLANES), :]
                         for c in range(x_ref.shape[2] // LANES)], axis=1)
    o_ref[0] = x_ref[0] + g_ref[0] * y


def _moe(x, gamma, sc, sh, gate, w_router, b_router, w_gate, w_up, w_down):
    b, s, d = x.shape
    n = b * s
    tm = PERM_TILE
    spt = s // tm
    rows = MOE_TILE
    n_tiles = n // rows + N_GROUPS
    p_rows = n_tiles * rows
    bmap = lambda i: (i // spt, 0, 0)
    tri = (jnp.arange(tm)[:, None] >= jnp.arange(tm)[None, :]).astype(BF16)
    hx, meta = pl.pallas_call(
        _moe_route_kernel,
        out_shape=(jax.ShapeDtypeStruct((n * SUBLANES, LANES), U32),
                   jax.ShapeDtypeStruct((SUBLANES, n), F32)),
        grid=(b * spt,),
        in_specs=[
            pl.BlockSpec((1, tm, d), lambda i: (i // spt, i % spt, 0)),
            pl.BlockSpec((1, d), lambda i: (0, 0)),
            pl.BlockSpec((1, 1, d), bmap),
            pl.BlockSpec((1, 1, d), bmap),
            pl.BlockSpec((d, ROUTER_LANES), lambda i: (0, 0)),
            pl.BlockSpec((1, ROUTER_LANES), lambda i: (0, 0)),
            pl.BlockSpec((tm, tm), lambda i: (0, 0)),
        ],
        out_specs=(pl.BlockSpec((tm * SUBLANES, LANES), lambda i: (i, 0)),
                   pl.BlockSpec((SUBLANES, tm), lambda i: (0, i))),
        scratch_shapes=[pltpu.VMEM((1, LANES), F32)],
        compiler_params=_params(("arbitrary",)),
        name="moe_route",
    )(x, gamma.reshape(1, d), sc, sh, w_router, b_router, tri)

    gidx = meta[0].astype(jnp.int32)
    rank = meta[1].astype(jnp.int32)
    onehot = (gidx[:, None] == jnp.arange(N_GROUPS)[None, :]).astype(jnp.int32)
    count = jnp.sum(onehot, axis=0)
    padded = (count + rows - 1) // rows * rows
    ends = jnp.cumsum(padded)
    dest = jnp.sum(onehot * (ends - padded)[None, :], axis=1) + rank
    n_used = (ends[-1] // rows).reshape(1).astype(jnp.int32)
    tile_start = jnp.arange(n_tiles, dtype=jnp.int32) * rows
    tile_group = jnp.minimum(jnp.sum(ends[None, :] <= tile_start[:, None], axis=1),
                             N_GROUPS - 1).astype(jnp.int32)
    seg_valid_end = ends - padded + count
    n_valid = jnp.clip(jnp.sum(jnp.where(jnp.arange(N_GROUPS)[None, :] == tile_group[:, None],
                                         seg_valid_end[None, :], 0), axis=1) - tile_start,
                       0, rows).astype(jnp.int32)

    ff = w_gate.shape[2]
    lanes = jnp.arange(ROUTER_LANES)[None, :, None]
    expert = N_GROUPS + EXPERTS_PER_GROUP * jnp.arange(N_GROUPS)[:, None, None] \
        + jnp.arange(ff)[None, None, :] // EXPERT_FF
    expand = (lanes == expert).astype(BF16)
    wmap = lambda t, tg, nt, nv, ds: (tg[t], 0, 0)
    y_tok = pl.pallas_call(
        _moe_expert_kernel,
        out_shape=jax.ShapeDtypeStruct((n * SUBLANES, LANES), F32),
        grid_spec=pltpu.PrefetchScalarGridSpec(
            num_scalar_prefetch=4,
            grid=(n_tiles,),
            in_specs=[
                pl.BlockSpec(memory_space=pl.ANY),
                pl.BlockSpec((1, d, ff), wmap),
                pl.BlockSpec((1, d, ff), wmap),
                pl.BlockSpec((1, ff, d), wmap),
                pl.BlockSpec((1, ROUTER_LANES, ff), wmap),
            ],
            out_specs=pl.BlockSpec(memory_space=pl.ANY),
            scratch_shapes=[pltpu.VMEM((2, rows * SUBLANES, LANES), U32),
                            pltpu.VMEM((2, rows * SUBLANES, LANES), F32),
                            pltpu.SMEM((p_rows,), jnp.int32),
                            pltpu.SemaphoreType.DMA((2,)),
                            pltpu.SemaphoreType.DMA((2,))],
        ),
        compiler_params=_params(("arbitrary",)),
        name="moe_experts",
    )(tile_group, n_used, n_valid, dest, hx, w_gate, w_up, w_down, expand)

    return pl.pallas_call(
        _moe_combine_kernel,
        out_shape=jax.ShapeDtypeStruct((b, s, d), F32),
        grid=(b * spt,),
        in_specs=[pl.BlockSpec((1, tm, d), lambda i: (i // spt, i % spt, 0)),
                  pl.BlockSpec((1, 1, d), bmap),
                  pl.BlockSpec((tm * SUBLANES, LANES), lambda i: (i, 0))],
        out_specs=pl.BlockSpec((1, tm, d), lambda i: (i // spt, i % spt, 0)),
        compiler_params=_params(("arbitrary",)),
        name="moe_combine",
    )(x, gate, y_tok)


def kernel(x, c, w_ada, b_ada, norm_mix, norm_ffn, rel_bias, a_w_in, a_w_out, a_q_norm,
           a_k_norm, b_w_in, b_f_bias, b_w_out, b_q_norm, b_k_norm, router_group_w,
           router_group_b, router_expert_w, router_expert_b, w_gate, w_up, w_down):
    b, s, d = x.shape
    depth = w_ada.shape[0]
    mod = _ada(c, w_ada, b_ada).reshape(depth, b, 6, 1, d)
    pad_r = ROUTER_LANES - N_GROUPS - N_EXPERTS
    for i in range(depth):
        sh_m, sc_m, g_m, sh_f, sc_f, g_f = (mod[i, :, k] for k in range(6))
        j = i // 2
        if i % 2 == 0:
            proj = _normproj_a(x, norm_mix[i], sc_m, sh_m, a_w_in[j].astype(BF16))
            outs, lses = [], []
            for g, (window, dil) in enumerate(A_DILATED):
                tab = rel_bias[:, g * N_HEADS:(g + 1) * N_HEADS]
                o, l = _attn_a_group(proj, g, tab, a_q_norm[j], a_k_norm[j], window, dil)
                outs.append(o)
                lses.append(l)
            x = _merge_out(outs, lses, x, g_m, a_w_out[j].astype(BF16))
        else:
            n_qkv = 3 * N_HEADS * HEAD_DIM
            w_in = b_w_in[j]
            w_f = jnp.pad(w_in[:, n_qkv:], ((0, 0), (0, LANES - N_HEADS))).astype(BF16)
            qkv, fg = _normproj(x, norm_mix[i], sc_m, sh_m, w_in[:, :n_qkv].astype(BF16), w_f)
            cum = _cum_forget(fg, b_f_bias[j])
            o = _fox_attention(qkv, cum, b_q_norm[j], b_k_norm[j])
            x = _out_proj(o, x, g_m, b_w_out[j].astype(BF16))
        w_router = jnp.pad(jnp.concatenate([router_group_w[i], router_expert_w[i]], axis=1),
                           ((0, 0), (0, pad_r)))
        b_router = jnp.pad(jnp.concatenate([router_group_b[i], router_expert_b[i]]),
                           (0, pad_r)).reshape(1, ROUTER_LANES)
        side_by_side = lambda w: (w.reshape(N_GROUPS, EXPERTS_PER_GROUP, d, EXPERT_FF)
                                  .transpose(0, 2, 1, 3)
                                  .reshape(N_GROUPS, d, EXPERTS_PER_GROUP * EXPERT_FF).astype(BF16))
        x = _moe(x, norm_ffn[i], sc_f, sh_f, g_f, w_router, b_router,
                 side_by_side(w_gate[i]), side_by_side(w_up[i]),
                 w_down[i].reshape(N_GROUPS, EXPERTS_PER_GROUP * EXPERT_FF, d).astype(BF16))
    return x
```

```python
import functools
import math

import jax
import jax.numpy as jnp
from jax import lax
from jax.experimental import pallas as pl
from jax.experimental.pallas import tpu as pltpu

F32 = jnp.float32
BF16 = jnp.bfloat16

D_MODEL = 1024
HEAD_DIM = 64
N_HEADS = 16
ATTN_SCALE = HEAD_DIM ** -0.5
A_DILATED = ((128, 1), (512, 4), (2048, 16))
Q_BLOCK = 128
REL_BUCKETS = 32
REL_MAX_DIST = 2048
N_GROUPS = 4
EXPERTS_PER_GROUP = 4
N_EXPERTS = 16
EXPERT_FF = 256
EPS = 1e-6
NEG_INF = -1e30

LANES = 128
VMEM_LIMIT = 48 * 1024 * 1024
LOG2E = 1.4426950408889634
LN2 = 0.6931471805599453
PERM_TILE = 1024
PAIR = 2 * HEAD_DIM
N_PAIRS = N_HEADS // 2


def _params(sem):
    return pltpu.CompilerParams(dimension_semantics=sem, vmem_limit_bytes=VMEM_LIMIT)


def _split3(x):
    hi = x.astype(BF16)
    r1 = x - hi.astype(F32)
    mid = r1.astype(BF16)
    lo = (r1 - mid.astype(F32)).astype(BF16)
    return hi, mid, lo


def _dot(a, b):
    return jnp.dot(a, b, preferred_element_type=F32)


def _dot_nt(a, b):
    return lax.dot_general(a, b, (((1,), (1,)), ((), ())), preferred_element_type=F32)


def _ada_kernel(c_ref, w_ref, b_ref, o_ref):
    c = c_ref[...]
    a = c * (1.0 / (1.0 + jnp.exp(-c)))
    o_ref[0] = _dot(a.astype(BF16), w_ref[0].astype(BF16)) + b_ref[0]


def _ada(c, w_ada, b_ada):
    depth, d, n6 = w_ada.shape
    b = c.shape[0]
    rows = 8
    c_pad = jnp.pad(c, ((0, rows - b), (0, 0)))
    tn = 1536
    out = pl.pallas_call(
        _ada_kernel,
        out_shape=jax.ShapeDtypeStruct((depth, rows, n6), F32),
        grid=(depth, n6 // tn),
        in_specs=[
            pl.BlockSpec((rows, d), lambda i, j: (0, 0)),
            pl.BlockSpec((1, d, tn), lambda i, j: (i, 0, j)),
            pl.BlockSpec((1, 1, tn), lambda i, j: (i, 0, j)),
        ],
        out_specs=pl.BlockSpec((1, rows, tn), lambda i, j: (i, 0, j)),
        compiler_params=_params(("arbitrary", "arbitrary")),
        name="ada_mod",
    )(c_pad, w_ada, b_ada.reshape(depth, 1, n6))
    return out[:, :b]


def _modnorm(x, gamma, sc, sh):
    ms = jnp.mean(x * x, axis=-1, keepdims=True)
    y = x * lax.rsqrt(ms + EPS) * gamma
    return y * (1.0 + sc) + sh


def _normproj_kernel(x_ref, g_ref, sc_ref, sh_ref, w_ref, o_ref, h_ref):
    @pl.when(pl.program_id(1) == 0)
    def _():
        h_ref[...] = _modnorm(x_ref[0], g_ref[...], sc_ref[0], sh_ref[0]).astype(BF16)

    o_ref[0] = _dot(h_ref[...], w_ref[...]).astype(BF16)


def _normproj_gate_kernel(x_ref, g_ref, sc_ref, sh_ref, w_ref, wf_ref, o_ref, f_ref, h_ref):
    @pl.when(pl.program_id(1) == 0)
    def _():
        h = _modnorm(x_ref[0], g_ref[...], sc_ref[0], sh_ref[0]).astype(BF16)
        h_ref[...] = h
        f_ref[0] = _dot(h, wf_ref[...])

    o_ref[0] = _dot(h_ref[...], w_ref[...]).astype(BF16)


def _normproj(x, gamma, sc, sh, w, w_f=None, tm=1024, tn=1024):
    b, s, d = x.shape
    ncols = w.shape[1]
    spt = s // tm
    grid = (b * spt, ncols // tn)
    xmap = lambda i, j: (i // spt, i % spt, 0)
    bmap = lambda i, j: (i // spt, 0, 0)
    in_specs = [
        pl.BlockSpec((1, tm, d), xmap),
        pl.BlockSpec((1, d), lambda i, j: (0, 0)),
        pl.BlockSpec((1, 1, d), bmap),
        pl.BlockSpec((1, 1, d), bmap),
        pl.BlockSpec((d, tn), lambda i, j: (0, j)),
    ]
    o_spec = pl.BlockSpec((1, tm, tn), lambda i, j: (i // spt, i % spt, j))
    scratch = [pltpu.VMEM((tm, d), BF16)]
    if w_f is None:
        return pl.pallas_call(
            _normproj_kernel,
            out_shape=jax.ShapeDtypeStruct((b, s, ncols), BF16),
            grid=grid, in_specs=in_specs, out_specs=o_spec, scratch_shapes=scratch,
            compiler_params=_params(("arbitrary", "arbitrary")),
            name="normproj",
        )(x, gamma.reshape(1, d), sc, sh, w)
    in_specs.append(pl.BlockSpec((d, LANES), lambda i, j: (0, 0)))
    f_spec = pl.BlockSpec((1, tm, LANES), xmap)
    return pl.pallas_call(
        _normproj_gate_kernel,
        out_shape=(jax.ShapeDtypeStruct((b, s, ncols), BF16),
                   jax.ShapeDtypeStruct((b, s, LANES), F32)),
        grid=grid, in_specs=in_specs, out_specs=(o_spec, f_spec), scratch_shapes=scratch,
        compiler_params=_params(("arbitrary", "arbitrary")),
        name="normproj_gate",
    )(x, gamma.reshape(1, d), sc, sh, w, w_f)


def _normproj_a_kernel(x_ref, g_ref, sc_ref, sh_ref, w_ref, o_ref, hf_ref, hp_ref):
    tm = x_ref.shape[1]
    n_slab = x_ref.shape[2] // LANES

    @pl.when(pl.program_id(1) == 0)
    def _():
        h = _modnorm(x_ref[0], g_ref[...], sc_ref[0], sh_ref[0])
        hp_ref[0] = h.astype(BF16)
        for c in range(n_slab):
            hf_ref[c] = h[:, c * LANES:(c + 1) * LANES]
        for g, (_, r) in enumerate(A_DILATED):
            if r == 1:
                continue
            n = tm // r
            for jr in range(r):
                for c in range(n_slab):
                    hp_ref[g, jr * n:(jr + 1) * n, c * LANES:(c + 1) * LANES] = (
                        hf_ref[c, pl.ds(jr, n, stride=r), :].astype(BF16))

    o_ref[0] = _dot(hp_ref[pl.program_id(1) // 3], w_ref[...]).astype(BF16)


def _normproj_a(x, gamma, sc, sh, w, tn=1024):
    b, s, d = x.shape
    tm = PERM_TILE
    ncols = w.shape[1]
    spt = s // tm
    bmap = lambda i, j: (i // spt, 0, 0)
    return pl.pallas_call(
        _normproj_a_kernel,
        out_shape=jax.ShapeDtypeStruct((b, s, ncols), BF16),
        grid=(b * spt, ncols // tn),
        in_specs=[
            pl.BlockSpec((1, tm, d), lambda i, j: (i // spt, i % spt, 0)),
            pl.BlockSpec((1, d), lambda i, j: (0, 0)),
            pl.BlockSpec((1, 1, d), bmap),
            pl.BlockSpec((1, 1, d), bmap),
            pl.BlockSpec((d, tn), lambda i, j: (0, j)),
        ],
        out_specs=pl.BlockSpec((1, tm, tn), lambda i, j: (i // spt, i % spt, j)),
        scratch_shapes=[pltpu.VMEM((d // LANES, tm, LANES), F32),
                        pltpu.VMEM((len(A_DILATED), tm, d), BF16)],
        compiler_params=_params(("arbitrary", "arbitrary")),
        name="normproj_a",
    )(x, gamma.reshape(1, d), sc, sh, w)


def _headnorm(t, gain):
    t = t.astype(F32)
    ms = jnp.mean(t * t, axis=-1, keepdims=True)
    return t * lax.rsqrt(ms + EPS) * gain


def _pairnorm(t, gain2, even):
    t = t.astype(F32)
    sq = t * t
    tot = jnp.sum(sq, axis=-1, keepdims=True)
    s_even = jnp.sum(jnp.where(even, sq, 0.0), axis=-1, keepdims=True)
    inv = jnp.where(even, lax.rsqrt(s_even * (1.0 / HEAD_DIM) + EPS),
                    lax.rsqrt((tot - s_even) * (1.0 / HEAD_DIM) + EPS))
    return t * inv * gain2


def _rows(ref, cols=slice(None)):
    if len(ref.shape) == 2:
        return ref[:, cols]
    return jnp.concatenate([ref[i, :, cols] for i in range(ref.shape[0])], axis=0)


def _store_rows(ref, cols, val):
    if len(ref.shape) == 2:
        ref[:, cols] = val
    else:
        n = ref.shape[1]
        for i in range(ref.shape[0]):
            ref[i, :, cols] = val[i * n:(i + 1) * n]


def _attn_a_kernel(q_ref, k_ref, v_ref, bias_ref, gq_ref, gk_ref, o_ref, lse_ref,
                   kn_ref, vv_ref):
    ub = pl.program_id(2)
    slot = ub % 2
    prev = 1 - slot

    @pl.when(ub == 0)
    def _():
        kn_ref[1] = jnp.zeros(kn_ref.shape[1:], BF16)
        vv_ref[1] = jnp.zeros(vv_ref.shape[1:], BF16)

    first = jnp.minimum(ub, 1)
    lane = lax.broadcasted_iota(jnp.int32, (Q_BLOCK, PAIR), 1)
    even = lane < HEAD_DIM
    lane_row = lax.broadcasted_iota(jnp.int32, (1, PAIR), 1)
    keep_even = jnp.where(lane_row < HEAD_DIM, 1.0, 0.0).astype(BF16)
    keep_odd = jnp.where(lane_row < HEAD_DIM, 0.0, 1.0).astype(BF16)
    gq2 = gq_ref[...] * (ATTN_SCALE * LOG2E)
    gk2 = gk_ref[...]
    for hp in range(N_PAIRS):
        cs = slice(hp * PAIR, (hp + 1) * PAIR)
        kn_ref[slot, :, cs] = _pairnorm(_rows(k_ref, cs), gk2, even).astype(BF16)
    vv_ref[slot] = _rows(v_ref)
    lse_tile = jnp.zeros((Q_BLOCK, LANES), F32)
    for hp in range(N_PAIRS):
        cs = slice(hp * PAIR, (hp + 1) * PAIR)
        qn = _pairnorm(_rows(q_ref, cs), gq2, even)
        qq = jnp.concatenate([jnp.where(even, qn, 0.0), jnp.where(even, 0.0, qn)],
                             axis=0).astype(BF16)
        kcat = jnp.concatenate([kn_ref[prev, :, cs], kn_ref[slot, :, cs]], axis=0)
        s = _dot_nt(qq, kcat) + bias_ref[first, hp]
        m = jnp.max(s, axis=-1, keepdims=True)
        p = jnp.exp2(s - m)
        l = jnp.sum(p, axis=-1, keepdims=True)
        pb = p.astype(BF16)
        vcat = jnp.concatenate([vv_ref[prev, :, cs], vv_ref[slot, :, cs]], axis=0)
        o = (_dot(pb[:Q_BLOCK], vcat * keep_even)
             + _dot(pb[Q_BLOCK:], vcat * keep_odd))
        inv = 1.0 / l
        o = o * jnp.where(even, inv[:Q_BLOCK], inv[Q_BLOCK:])
        _store_rows(o_ref, cs, o.astype(BF16))
        lse = m * LN2 + jnp.log(l)
        lse_tile = jnp.where(lane == 2 * hp, lse[:Q_BLOCK],
                             jnp.where(lane == 2 * hp + 1, lse[Q_BLOCK:], lse_tile))
    _store_rows(lse_ref, slice(None), lse_tile)


def _t5_bucket(dist):
    max_exact = REL_BUCKETS // 2
    d = jnp.maximum(dist, 0)
    large = max_exact + (jnp.log(jnp.maximum(d, 1).astype(F32) / max_exact)
                         / math.log(REL_MAX_DIST / max_exact)
                         * (REL_BUCKETS - max_exact)).astype(jnp.int32)
    large = jnp.minimum(large, REL_BUCKETS - 1)
    return jnp.where(d < max_exact, d, large)


def _bias_kernel(tab_ref, onehot_ref, o_ref):
    hi, mid, lo = _split3(tab_ref[...])
    oh = onehot_ref[...]
    o_ref[...] = (_dot(hi, oh) + _dot(mid, oh) + _dot(lo, oh)) * LOG2E


def _bias_tables(rel_bias_g, window, dilation):
    qi = jnp.arange(Q_BLOCK)[:, None]
    kj = jnp.arange(2 * Q_BLOCK)[None, :]
    dist = qi + Q_BLOCK - kj
    bucket = _t5_bucket(dist * dilation).reshape(1, -1)
    onehot = (bucket == jnp.arange(REL_BUCKETS)[:, None]).astype(BF16)
    npos = onehot.shape[1]
    chunk = 4096
    bias = pl.pallas_call(
        _bias_kernel,
        out_shape=jax.ShapeDtypeStruct((N_HEADS, npos), F32),
        grid=(npos // chunk,),
        in_specs=[pl.BlockSpec((N_HEADS, REL_BUCKETS), lambda i: (0, 0)),
                  pl.BlockSpec((REL_BUCKETS, chunk), lambda i: (0, i))],
        out_specs=pl.BlockSpec((N_HEADS, chunk), lambda i: (0, i)),
        compiler_params=_params(("arbitrary",)),
        name="rel_bias_table",
    )(rel_bias_g.T.astype(F32), onehot)
    bias = bias.reshape(N_PAIRS, 2 * Q_BLOCK, 2 * Q_BLOCK)
    band = (dist >= 0) & (dist <= window // dilation)
    first = band & (qi - dist >= 0)
    band = jnp.concatenate([band, band], axis=0)[None]
    first = jnp.concatenate([first, first], axis=0)[None]
    return jnp.stack([jnp.where(first, bias, NEG_INF), jnp.where(band, bias, NEG_INF)], axis=0)


def _attn_a_group(proj, g, rel_bias_g, gq, gk, window, dilation):
    b, s, cols = proj.shape
    r = dilation
    nt = s // PERM_TILE
    n_t = PERM_TILE // r
    nb = s // r // Q_BLOCK
    gw = N_HEADS * HEAD_DIM
    tables = _bias_tables(rel_bias_g, window, dilation)
    base = g * 3
    if n_t >= Q_BLOCK:
        bpt = n_t // Q_BLOCK
        blk = lambda w: (None, None, None, Q_BLOCK, w)
        imap = lambda col: (lambda bi, jr, ub: (bi, ub // bpt, jr, ub % bpt, col))
    else:
        blk = lambda w: (None, Q_BLOCK // n_t, None, n_t, w)
        imap = lambda col: (lambda bi, jr, ub: (bi, ub, jr, 0, col))
    pv = proj.reshape(b, nt, r, n_t, cols)
    gains = lambda t: jnp.concatenate([t, t]).reshape(1, PAIR)
    o, lse = pl.pallas_call(
        _attn_a_kernel,
        out_shape=(jax.ShapeDtypeStruct((b, nt, r, n_t, gw), BF16),
                   jax.ShapeDtypeStruct((b, nt, r, n_t, LANES), F32)),
        grid=(b, r, nb),
        in_specs=[
            pl.BlockSpec(blk(gw), imap(base)),
            pl.BlockSpec(blk(gw), imap(base + 1)),
            pl.BlockSpec(blk(gw), imap(base + 2)),
            pl.BlockSpec(tables.shape, lambda bi, jr, ub: (0, 0, 0, 0)),
            pl.BlockSpec((1, PAIR), lambda bi, jr, ub: (0, 0)),
            pl.BlockSpec((1, PAIR), lambda bi, jr, ub: (0, 0)),
        ],
        out_specs=(pl.BlockSpec(blk(gw), imap(0)), pl.BlockSpec(blk(LANES), imap(0))),
        scratch_shapes=[pltpu.VMEM((2, Q_BLOCK, gw), BF16), pltpu.VMEM((2, Q_BLOCK, gw), BF16)],
        compiler_params=_params(("arbitrary", "arbitrary", "arbitrary")),
        name=f"attn_a_g{g}",
    )(pv, pv, pv, tables, gains(gq), gains(gk))
    return o.reshape(b, s, gw), lse.reshape(b, s, LANES)


MERGE_CHUNK = 128


def _merge_out_kernel(o0_ref, o1_ref, o2_ref, l0_ref, l1_ref, l2_ref, x_ref, g_ref, w_ref,
                      e_ref, out_ref, lun_ref, oun_ref, om_ref):
    tm = o0_ref.shape[1]
    n_slab = o0_ref.shape[2] // LANES

    @pl.when(pl.program_id(1) == 0)
    def _():
        for gi, (o_ref, l_ref) in enumerate(((o1_ref, l1_ref), (o2_ref, l2_ref))):
            r = A_DILATED[gi + 1][1]
            n = tm // r
            for jr in range(r):
                rows = slice(jr * n, (jr + 1) * n)
                lun_ref[gi, pl.ds(jr, n, stride=r), :] = l_ref[0, rows, :]
                for c in range(n_slab):
                    oun_ref[gi, c, pl.ds(jr, n, stride=r), :] = (
                        o_ref[0, rows, c * LANES:(c + 1) * LANES].astype(F32))
        expand = e_ref[...]

        def widen(w):
            hi = w.astype(BF16)
            lo = (w - hi.astype(F32)).astype(BF16)
            return _dot(hi, expand) + _dot(lo, expand)

        for k in range(tm // MERGE_CHUNK):
            rs = slice(k * MERGE_CHUNK, (k + 1) * MERGE_CHUNK)
            l0, l1, l2 = l0_ref[0, rs, :], lun_ref[0, rs, :], lun_ref[1, rs, :]
            m = jnp.maximum(jnp.maximum(l0, l1), l2)
            e0, e1, e2 = jnp.exp(l0 - m), jnp.exp(l1 - m), jnp.exp(l2 - m)
            inv = 1.0 / (e0 + e1 + e2)
            o1 = jnp.concatenate([oun_ref[0, c, rs, :] for c in range(n_slab)], axis=1)
            o2 = jnp.concatenate([oun_ref[1, c, rs, :] for c in range(n_slab)], axis=1)
            o = (widen(e0 * inv) * o0_ref[0, rs, :].astype(F32)
                 + widen(e1 * inv) * o1 + widen(e2 * inv) * o2)
            om_ref[rs, :] = o.astype(BF16)

    out_ref[0] = x_ref[0] + g_ref[0] * _dot(om_ref[...], w_ref[...])


def _merge_out(outs, lses, x, gate, w_out, tn=512):
    b, s, d = x.shape
    tm = PERM_TILE
    spt = s // tm
    rmap = lambda i, j: (i // spt, i % spt, 0)
    cmap = lambda i, j: (i // spt, i % spt, j)
    expand = ((jnp.arange(d)[None, :] // HEAD_DIM == jnp.arange(LANES)[:, None])
              & (jnp.arange(LANES)[:, None] < N_HEADS)).astype(BF16)
    o_spec = pl.BlockSpec((1, tm, d), rmap)
    l_spec = pl.BlockSpec((1, tm, LANES), rmap)
    return pl.pallas_call(
        _merge_out_kernel,
        out_shape=jax.ShapeDtypeStruct((b, s, d), F32),
        grid=(b * spt, d // tn),
        in_specs=[o_spec, o_spec, o_spec, l_spec, l_spec, l_spec,
                  pl.BlockSpec((1, tm, tn), cmap),
                  pl.BlockSpec((1, 1, tn), lambda i, j: (i // spt, 0, j)),
                  pl.BlockSpec((d, tn), lambda i, j: (0, j)),
                  pl.BlockSpec((LANES, d), lambda i, j: (0, 0))],
        out_specs=pl.BlockSpec((1, tm, tn), cmap),
        scratch_shapes=[pltpu.VMEM((2, tm, LANES), F32),
                        pltpu.VMEM((2, d // LANES, tm, LANES), F32),
                        pltpu.VMEM((tm, d), BF16)],
        compiler_params=_params(("arbitrary", "arbitrary")),
        name="merge_out",
    )(*outs, *lses, x, gate, w_out, expand)


def _out_kernel(o_ref, x_ref, g_ref, w_ref, out_ref):
    out_ref[0] = x_ref[0] + g_ref[0] * _dot(o_ref[0], w_ref[...])


def _out_proj(o, x, gate, w_out, tm=512):
    b, s, d = x.shape
    spt = s // tm
    xmap = lambda i: (i // spt, i % spt, 0)
    return pl.pallas_call(
        _out_kernel,
        out_shape=jax.ShapeDtypeStruct((b, s, d), F32),
        grid=(b * spt,),
        in_specs=[pl.BlockSpec((1, tm, d), xmap),
                  pl.BlockSpec((1, tm, d), xmap),
                  pl.BlockSpec((1, 1, d), lambda i: (i // spt, 0, 0)),
                  pl.BlockSpec((d, d), lambda i: (0, 0))],
        out_specs=pl.BlockSpec((1, tm, d), xmap),
        compiler_params=_params(("arbitrary",)),
        name="out_proj",
    )(o, x, gate, w_out)


def _cum_kernel(f_ref, fb_ref, tri_ref, o_ref, carry_ref):
    @pl.when(pl.program_id(1) == 0)
    def _():
        carry_ref[...] = jnp.zeros_like(carry_ref)

    z = f_ref[0] + fb_ref[...]
    logf = jnp.minimum(z, 0.0) - jnp.log(1.0 + jnp.exp(-jnp.abs(z)))
    hi, mid, lo = _split3(logf)
    tri = tri_ref[...]
    cum = _dot(tri, hi) + _dot(tri, mid) + _dot(tri, lo) + carry_ref[...]
    o_ref[0] = cum
    carry_ref[...] = cum[-1:, :]


def _cum_forget(fg, f_bias, tc=512):
    b, s, w = fg.shape
    fb = jnp.pad(f_bias, (0, w - f_bias.shape[0])).reshape(1, w)
    tri = (jnp.arange(tc)[:, None] >= jnp.arange(tc)[None, :]).astype(BF16)
    return pl.pallas_call(
        _cum_kernel,
        out_shape=jax.ShapeDtypeStruct((b, s, w), F32),
        grid=(b, s // tc),
        in_specs=[pl.BlockSpec((1, tc, w), lambda i, j: (i, j, 0)),
                  pl.BlockSpec((1, w), lambda i, j: (0, 0)),
                  pl.BlockSpec((tc, tc), lambda i, j: (0, 0))],
        out_specs=pl.BlockSpec((1, tc, w), lambda i, j: (i, j, 0)),
        scratch_shapes=[pltpu.VMEM((1, w), F32)],
        compiler_params=_params(("arbitrary", "arbitrary")),
        name="cum_forget",
    )(fg, fb, tri)


def _gate_columns(cum_col, lead):
    rows = cum_col.shape[0]
    lane = lax.broadcasted_iota(jnp.int32, (rows, HEAD_DIM), 1)
    hi, mid, lo = (p.astype(F32) for p in _split3(cum_col if lead else -cum_col))
    off = 0 if lead else 3
    ext = jnp.where(lane < 6, 1.0, 0.0)
    ext = jnp.where(lane == off, hi, ext)
    ext = jnp.where(lane == off + 1, mid, ext)
    ext = jnp.where(lane == off + 2, lo, ext)
    return ext.astype(BF16)


def _pick_lane(tile, idx):
    lane = lax.broadcasted_iota(jnp.int32, tile.shape, 1)
    return jnp.sum(jnp.where(lane == idx, tile, 0.0), axis=-1, keepdims=True)


ONES_ROWS = 16
FOX_HEADS_PER_STEP = 8


def _fox_kernel(q_ref, k_ref, v_ref, cq_ref, ck_ref, gq_ref, gk_ref, o_ref, ka_ref, vt_ref,
                qat_ref, s_ref, p_ref, acc_ref, *, tq, tk):
    hg = pl.program_id(1)
    qi = pl.program_id(2)
    n_kt = vt_ref.shape[1]
    nh = ka_ref.shape[0]
    heads = range(nh)

    @pl.when(qi == 0)
    def _():
        gk = gk_ref[...]
        for hh in heads:
            sl = slice(hh * HEAD_DIM, (hh + 1) * HEAD_DIM)
            ka_ref[hh, :, :HEAD_DIM] = _headnorm(k_ref[0, :, sl], gk).astype(BF16)
            ck = _pick_lane(ck_ref[0], nh * hg + hh) * LOG2E
            ka_ref[hh, :, HEAD_DIM:] = _gate_columns(ck, lead=False)
        for pr in range(nh // 2):
            vt = v_ref[0, :, pr * PAIR:(pr + 1) * PAIR].astype(F32).T.astype(BF16)
            for hh in (2 * pr, 2 * pr + 1):
                rows = slice((hh % 2) * HEAD_DIM, (hh % 2 + 1) * HEAD_DIM)
                for jj in range(n_kt):
                    vt_ref[hh, jj, :HEAD_DIM] = vt[rows, jj * tk:(jj + 1) * tk]
                    vt_ref[hh, jj, HEAD_DIM:] = jnp.ones((ONES_ROWS, tk), BF16)

    gq = gq_ref[...] * (ATTN_SCALE * LOG2E)
    for hh in heads:
        sl = slice(hh * HEAD_DIM, (hh + 1) * HEAD_DIM)
        qn = _headnorm(q_ref[0, :, sl], gq)
        cq = _pick_lane(cq_ref[0], nh * hg + hh) * LOG2E
        qa = jnp.concatenate([qn, _gate_columns(cq, lead=True).astype(F32)], axis=1)
        qat_ref[hh] = qa.T.astype(BF16)
    key = lax.broadcasted_iota(jnp.int32, (tk, tq), 0)
    qry = lax.broadcasted_iota(jnp.int32, (tk, tq), 1)
    ahead = key - qry

    def scores(t, hh, diagonal):
        start = pl.multiple_of(t * tk, tk)
        st = _dot(ka_ref[hh, pl.ds(start, tk), :], qat_ref[hh])
        if diagonal is None:
            st = jnp.where(ahead > jnp.where(t == qi, 0, tk), NEG_INF, st)
        elif diagonal:
            st = jnp.where(ahead > 0, NEG_INF, st)
        s_ref[hh] = st
        return jnp.max(st, axis=0, keepdims=True)

    def pv(t, hh, alpha_prev):
        acc_ref[hh] = alpha_prev * acc_ref[hh] + _dot(vt_ref[hh, t], p_ref[hh])

    def trip(i, carry, next_diagonal):
        out = []
        for hh in heads:
            m, alpha_prev, mx = carry[hh]
            pv(jnp.maximum(i - 1, 0), hh, alpha_prev)
            m_new = jnp.maximum(m, mx)
            alpha = jnp.exp2(m - m_new)
            p_ref[hh] = jnp.exp2(s_ref[hh] - m_new).astype(BF16)
            if next_diagonal is not None:
                mx = scores(i + 1, hh, next_diagonal)
            out.append((m_new, alpha, mx))
        return tuple(out)

    p_ref[...] = jnp.zeros_like(p_ref)
    acc_ref[...] = jnp.zeros_like(acc_ref)
    carry = tuple((jnp.full((1, tq), NEG_INF, F32), jnp.ones((1, tq), F32),
                   scores(0, hh, None)) for hh in heads)
    carry = lax.fori_loop(0, jnp.maximum(qi - 1, 0),
                          functools.partial(trip, next_diagonal=False), carry)
    carry = lax.fori_loop(jnp.maximum(qi - 1, 0), qi,
                          functools.partial(trip, next_diagonal=True), carry)
    carry = trip(qi, carry, None)
    for pr in range(nh // 2):
        o_t = []
        for hh in (2 * pr, 2 * pr + 1):
            pv(qi, hh, carry[hh][1])
            acc = acc_ref[hh]
            o_t.append(acc[:HEAD_DIM] / acc[HEAD_DIM:HEAD_DIM + 1])
        o_ref[0, :, pr * PAIR:(pr + 1) * PAIR] = (
            jnp.concatenate(o_t, axis=0).T.astype(BF16))


def _fox_attention(qkv, cum, gq, gk, tq=256):
    b, s, _ = qkv.shape
    tk = tq
    nq = s // tq
    nh = FOX_HEADS_PER_STEP
    groups = N_HEADS // nh
    gw = nh * HEAD_DIM
    vrows = HEAD_DIM + ONES_ROWS
    return pl.pallas_call(
        functools.partial(_fox_kernel, tq=tq, tk=tk),
        out_shape=jax.ShapeDtypeStruct((b, s, N_HEADS * HEAD_DIM), BF16),
        grid=(b, groups, nq),
        in_specs=[
            pl.BlockSpec((1, tq, gw), lambda bi, hg, qi: (bi, qi, hg)),
            pl.BlockSpec((1, s, gw), lambda bi, hg, qi: (bi, 0, groups + hg)),
            pl.BlockSpec((1, s, gw), lambda bi, hg, qi: (bi, 0, 2 * groups + hg)),
            pl.BlockSpec((1, tq, LANES), lambda bi, hg, qi: (bi, qi, 0)),
            pl.BlockSpec((1, s, LANES), lambda bi, hg, qi: (bi, 0, 0)),
            pl.BlockSpec((1, HEAD_DIM), lambda bi, hg, qi: (0, 0)),
            pl.BlockSpec((1, HEAD_DIM), lambda bi, hg, qi: (0, 0)),
        ],
        out_specs=pl.BlockSpec((1, tq, gw), lambda bi, hg, qi: (bi, qi, hg)),
        scratch_shapes=[pltpu.VMEM((nh, s, 2 * HEAD_DIM), BF16),
                        pltpu.VMEM((nh, s // tk, vrows, tk), BF16),
                        pltpu.VMEM((nh, 2 * HEAD_DIM, tq), BF16),
                        pltpu.VMEM((nh, tk, tq), F32),
                        pltpu.VMEM((nh, tk, tq), BF16),
                        pltpu.VMEM((nh, vrows, tq), F32)],
        compiler_params=_params(("arbitrary", "arbitrary", "arbitrary")),
        name="fox_attn",
    )(qkv, qkv, qkv, cum, cum, gq.reshape(1, HEAD_DIM), gk.reshape(1, HEAD_DIM))


ROUTER_LANES = LANES


def _route(logits):
    lane = lax.broadcasted_iota(jnp.int32, logits.shape, 1)
    lane_f = lane.astype(F32)
    big = float(ROUTER_LANES)
    is_g = lane < N_GROUPS
    gl = jnp.where(is_g, logits, -jnp.inf)
    gmax = jnp.max(gl, axis=-1, keepdims=True)
    gsum = jnp.sum(jnp.where(is_g, jnp.exp(logits - gmax), 0.0), axis=-1, keepdims=True)
    g_w = 1.0 / gsum
    gidx = jnp.min(jnp.where(gl == gmax, lane_f, big), axis=-1, keepdims=True)
    lo = N_GROUPS + EXPERTS_PER_GROUP * gidx
    in_grp = (lane_f >= lo) & (lane_f < lo + EXPERTS_PER_GROUP)
    el = jnp.where(in_grp, logits, -jnp.inf)
    t1 = jnp.max(el, axis=-1, keepdims=True)
    i1 = jnp.min(jnp.where(el == t1, lane_f, big), axis=-1, keepdims=True)
    el2 = jnp.where(lane_f == i1, -jnp.inf, el)
    t2 = jnp.max(el2, axis=-1, keepdims=True)
    i2 = jnp.min(jnp.where(el2 == t2, lane_f, big), axis=-1, keepdims=True)
    e2 = jnp.exp(t2 - t1)
    w1 = g_w / (1.0 + e2)
    w2 = w1 * e2
    gates = jnp.where(lane_f == i1, w1, jnp.where(lane_f == i2, w2, 0.0))
    return jnp.where(lane == 0, gidx, gates)


MOE_TILE = 512
HALF_D = D_MODEL // 2
ROW_WORDS = HALF_D + ROUTER_LANES
U32 = jnp.uint32


def _pack_pairs(x):
    n = x.shape[1] // 2
    hi = pltpu.bitcast(x[:, :n].astype(BF16).astype(F32), U32)
    lo = pltpu.bitcast(x[:, n:].astype(BF16).astype(F32), U32)
    return hi | (lo >> 16)


def _unpack_pairs(w):
    hi = pltpu.bitcast(w & U32(0xFFFF0000), F32)
    lo = pltpu.bitcast(w << 16, F32)
    return jnp.concatenate([hi, lo], axis=1)


SUBLANES = 8


def _moe_route_kernel(x_ref, gam_ref, sc_ref, sh_ref, wr_ref, br_ref, tri_ref, o_ref, meta_ref,
                      cnt_ref):
    h = _modnorm(x_ref[0], gam_ref[...], sc_ref[0], sh_ref[0])
    h_hi = h.astype(BF16)
    h_lo = (h - h_hi.astype(F32)).astype(BF16)
    wr = wr_ref[...]
    w_hi = wr.astype(BF16)
    w_lo = (wr - w_hi.astype(F32)).astype(BF16)
    logits = _dot(h_hi, w_hi) + _dot(h_hi, w_lo) + _dot(h_lo, w_hi) + br_ref[...]
    route = _route(logits)
    tm = route.shape[0]

    @pl.when(pl.program_id(0) == 0)
    def _():
        cnt_ref[...] = jnp.zeros_like(cnt_ref)

    lane = lax.broadcasted_iota(jnp.int32, route.shape, 1)
    member = jnp.where((lane.astype(F32) == route[:, 0:1]) & (lane < N_GROUPS), 1.0, 0.0)
    incl = _dot(tri_ref[...], member.astype(BF16))
    rank = jnp.sum(member * (incl - member + cnt_ref[...]), axis=-1, keepdims=True)
    cnt_ref[...] = cnt_ref[...] + incl[tm - 1:tm, :]
    meta = jnp.where(lane == 0, route[:, 0:1], jnp.where(lane == 1, rank, 0.0))
    meta_ref[...] = meta.T[:SUBLANES, :]

    packed = _pack_pairs(h)
    for c in range(HALF_D // LANES):
        o_ref[pl.ds(c, tm, stride=SUBLANES), :] = packed[:, c * LANES:(c + 1) * LANES]
    o_ref[pl.ds(HALF_D // LANES, tm, stride=SUBLANES), :] = pltpu.bitcast(route, U32)
    for c in range(HALF_D // LANES + 1, SUBLANES):
        o_ref[pl.ds(c, tm, stride=SUBLANES), :] = jnp.zeros((tm, LANES), U32)


def _token_copy(src, src_tok, dst, dst_tok, sem):
    s0 = pl.multiple_of(src_tok * SUBLANES, SUBLANES)
    d0 = pl.multiple_of(dst_tok * SUBLANES, SUBLANES)
    return pltpu.make_async_copy(src.at[pl.ds(s0, SUBLANES), :], dst.at[pl.ds(d0, SUBLANES), :], sem)


def _moe_expert_kernel(tg_ref, nt_ref, nv_ref, dest_ref, hx_hbm, wg_ref, wu_ref, wd_ref, e_ref,
                       y_hbm, buf, ybuf, src_ref, gsem, ssem):
    t = pl.program_id(0)
    n_used = nt_ref[0]
    slot = t % 2
    rows = buf.shape[1] // SUBLANES
    unroll = 8

    @pl.when(t == 0)
    def _():
        def clear(p, c):
            src_ref[p] = 0
            return c
        lax.fori_loop(0, src_ref.shape[0], clear, 0, unroll=unroll)

        def invert(tok, c):
            src_ref[dest_ref[tok]] = tok
            return c
        lax.fori_loop(0, dest_ref.shape[0], invert, 0, unroll=unroll)

    def gather_start(tile, sl):
        def body(r, c):
            _token_copy(hx_hbm, src_ref[tile * rows + r], buf.at[sl], r, gsem.at[sl]).start()
            return c
        lax.fori_loop(0, rows, body, 0, unroll=unroll)

    def gather_wait(sl):
        def body(r, c):
            _token_copy(hx_hbm, 0, buf.at[sl], r, gsem.at[sl]).wait()
            return c
        lax.fori_loop(0, rows, body, 0, unroll=unroll)

    def valid_rows_loop(tile, body):
        n_valid = nv_ref[tile]
        chunks = n_valid // unroll

        def chunk(k, c):
            for i in range(unroll):
                body(k * unroll + i)
            return c
        lax.fori_loop(0, chunks, chunk, 0)

        def single(r, c):
            body(r)
            return c
        lax.fori_loop(chunks * unroll, n_valid, single, 0)

    def scatter_start(tile, sl):
        valid_rows_loop(tile, lambda r: _token_copy(
            ybuf.at[sl], r, y_hbm, src_ref[tile * rows + r], ssem.at[sl]).start(priority=1))

    def scatter_wait(tile, sl):
        valid_rows_loop(tile, lambda r: _token_copy(ybuf.at[sl], r, y_hbm, 0, ssem.at[sl]).wait())

    @pl.when(t == 0)
    def _():
        gather_start(0, 0)

    @pl.when(t + 1 < n_used)
    def _():
        gather_start(t + 1, 1 - slot)

    @pl.when(t < n_used)
    def _():
        gather_wait(slot)
        slab = lambda c: buf[slot, pl.ds(c, rows, stride=SUBLANES), :]
        n_pair_slabs = HALF_D // LANES
        w = jnp.concatenate([slab(c) for c in range(n_pair_slabs)], axis=1)
        h = _unpack_pairs(w).astype(BF16)
        gates = pltpu.bitcast(slab(n_pair_slabs), F32)
        g_hi = gates.astype(BF16)
        g_lo = (gates - g_hi.astype(F32)).astype(BF16)
        y = None
        for e in range(EXPERTS_PER_GROUP):
            a = _dot(h, wg_ref[e])
            u = _dot(h, wu_ref[e])
            spread = e_ref[0, :, e * EXPERT_FF:(e + 1) * EXPERT_FF]
            gexp = _dot(g_hi, spread) + _dot(g_lo, spread)
            hid = a * (1.0 / (1.0 + jnp.exp(-a))) * u * gexp
            part = _dot(hid.astype(BF16), wd_ref[e])
            y = part if y is None else y + part

        @pl.when(t >= 2)
        def _():
            scatter_wait(t - 2, slot)

        for c in range(y.shape[1] // LANES):
            ybuf[slot, pl.ds(c, rows, stride=SUBLANES), :] = y[:, c * LANES:(c + 1) * LANES]
        scatter_start(t, slot)

    @pl.when(t == pl.num_programs(0) - 1)
    def _():
        for back in (2, 1):
            scatter_wait(n_used - back, (n_used - back) % 2)


def _moe_combine_kernel(x_ref, g_ref, y_ref, o_ref):
    tm = x_ref.shape[1]
    y = jnp.concatenate([y_ref[pl.ds(c, tm, stride=SUBLANES), :]
                         for c in range(x_ref.shape[2] // LANES)], axis=1)
    o_ref[0] = x_ref[0] + g_ref[0] * y


def _moe(x, gamma, sc, sh, gate, w_router, b_router, w_gate, w_up, w_down):
    b, s, d = x.shape
    n = b * s
    tm = PERM_TILE
    spt = s // tm
    rows = MOE_TILE
    n_tiles = n // rows + N_GROUPS
    p_rows = n_tiles * rows
    bmap = lambda i: (i // spt, 0, 0)
    tri = (jnp.arange(tm)[:, None] >= jnp.arange(tm)[None, :]).astype(BF16)
    hx, meta = pl.pallas_call(
        _moe_route_kernel,
        out_shape=(jax.ShapeDtypeStruct((n * SUBLANES, LANES), U32),
                   jax.ShapeDtypeStruct((SUBLANES, n), F32)),
        grid=(b * spt,),
        in_specs=[
            pl.BlockSpec((1, tm, d), lambda i: (i // spt, i % spt, 0)),
            pl.BlockSpec((1, d), lambda i: (0, 0)),
            pl.BlockSpec((1, 1, d), bmap),
            pl.BlockSpec((1, 1, d), bmap),
            pl.BlockSpec((d, ROUTER_LANES), lambda i: (0, 0)),
            pl.BlockSpec((1, ROUTER_LANES), lambda i: (0, 0)),
            pl.BlockSpec((tm, tm), lambda i: (0, 0)),
        ],
        out_specs=(pl.BlockSpec((tm * SUBLANES, LANES), lambda i: (i, 0)),
                   pl.BlockSpec((SUBLANES, tm), lambda i: (0, i))),
        scratch_shapes=[pltpu.VMEM((1, LANES), F32)],
        compiler_params=_params(("arbitrary",)),
        name="moe_route",
    )(x, gamma.reshape(1, d), sc, sh, w_router, b_router, tri)

    gidx = meta[0].astype(jnp.int32)
    rank = meta[1].astype(jnp.int32)
    onehot = (gidx[:, None] == jnp.arange(N_GROUPS)[None, :]).astype(jnp.int32)
    count = jnp.sum(onehot, axis=0)
    padded = (count + rows - 1) // rows * rows
    ends = jnp.cumsum(padded)
    dest = jnp.sum(onehot * (ends - padded)[None, :], axis=1) + rank
    n_used = (ends[-1] // rows).reshape(1).astype(jnp.int32)
    tile_start = jnp.arange(n_tiles, dtype=jnp.int32) * rows
    tile_group = jnp.minimum(jnp.sum(ends[None, :] <= tile_start[:, None], axis=1),
                             N_GROUPS - 1).astype(jnp.int32)
    seg_valid_end = ends - padded + count
    n_valid = jnp.clip(jnp.sum(jnp.where(jnp.arange(N_GROUPS)[None, :] == tile_group[:, None],
                                         seg_valid_end[None, :], 0), axis=1) - tile_start,
                       0, rows).astype(jnp.int32)

    ff = EXPERTS_PER_GROUP * EXPERT_FF
    lanes = jnp.arange(ROUTER_LANES)[None, :, None]
    expert = N_GROUPS + EXPERTS_PER_GROUP * jnp.arange(N_GROUPS)[:, None, None] \
        + jnp.arange(ff)[None, None, :] // EXPERT_FF
    expand = (lanes == expert).astype(BF16)
    wmap = lambda t, tg, nt, nv, ds: (tg[t], 0, 0)
    y_tok = pl.pallas_call(
        _moe_expert_kernel,
        out_shape=jax.ShapeDtypeStruct((n * SUBLANES, LANES), F32),
        grid_spec=pltpu.PrefetchScalarGridSpec(
            num_scalar_prefetch=4,
            grid=(n_tiles,),
            in_specs=[
                pl.BlockSpec(memory_space=pl.ANY),
                pl.BlockSpec((EXPERTS_PER_GROUP, d, EXPERT_FF), wmap),
                pl.BlockSpec((EXPERTS_PER_GROUP, d, EXPERT_FF), wmap),
                pl.BlockSpec((EXPERTS_PER_GROUP, EXPERT_FF, d), wmap),
                pl.BlockSpec((1, ROUTER_LANES, ff), wmap),
            ],
            out_specs=pl.BlockSpec(memory_space=pl.ANY),
            scratch_shapes=[pltpu.VMEM((2, rows * SUBLANES, LANES), U32),
                            pltpu.VMEM((2, rows * SUBLANES, LANES), F32),
                            pltpu.SMEM((p_rows,), jnp.int32),
                            pltpu.SemaphoreType.DMA((2,)),
                            pltpu.SemaphoreType.DMA((2,))],
        ),
        compiler_params=_params(("arbitrary",)),
        name="moe_experts",
    )(tile_group, n_used, n_valid, dest, hx, w_gate, w_up, w_down, expand)

    return pl.pallas_call(
        _moe_combine_kernel,
        out_shape=jax.ShapeDtypeStruct((b, s, d), F32),
        grid=(b * spt,),
        in_specs=[pl.BlockSpec((1, tm, d), lambda i: (i // spt, i % spt, 0)),
                  pl.BlockSpec((1, 1, d), bmap),
                  pl.BlockSpec((tm * SUBLANES, LANES), lambda i: (i, 0))],
        out_specs=pl.BlockSpec((1, tm, d), lambda i: (i // spt, i % spt, 0)),
        compiler_params=_params(("arbitrary",)),
        name="moe_combine",
    )(x, gate, y_tok)


def kernel(x, c, w_ada, b_ada, norm_mix, norm_ffn, rel_bias, a_w_in, a_w_out, a_q_norm,
           a_k_norm, b_w_in, b_f_bias, b_w_out, b_q_norm, b_k_norm, router_group_w,
           router_group_b, router_expert_w, router_expert_b, w_gate, w_up, w_down):
    b, s, d = x.shape
    depth = w_ada.shape[0]
    mod = _ada(c, w_ada, b_ada).reshape(depth, b, 6, 1, d)
    pad_r = ROUTER_LANES - N_GROUPS - N_EXPERTS
    for i in range(depth):
        sh_m, sc_m, g_m, sh_f, sc_f, g_f = (mod[i, :, k] for k in range(6))
        j = i // 2
        if i % 2 == 0:
            proj = _normproj_a(x, norm_mix[i], sc_m, sh_m, a_w_in[j].astype(BF16))
            outs, lses = [], []
            for g, (window, dil) in enumerate(A_DILATED):
                tab = rel_bias[:, g * N_HEADS:(g + 1) * N_HEADS]
                o, l = _attn_a_group(proj, g, tab, a_q_norm[j], a_k_norm[j], window, dil)
                outs.append(o)
                lses.append(l)
            x = _merge_out(outs, lses, x, g_m, a_w_out[j].astype(BF16))
        else:
            n_qkv = 3 * N_HEADS * HEAD_DIM
            w_in = b_w_in[j]
            w_f = jnp.pad(w_in[:, n_qkv:], ((0, 0), (0, LANES - N_HEADS))).astype(BF16)
            qkv, fg = _normproj(x, norm_mix[i], sc_m, sh_m, w_in[:, :n_qkv].astype(BF16), w_f)
            cum = _cum_forget(fg, b_f_bias[j])
            o = _fox_attention(qkv, cum, b_q_norm[j], b_k_norm[j])
            x = _out_proj(o, x, g_m, b_w_out[j].astype(BF16))
        w_router = jnp.pad(jnp.concatenate([router_group_w[i], router_expert_w[i]], axis=1),
                           ((0, 0), (0, pad_r)))
        b_router = jnp.pad(jnp.concatenate([router_group_b[i], router_expert_b[i]]),
                           (0, pad_r)).reshape(1, ROUTER_LANES)
        x = _moe(x, norm_ffn[i], sc_f, sh_f, g_f, w_router, b_router,
                 w_gate[i].astype(BF16), w_up[i].astype(BF16), w_down[i].astype(BF16))
    return x
```

```python
import functools
import math

import jax
import jax.numpy as jnp
from jax import lax
from jax.experimental import pallas as pl
from jax.experimental.pallas import tpu as pltpu

F32 = jnp.float32
BF16 = jnp.bfloat16

D_MODEL = 1024
HEAD_DIM = 64
N_HEADS = 16
ATTN_SCALE = HEAD_DIM ** -0.5
A_DILATED = ((128, 1), (512, 4), (2048, 16))
Q_BLOCK = 128
REL_BUCKETS = 32
REL_MAX_DIST = 2048
N_GROUPS = 4
EXPERTS_PER_GROUP = 4
N_EXPERTS = 16
EXPERT_FF = 256
EPS = 1e-6
NEG_INF = -1e30

LANES = 128
VMEM_LIMIT = 48 * 1024 * 1024
LOG2E = 1.4426950408889634
LN2 = 0.6931471805599453
PERM_TILE = 1024
PAIR = 2 * HEAD_DIM
N_PAIRS = N_HEADS // 2


def _params(sem):
    return pltpu.CompilerParams(dimension_semantics=sem, vmem_limit_bytes=VMEM_LIMIT)


def _split3(x):
    hi = x.astype(BF16)
    r1 = x - hi.astype(F32)
    mid = r1.astype(BF16)
    lo = (r1 - mid.astype(F32)).astype(BF16)
    return hi, mid, lo


def _dot(a, b):
    return jnp.dot(a, b, preferred_element_type=F32)


def _dot_nt(a, b):
    return lax.dot_general(a, b, (((1,), (1,)), ((), ())), preferred_element_type=F32)


def _ada_kernel(c_ref, w_ref, b_ref, o_ref):
    c = c_ref[...]
    a = c * (1.0 / (1.0 + jnp.exp(-c)))
    o_ref[0] = _dot(a.astype(BF16), w_ref[0].astype(BF16)) + b_ref[0]


def _ada(c, w_ada, b_ada):
    depth, d, n6 = w_ada.shape
    b = c.shape[0]
    rows = 8
    c_pad = jnp.pad(c, ((0, rows - b), (0, 0)))
    tn = 1536
    out = pl.pallas_call(
        _ada_kernel,
        out_shape=jax.ShapeDtypeStruct((depth, rows, n6), F32),
        grid=(depth, n6 // tn),
        in_specs=[
            pl.BlockSpec((rows, d), lambda i, j: (0, 0)),
            pl.BlockSpec((1, d, tn), lambda i, j: (i, 0, j)),
            pl.BlockSpec((1, 1, tn), lambda i, j: (i, 0, j)),
        ],
        out_specs=pl.BlockSpec((1, rows, tn), lambda i, j: (i, 0, j)),
        compiler_params=_params(("arbitrary", "arbitrary")),
        name="ada_mod",
    )(c_pad, w_ada, b_ada.reshape(depth, 1, n6))
    return out[:, :b]


def _modnorm(x, gamma, sc, sh):
    ms = jnp.mean(x * x, axis=-1, keepdims=True)
    y = x * lax.rsqrt(ms + EPS) * gamma
    return y * (1.0 + sc) + sh


def _normproj_kernel(x_ref, g_ref, sc_ref, sh_ref, w_ref, o_ref, h_ref):
    @pl.when(pl.program_id(1) == 0)
    def _():
        h_ref[...] = _modnorm(x_ref[0], g_ref[...], sc_ref[0], sh_ref[0]).astype(BF16)

    o_ref[0] = _dot(h_ref[...], w_ref[...]).astype(BF16)


def _normproj_gate_kernel(x_ref, g_ref, sc_ref, sh_ref, w_ref, wf_ref, o_ref, f_ref, h_ref):
    @pl.when(pl.program_id(1) == 0)
    def _():
        h = _modnorm(x_ref[0], g_ref[...], sc_ref[0], sh_ref[0]).astype(BF16)
        h_ref[...] = h
        f_ref[0] = _dot(h, wf_ref[...])

    o_ref[0] = _dot(h_ref[...], w_ref[...]).astype(BF16)


def _normproj(x, gamma, sc, sh, w, w_f=None, tm=1024, tn=1024):
    b, s, d = x.shape
    ncols = w.shape[1]
    spt = s // tm
    grid = (b * spt, ncols // tn)
    xmap = lambda i, j: (i // spt, i % spt, 0)
    bmap = lambda i, j: (i // spt, 0, 0)
    in_specs = [
        pl.BlockSpec((1, tm, d), xmap),
        pl.BlockSpec((1, d), lambda i, j: (0, 0)),
        pl.BlockSpec((1, 1, d), bmap),
        pl.BlockSpec((1, 1, d), bmap),
        pl.BlockSpec((d, tn), lambda i, j: (0, j)),
    ]
    o_spec = pl.BlockSpec((1, tm, tn), lambda i, j: (i // spt, i % spt, j))
    scratch = [pltpu.VMEM((tm, d), BF16)]
    if w_f is None:
        return pl.pallas_call(
            _normproj_kernel,
            out_shape=jax.ShapeDtypeStruct((b, s, ncols), BF16),
            grid=grid, in_specs=in_specs, out_specs=o_spec, scratch_shapes=scratch,
            compiler_params=_params(("arbitrary", "arbitrary")),
            name="normproj",
        )(x, gamma.reshape(1, d), sc, sh, w)
    in_specs.append(pl.BlockSpec((d, LANES), lambda i, j: (0, 0)))
    f_spec = pl.BlockSpec((1, tm, LANES), xmap)
    return pl.pallas_call(
        _normproj_gate_kernel,
        out_shape=(jax.ShapeDtypeStruct((b, s, ncols), BF16),
                   jax.ShapeDtypeStruct((b, s, LANES), F32)),
        grid=grid, in_specs=in_specs, out_specs=(o_spec, f_spec), scratch_shapes=scratch,
        compiler_params=_params(("arbitrary", "arbitrary")),
        name="normproj_gate",
    )(x, gamma.reshape(1, d), sc, sh, w, w_f)


def _normproj_a_kernel(x_ref, g_ref, sc_ref, sh_ref, w_ref, o_ref, hf_ref, hp_ref):
    tm = x_ref.shape[1]
    n_slab = x_ref.shape[2] // LANES

    @pl.when(pl.program_id(1) == 0)
    def _():
        h = _modnorm(x_ref[0], g_ref[...], sc_ref[0], sh_ref[0])
        hp_ref[0] = h.astype(BF16)
        for c in range(n_slab):
            hf_ref[c] = h[:, c * LANES:(c + 1) * LANES]
        for g, (_, r) in enumerate(A_DILATED):
            if r == 1:
                continue
            n = tm // r
            for jr in range(r):
                for c in range(n_slab):
                    hp_ref[g, jr * n:(jr + 1) * n, c * LANES:(c + 1) * LANES] = (
                        hf_ref[c, pl.ds(jr, n, stride=r), :].astype(BF16))

    o_ref[0] = _dot(hp_ref[pl.program_id(1) // 3], w_ref[...]).astype(BF16)


def _normproj_a(x, gamma, sc, sh, w, tn=1024):
    b, s, d = x.shape
    tm = PERM_TILE
    ncols = w.shape[1]
    spt = s // tm
    bmap = lambda i, j: (i // spt, 0, 0)
    return pl.pallas_call(
        _normproj_a_kernel,
        out_shape=jax.ShapeDtypeStruct((b, s, ncols), BF16),
        grid=(b * spt, ncols // tn),
        in_specs=[
            pl.BlockSpec((1, tm, d), lambda i, j: (i // spt, i % spt, 0)),
            pl.BlockSpec((1, d), lambda i, j: (0, 0)),
            pl.BlockSpec((1, 1, d), bmap),
            pl.BlockSpec((1, 1, d), bmap),
            pl.BlockSpec((d, tn), lambda i, j: (0, j)),
        ],
        out_specs=pl.BlockSpec((1, tm, tn), lambda i, j: (i // spt, i % spt, j)),
        scratch_shapes=[pltpu.VMEM((d // LANES, tm, LANES), F32),
                        pltpu.VMEM((len(A_DILATED), tm, d), BF16)],
        compiler_params=_params(("arbitrary", "arbitrary")),
        name="normproj_a",
    )(x, gamma.reshape(1, d), sc, sh, w)


def _headnorm(t, gain):
    t = t.astype(F32)
    ms = jnp.mean(t * t, axis=-1, keepdims=True)
    return t * lax.rsqrt(ms + EPS) * gain


def _pairnorm(t, gain2, even):
    t = t.astype(F32)
    sq = t * t
    tot = jnp.sum(sq, axis=-1, keepdims=True)
    s_even = jnp.sum(jnp.where(even, sq, 0.0), axis=-1, keepdims=True)
    inv = jnp.where(even, lax.rsqrt(s_even * (1.0 / HEAD_DIM) + EPS),
                    lax.rsqrt((tot - s_even) * (1.0 / HEAD_DIM) + EPS))
    return t * inv * gain2


def _rows(ref, cols=slice(None)):
    if len(ref.shape) == 2:
        return ref[:, cols]
    return jnp.concatenate([ref[i, :, cols] for i in range(ref.shape[0])], axis=0)


def _store_rows(ref, cols, val):
    if len(ref.shape) == 2:
        ref[:, cols] = val
    else:
        n = ref.shape[1]
        for i in range(ref.shape[0]):
            ref[i, :, cols] = val[i * n:(i + 1) * n]


def _attn_a_kernel(q_ref, k_ref, v_ref, bias_ref, gq_ref, gk_ref, o_ref, lse_ref,
                   kn_ref, vv_ref):
    ub = pl.program_id(2)
    slot = ub % 2
    prev = 1 - slot

    @pl.when(ub == 0)
    def _():
        kn_ref[1] = jnp.zeros(kn_ref.shape[1:], BF16)
        vv_ref[1] = jnp.zeros(vv_ref.shape[1:], BF16)

    first = jnp.minimum(ub, 1)
    lane = lax.broadcasted_iota(jnp.int32, (Q_BLOCK, PAIR), 1)
    even = lane < HEAD_DIM
    lane_row = lax.broadcasted_iota(jnp.int32, (1, PAIR), 1)
    keep_even = jnp.where(lane_row < HEAD_DIM, 1.0, 0.0).astype(BF16)
    keep_odd = jnp.where(lane_row < HEAD_DIM, 0.0, 1.0).astype(BF16)
    gq2 = gq_ref[...] * (ATTN_SCALE * LOG2E)
    gk2 = gk_ref[...]
    for hp in range(N_PAIRS):
        cs = slice(hp * PAIR, (hp + 1) * PAIR)
        kn_ref[slot, :, cs] = _pairnorm(_rows(k_ref, cs), gk2, even).astype(BF16)
    vv_ref[slot] = _rows(v_ref)
    lse_tile = jnp.zeros((Q_BLOCK, LANES), F32)
    for hp in range(N_PAIRS):
        cs = slice(hp * PAIR, (hp + 1) * PAIR)
        qn = _pairnorm(_rows(q_ref, cs), gq2, even)
        qq = jnp.concatenate([jnp.where(even, qn, 0.0), jnp.where(even, 0.0, qn)],
                             axis=0).astype(BF16)
        kcat = jnp.concatenate([kn_ref[prev, :, cs], kn_ref[slot, :, cs]], axis=0)
        s = _dot_nt(qq, kcat) + bias_ref[first, hp]
        m = jnp.max(s, axis=-1, keepdims=True)
        p = jnp.exp2(s - m)
        l = jnp.sum(p, axis=-1, keepdims=True)
        pb = p.astype(BF16)
        vcat = jnp.concatenate([vv_ref[prev, :, cs], vv_ref[slot, :, cs]], axis=0)
        o = (_dot(pb[:Q_BLOCK], vcat * keep_even)
             + _dot(pb[Q_BLOCK:], vcat * keep_odd))
        inv = 1.0 / l
        o = o * jnp.where(even, inv[:Q_BLOCK], inv[Q_BLOCK:])
        _store_rows(o_ref, cs, o.astype(BF16))
        lse = m * LN2 + jnp.log(l)
        lse_tile = jnp.where(lane == 2 * hp, lse[:Q_BLOCK],
                             jnp.where(lane == 2 * hp + 1, lse[Q_BLOCK:], lse_tile))
    _store_rows(lse_ref, slice(None), lse_tile)


def _t5_bucket(dist):
    max_exact = REL_BUCKETS // 2
    d = jnp.maximum(dist, 0)
    large = max_exact + (jnp.log(jnp.maximum(d, 1).astype(F32) / max_exact)
                         / math.log(REL_MAX_DIST / max_exact)
                         * (REL_BUCKETS - max_exact)).astype(jnp.int32)
    large = jnp.minimum(large, REL_BUCKETS - 1)
    return jnp.where(d < max_exact, d, large)


def _bias_kernel(tab_ref, onehot_ref, o_ref):
    hi, mid, lo = _split3(tab_ref[...])
    oh = onehot_ref[...]
    o_ref[...] = (_dot(hi, oh) + _dot(mid, oh) + _dot(lo, oh)) * LOG2E


def _bias_tables(rel_bias_g, window, dilation):
    qi = jnp.arange(Q_BLOCK)[:, None]
    kj = jnp.arange(2 * Q_BLOCK)[None, :]
    dist = qi + Q_BLOCK - kj
    bucket = _t5_bucket(dist * dilation).reshape(1, -1)
    onehot = (bucket == jnp.arange(REL_BUCKETS)[:, None]).astype(BF16)
    npos = onehot.shape[1]
    chunk = 4096
    bias = pl.pallas_call(
        _bias_kernel,
        out_shape=jax.ShapeDtypeStruct((N_HEADS, npos), F32),
        grid=(npos // chunk,),
        in_specs=[pl.BlockSpec((N_HEADS, REL_BUCKETS), lambda i: (0, 0)),
                  pl.BlockSpec((REL_BUCKETS, chunk), lambda i: (0, i))],
        out_specs=pl.BlockSpec((N_HEADS, chunk), lambda i: (0, i)),
        compiler_params=_params(("arbitrary",)),
        name="rel_bias_table",
    )(rel_bias_g.T.astype(F32), onehot)
    bias = bias.reshape(N_PAIRS, 2 * Q_BLOCK, 2 * Q_BLOCK)
    band = (dist >= 0) & (dist <= window // dilation)
    first = band & (qi - dist >= 0)
    band = jnp.concatenate([band, band], axis=0)[None]
    first = jnp.concatenate([first, first], axis=0)[None]
    return jnp.stack([jnp.where(first, bias, NEG_INF), jnp.where(band, bias, NEG_INF)], axis=0)


def _attn_a_group(proj, g, rel_bias_g, gq, gk, window, dilation):
    b, s, cols = proj.shape
    r = dilation
    nt = s // PERM_TILE
    n_t = PERM_TILE // r
    nb = s // r // Q_BLOCK
    gw = N_HEADS * HEAD_DIM
    tables = _bias_tables(rel_bias_g, window, dilation)
    base = g * 3
    if n_t >= Q_BLOCK:
        bpt = n_t // Q_BLOCK
        blk = lambda w: (None, None, None, Q_BLOCK, w)
        imap = lambda col: (lambda bi, jr, ub: (bi, ub // bpt, jr, ub % bpt, col))
    else:
        blk = lambda w: (None, Q_BLOCK // n_t, None, n_t, w)
        imap = lambda col: (lambda bi, jr, ub: (bi, ub, jr, 0, col))
    pv = proj.reshape(b, nt, r, n_t, cols)
    gains = lambda t: jnp.concatenate([t, t]).reshape(1, PAIR)
    o, lse = pl.pallas_call(
        _attn_a_kernel,
        out_shape=(jax.ShapeDtypeStruct((b, nt, r, n_t, gw), BF16),
                   jax.ShapeDtypeStruct((b, nt, r, n_t, LANES), F32)),
        grid=(b, r, nb),
        in_specs=[
            pl.BlockSpec(blk(gw), imap(base)),
            pl.BlockSpec(blk(gw), imap(base + 1)),
            pl.BlockSpec(blk(gw), imap(base + 2)),
            pl.BlockSpec(tables.shape, lambda bi, jr, ub: (0, 0, 0, 0)),
            pl.BlockSpec((1, PAIR), lambda bi, jr, ub: (0, 0)),
            pl.BlockSpec((1, PAIR), lambda bi, jr, ub: (0, 0)),
        ],
        out_specs=(pl.BlockSpec(blk(gw), imap(0)), pl.BlockSpec(blk(LANES), imap(0))),
        scratch_shapes=[pltpu.VMEM((2, Q_BLOCK, gw), BF16), pltpu.VMEM((2, Q_BLOCK, gw), BF16)],
        compiler_params=_params(("arbitrary", "arbitrary", "arbitrary")),
        name=f"attn_a_g{g}",
    )(pv, pv, pv, tables, gains(gq), gains(gk))
    return o.reshape(b, s, gw), lse.reshape(b, s, LANES)


MERGE_CHUNK = 128


def _merge_out_kernel(o0_ref, o1_ref, o2_ref, l0_ref, l1_ref, l2_ref, x_ref, g_ref, w_ref,
                      e_ref, out_ref, lun_ref, oun_ref, om_ref):
    tm = o0_ref.shape[1]
    n_slab = o0_ref.shape[2] // LANES

    @pl.when(pl.program_id(1) == 0)
    def _():
        for gi, (o_ref, l_ref) in enumerate(((o1_ref, l1_ref), (o2_ref, l2_ref))):
            r = A_DILATED[gi + 1][1]
            n = tm // r
            for jr in range(r):
                rows = slice(jr * n, (jr + 1) * n)
                lun_ref[gi, pl.ds(jr, n, stride=r), :] = l_ref[0, rows, :]
                for c in range(n_slab):
                    oun_ref[gi, c, pl.ds(jr, n, stride=r), :] = (
                        o_ref[0, rows, c * LANES:(c + 1) * LANES].astype(F32))
        expand = e_ref[...]

        def widen(w):
            hi = w.astype(BF16)
            lo = (w - hi.astype(F32)).astype(BF16)
            return _dot(hi, expand) + _dot(lo, expand)

        for k in range(tm // MERGE_CHUNK):
            rs = slice(k * MERGE_CHUNK, (k + 1) * MERGE_CHUNK)
            l0, l1, l2 = l0_ref[0, rs, :], lun_ref[0, rs, :], lun_ref[1, rs, :]
            m = jnp.maximum(jnp.maximum(l0, l1), l2)
            e0, e1, e2 = jnp.exp(l0 - m), jnp.exp(l1 - m), jnp.exp(l2 - m)
            inv = 1.0 / (e0 + e1 + e2)
            o1 = jnp.concatenate([oun_ref[0, c, rs, :] for c in range(n_slab)], axis=1)
            o2 = jnp.concatenate([oun_ref[1, c, rs, :] for c in range(n_slab)], axis=1)
            o = (widen(e0 * inv) * o0_ref[0, rs, :].astype(F32)
                 + widen(e1 * inv) * o1 + widen(e2 * inv) * o2)
            om_ref[rs, :] = o.astype(BF16)

    out_ref[0] = x_ref[0] + g_ref[0] * _dot(om_ref[...], w_ref[...])


def _merge_out(outs, lses, x, gate, w_out, tn=512):
    b, s, d = x.shape
    tm = PERM_TILE
    spt = s // tm
    rmap = lambda i, j: (i // spt, i % spt, 0)
    cmap = lambda i, j: (i // spt, i % spt, j)
    expand = ((jnp.arange(d)[None, :] // HEAD_DIM == jnp.arange(LANES)[:, None])
              & (jnp.arange(LANES)[:, None] < N_HEADS)).astype(BF16)
    o_spec = pl.BlockSpec((1, tm, d), rmap)
    l_spec = pl.BlockSpec((1, tm, LANES), rmap)
    return pl.pallas_call(
        _merge_out_kernel,
        out_shape=jax.ShapeDtypeStruct((b, s, d), F32),
        grid=(b * spt, d // tn),
        in_specs=[o_spec, o_spec, o_spec, l_spec, l_spec, l_spec,
                  pl.BlockSpec((1, tm, tn), cmap),
                  pl.BlockSpec((1, 1, tn), lambda i, j: (i // spt, 0, j)),
                  pl.BlockSpec((d, tn), lambda i, j: (0, j)),
                  pl.BlockSpec((LANES, d), lambda i, j: (0, 0))],
        out_specs=pl.BlockSpec((1, tm, tn), cmap),
        scratch_shapes=[pltpu.VMEM((2, tm, LANES), F32),
                        pltpu.VMEM((2, d // LANES, tm, LANES), F32),
                        pltpu.VMEM((tm, d), BF16)],
        compiler_params=_params(("arbitrary", "arbitrary")),
        name="merge_out",
    )(*outs, *lses, x, gate, w_out, expand)


def _out_kernel(o_ref, x_ref, g_ref, w_ref, out_ref):
    out_ref[0] = x_ref[0] + g_ref[0] * _dot(o_ref[0], w_ref[...])


def _out_proj(o, x, gate, w_out, tm=512):
    b, s, d = x.shape
    spt = s // tm
    xmap = lambda i: (i // spt, i % spt, 0)
    return pl.pallas_call(
        _out_kernel,
        out_shape=jax.ShapeDtypeStruct((b, s, d), F32),
        grid=(b * spt,),
        in_specs=[pl.BlockSpec((1, tm, d), xmap),
                  pl.BlockSpec((1, tm, d), xmap),
                  pl.BlockSpec((1, 1, d), lambda i: (i // spt, 0, 0)),
                  pl.BlockSpec((d, d), lambda i: (0, 0))],
        out_specs=pl.BlockSpec((1, tm, d), xmap),
        compiler_params=_params(("arbitrary",)),
        name="out_proj",
    )(o, x, gate, w_out)


def _cum_kernel(f_ref, fb_ref, tri_ref, o_ref, carry_ref):
    @pl.when(pl.program_id(1) == 0)
    def _():
        carry_ref[...] = jnp.zeros_like(carry_ref)

    z = f_ref[0] + fb_ref[...]
    logf = jnp.minimum(z, 0.0) - jnp.log(1.0 + jnp.exp(-jnp.abs(z)))
    hi, mid, lo = _split3(logf)
    tri = tri_ref[...]
    cum = _dot(tri, hi) + _dot(tri, mid) + _dot(tri, lo) + carry_ref[...]
    o_ref[0] = cum
    carry_ref[...] = cum[-1:, :]


def _cum_forget(fg, f_bias, tc=512):
    b, s, w = fg.shape
    fb = jnp.pad(f_bias, (0, w - f_bias.shape[0])).reshape(1, w)
    tri = (jnp.arange(tc)[:, None] >= jnp.arange(tc)[None, :]).astype(BF16)
    return pl.pallas_call(
        _cum_kernel,
        out_shape=jax.ShapeDtypeStruct((b, s, w), F32),
        grid=(b, s // tc),
        in_specs=[pl.BlockSpec((1, tc, w), lambda i, j: (i, j, 0)),
                  pl.BlockSpec((1, w), lambda i, j: (0, 0)),
                  pl.BlockSpec((tc, tc), lambda i, j: (0, 0))],
        out_specs=pl.BlockSpec((1, tc, w), lambda i, j: (i, j, 0)),
        scratch_shapes=[pltpu.VMEM((1, w), F32)],
        compiler_params=_params(("arbitrary", "arbitrary")),
        name="cum_forget",
    )(fg, fb, tri)


def _gate_columns(cum_col, lead):
    rows = cum_col.shape[0]
    lane = lax.broadcasted_iota(jnp.int32, (rows, HEAD_DIM), 1)
    hi, mid, lo = (p.astype(F32) for p in _split3(cum_col if lead else -cum_col))
    off = 0 if lead else 3
    ext = jnp.where(lane < 6, 1.0, 0.0)
    ext = jnp.where(lane == off, hi, ext)
    ext = jnp.where(lane == off + 1, mid, ext)
    ext = jnp.where(lane == off + 2, lo, ext)
    return ext.astype(BF16)


def _pick_lane(tile, idx):
    lane = lax.broadcasted_iota(jnp.int32, tile.shape, 1)
    return jnp.sum(jnp.where(lane == idx, tile, 0.0), axis=-1, keepdims=True)


ONES_ROWS = 16
FOX_HEADS_PER_STEP = 8


def _fox_kernel(q_ref, k_ref, v_ref, cq_ref, ck_ref, gq_ref, gk_ref, o_ref, ka_ref, vt_ref,
                qat_ref, s_ref, p_ref, acc_ref, *, tq, tk):
    hg = pl.program_id(1)
    qi = pl.program_id(2)
    n_kt = vt_ref.shape[1]
    nh = ka_ref.shape[0]
    heads = range(nh)

    @pl.when(qi == 0)
    def _():
        gk = gk_ref[...]
        for hh in heads:
            sl = slice(hh * HEAD_DIM, (hh + 1) * HEAD_DIM)
            ka_ref[hh, :, :HEAD_DIM] = _headnorm(k_ref[0, :, sl], gk).astype(BF16)
            ck = _pick_lane(ck_ref[0], nh * hg + hh) * LOG2E
            ka_ref[hh, :, HEAD_DIM:] = _gate_columns(ck, lead=False)
        for pr in range(nh // 2):
            vt = v_ref[0, :, pr * PAIR:(pr + 1) * PAIR].astype(F32).T.astype(BF16)
            for hh in (2 * pr, 2 * pr + 1):
                rows = slice((hh % 2) * HEAD_DIM, (hh % 2 + 1) * HEAD_DIM)
                for jj in range(n_kt):
                    vt_ref[hh, jj, :HEAD_DIM] = vt[rows, jj * tk:(jj + 1) * tk]
                    vt_ref[hh, jj, HEAD_DIM:] = jnp.ones((ONES_ROWS, tk), BF16)

    gq = gq_ref[...] * (ATTN_SCALE * LOG2E)
    for hh in heads:
        sl = slice(hh * HEAD_DIM, (hh + 1) * HEAD_DIM)
        qn = _headnorm(q_ref[0, :, sl], gq)
        cq = _pick_lane(cq_ref[0], nh * hg + hh) * LOG2E
        qa = jnp.concatenate([qn, _gate_columns(cq, lead=True).astype(F32)], axis=1)
        qat_ref[hh] = qa.T.astype(BF16)
    key = lax.broadcasted_iota(jnp.int32, (tk, tq), 0)
    qry = lax.broadcasted_iota(jnp.int32, (tk, tq), 1)
    ahead = key - qry

    def scores(t, hh, diagonal):
        start = pl.multiple_of(t * tk, tk)
        st = _dot(ka_ref[hh, pl.ds(start, tk), :], qat_ref[hh])
        if diagonal is None:
            st = jnp.where(ahead > jnp.where(t == qi, 0, tk), NEG_INF, st)
        elif diagonal:
            st = jnp.where(ahead > 0, NEG_INF, st)
        s_ref[hh] = st
        return jnp.max(st, axis=0, keepdims=True)

    def pv(t, hh, alpha_prev):
        acc_ref[hh] = alpha_prev * acc_ref[hh] + _dot(vt_ref[hh, t], p_ref[hh])

    def trip(i, carry, next_diagonal):
        out = []
        for hh in heads:
            m, alpha_prev, mx = carry[hh]
            pv(jnp.maximum(i - 1, 0), hh, alpha_prev)
            m_new = jnp.maximum(m, mx)
            alpha = jnp.exp2(m - m_new)
            p_ref[hh] = jnp.exp2(s_ref[hh] - m_new).astype(BF16)
            if next_diagonal is not None:
                mx = scores(i + 1, hh, next_diagonal)
            out.append((m_new, alpha, mx))
        return tuple(out)

    p_ref[...] = jnp.zeros_like(p_ref)
    acc_ref[...] = jnp.zeros_like(acc_ref)
    carry = tuple((jnp.full((1, tq), NEG_INF, F32), jnp.ones((1, tq), F32),
                   scores(0, hh, None)) for hh in heads)
    carry = lax.fori_loop(0, jnp.maximum(qi - 1, 0),
                          functools.partial(trip, next_diagonal=False), carry)
    carry = lax.fori_loop(jnp.maximum(qi - 1, 0), qi,
                          functools.partial(trip, next_diagonal=True), carry)
    carry = trip(qi, carry, None)
    for pr in range(nh // 2):
        o_t = []
        for hh in (2 * pr, 2 * pr + 1):
            pv(qi, hh, carry[hh][1])
            acc = acc_ref[hh]
            o_t.append(acc[:HEAD_DIM] / acc[HEAD_DIM:HEAD_DIM + 1])
        o_ref[0, :, pr * PAIR:(pr + 1) * PAIR] = (
            jnp.concatenate(o_t, axis=0).T.astype(BF16))


def _fox_attention(qkv, cum, gq, gk, tq=256):
    b, s, _ = qkv.shape
    tk = tq
    nq = s // tq
    nh = FOX_HEADS_PER_STEP
    groups = N_HEADS // nh
    gw = nh * HEAD_DIM
    vrows = HEAD_DIM + ONES_ROWS
    return pl.pallas_call(
        functools.partial(_fox_kernel, tq=tq, tk=tk),
        out_shape=jax.ShapeDtypeStruct((b, s, N_HEADS * HEAD_DIM), BF16),
        grid=(b, groups, nq),
        in_specs=[
            pl.BlockSpec((1, tq, gw), lambda bi, hg, qi: (bi, qi, hg)),
            pl.BlockSpec((1, s, gw), lambda bi, hg, qi: (bi, 0, groups + hg)),
            pl.BlockSpec((1, s, gw), lambda bi, hg, qi: (bi, 0, 2 * groups + hg)),
            pl.BlockSpec((1, tq, LANES), lambda bi, hg, qi: (bi, qi, 0)),
            pl.BlockSpec((1, s, LANES), lambda bi, hg, qi: (bi, 0, 0)),
            pl.BlockSpec((1, HEAD_DIM), lambda bi, hg, qi: (0, 0)),
            pl.BlockSpec((1, HEAD_DIM), lambda bi, hg, qi: (0, 0)),
        ],
        out_specs=pl.BlockSpec((1, tq, gw), lambda bi, hg, qi: (bi, qi, hg)),
        scratch_shapes=[pltpu.VMEM((nh, s, 2 * HEAD_DIM), BF16),
                        pltpu.VMEM((nh, s // tk, vrows, tk), BF16),
                        pltpu.VMEM((nh, 2 * HEAD_DIM, tq), BF16),
                        pltpu.VMEM((nh, tk, tq), F32),
                        pltpu.VMEM((nh, tk, tq), BF16),
                        pltpu.VMEM((nh, vrows, tq), F32)],
        compiler_params=_params(("arbitrary", "arbitrary", "arbitrary")),
        name="fox_attn",
    )(qkv, qkv, qkv, cum, cum, gq.reshape(1, HEAD_DIM), gk.reshape(1, HEAD_DIM))


ROUTER_LANES = LANES


def _route(logits):
    lane = lax.broadcasted_iota(jnp.int32, logits.shape, 1)
    lane_f = lane.astype(F32)
    big = float(ROUTER_LANES)
    is_g = lane < N_GROUPS
    gl = jnp.where(is_g, logits, -jnp.inf)
    gmax = jnp.max(gl, axis=-1, keepdims=True)
    gsum = jnp.sum(jnp.where(is_g, jnp.exp(logits - gmax), 0.0), axis=-1, keepdims=True)
    g_w = 1.0 / gsum
    gidx = jnp.min(jnp.where(gl == gmax, lane_f, big), axis=-1, keepdims=True)
    lo = N_GROUPS + EXPERTS_PER_GROUP * gidx
    in_grp = (lane_f >= lo) & (lane_f < lo + EXPERTS_PER_GROUP)
    el = jnp.where(in_grp, logits, -jnp.inf)
    t1 = jnp.max(el, axis=-1, keepdims=True)
    i1 = jnp.min(jnp.where(el == t1, lane_f, big), axis=-1, keepdims=True)
    el2 = jnp.where(lane_f == i1, -jnp.inf, el)
    t2 = jnp.max(el2, axis=-1, keepdims=True)
    i2 = jnp.min(jnp.where(el2 == t2, lane_f, big), axis=-1, keepdims=True)
    e2 = jnp.exp(t2 - t1)
    w1 = g_w / (1.0 + e2)
    w2 = w1 * e2
    gates = jnp.where(lane_f == i1, w1, jnp.where(lane_f == i2, w2, 0.0))
    return jnp.where(lane == 0, gidx, gates)


MOE_TILE = 512
HALF_D = D_MODEL // 2
ROW_WORDS = HALF_D + ROUTER_LANES
U32 = jnp.uint32


def _pack_pairs(x):
    n = x.shape[1] // 2
    hi = pltpu.bitcast(x[:, :n].astype(BF16).astype(F32), U32)
    lo = pltpu.bitcast(x[:, n:].astype(BF16).astype(F32), U32)
    return hi | (lo >> 16)


def _unpack_pairs(w):
    hi = pltpu.bitcast(w & U32(0xFFFF0000), F32)
    lo = pltpu.bitcast(w << 16, F32)
    return jnp.concatenate([hi, lo], axis=1)


SUBLANES = 8


def _moe_route_kernel(x_ref, gam_ref, sc_ref, sh_ref, wr_ref, br_ref, tri_ref, o_ref, meta_ref,
                      cnt_ref):
    h = _modnorm(x_ref[0], gam_ref[...], sc_ref[0], sh_ref[0])
    h_hi = h.astype(BF16)
    h_lo = (h - h_hi.astype(F32)).astype(BF16)
    wr = wr_ref[...]
    w_hi = wr.astype(BF16)
    w_lo = (wr - w_hi.astype(F32)).astype(BF16)
    logits = _dot(h_hi, w_hi) + _dot(h_hi, w_lo) + _dot(h_lo, w_hi) + br_ref[...]
    route = _route(logits)
    tm = route.shape[0]

    @pl.when(pl.program_id(0) == 0)
    def _():
        cnt_ref[...] = jnp.zeros_like(cnt_ref)

    lane = lax.broadcasted_iota(jnp.int32, route.shape, 1)
    member = jnp.where((lane.astype(F32) == route[:, 0:1]) & (lane < N_GROUPS), 1.0, 0.0)
    incl = _dot(tri_ref[...], member.astype(BF16))
    rank = jnp.sum(member * (incl - member + cnt_ref[...]), axis=-1, keepdims=True)
    cnt_ref[...] = cnt_ref[...] + incl[tm - 1:tm, :]
    meta = jnp.where(lane == 0, route[:, 0:1], jnp.where(lane == 1, rank, 0.0))
    meta_ref[...] = meta.T[:SUBLANES, :]

    packed = _pack_pairs(h)
    for c in range(HALF_D // LANES):
        o_ref[pl.ds(c, tm, stride=SUBLANES), :] = packed[:, c * LANES:(c + 1) * LANES]
    o_ref[pl.ds(HALF_D // LANES, tm, stride=SUBLANES), :] = pltpu.bitcast(route, U32)
    for c in range(HALF_D // LANES + 1, SUBLANES):
        o_ref[pl.ds(c, tm, stride=SUBLANES), :] = jnp.zeros((tm, LANES), U32)


def _token_copy(src, src_tok, dst, dst_tok, sem):
    s0 = pl.multiple_of(src_tok * SUBLANES, SUBLANES)
    d0 = pl.multiple_of(dst_tok * SUBLANES, SUBLANES)
    return pltpu.make_async_copy(src.at[pl.ds(s0, SUBLANES), :], dst.at[pl.ds(d0, SUBLANES), :], sem)


def _moe_expert_kernel(tg_ref, nt_ref, nv_ref, dest_ref, hx_hbm, wg_ref, wu_ref, wd_ref, e_ref,
                       y_hbm, buf, ybuf, h_ref, g_ref, y_ref, src_ref, gsem, ssem):
    t = pl.program_id(0)
    n_used = nt_ref[0]
    slot = t % 2
    rows = buf.shape[1] // SUBLANES
    unroll = 8

    @pl.when(t == 0)
    def _():
        def clear(p, c):
            src_ref[p] = 0
            return c
        lax.fori_loop(0, src_ref.shape[0], clear, 0, unroll=unroll)

        def invert(tok, c):
            src_ref[dest_ref[tok]] = tok
            return c
        lax.fori_loop(0, dest_ref.shape[0], invert, 0, unroll=unroll)

    def rows_loop(lo, hi, body):
        chunks = (hi - lo) // unroll

        def chunk(k, c):
            for i in range(unroll):
                body(lo + k * unroll + i)
            return c
        lax.fori_loop(0, chunks, chunk, 0)

        def single(r, c):
            body(r)
            return c
        lax.fori_loop(lo + chunks * unroll, hi, single, 0)

    part_rows = rows // EXPERTS_PER_GROUP

    def gather_part(tile, sl, lo, hi):
        rows_loop(lo, hi, lambda r: _token_copy(
            hx_hbm, src_ref[tile * rows + r], buf.at[sl], r, gsem.at[sl]).start())

    def gather_wait(sl):
        rows_loop(0, rows, lambda r: _token_copy(hx_hbm, 0, buf.at[sl], r, gsem.at[sl]).wait())

    def scatter_part(tile, sl, lo, hi):
        hi = jnp.minimum(hi, nv_ref[tile])
        lo = jnp.minimum(lo, hi)
        rows_loop(lo, hi, lambda r: _token_copy(
            ybuf.at[sl], r, y_hbm, src_ref[tile * rows + r], ssem.at[sl]).start(priority=1))

    def scatter_wait(tile, sl):
        rows_loop(0, nv_ref[tile],
                  lambda r: _token_copy(ybuf.at[sl], r, y_hbm, 0, ssem.at[sl]).wait())

    @pl.when(t == 0)
    def _():
        gather_part(0, 0, 0, rows)

    @pl.when(t < n_used)
    def _():
        gather_wait(slot)
        slab = lambda c: buf[slot, pl.ds(c, rows, stride=SUBLANES), :]
        n_pair_slabs = HALF_D // LANES
        w = jnp.concatenate([slab(c) for c in range(n_pair_slabs)], axis=1)
        h_ref[...] = _unpack_pairs(w).astype(BF16)
        gates = pltpu.bitcast(slab(n_pair_slabs), F32)
        g_hi = gates.astype(BF16)
        g_ref[0] = g_hi
        g_ref[1] = (gates - g_hi.astype(F32)).astype(BF16)

    for e in range(EXPERTS_PER_GROUP):
        lo, hi = e * part_rows, (e + 1) * part_rows

        @pl.when(t + 1 < n_used)
        def _():
            gather_part(t + 1, 1 - slot, lo, hi)

        @pl.when((t >= 1) & (t <= n_used))
        def _():
            scatter_part(t - 1, 1 - slot, lo, hi)

        @pl.when(t < n_used)
        def _():
            h = h_ref[...]
            a = _dot(h, wg_ref[e])
            u = _dot(h, wu_ref[e])
            spread = e_ref[0, :, e * EXPERT_FF:(e + 1) * EXPERT_FF]
            gexp = _dot(g_ref[0], spread) + _dot(g_ref[1], spread)
            hid = a * (1.0 / (1.0 + jnp.exp(-a))) * u * gexp
            part = _dot(hid.astype(BF16), wd_ref[e])
            if e == 0:
                y_ref[...] = part
            else:
                y_ref[...] += part

    @pl.when(t < n_used)
    def _():
        @pl.when(t >= 2)
        def _():
            scatter_wait(t - 2, slot)

        for c in range(y_ref.shape[1] // LANES):
            ybuf[slot, pl.ds(c, rows, stride=SUBLANES), :] = y_ref[:, c * LANES:(c + 1) * LANES]

    @pl.when(t == pl.num_programs(0) - 1)
    def _():
        for back in (2, 1):
            scatter_wait(n_used - back, (n_used - back) % 2)


def _moe_combine_kernel(x_ref, g_ref, y_ref, o_ref):
    tm = x_ref.shape[1]
    y = jnp.concatenate([y_ref[pl.ds(c, tm, stride=SUBLANES), :]
                         for c in range(x_ref.shape[2] // LANES)], axis=1)
    o_ref[0] = x_ref[0] + g_ref[0] * y


def _moe(x, gamma, sc, sh, gate, w_router, b_router, w_gate, w_up, w_down):
    b, s, d = x.shape
    n = b * s
    tm = PERM_TILE
    spt = s // tm
    rows = MOE_TILE
    n_tiles = n // rows + N_GROUPS
    p_rows = n_tiles * rows
    bmap = lambda i: (i // spt, 0, 0)
    tri = (jnp.arange(tm)[:, None] >= jnp.arange(tm)[None, :]).astype(BF16)
    hx, meta = pl.pallas_call(
        _moe_route_kernel,
        out_shape=(jax.ShapeDtypeStruct((n * SUBLANES, LANES), U32),
                   jax.ShapeDtypeStruct((SUBLANES, n), F32)),
        grid=(b * spt,),
        in_specs=[
            pl.BlockSpec((1, tm, d), lambda i: (i // spt, i % spt, 0)),
            pl.BlockSpec((1, d), lambda i: (0, 0)),
            pl.BlockSpec((1, 1, d), bmap),
            pl.BlockSpec((1, 1, d), bmap),
            pl.BlockSpec((d, ROUTER_LANES), lambda i: (0, 0)),
            pl.BlockSpec((1, ROUTER_LANES), lambda i: (0, 0)),
            pl.BlockSpec((tm, tm), lambda i: (0, 0)),
        ],
        out_specs=(pl.BlockSpec((tm * SUBLANES, LANES), lambda i: (i, 0)),
                   pl.BlockSpec((SUBLANES, tm), lambda i: (0, i))),
        scratch_shapes=[pltpu.VMEM((1, LANES), F32)],
        compiler_params=_params(("arbitrary",)),
        name="moe_route",
    )(x, gamma.reshape(1, d), sc, sh, w_router, b_router, tri)

    gidx = meta[0].astype(jnp.int32)
    rank = meta[1].astype(jnp.int32)
    onehot = (gidx[:, None] == jnp.arange(N_GROUPS)[None, :]).astype(jnp.int32)
    count = jnp.sum(onehot, axis=0)
    padded = (count + rows - 1) // rows * rows
    ends = jnp.cumsum(padded)
    dest = jnp.sum(onehot * (ends - padded)[None, :], axis=1) + rank
    n_used = (ends[-1] // rows).reshape(1).astype(jnp.int32)
    tile_start = jnp.arange(n_tiles + 1, dtype=jnp.int32) * rows
    tile_group = jnp.minimum(jnp.sum(ends[None, :] <= tile_start[:, None], axis=1),
                             N_GROUPS - 1).astype(jnp.int32)
    seg_valid_end = ends - padded + count
    n_valid = jnp.clip(jnp.sum(jnp.where(jnp.arange(N_GROUPS)[None, :] == tile_group[:, None],
                                         seg_valid_end[None, :], 0), axis=1) - tile_start,
                       0, rows).astype(jnp.int32)

    ff = EXPERTS_PER_GROUP * EXPERT_FF
    lanes = jnp.arange(ROUTER_LANES)[None, :, None]
    expert = N_GROUPS + EXPERTS_PER_GROUP * jnp.arange(N_GROUPS)[:, None, None] \
        + jnp.arange(ff)[None, None, :] // EXPERT_FF
    expand = (lanes == expert).astype(BF16)
    wmap = lambda t, tg, nt, nv, ds: (tg[t], 0, 0)
    y_tok = pl.pallas_call(
        _moe_expert_kernel,
        out_shape=jax.ShapeDtypeStruct((n * SUBLANES, LANES), F32),
        grid_spec=pltpu.PrefetchScalarGridSpec(
            num_scalar_prefetch=4,
            grid=(n_tiles + 1,),
            in_specs=[
                pl.BlockSpec(memory_space=pl.ANY),
                pl.BlockSpec((EXPERTS_PER_GROUP, d, EXPERT_FF), wmap),
                pl.BlockSpec((EXPERTS_PER_GROUP, d, EXPERT_FF), wmap),
                pl.BlockSpec((EXPERTS_PER_GROUP, EXPERT_FF, d), wmap),
                pl.BlockSpec((1, ROUTER_LANES, ff), wmap),
            ],
            out_specs=pl.BlockSpec(memory_space=pl.ANY),
            scratch_shapes=[pltpu.VMEM((2, rows * SUBLANES, LANES), U32),
                            pltpu.VMEM((2, rows * SUBLANES, LANES), F32),
                            pltpu.VMEM((rows, d), BF16),
                            pltpu.VMEM((2, rows, ROUTER_LANES), BF16),
                            pltpu.VMEM((rows, d), F32),
                            pltpu.SMEM((p_rows,), jnp.int32),
                            pltpu.SemaphoreType.DMA((2,)),
                            pltpu.SemaphoreType.DMA((2,))],
        ),
        compiler_params=_params(("arbitrary",)),
        name="moe_experts",
    )(tile_group, n_used, n_valid, dest, hx, w_gate, w_up, w_down, expand)

    return pl.pallas_call(
        _moe_combine_kernel,
        out_shape=jax.ShapeDtypeStruct((b, s, d), F32),
        grid=(b * spt,),
        in_specs=[pl.BlockSpec((1, tm, d), lambda i: (i // spt, i % spt, 0)),
                  pl.BlockSpec((1, 1, d), bmap),
                  pl.BlockSpec((tm * SUBLANES, LANES), lambda i: (i, 0))],
        out_specs=pl.BlockSpec((1, tm, d), lambda i: (i // spt, i % spt, 0)),
        compiler_params=_params(("arbitrary",)),
        name="moe_combine",
    )(x, gate, y_tok)


def kernel(x, c, w_ada, b_ada, norm_mix, norm_ffn, rel_bias, a_w_in, a_w_out, a_q_norm,
           a_k_norm, b_w_in, b_f_bias, b_w_out, b_q_norm, b_k_norm, router_group_w,
           router_group_b, router_expert_w, router_expert_b, w_gate, w_up, w_down):
    b, s, d = x.shape
    depth = w_ada.shape[0]
    mod = _ada(c, w_ada, b_ada).reshape(depth, b, 6, 1, d)
    pad_r = ROUTER_LANES - N_GROUPS - N_EXPERTS
    for i in range(depth):
        sh_m, sc_m, g_m, sh_f, sc_f, g_f = (mod[i, :, k] for k in range(6))
        j = i // 2
        if i % 2 == 0:
            proj = _normproj_a(x, norm_mix[i], sc_m, sh_m, a_w_in[j].astype(BF16))
            outs, lses = [], []
            for g, (window, dil) in enumerate(A_DILATED):
                tab = rel_bias[:, g * N_HEADS:(g + 1) * N_HEADS]
                o, l = _attn_a_group(proj, g, tab, a_q_norm[j], a_k_norm[j], window, dil)
                outs.append(o)
                lses.append(l)
            x = _merge_out(outs, lses, x, g_m, a_w_out[j].astype(BF16))
        else:
            n_qkv = 3 * N_HEADS * HEAD_DIM
            w_in = b_w_in[j]
            w_f = jnp.pad(w_in[:, n_qkv:], ((0, 0), (0, LANES - N_HEADS))).astype(BF16)
            qkv, fg = _normproj(x, norm_mix[i], sc_m, sh_m, w_in[:, :n_qkv].astype(BF16), w_f)
            cum = _cum_forget(fg, b_f_bias[j])
            o = _fox_attention(qkv, cum, b_q_norm[j], b_k_norm[j])
            x = _out_proj(o, x, g_m, b_w_out[j].astype(BF16))
        w_router = jnp.pad(jnp.concatenate([router_group_w[i], router_expert_w[i]], axis=1),
                           ((0, 0), (0, pad_r)))
        b_router = jnp.pad(jnp.concatenate([router_group_b[i], router_expert_b[i]]),
                           (0, pad_r)).reshape(1, ROUTER_LANES)
        x = _moe(x, norm_ffn[i], sc_f, sh_f, g_f, w_router, b_router,
                 w_gate[i].astype(BF16), w_up[i].astype(BF16), w_down[i].astype(BF16))
    return x
```

```python
import functools
import math

import jax
import jax.numpy as jnp
import numpy as np
from jax import lax
from jax.experimental import pallas as pl
from jax.experimental.pallas import tpu as pltpu

F32 = jnp.float32
BF16 = jnp.bfloat16

D_MODEL = 1024
HEAD_DIM = 64
N_HEADS = 16
ATTN_SCALE = HEAD_DIM ** -0.5
A_DILATED = ((128, 1), (512, 4), (2048, 16))
Q_BLOCK = 128
REL_BUCKETS = 32
REL_MAX_DIST = 2048
N_GROUPS = 4
EXPERTS_PER_GROUP = 4
N_EXPERTS = 16
EXPERT_FF = 256
EPS = 1e-6
NEG_INF = -1e30

LANES = 128
VMEM_LIMIT = 48 * 1024 * 1024
LOG2E = 1.4426950408889634
LN2 = 0.6931471805599453
PERM_TILE = 1024
PAIR = 2 * HEAD_DIM
N_PAIRS = N_HEADS // 2


def _params(sem):
    return pltpu.CompilerParams(dimension_semantics=sem, vmem_limit_bytes=VMEM_LIMIT)


def _split3(x):
    hi = x.astype(BF16)
    r1 = x - hi.astype(F32)
    mid = r1.astype(BF16)
    lo = (r1 - mid.astype(F32)).astype(BF16)
    return hi, mid, lo


def _dot(a, b):
    return jnp.dot(a, b, preferred_element_type=F32)


def _dot_nt(a, b):
    return lax.dot_general(a, b, (((1,), (1,)), ((), ())), preferred_element_type=F32)


def _ada_kernel(c_ref, w_ref, b_ref, o_ref):
    c = c_ref[...]
    a = c * (1.0 / (1.0 + jnp.exp(-c)))
    o_ref[0] = _dot(a.astype(BF16), w_ref[0].astype(BF16)) + b_ref[0]


def _ada(c, w_ada, b_ada):
    depth, d, n6 = w_ada.shape
    b = c.shape[0]
    rows = 8
    c_pad = jnp.pad(c, ((0, rows - b), (0, 0)))
    tn = 1536
    out = pl.pallas_call(
        _ada_kernel,
        out_shape=jax.ShapeDtypeStruct((depth, rows, n6), F32),
        grid=(depth, n6 // tn),
        in_specs=[
            pl.BlockSpec((rows, d), lambda i, j: (0, 0)),
            pl.BlockSpec((1, d, tn), lambda i, j: (i, 0, j)),
            pl.BlockSpec((1, 1, tn), lambda i, j: (i, 0, j)),
        ],
        out_specs=pl.BlockSpec((1, rows, tn), lambda i, j: (i, 0, j)),
        compiler_params=_params(("arbitrary", "arbitrary")),
        name="ada_mod",
    )(c_pad, w_ada, b_ada.reshape(depth, 1, n6))
    return out[:, :b]


def _modnorm(x, gamma, sc, sh):
    ms = jnp.mean(x * x, axis=-1, keepdims=True)
    y = x * lax.rsqrt(ms + EPS) * gamma
    return y * (1.0 + sc) + sh


def _normproj_kernel(x_ref, g_ref, sc_ref, sh_ref, w_ref, o_ref, h_ref):
    @pl.when(pl.program_id(1) == 0)
    def _():
        h_ref[...] = _modnorm(x_ref[0], g_ref[...], sc_ref[0], sh_ref[0]).astype(BF16)

    o_ref[0] = _dot(h_ref[...], w_ref[...]).astype(BF16)


def _normproj_gate_kernel(x_ref, g_ref, sc_ref, sh_ref, w_ref, wf_ref, o_ref, f_ref, h_ref):
    @pl.when(pl.program_id(1) == 0)
    def _():
        h = _modnorm(x_ref[0], g_ref[...], sc_ref[0], sh_ref[0]).astype(BF16)
        h_ref[...] = h
        f_ref[0] = _dot(h, wf_ref[...])

    o_ref[0] = _dot(h_ref[...], w_ref[...]).astype(BF16)


def _normproj(x, gamma, sc, sh, w, w_f=None, tm=1024, tn=1536):
    b, s, d = x.shape
    ncols = w.shape[1]
    spt = s // tm
    grid = (b * spt, ncols // tn)
    xmap = lambda i, j: (i // spt, i % spt, 0)
    bmap = lambda i, j: (i // spt, 0, 0)
    in_specs = [
        pl.BlockSpec((1, tm, d), xmap),
        pl.BlockSpec((1, d), lambda i, j: (0, 0)),
        pl.BlockSpec((1, 1, d), bmap),
        pl.BlockSpec((1, 1, d), bmap),
        pl.BlockSpec((d, tn), lambda i, j: (0, j)),
    ]
    o_spec = pl.BlockSpec((1, tm, tn), lambda i, j: (i // spt, i % spt, j))
    scratch = [pltpu.VMEM((tm, d), BF16)]
    if w_f is None:
        return pl.pallas_call(
            _normproj_kernel,
            out_shape=jax.ShapeDtypeStruct((b, s, ncols), BF16),
            grid=grid, in_specs=in_specs, out_specs=o_spec, scratch_shapes=scratch,
            compiler_params=_params(("arbitrary", "arbitrary")),
            name="normproj",
        )(x, gamma.reshape(1, d), sc, sh, w)
    in_specs.append(pl.BlockSpec((d, LANES), lambda i, j: (0, 0)))
    f_spec = pl.BlockSpec((1, tm, LANES), xmap)
    return pl.pallas_call(
        _normproj_gate_kernel,
        out_shape=(jax.ShapeDtypeStruct((b, s, ncols), BF16),
                   jax.ShapeDtypeStruct((b, s, LANES), F32)),
        grid=grid, in_specs=in_specs, out_specs=(o_spec, f_spec), scratch_shapes=scratch,
        compiler_params=_params(("arbitrary", "arbitrary")),
        name="normproj_gate",
    )(x, gamma.reshape(1, d), sc, sh, w, w_f)


def _normproj_a_kernel(x_ref, g_ref, sc_ref, sh_ref, w_ref, o_ref, hf_ref, hp_ref):
    tm = x_ref.shape[1]
    n_slab = x_ref.shape[2] // LANES

    @pl.when(pl.program_id(1) == 0)
    def _():
        h = _modnorm(x_ref[0], g_ref[...], sc_ref[0], sh_ref[0])
        hp_ref[0] = h.astype(BF16)
        for c in range(n_slab):
            hf_ref[c] = h[:, c * LANES:(c + 1) * LANES]
        for g, (_, r) in enumerate(A_DILATED):
            if r == 1:
                continue
            n = tm // r
            for jr in range(r):
                for c in range(n_slab):
                    hp_ref[g, jr * n:(jr + 1) * n, c * LANES:(c + 1) * LANES] = (
                        hf_ref[c, pl.ds(jr, n, stride=r), :].astype(BF16))

    group = (pl.program_id(1) * w_ref.shape[1]) // (3 * N_HEADS * HEAD_DIM)
    o_ref[0] = _dot(hp_ref[group], w_ref[...]).astype(BF16)


def _normproj_a(x, gamma, sc, sh, w, tn=1536):
    b, s, d = x.shape
    tm = PERM_TILE
    ncols = w.shape[1]
    spt = s // tm
    bmap = lambda i, j: (i // spt, 0, 0)
    return pl.pallas_call(
        _normproj_a_kernel,
        out_shape=jax.ShapeDtypeStruct((b, s, ncols), BF16),
        grid=(b * spt, ncols // tn),
        in_specs=[
            pl.BlockSpec((1, tm, d), lambda i, j: (i // spt, i % spt, 0)),
            pl.BlockSpec((1, d), lambda i, j: (0, 0)),
            pl.BlockSpec((1, 1, d), bmap),
            pl.BlockSpec((1, 1, d), bmap),
            pl.BlockSpec((d, tn), lambda i, j: (0, j)),
        ],
        out_specs=pl.BlockSpec((1, tm, tn), lambda i, j: (i // spt, i % spt, j)),
        scratch_shapes=[pltpu.VMEM((d // LANES, tm, LANES), F32),
                        pltpu.VMEM((len(A_DILATED), tm, d), BF16)],
        compiler_params=_params(("arbitrary", "arbitrary")),
        name="normproj_a",
    )(x, gamma.reshape(1, d), sc, sh, w)


def _headnorm(t, gain):
    t = t.astype(F32)
    ms = jnp.mean(t * t, axis=-1, keepdims=True)
    return t * lax.rsqrt(ms + EPS) * gain


def _pairnorm(t, gain2, even):
    t = t.astype(F32)
    sq = t * t
    tot = jnp.sum(sq, axis=-1, keepdims=True)
    s_even = jnp.sum(jnp.where(even, sq, 0.0), axis=-1, keepdims=True)
    inv = jnp.where(even, lax.rsqrt(s_even * (1.0 / HEAD_DIM) + EPS),
                    lax.rsqrt((tot - s_even) * (1.0 / HEAD_DIM) + EPS))
    return t * inv * gain2


def _rows(ref, cols=slice(None)):
    if len(ref.shape) == 2:
        return ref[:, cols]
    return jnp.concatenate([ref[i, :, cols] for i in range(ref.shape[0])], axis=0)


def _store_rows(ref, cols, val):
    if len(ref.shape) == 2:
        ref[:, cols] = val
    else:
        n = ref.shape[1]
        for i in range(ref.shape[0]):
            ref[i, :, cols] = val[i * n:(i + 1) * n]


def _attn_a_kernel(q_ref, k_ref, v_ref, bias_ref, gq_ref, gk_ref, o_ref, lse_ref,
                   kn_ref, vv_ref):
    ub = pl.program_id(2)
    slot = ub % 2
    prev = 1 - slot

    @pl.when(ub == 0)
    def _():
        kn_ref[1] = jnp.zeros(kn_ref.shape[1:], BF16)
        vv_ref[1] = jnp.zeros(vv_ref.shape[1:], BF16)

    first = jnp.minimum(ub, 1)
    lane = lax.broadcasted_iota(jnp.int32, (Q_BLOCK, PAIR), 1)
    even = lane < HEAD_DIM
    lane_row = lax.broadcasted_iota(jnp.int32, (1, PAIR), 1)
    keep_even = jnp.where(lane_row < HEAD_DIM, 1.0, 0.0).astype(BF16)
    keep_odd = jnp.where(lane_row < HEAD_DIM, 0.0, 1.0).astype(BF16)
    gq2 = gq_ref[...] * (ATTN_SCALE * LOG2E)
    gk2 = gk_ref[...]
    for hp in range(N_PAIRS):
        cs = slice(hp * PAIR, (hp + 1) * PAIR)
        kn_ref[slot, :, cs] = _pairnorm(_rows(k_ref, cs), gk2, even).astype(BF16)
    vv_ref[slot] = _rows(v_ref)
    lse_tile = jnp.zeros((Q_BLOCK, LANES), F32)
    for hp in range(N_PAIRS):
        cs = slice(hp * PAIR, (hp + 1) * PAIR)
        qn = _pairnorm(_rows(q_ref, cs), gq2, even)
        qq = jnp.concatenate([jnp.where(even, qn, 0.0), jnp.where(even, 0.0, qn)],
                             axis=0).astype(BF16)
        kcat = jnp.concatenate([kn_ref[prev, :, cs], kn_ref[slot, :, cs]], axis=0)
        s = _dot_nt(qq, kcat) + bias_ref[first, hp]
        m = jnp.max(s, axis=-1, keepdims=True)
        p = jnp.exp2(s - m)
        l = jnp.sum(p, axis=-1, keepdims=True)
        pb = p.astype(BF16)
        vcat = jnp.concatenate([vv_ref[prev, :, cs], vv_ref[slot, :, cs]], axis=0)
        o = (_dot(pb[:Q_BLOCK], vcat * keep_even)
             + _dot(pb[Q_BLOCK:], vcat * keep_odd))
        inv = 1.0 / l
        o = o * jnp.where(even, inv[:Q_BLOCK], inv[Q_BLOCK:])
        _store_rows(o_ref, cs, o.astype(BF16))
        lse = m * LN2 + jnp.log(l)
        lse_tile = jnp.where(lane == 2 * hp, lse[:Q_BLOCK],
                             jnp.where(lane == 2 * hp + 1, lse[Q_BLOCK:], lse_tile))
    _store_rows(lse_ref, slice(None), lse_tile)


def _t5_bucket(dist):
    max_exact = REL_BUCKETS // 2
    d = np.maximum(dist, 0)
    large = max_exact + (np.log(np.maximum(d, 1).astype(np.float32) / max_exact)
                         / math.log(REL_MAX_DIST / max_exact)
                         * (REL_BUCKETS - max_exact)).astype(np.int32)
    large = np.minimum(large, REL_BUCKETS - 1)
    return np.where(d < max_exact, d, large)


def _bias_kernel(tab_ref, onehot_ref, o_ref):
    hi, mid, lo = _split3(tab_ref[...])
    oh = onehot_ref[...]
    o_ref[...] = (_dot(hi, oh) + _dot(mid, oh) + _dot(lo, oh)) * LOG2E


def _bias_tables(rel_bias_g, window, dilation):
    qi = np.arange(Q_BLOCK)[:, None]
    kj = np.arange(2 * Q_BLOCK)[None, :]
    dist = qi + Q_BLOCK - kj
    bucket = _t5_bucket(dist * dilation).reshape(1, -1)
    onehot = jnp.asarray(bucket == np.arange(REL_BUCKETS)[:, None], BF16)
    npos = onehot.shape[1]
    chunk = 4096
    bias = pl.pallas_call(
        _bias_kernel,
        out_shape=jax.ShapeDtypeStruct((N_HEADS, npos), F32),
        grid=(npos // chunk,),
        in_specs=[pl.BlockSpec((N_HEADS, REL_BUCKETS), lambda i: (0, 0)),
                  pl.BlockSpec((REL_BUCKETS, chunk), lambda i: (0, i))],
        out_specs=pl.BlockSpec((N_HEADS, chunk), lambda i: (0, i)),
        compiler_params=_params(("arbitrary",)),
        name="rel_bias_table",
    )(rel_bias_g.T.astype(F32), onehot)
    bias = bias.reshape(N_PAIRS, 2 * Q_BLOCK, 2 * Q_BLOCK)
    band = (dist >= 0) & (dist <= window // dilation)
    first = band & (qi - dist >= 0)
    band = np.concatenate([band, band], axis=0)[None]
    first = np.concatenate([first, first], axis=0)[None]
    return jnp.stack([jnp.where(first, bias, NEG_INF), jnp.where(band, bias, NEG_INF)], axis=0)


def _attn_a_group(proj, g, rel_bias_g, gq, gk, window, dilation):
    b, s, cols = proj.shape
    r = dilation
    nt = s // PERM_TILE
    n_t = PERM_TILE // r
    nb = s // r // Q_BLOCK
    gw = N_HEADS * HEAD_DIM
    tables = _bias_tables(rel_bias_g, window, dilation)
    base = g * 3
    if n_t >= Q_BLOCK:
        bpt = n_t // Q_BLOCK
        blk = lambda w: (None, None, None, Q_BLOCK, w)
        imap = lambda col: (lambda bi, jr, ub: (bi, ub // bpt, jr, ub % bpt, col))
    else:
        blk = lambda w: (None, Q_BLOCK // n_t, None, n_t, w)
        imap = lambda col: (lambda bi, jr, ub: (bi, ub, jr, 0, col))
    pv = proj.reshape(b, nt, r, n_t, cols)
    gains = lambda t: jnp.concatenate([t, t]).reshape(1, PAIR)
    o, lse = pl.pallas_call(
        _attn_a_kernel,
        out_shape=(jax.ShapeDtypeStruct((b, nt, r, n_t, gw), BF16),
                   jax.ShapeDtypeStruct((b, nt, r, n_t, LANES), F32)),
        grid=(b, r, nb),
        in_specs=[
            pl.BlockSpec(blk(gw), imap(base)),
            pl.BlockSpec(blk(gw), imap(base + 1)),
            pl.BlockSpec(blk(gw), imap(base + 2)),
            pl.BlockSpec(tables.shape, lambda bi, jr, ub: (0, 0, 0, 0)),
            pl.BlockSpec((1, PAIR), lambda bi, jr, ub: (0, 0)),
            pl.BlockSpec((1, PAIR), lambda bi, jr, ub: (0, 0)),
        ],
        out_specs=(pl.BlockSpec(blk(gw), imap(0)), pl.BlockSpec(blk(LANES), imap(0))),
        scratch_shapes=[pltpu.VMEM((2, Q_BLOCK, gw), BF16), pltpu.VMEM((2, Q_BLOCK, gw), BF16)],
        compiler_params=_params(("arbitrary", "arbitrary", "arbitrary")),
        name=f"attn_a_g{g}",
    )(pv, pv, pv, tables, gains(gq), gains(gk))
    return o.reshape(b, s, gw), lse.reshape(b, s, LANES)


MERGE_CHUNK = 128


def _merge_out_kernel(o0_ref, o1_ref, o2_ref, l0_ref, l1_ref, l2_ref, x_ref, g_ref, w_ref,
                      e_ref, out_ref, lun_ref, oun_ref, om_ref):
    tm = o0_ref.shape[1]
    n_slab = o0_ref.shape[2] // LANES

    @pl.when(pl.program_id(1) == 0)
    def _():
        for gi, (o_ref, l_ref) in enumerate(((o1_ref, l1_ref), (o2_ref, l2_ref))):
            r = A_DILATED[gi + 1][1]
            n = tm // r
            for jr in range(r):
                rows = slice(jr * n, (jr + 1) * n)
                lun_ref[gi, pl.ds(jr, n, stride=r), :] = l_ref[0, rows, :]
                for c in range(n_slab):
                    oun_ref[gi, c, pl.ds(jr, n, stride=r), :] = (
                        o_ref[0, rows, c * LANES:(c + 1) * LANES].astype(F32))
        expand = e_ref[...]

        def widen(w):
            hi = w.astype(BF16)
            lo = (w - hi.astype(F32)).astype(BF16)
            return _dot(hi, expand) + _dot(lo, expand)

        for k in range(tm // MERGE_CHUNK):
            rs = slice(k * MERGE_CHUNK, (k + 1) * MERGE_CHUNK)
            l0, l1, l2 = l0_ref[0, rs, :], lun_ref[0, rs, :], lun_ref[1, rs, :]
            m = jnp.maximum(jnp.maximum(l0, l1), l2)
            e0, e1, e2 = jnp.exp(l0 - m), jnp.exp(l1 - m), jnp.exp(l2 - m)
            inv = 1.0 / (e0 + e1 + e2)
            o1 = jnp.concatenate([oun_ref[0, c, rs, :] for c in range(n_slab)], axis=1)
            o2 = jnp.concatenate([oun_ref[1, c, rs, :] for c in range(n_slab)], axis=1)
            o = (widen(e0 * inv) * o0_ref[0, rs, :].astype(F32)
                 + widen(e1 * inv) * o1 + widen(e2 * inv) * o2)
            om_ref[rs, :] = o.astype(BF16)

    out_ref[0] = x_ref[0] + g_ref[0] * _dot(om_ref[...], w_ref[...])


def _merge_out(outs, lses, x, gate, w_out, tn=512):
    b, s, d = x.shape
    tm = PERM_TILE
    spt = s // tm
    rmap = lambda i, j: (i // spt, i % spt, 0)
    cmap = lambda i, j: (i // spt, i % spt, j)
    expand = ((jnp.arange(d)[None, :] // HEAD_DIM == jnp.arange(LANES)[:, None])
              & (jnp.arange(LANES)[:, None] < N_HEADS)).astype(BF16)
    o_spec = pl.BlockSpec((1, tm, d), rmap)
    l_spec = pl.BlockSpec((1, tm, LANES), rmap)
    return pl.pallas_call(
        _merge_out_kernel,
        out_shape=jax.ShapeDtypeStruct((b, s, d), F32),
        grid=(b * spt, d // tn),
        in_specs=[o_spec, o_spec, o_spec, l_spec, l_spec, l_spec,
                  pl.BlockSpec((1, tm, tn), cmap),
                  pl.BlockSpec((1, 1, tn), lambda i, j: (i // spt, 0, j)),
                  pl.BlockSpec((d, tn), lambda i, j: (0, j)),
                  pl.BlockSpec((LANES, d), lambda i, j: (0, 0))],
        out_specs=pl.BlockSpec((1, tm, tn), cmap),
        scratch_shapes=[pltpu.VMEM((2, tm, LANES), F32),
                        pltpu.VMEM((2, d // LANES, tm, LANES), F32),
                        pltpu.VMEM((tm, d), BF16)],
        compiler_params=_params(("arbitrary", "arbitrary")),
        name="merge_out",
    )(*outs, *lses, x, gate, w_out, expand)


def _out_kernel(o_ref, x_ref, g_ref, w_ref, out_ref):
    out_ref[0] = x_ref[0] + g_ref[0] * _dot(o_ref[0], w_ref[...])


def _out_proj(o, x, gate, w_out, tm=512):
    b, s, d = x.shape
    spt = s // tm
    xmap = lambda i: (i // spt, i % spt, 0)
    return pl.pallas_call(
        _out_kernel,
        out_shape=jax.ShapeDtypeStruct((b, s, d), F32),
        grid=(b * spt,),
        in_specs=[pl.BlockSpec((1, tm, d), xmap),
                  pl.BlockSpec((1, tm, d), xmap),
                  pl.BlockSpec((1, 1, d), lambda i: (i // spt, 0, 0)),
                  pl.BlockSpec((d, d), lambda i: (0, 0))],
        out_specs=pl.BlockSpec((1, tm, d), xmap),
        compiler_params=_params(("arbitrary",)),
        name="out_proj",
    )(o, x, gate, w_out)


def _cum_kernel(f_ref, fb_ref, tri_ref, o_ref, carry_ref):
    @pl.when(pl.program_id(1) == 0)
    def _():
        carry_ref[...] = jnp.zeros_like(carry_ref)

    z = f_ref[0] + fb_ref[...]
    logf = jnp.minimum(z, 0.0) - jnp.log(1.0 + jnp.exp(-jnp.abs(z)))
    hi, mid, lo = _split3(logf)
    tri = tri_ref[...]
    cum = _dot(tri, hi) + _dot(tri, mid) + _dot(tri, lo) + carry_ref[...]
    o_ref[0] = cum
    carry_ref[...] = cum[-1:, :]


def _cum_forget(fg, f_bias, tc=512):
    b, s, w = fg.shape
    fb = jnp.pad(f_bias, (0, w - f_bias.shape[0])).reshape(1, w)
    tri = (jnp.arange(tc)[:, None] >= jnp.arange(tc)[None, :]).astype(BF16)
    return pl.pallas_call(
        _cum_kernel,
        out_shape=jax.ShapeDtypeStruct((b, s, w), F32),
        grid=(b, s // tc),
        in_specs=[pl.BlockSpec((1, tc, w), lambda i, j: (i, j, 0)),
                  pl.BlockSpec((1, w), lambda i, j: (0, 0)),
                  pl.BlockSpec((tc, tc), lambda i, j: (0, 0))],
        out_specs=pl.BlockSpec((1, tc, w), lambda i, j: (i, j, 0)),
        scratch_shapes=[pltpu.VMEM((1, w), F32)],
        compiler_params=_params(("arbitrary", "arbitrary")),
        name="cum_forget",
    )(fg, fb, tri)


def _gate_columns(cum_col, lead):
    rows = cum_col.shape[0]
    lane = lax.broadcasted_iota(jnp.int32, (rows, HEAD_DIM), 1)
    hi, mid, lo = (p.astype(F32) for p in _split3(cum_col if lead else -cum_col))
    off = 0 if lead else 3
    ext = jnp.where(lane < 6, 1.0, 0.0)
    ext = jnp.where(lane == off, hi, ext)
    ext = jnp.where(lane == off + 1, mid, ext)
    ext = jnp.where(lane == off + 2, lo, ext)
    return ext.astype(BF16)


def _pick_lane(tile, idx):
    lane = lax.broadcasted_iota(jnp.int32, tile.shape, 1)
    return jnp.sum(jnp.where(lane == idx, tile, 0.0), axis=-1, keepdims=True)


ONES_ROWS = 16
FOX_HEADS_PER_STEP = 8


def _fox_kernel(q_ref, k_ref, v_ref, cq_ref, ck_ref, gq_ref, gk_ref, o_ref, ka_ref, vt_ref,
                qat_ref, s_ref, p_ref, acc_ref, *, tq, tk):
    hg = pl.program_id(1)
    qi = pl.program_id(2)
    n_kt = vt_ref.shape[1]
    nh = ka_ref.shape[0]
    heads = range(nh)

    @pl.when(qi == 0)
    def _():
        gk = gk_ref[...]
        for hh in heads:
            sl = slice(hh * HEAD_DIM, (hh + 1) * HEAD_DIM)
            ka_ref[hh, :, :HEAD_DIM] = _headnorm(k_ref[0, :, sl], gk).astype(BF16)
            ck = _pick_lane(ck_ref[0], nh * hg + hh) * LOG2E
            ka_ref[hh, :, HEAD_DIM:] = _gate_columns(ck, lead=False)
        for pr in range(nh // 2):
            vt = v_ref[0, :, pr * PAIR:(pr + 1) * PAIR].astype(F32).T.astype(BF16)
            for hh in (2 * pr, 2 * pr + 1):
                rows = slice((hh % 2) * HEAD_DIM, (hh % 2 + 1) * HEAD_DIM)
                for jj in range(n_kt):
                    vt_ref[hh, jj, :HEAD_DIM] = vt[rows, jj * tk:(jj + 1) * tk]
                    vt_ref[hh, jj, HEAD_DIM:] = jnp.ones((ONES_ROWS, tk), BF16)

    gq = gq_ref[...] * (ATTN_SCALE * LOG2E)
    for hh in heads:
        sl = slice(hh * HEAD_DIM, (hh + 1) * HEAD_DIM)
        qn = _headnorm(q_ref[0, :, sl], gq)
        cq = _pick_lane(cq_ref[0], nh * hg + hh) * LOG2E
        qa = jnp.concatenate([qn, _gate_columns(cq, lead=True).astype(F32)], axis=1)
        qat_ref[hh] = qa.T.astype(BF16)
    key = lax.broadcasted_iota(jnp.int32, (tk, tq), 0)
    qry = lax.broadcasted_iota(jnp.int32, (tk, tq), 1)
    ahead = key - qry

    def scores(t, hh, diagonal):
        start = pl.multiple_of(t * tk, tk)
        st = _dot(ka_ref[hh, pl.ds(start, tk), :], qat_ref[hh])
        if diagonal is None:
            st = jnp.where(ahead > jnp.where(t == qi, 0, tk), NEG_INF, st)
        elif diagonal:
            st = jnp.where(ahead > 0, NEG_INF, st)
        s_ref[hh] = st
        return jnp.max(st, axis=0, keepdims=True)

    def pv(t, hh, alpha_prev):
        acc_ref[hh] = alpha_prev * acc_ref[hh] + _dot(vt_ref[hh, t], p_ref[hh])

    def trip(i, carry, next_diagonal):
        out = []
        for hh in heads:
            m, alpha_prev, mx = carry[hh]
            pv(jnp.maximum(i - 1, 0), hh, alpha_prev)
            m_new = jnp.maximum(m, mx)
            alpha = jnp.exp2(m - m_new)
            p_ref[hh] = jnp.exp2(s_ref[hh] - m_new).astype(BF16)
            if next_diagonal is not None:
                mx = scores(i + 1, hh, next_diagonal)
            out.append((m_new, alpha, mx))
        return tuple(out)

    p_ref[...] = jnp.zeros_like(p_ref)
    acc_ref[...] = jnp.zeros_like(acc_ref)
    carry = tuple((jnp.full((1, tq), NEG_INF, F32), jnp.ones((1, tq), F32),
                   scores(0, hh, None)) for hh in heads)
    carry = lax.fori_loop(0, jnp.maximum(qi - 1, 0),
                          functools.partial(trip, next_diagonal=False), carry)
    carry = lax.fori_loop(jnp.maximum(qi - 1, 0), qi,
                          functools.partial(trip, next_diagonal=True), carry)
    carry = trip(qi, carry, None)
    for pr in range(nh // 2):
        o_t = []
        for hh in (2 * pr, 2 * pr + 1):
            pv(qi, hh, carry[hh][1])
            acc = acc_ref[hh]
            o_t.append(acc[:HEAD_DIM] / acc[HEAD_DIM:HEAD_DIM + 1])
        o_ref[0, :, pr * PAIR:(pr + 1) * PAIR] = (
            jnp.concatenate(o_t, axis=0).T.astype(BF16))


def _fox_attention(qkv, cum, gq, gk, tq=256):
    b, s, _ = qkv.shape
    tk = tq
    nq = s // tq
    nh = FOX_HEADS_PER_STEP
    groups = N_HEADS // nh
    gw = nh * HEAD_DIM
    vrows = HEAD_DIM + ONES_ROWS
    return pl.pallas_call(
        functools.partial(_fox_kernel, tq=tq, tk=tk),
        out_shape=jax.ShapeDtypeStruct((b, s, N_HEADS * HEAD_DIM), BF16),
        grid=(b, groups, nq),
        in_specs=[
            pl.BlockSpec((1, tq, gw), lambda bi, hg, qi: (bi, qi, hg)),
            pl.BlockSpec((1, s, gw), lambda bi, hg, qi: (bi, 0, groups + hg)),
            pl.BlockSpec((1, s, gw), lambda bi, hg, qi: (bi, 0, 2 * groups + hg)),
            pl.BlockSpec((1, tq, LANES), lambda bi, hg, qi: (bi, qi, 0)),
            pl.BlockSpec((1, s, LANES), lambda bi, hg, qi: (bi, 0, 0)),
            pl.BlockSpec((1, HEAD_DIM), lambda bi, hg, qi: (0, 0)),
            pl.BlockSpec((1, HEAD_DIM), lambda bi, hg, qi: (0, 0)),
        ],
        out_specs=pl.BlockSpec((1, tq, gw), lambda bi, hg, qi: (bi, qi, hg)),
        scratch_shapes=[pltpu.VMEM((nh, s, 2 * HEAD_DIM), BF16),
                        pltpu.VMEM((nh, s // tk, vrows, tk), BF16),
                        pltpu.VMEM((nh, 2 * HEAD_DIM, tq), BF16),
                        pltpu.VMEM((nh, tk, tq), F32),
                        pltpu.VMEM((nh, tk, tq), BF16),
                        pltpu.VMEM((nh, vrows, tq), F32)],
        compiler_params=_params(("arbitrary", "arbitrary", "arbitrary")),
        name="fox_attn",
    )(qkv, qkv, qkv, cum, cum, gq.reshape(1, HEAD_DIM), gk.reshape(1, HEAD_DIM))


ROUTER_LANES = LANES


def _route(logits):
    lane = lax.broadcasted_iota(jnp.int32, logits.shape, 1)
    lane_f = lane.astype(F32)
    big = float(ROUTER_LANES)
    is_g = lane < N_GROUPS
    gl = jnp.where(is_g, logits, -jnp.inf)
    gmax = jnp.max(gl, axis=-1, keepdims=True)
    gsum = jnp.sum(jnp.where(is_g, jnp.exp(logits - gmax), 0.0), axis=-1, keepdims=True)
    g_w = 1.0 / gsum
    gidx = jnp.min(jnp.where(gl == gmax, lane_f, big), axis=-1, keepdims=True)
    lo = N_GROUPS + EXPERTS_PER_GROUP * gidx
    in_grp = (lane_f >= lo) & (lane_f < lo + EXPERTS_PER_GROUP)
    el = jnp.where(in_grp, logits, -jnp.inf)
    t1 = jnp.max(el, axis=-1, keepdims=True)
    i1 = jnp.min(jnp.where(el == t1, lane_f, big), axis=-1, keepdims=True)
    el2 = jnp.where(lane_f == i1, -jnp.inf, el)
    t2 = jnp.max(el2, axis=-1, keepdims=True)
    i2 = jnp.min(jnp.where(el2 == t2, lane_f, big), axis=-1, keepdims=True)
    e2 = jnp.exp(t2 - t1)
    w1 = g_w / (1.0 + e2)
    w2 = w1 * e2
    gates = jnp.where(lane_f == i1, w1, jnp.where(lane_f == i2, w2, 0.0))
    return jnp.where(lane == 0, gidx, gates)


MOE_TILE = 512
SUBLANES = 8
D_ROWS = D_MODEL // LANES
IN_ROWS = D_ROWS + 1
IN_PITCH = 2 * SUBLANES


def _moe_route_kernel(x_ref, gam_ref, sc_ref, sh_ref, wr_ref, br_ref, tri_ref, o_ref, meta_ref,
                      cnt_ref):
    h = _modnorm(x_ref[0], gam_ref[...], sc_ref[0], sh_ref[0])
    h_hi = h.astype(BF16)
    h_lo = (h - h_hi.astype(F32)).astype(BF16)
    wr = wr_ref[...]
    w_hi = wr.astype(BF16)
    w_lo = (wr - w_hi.astype(F32)).astype(BF16)
    logits = _dot(h_hi, w_hi) + _dot(h_hi, w_lo) + _dot(h_lo, w_hi) + br_ref[...]
    route = _route(logits)
    tm = route.shape[0]

    @pl.when(pl.program_id(0) == 0)
    def _():
        cnt_ref[...] = jnp.zeros_like(cnt_ref)

    lane = lax.broadcasted_iota(jnp.int32, route.shape, 1)
    member = jnp.where((lane.astype(F32) == route[:, 0:1]) & (lane < N_GROUPS), 1.0, 0.0)
    incl = _dot(tri_ref[...], member.astype(BF16))
    rank = jnp.sum(member * (incl - member + cnt_ref[...]), axis=-1, keepdims=True)
    cnt_ref[...] = cnt_ref[...] + incl[tm - 1:tm, :]
    meta = jnp.where(lane == 0, route[:, 0:1], jnp.where(lane == 1, rank, 0.0))
    meta_ref[...] = meta.T[:SUBLANES, :]

    for c in range(D_ROWS):
        o_ref[pl.ds(c, tm, stride=IN_PITCH), :] = h[:, c * LANES:(c + 1) * LANES]
    o_ref[pl.ds(D_ROWS, tm, stride=IN_PITCH), :] = route
    for c in range(IN_ROWS, IN_PITCH):
        o_ref[pl.ds(c, tm, stride=IN_PITCH), :] = jnp.zeros((tm, LANES), F32)


def _token_copy(src, src_tok, dst, dst_tok, sem, pitch, nrows):
    s0 = pl.multiple_of(src_tok * pitch, pitch)
    d0 = pl.multiple_of(dst_tok * pitch, pitch)
    return pltpu.make_async_copy(src.at[pl.ds(s0, nrows), :], dst.at[pl.ds(d0, nrows), :], sem)


_gather_copy = functools.partial(_token_copy, pitch=IN_PITCH, nrows=IN_ROWS)
_scatter_copy = functools.partial(_token_copy, pitch=D_ROWS, nrows=D_ROWS)


def _moe_expert_kernel(tg_ref, nt_ref, nv_ref, dest_ref, hx_hbm, wg_ref, wu_ref, wd_ref, e_ref,
                       y_hbm, buf, ybuf, src_ref, gsem, ssem):
    t = pl.program_id(0)
    n_used = nt_ref[0]
    slot = t % 2
    rows = buf.shape[1] // IN_PITCH
    unroll = 8

    @pl.when(t == 0)
    def _():
        def clear(p, c):
            src_ref[p] = 0
            return c
        lax.fori_loop(0, src_ref.shape[0], clear, 0, unroll=unroll)

        def invert(tok, c):
            src_ref[dest_ref[tok]] = tok
            return c
        lax.fori_loop(0, dest_ref.shape[0], invert, 0, unroll=unroll)

    def gather_start(tile, sl):
        def body(r, c):
            _gather_copy(hx_hbm, src_ref[tile * rows + r], buf.at[sl], r, gsem.at[sl]).start()
            return c
        lax.fori_loop(0, rows, body, 0, unroll=unroll)

    def gather_wait(sl):
        def body(r, c):
            _gather_copy(hx_hbm, 0, buf.at[sl], r, gsem.at[sl]).wait()
            return c
        lax.fori_loop(0, rows, body, 0, unroll=unroll)

    def valid_rows_loop(tile, body):
        n_valid = nv_ref[tile]
        chunks = n_valid // unroll

        def chunk(k, c):
            for i in range(unroll):
                body(k * unroll + i)
            return c
        lax.fori_loop(0, chunks, chunk, 0)

        def single(r, c):
            body(r)
            return c
        lax.fori_loop(chunks * unroll, n_valid, single, 0)

    def scatter_start(tile, sl):
        valid_rows_loop(tile, lambda r: _scatter_copy(
            ybuf.at[sl], r, y_hbm, src_ref[tile * rows + r], ssem.at[sl]).start(priority=1))

    def scatter_wait(tile, sl):
        valid_rows_loop(tile, lambda r: _scatter_copy(ybuf.at[sl], r, y_hbm, 0, ssem.at[sl]).wait())

    @pl.when(t == 0)
    def _():
        gather_start(0, 0)

    @pl.when(t + 1 < n_used)
    def _():
        gather_start(t + 1, 1 - slot)

    @pl.when(t < n_used)
    def _():
        gather_wait(slot)
        slab = lambda c: buf[slot, pl.ds(c, rows, stride=IN_PITCH), :]
        h = jnp.concatenate([slab(c) for c in range(D_ROWS)], axis=1).astype(BF16)
        gates = slab(D_ROWS)
        g_hi = gates.astype(BF16)
        g_lo = (gates - g_hi.astype(F32)).astype(BF16)
        y = None
        for e in range(EXPERTS_PER_GROUP):
            a = _dot(h, wg_ref[e])
            u = _dot(h, wu_ref[e])
            spread = e_ref[0, :, e * EXPERT_FF:(e + 1) * EXPERT_FF]
            gexp = _dot(g_hi, spread) + _dot(g_lo, spread)
            hid = a * (1.0 / (1.0 + jnp.exp(-a))) * u * gexp
            part = _dot(hid.astype(BF16), wd_ref[e])
            y = part if y is None else y + part

        @pl.when(t >= 2)
        def _():
            scatter_wait(t - 2, slot)

        for c in range(D_ROWS):
            ybuf[slot, pl.ds(c, rows, stride=D_ROWS), :] = y[:, c * LANES:(c + 1) * LANES]
        scatter_start(t, slot)

    @pl.when(t == pl.num_programs(0) - 1)
    def _():
        for back in (2, 1):
            scatter_wait(n_used - back, (n_used - back) % 2)


def _moe_combine_kernel(x_ref, g_ref, y_ref, o_ref):
    tm = x_ref.shape[1]
    y = jnp.concatenate([y_ref[pl.ds(c, tm, stride=D_ROWS), :] for c in range(D_ROWS)], axis=1)
    o_ref[0] = x_ref[0] + g_ref[0] * y


def _moe(x, gamma, sc, sh, gate, w_router, b_router, w_gate, w_up, w_down):
    b, s, d = x.shape
    n = b * s
    tm = PERM_TILE
    spt = s // tm
    rows = MOE_TILE
    n_tiles = n // rows + N_GROUPS
    p_rows = n_tiles * rows
    bmap = lambda i: (i // spt, 0, 0)
    tri = (jnp.arange(tm)[:, None] >= jnp.arange(tm)[None, :]).astype(BF16)
    hx, meta = pl.pallas_call(
        _moe_route_kernel,
        out_shape=(jax.ShapeDtypeStruct((n * IN_PITCH, LANES), F32),
                   jax.ShapeDtypeStruct((SUBLANES, n), F32)),
        grid=(b * spt,),
        in_specs=[
            pl.BlockSpec((1, tm, d), lambda i: (i // spt, i % spt, 0)),
            pl.BlockSpec((1, d), lambda i: (0, 0)),
            pl.BlockSpec((1, 1, d), bmap),
            pl.BlockSpec((1, 1, d), bmap),
            pl.BlockSpec((d, ROUTER_LANES), lambda i: (0, 0)),
            pl.BlockSpec((1, ROUTER_LANES), lambda i: (0, 0)),
            pl.BlockSpec((tm, tm), lambda i: (0, 0)),
        ],
        out_specs=(pl.BlockSpec((tm * IN_PITCH, LANES), lambda i: (i, 0)),
                   pl.BlockSpec((SUBLANES, tm), lambda i: (0, i))),
        scratch_shapes=[pltpu.VMEM((1, LANES), F32)],
        compiler_params=_params(("arbitrary",)),
        name="moe_route",
    )(x, gamma.reshape(1, d), sc, sh, w_router, b_router, tri)

    gidx = meta[0].astype(jnp.int32)
    rank = meta[1].astype(jnp.int32)
    onehot = (gidx[:, None] == jnp.arange(N_GROUPS)[None, :]).astype(jnp.int32)
    count = jnp.sum(onehot, axis=0)
    padded = (count + rows - 1) // rows * rows
    ends = jnp.cumsum(padded)
    dest = jnp.sum(onehot * (ends - padded)[None, :], axis=1) + rank
    n_used = (ends[-1] // rows).reshape(1).astype(jnp.int32)
    tile_start = jnp.arange(n_tiles, dtype=jnp.int32) * rows
    tile_group = jnp.minimum(jnp.sum(ends[None, :] <= tile_start[:, None], axis=1),
                             N_GROUPS - 1).astype(jnp.int32)
    seg_valid_end = ends - padded + count
    n_valid = jnp.clip(jnp.sum(jnp.where(jnp.arange(N_GROUPS)[None, :] == tile_group[:, None],
                                         seg_valid_end[None, :], 0), axis=1) - tile_start,
                       0, rows).astype(jnp.int32)

    ff = EXPERTS_PER_GROUP * EXPERT_FF
    lanes = jnp.arange(ROUTER_LANES)[None, :, None]
    expert = N_GROUPS + EXPERTS_PER_GROUP * jnp.arange(N_GROUPS)[:, None, None] \
        + jnp.arange(ff)[None, None, :] // EXPERT_FF
    expand = (lanes == expert).astype(BF16)
    wmap = lambda t, tg, nt, nv, ds: (tg[t], 0, 0)
    y_tok = pl.pallas_call(
        _moe_expert_kernel,
        out_shape=jax.ShapeDtypeStruct((n * D_ROWS, LANES), F32),
        grid_spec=pltpu.PrefetchScalarGridSpec(
            num_scalar_prefetch=4,
            grid=(n_tiles,),
            in_specs=[
                pl.BlockSpec(memory_space=pl.ANY),
                pl.BlockSpec((EXPERTS_PER_GROUP, d, EXPERT_FF), wmap),
                pl.BlockSpec((EXPERTS_PER_GROUP, d, EXPERT_FF), wmap),
                pl.BlockSpec((EXPERTS_PER_GROUP, EXPERT_FF, d), wmap),
                pl.BlockSpec((1, ROUTER_LANES, ff), wmap),
            ],
            out_specs=pl.BlockSpec(memory_space=pl.ANY),
            scratch_shapes=[pltpu.VMEM((2, rows * IN_PITCH, LANES), F32),
                            pltpu.VMEM((2, rows * D_ROWS, LANES), F32),
                            pltpu.SMEM((p_rows,), jnp.int32),
                            pltpu.SemaphoreType.DMA((2,)),
                            pltpu.SemaphoreType.DMA((2,))],
        ),
        compiler_params=_params(("arbitrary",)),
        name="moe_experts",
    )(tile_group, n_used, n_valid, dest, hx, w_gate, w_up, w_down, expand)

    return pl.pallas_call(
        _moe_combine_kernel,
        out_shape=jax.ShapeDtypeStruct((b, s, d), F32),
        grid=(b * spt,),
        in_specs=[pl.BlockSpec((1, tm, d), lambda i: (i // spt, i % spt, 0)),
                  pl.BlockSpec((1, 1, d), bmap),
                  pl.BlockSpec((tm * D_ROWS, LANES), lambda i: (i, 0))],
        out_specs=pl.BlockSpec((1, tm, d), lambda i: (i // spt, i % spt, 0)),
        compiler_params=_params(("arbitrary",)),
        name="moe_combine",
    )(x, gate, y_tok)


def kernel(x, c, w_ada, b_ada, norm_mix, norm_ffn, rel_bias, a_w_in, a_w_out, a_q_norm,
           a_k_norm, b_w_in, b_f_bias, b_w_out, b_q_norm, b_k_norm, router_group_w,
           router_group_b, router_expert_w, router_expert_b, w_gate, w_up, w_down):
    b, s, d = x.shape
    depth = w_ada.shape[0]
    mod = _ada(c, w_ada, b_ada).reshape(depth, b, 6, 1, d)
    pad_r = ROUTER_LANES - N_GROUPS - N_EXPERTS
    for i in range(depth):
        sh_m, sc_m, g_m, sh_f, sc_f, g_f = (mod[i, :, k] for k in range(6))
        j = i // 2
        if i % 2 == 0:
            proj = _normproj_a(x, norm_mix[i], sc_m, sh_m, a_w_in[j].astype(BF16))
            outs, lses = [], []
            for g, (window, dil) in enumerate(A_DILATED):
                tab = rel_bias[:, g * N_HEADS:(g + 1) * N_HEADS]
                o, l = _attn_a_group(proj, g, tab, a_q_norm[j], a_k_norm[j], window, dil)
                outs.append(o)
                lses.append(l)
            x = _merge_out(outs, lses, x, g_m, a_w_out[j].astype(BF16))
        else:
            n_qkv = 3 * N_HEADS * HEAD_DIM
            w_in = b_w_in[j]
            w_f = jnp.pad(w_in[:, n_qkv:], ((0, 0), (0, LANES - N_HEADS))).astype(BF16)
            qkv, fg = _normproj(x, norm_mix[i], sc_m, sh_m, w_in[:, :n_qkv].astype(BF16), w_f)
            cum = _cum_forget(fg, b_f_bias[j])
            o = _fox_attention(qkv, cum, b_q_norm[j], b_k_norm[j])
            x = _out_proj(o, x, g_m, b_w_out[j].astype(BF16))
        w_router = jnp.pad(jnp.concatenate([router_group_w[i], router_expert_w[i]], axis=1),
                           ((0, 0), (0, pad_r)))
        b_router = jnp.pad(jnp.concatenate([router_group_b[i], router_expert_b[i]]),
                           (0, pad_r)).reshape(1, ROUTER_LANES)
        x = _moe(x, norm_ffn[i], sc_f, sh_f, g_f, w_router, b_router,
                 w_gate[i].astype(BF16), w_up[i].astype(BF16), w_down[i].astype(BF16))
    return x
```

```python
import functools
import math

import jax
import jax.numpy as jnp
import numpy as np
from jax import lax
from jax.experimental import pallas as pl
from jax.experimental.pallas import tpu as pltpu

F32 = jnp.float32
BF16 = jnp.bfloat16

D_MODEL = 1024
HEAD_DIM = 64
N_HEADS = 16
ATTN_SCALE = HEAD_DIM ** -0.5
A_DILATED = ((128, 1), (512, 4), (2048, 16))
Q_BLOCK = 128
REL_BUCKETS = 32
REL_MAX_DIST = 2048
N_GROUPS = 4
EXPERTS_PER_GROUP = 4
N_EXPERTS = 16
EXPERT_FF = 256
EPS = 1e-6
NEG_INF = -1e30

LANES = 128
VMEM_LIMIT = 48 * 1024 * 1024
LOG2E = 1.4426950408889634
LN2 = 0.6931471805599453
PERM_TILE = 1024
PAIR = 2 * HEAD_DIM
N_PAIRS = N_HEADS // 2


def _params(sem):
    return pltpu.CompilerParams(dimension_semantics=sem, vmem_limit_bytes=VMEM_LIMIT)


def _split3(x):
    hi = x.astype(BF16)
    r1 = x - hi.astype(F32)
    mid = r1.astype(BF16)
    lo = (r1 - mid.astype(F32)).astype(BF16)
    return hi, mid, lo


def _dot(a, b):
    return jnp.dot(a, b, preferred_element_type=F32)


def _dot_nt(a, b):
    return lax.dot_general(a, b, (((1,), (1,)), ((), ())), preferred_element_type=F32)


def _ada_kernel(c_ref, w_ref, b_ref, o_ref):
    c = c_ref[...]
    a = c * (1.0 / (1.0 + jnp.exp(-c)))
    o_ref[0] = _dot(a.astype(BF16), w_ref[0].astype(BF16)) + b_ref[0]


def _ada(c, w_ada, b_ada):
    depth, d, n6 = w_ada.shape
    b = c.shape[0]
    rows = 8
    c_pad = jnp.pad(c, ((0, rows - b), (0, 0)))
    tn = 1536
    out = pl.pallas_call(
        _ada_kernel,
        out_shape=jax.ShapeDtypeStruct((depth, rows, n6), F32),
        grid=(depth, n6 // tn),
        in_specs=[
            pl.BlockSpec((rows, d), lambda i, j: (0, 0)),
            pl.BlockSpec((1, d, tn), lambda i, j: (i, 0, j)),
            pl.BlockSpec((1, 1, tn), lambda i, j: (i, 0, j)),
        ],
        out_specs=pl.BlockSpec((1, rows, tn), lambda i, j: (i, 0, j)),
        compiler_params=_params(("arbitrary", "arbitrary")),
        name="ada_mod",
    )(c_pad, w_ada, b_ada.reshape(depth, 1, n6))
    return out[:, :b]


def _modnorm(x, gamma, sc, sh):
    ms = jnp.mean(x * x, axis=-1, keepdims=True)
    y = x * lax.rsqrt(ms + EPS) * gamma
    return y * (1.0 + sc) + sh


def _normproj_kernel(x_ref, g_ref, sc_ref, sh_ref, w_ref, o_ref, h_ref):
    @pl.when(pl.program_id(1) == 0)
    def _():
        h_ref[...] = _modnorm(x_ref[0], g_ref[...], sc_ref[0], sh_ref[0]).astype(BF16)

    o_ref[0] = _dot(h_ref[...], w_ref[...]).astype(BF16)


def _normproj_gate_kernel(x_ref, g_ref, sc_ref, sh_ref, w_ref, wf_ref, o_ref, f_ref, h_ref):
    @pl.when(pl.program_id(1) == 0)
    def _():
        h = _modnorm(x_ref[0], g_ref[...], sc_ref[0], sh_ref[0]).astype(BF16)
        h_ref[...] = h
        f_ref[0] = _dot(h, wf_ref[...])

    o_ref[0] = _dot(h_ref[...], w_ref[...]).astype(BF16)


def _normproj(x, gamma, sc, sh, w, w_f=None, tm=1024, tn=1536):
    b, s, d = x.shape
    ncols = w.shape[1]
    spt = s // tm
    grid = (b * spt, ncols // tn)
    xmap = lambda i, j: (i // spt, i % spt, 0)
    bmap = lambda i, j: (i // spt, 0, 0)
    in_specs = [
        pl.BlockSpec((1, tm, d), xmap),
        pl.BlockSpec((1, d), lambda i, j: (0, 0)),
        pl.BlockSpec((1, 1, d), bmap),
        pl.BlockSpec((1, 1, d), bmap),
        pl.BlockSpec((d, tn), lambda i, j: (0, j)),
    ]
    o_spec = pl.BlockSpec((1, tm, tn), lambda i, j: (i // spt, i % spt, j))
    scratch = [pltpu.VMEM((tm, d), BF16)]
    if w_f is None:
        return pl.pallas_call(
            _normproj_kernel,
            out_shape=jax.ShapeDtypeStruct((b, s, ncols), BF16),
            grid=grid, in_specs=in_specs, out_specs=o_spec, scratch_shapes=scratch,
            compiler_params=_params(("arbitrary", "arbitrary")),
            name="normproj",
        )(x, gamma.reshape(1, d), sc, sh, w)
    in_specs.append(pl.BlockSpec((d, LANES), lambda i, j: (0, 0)))
    f_spec = pl.BlockSpec((1, tm, LANES), xmap)
    return pl.pallas_call(
        _normproj_gate_kernel,
        out_shape=(jax.ShapeDtypeStruct((b, s, ncols), BF16),
                   jax.ShapeDtypeStruct((b, s, LANES), F32)),
        grid=grid, in_specs=in_specs, out_specs=(o_spec, f_spec), scratch_shapes=scratch,
        compiler_params=_params(("arbitrary", "arbitrary")),
        name="normproj_gate",
    )(x, gamma.reshape(1, d), sc, sh, w, w_f)


def _normproj_a_kernel(x_ref, g_ref, sc_ref, sh_ref, w_ref, o_ref, hf_ref, hp_ref):
    tm = x_ref.shape[1]
    n_slab = x_ref.shape[2] // LANES

    @pl.when(pl.program_id(1) == 0)
    def _():
        h = _modnorm(x_ref[0], g_ref[...], sc_ref[0], sh_ref[0])
        hp_ref[0] = h.astype(BF16)
        for c in range(n_slab):
            hf_ref[c] = h[:, c * LANES:(c + 1) * LANES]
        for g, (_, r) in enumerate(A_DILATED):
            if r == 1:
                continue
            n = tm // r
            for jr in range(r):
                for c in range(n_slab):
                    hp_ref[g, jr * n:(jr + 1) * n, c * LANES:(c + 1) * LANES] = (
                        hf_ref[c, pl.ds(jr, n, stride=r), :].astype(BF16))

    group = (pl.program_id(1) * w_ref.shape[1]) // (3 * N_HEADS * HEAD_DIM)
    o_ref[0] = _dot(hp_ref[group], w_ref[...]).astype(BF16)


def _normproj_a(x, gamma, sc, sh, w, tn=1536):
    b, s, d = x.shape
    tm = PERM_TILE
    ncols = w.shape[1]
    spt = s // tm
    bmap = lambda i, j: (i // spt, 0, 0)
    return pl.pallas_call(
        _normproj_a_kernel,
        out_shape=jax.ShapeDtypeStruct((b, s, ncols), BF16),
        grid=(b * spt, ncols // tn),
        in_specs=[
            pl.BlockSpec((1, tm, d), lambda i, j: (i // spt, i % spt, 0)),
            pl.BlockSpec((1, d), lambda i, j: (0, 0)),
            pl.BlockSpec((1, 1, d), bmap),
            pl.BlockSpec((1, 1, d), bmap),
            pl.BlockSpec((d, tn), lambda i, j: (0, j)),
        ],
        out_specs=pl.BlockSpec((1, tm, tn), lambda i, j: (i // spt, i % spt, j)),
        scratch_shapes=[pltpu.VMEM((d // LANES, tm, LANES), F32),
                        pltpu.VMEM((len(A_DILATED), tm, d), BF16)],
        compiler_params=_params(("arbitrary", "arbitrary")),
        name="normproj_a",
    )(x, gamma.reshape(1, d), sc, sh, w)


def _headnorm(t, gain):
    t = t.astype(F32)
    ms = jnp.mean(t * t, axis=-1, keepdims=True)
    return t * lax.rsqrt(ms + EPS) * gain


def _pairnorm(t, gain2, even):
    t = t.astype(F32)
    sq = t * t
    tot = jnp.sum(sq, axis=-1, keepdims=True)
    s_even = jnp.sum(jnp.where(even, sq, 0.0), axis=-1, keepdims=True)
    inv = jnp.where(even, lax.rsqrt(s_even * (1.0 / HEAD_DIM) + EPS),
                    lax.rsqrt((tot - s_even) * (1.0 / HEAD_DIM) + EPS))
    return t * inv * gain2


def _rows(ref, cols=slice(None)):
    if len(ref.shape) == 2:
        return ref[:, cols]
    return jnp.concatenate([ref[i, :, cols] for i in range(ref.shape[0])], axis=0)


def _store_rows(ref, cols, val):
    if len(ref.shape) == 2:
        ref[:, cols] = val
    else:
        n = ref.shape[1]
        for i in range(ref.shape[0]):
            ref[i, :, cols] = val[i * n:(i + 1) * n]


def _attn_a_kernel(q_ref, k_ref, v_ref, bias_ref, gq_ref, gk_ref, o_ref, lse_ref,
                   kn_ref, vv_ref):
    ub = pl.program_id(2)
    slot = ub % 2
    prev = 1 - slot

    @pl.when(ub == 0)
    def _():
        kn_ref[1] = jnp.zeros(kn_ref.shape[1:], BF16)
        vv_ref[1] = jnp.zeros(vv_ref.shape[1:], BF16)

    first = jnp.minimum(ub, 1)
    lane = lax.broadcasted_iota(jnp.int32, (Q_BLOCK, PAIR), 1)
    even = lane < HEAD_DIM
    lane_row = lax.broadcasted_iota(jnp.int32, (1, PAIR), 1)
    keep_even = jnp.where(lane_row < HEAD_DIM, 1.0, 0.0).astype(BF16)
    keep_odd = jnp.where(lane_row < HEAD_DIM, 0.0, 1.0).astype(BF16)
    gq2 = gq_ref[...] * (ATTN_SCALE * LOG2E)
    gk2 = gk_ref[...]
    for hp in range(N_PAIRS):
        cs = slice(hp * PAIR, (hp + 1) * PAIR)
        kn_ref[slot, :, cs] = _pairnorm(_rows(k_ref, cs), gk2, even).astype(BF16)
    vv_ref[slot] = _rows(v_ref)
    lse_tile = jnp.zeros((Q_BLOCK, LANES), F32)
    for hp in range(N_PAIRS):
        cs = slice(hp * PAIR, (hp + 1) * PAIR)
        qn = _pairnorm(_rows(q_ref, cs), gq2, even)
        qq = jnp.concatenate([jnp.where(even, qn, 0.0), jnp.where(even, 0.0, qn)],
                             axis=0).astype(BF16)
        kcat = jnp.concatenate([kn_ref[prev, :, cs], kn_ref[slot, :, cs]], axis=0)
        s = _dot_nt(qq, kcat) + bias_ref[first, hp]
        m = jnp.max(s, axis=-1, keepdims=True)
        p = jnp.exp2(s - m)
        l = jnp.sum(p, axis=-1, keepdims=True)
        pb = p.astype(BF16)
        vcat = jnp.concatenate([vv_ref[prev, :, cs], vv_ref[slot, :, cs]], axis=0)
        o = (_dot(pb[:Q_BLOCK], vcat * keep_even)
             + _dot(pb[Q_BLOCK:], vcat * keep_odd))
        inv = 1.0 / l
        o = o * jnp.where(even, inv[:Q_BLOCK], inv[Q_BLOCK:])
        _store_rows(o_ref, cs, o.astype(BF16))
        lse = m * LN2 + jnp.log(l)
        lse_tile = jnp.where(lane == 2 * hp, lse[:Q_BLOCK],
                             jnp.where(lane == 2 * hp + 1, lse[Q_BLOCK:], lse_tile))
    _store_rows(lse_ref, slice(None), lse_tile)


def _t5_bucket(dist):
    max_exact = REL_BUCKETS // 2
    d = np.maximum(dist, 0)
    large = max_exact + (np.log(np.maximum(d, 1).astype(np.float32) / max_exact)
                         / math.log(REL_MAX_DIST / max_exact)
                         * (REL_BUCKETS - max_exact)).astype(np.int32)
    large = np.minimum(large, REL_BUCKETS - 1)
    return np.where(d < max_exact, d, large)


def _bias_kernel(tab_ref, onehot_ref, o_ref):
    hi, mid, lo = _split3(tab_ref[...])
    oh = onehot_ref[...]
    o_ref[...] = (_dot(hi, oh) + _dot(mid, oh) + _dot(lo, oh)) * LOG2E


def _bias_tables(rel_bias_g, window, dilation):
    qi = np.arange(Q_BLOCK)[:, None]
    kj = np.arange(2 * Q_BLOCK)[None, :]
    dist = qi + Q_BLOCK - kj
    bucket = _t5_bucket(dist * dilation).reshape(1, -1)
    onehot = jnp.asarray(bucket == np.arange(REL_BUCKETS)[:, None], BF16)
    npos = onehot.shape[1]
    chunk = 4096
    bias = pl.pallas_call(
        _bias_kernel,
        out_shape=jax.ShapeDtypeStruct((N_HEADS, npos), F32),
        grid=(npos // chunk,),
        in_specs=[pl.BlockSpec((N_HEADS, REL_BUCKETS), lambda i: (0, 0)),
                  pl.BlockSpec((REL_BUCKETS, chunk), lambda i: (0, i))],
        out_specs=pl.BlockSpec((N_HEADS, chunk), lambda i: (0, i)),
        compiler_params=_params(("arbitrary",)),
        name="rel_bias_table",
    )(rel_bias_g.T.astype(F32), onehot)
    bias = bias.reshape(N_PAIRS, 2 * Q_BLOCK, 2 * Q_BLOCK)
    band = (dist >= 0) & (dist <= window // dilation)
    first = band & (qi - dist >= 0)
    band = np.concatenate([band, band], axis=0)[None]
    first = np.concatenate([first, first], axis=0)[None]
    return jnp.stack([jnp.where(first, bias, NEG_INF), jnp.where(band, bias, NEG_INF)], axis=0)


def _attn_a_group(proj, g, rel_bias_g, gq, gk, window, dilation):
    b, s, cols = proj.shape
    r = dilation
    nt = s // PERM_TILE
    n_t = PERM_TILE // r
    nb = s // r // Q_BLOCK
    gw = N_HEADS * HEAD_DIM
    tables = _bias_tables(rel_bias_g, window, dilation)
    base = g * 3
    if n_t >= Q_BLOCK:
        bpt = n_t // Q_BLOCK
        blk = lambda w: (None, None, None, Q_BLOCK, w)
        imap = lambda col: (lambda bi, jr, ub: (bi, ub // bpt, jr, ub % bpt, col))
    else:
        blk = lambda w: (None, Q_BLOCK // n_t, None, n_t, w)
        imap = lambda col: (lambda bi, jr, ub: (bi, ub, jr, 0, col))
    pv = proj.reshape(b, nt, r, n_t, cols)
    gains = lambda t: jnp.concatenate([t, t]).reshape(1, PAIR)
    o, lse = pl.pallas_call(
        _attn_a_kernel,
        out_shape=(jax.ShapeDtypeStruct((b, nt, r, n_t, gw), BF16),
                   jax.ShapeDtypeStruct((b, nt, r, n_t, LANES), F32)),
        grid=(b, r, nb),
        in_specs=[
            pl.BlockSpec(blk(gw), imap(base)),
            pl.BlockSpec(blk(gw), imap(base + 1)),
            pl.BlockSpec(blk(gw), imap(base + 2)),
            pl.BlockSpec(tables.shape, lambda bi, jr, ub: (0, 0, 0, 0)),
            pl.BlockSpec((1, PAIR), lambda bi, jr, ub: (0, 0)),
            pl.BlockSpec((1, PAIR), lambda bi, jr, ub: (0, 0)),
        ],
        out_specs=(pl.BlockSpec(blk(gw), imap(0)), pl.BlockSpec(blk(LANES), imap(0))),
        scratch_shapes=[pltpu.VMEM((2, Q_BLOCK, gw), BF16), pltpu.VMEM((2, Q_BLOCK, gw), BF16)],
        compiler_params=_params(("arbitrary", "arbitrary", "arbitrary")),
        name=f"attn_a_g{g}",
    )(pv, pv, pv, tables, gains(gq), gains(gk))
    return o.reshape(b, s, gw), lse.reshape(b, s, LANES)


MERGE_CHUNK = 128


def _merge_out_kernel(o0_ref, o1_ref, o2_ref, l0_ref, l1_ref, l2_ref, x_ref, g_ref, w_ref,
                      e_ref, out_ref, lun_ref, oun_ref, om_ref):
    tm = o0_ref.shape[1]
    n_slab = o0_ref.shape[2] // LANES

    @pl.when(pl.program_id(1) == 0)
    def _():
        for gi, (o_ref, l_ref) in enumerate(((o1_ref, l1_ref), (o2_ref, l2_ref))):
            r = A_DILATED[gi + 1][1]
            n = tm // r
            for jr in range(r):
                rows = slice(jr * n, (jr + 1) * n)
                lun_ref[gi, pl.ds(jr, n, stride=r), :] = l_ref[0, rows, :]
                for c in range(n_slab):
                    oun_ref[gi, c, pl.ds(jr, n, stride=r), :] = (
                        o_ref[0, rows, c * LANES:(c + 1) * LANES].astype(F32))
        expand = e_ref[...]

        def widen(w):
            hi = w.astype(BF16)
            lo = (w - hi.astype(F32)).astype(BF16)
            return _dot(hi, expand) + _dot(lo, expand)

        for k in range(tm // MERGE_CHUNK):
            rs = slice(k * MERGE_CHUNK, (k + 1) * MERGE_CHUNK)
            l0, l1, l2 = l0_ref[0, rs, :], lun_ref[0, rs, :], lun_ref[1, rs, :]
            m = jnp.maximum(jnp.maximum(l0, l1), l2)
            e0, e1, e2 = jnp.exp(l0 - m), jnp.exp(l1 - m), jnp.exp(l2 - m)
            inv = 1.0 / (e0 + e1 + e2)
            o1 = jnp.concatenate([oun_ref[0, c, rs, :] for c in range(n_slab)], axis=1)
            o2 = jnp.concatenate([oun_ref[1, c, rs, :] for c in range(n_slab)], axis=1)
            o = (widen(e0 * inv) * o0_ref[0, rs, :].astype(F32)
                 + widen(e1 * inv) * o1 + widen(e2 * inv) * o2)
            om_ref[rs, :] = o.astype(BF16)

    out_ref[0] = x_ref[0] + g_ref[0] * _dot(om_ref[...], w_ref[...])


def _merge_out(outs, lses, x, gate, w_out, tn=512):
    b, s, d = x.shape
    tm = PERM_TILE
    spt = s // tm
    rmap = lambda i, j: (i // spt, i % spt, 0)
    cmap = lambda i, j: (i // spt, i % spt, j)
    expand = ((jnp.arange(d)[None, :] // HEAD_DIM == jnp.arange(LANES)[:, None])
              & (jnp.arange(LANES)[:, None] < N_HEADS)).astype(BF16)
    o_spec = pl.BlockSpec((1, tm, d), rmap)
    l_spec = pl.BlockSpec((1, tm, LANES), rmap)
    return pl.pallas_call(
        _merge_out_kernel,
        out_shape=jax.ShapeDtypeStruct((b, s, d), F32),
        grid=(b * spt, d // tn),
        in_specs=[o_spec, o_spec, o_spec, l_spec, l_spec, l_spec,
                  pl.BlockSpec((1, tm, tn), cmap),
                  pl.BlockSpec((1, 1, tn), lambda i, j: (i // spt, 0, j)),
                  pl.BlockSpec((d, tn), lambda i, j: (0, j)),
                  pl.BlockSpec((LANES, d), lambda i, j: (0, 0))],
        out_specs=pl.BlockSpec((1, tm, tn), cmap),
        scratch_shapes=[pltpu.VMEM((2, tm, LANES), F32),
                        pltpu.VMEM((2, d // LANES, tm, LANES), F32),
                        pltpu.VMEM((tm, d), BF16)],
        compiler_params=_params(("arbitrary", "arbitrary")),
        name="merge_out",
    )(*outs, *lses, x, gate, w_out, expand)


def _out_kernel(o_ref, x_ref, g_ref, w_ref, out_ref):
    out_ref[0] = x_ref[0] + g_ref[0] * _dot(o_ref[0], w_ref[...])


def _out_proj(o, x, gate, w_out, tm=512):
    b, s, d = x.shape
    spt = s // tm
    xmap = lambda i: (i // spt, i % spt, 0)
    return pl.pallas_call(
        _out_kernel,
        out_shape=jax.ShapeDtypeStruct((b, s, d), F32),
        grid=(b * spt,),
        in_specs=[pl.BlockSpec((1, tm, d), xmap),
                  pl.BlockSpec((1, tm, d), xmap),
                  pl.BlockSpec((1, 1, d), lambda i: (i // spt, 0, 0)),
                  pl.BlockSpec((d, d), lambda i: (0, 0))],
        out_specs=pl.BlockSpec((1, tm, d), xmap),
        compiler_params=_params(("arbitrary",)),
        name="out_proj",
    )(o, x, gate, w_out)


def _cum_kernel(f_ref, fb_ref, tri_ref, o_ref, carry_ref):
    @pl.when(pl.program_id(1) == 0)
    def _():
        carry_ref[...] = jnp.zeros_like(carry_ref)

    z = f_ref[0] + fb_ref[...]
    logf = jnp.minimum(z, 0.0) - jnp.log(1.0 + jnp.exp(-jnp.abs(z)))
    hi, mid, lo = _split3(logf)
    tri = tri_ref[...]
    cum = _dot(tri, hi) + _dot(tri, mid) + _dot(tri, lo) + carry_ref[...]
    o_ref[0] = cum
    carry_ref[...] = cum[-1:, :]


def _cum_forget(fg, f_bias, tc=512):
    b, s, w = fg.shape
    fb = jnp.pad(f_bias, (0, w - f_bias.shape[0])).reshape(1, w)
    tri = (jnp.arange(tc)[:, None] >= jnp.arange(tc)[None, :]).astype(BF16)
    return pl.pallas_call(
        _cum_kernel,
        out_shape=jax.ShapeDtypeStruct((b, s, w), F32),
        grid=(b, s // tc),
        in_specs=[pl.BlockSpec((1, tc, w), lambda i, j: (i, j, 0)),
                  pl.BlockSpec((1, w), lambda i, j: (0, 0)),
                  pl.BlockSpec((tc, tc), lambda i, j: (0, 0))],
        out_specs=pl.BlockSpec((1, tc, w), lambda i, j: (i, j, 0)),
        scratch_shapes=[pltpu.VMEM((1, w), F32)],
        compiler_params=_params(("arbitrary", "arbitrary")),
        name="cum_forget",
    )(fg, fb, tri)


def _gate_columns(cum_col, lead):
    rows = cum_col.shape[0]
    lane = lax.broadcasted_iota(jnp.int32, (rows, HEAD_DIM), 1)
    hi, mid, lo = (p.astype(F32) for p in _split3(cum_col if lead else -cum_col))
    off = 0 if lead else 3
    ext = jnp.where(lane < 6, 1.0, 0.0)
    ext = jnp.where(lane == off, hi, ext)
    ext = jnp.where(lane == off + 1, mid, ext)
    ext = jnp.where(lane == off + 2, lo, ext)
    return ext.astype(BF16)


def _pick_lane(tile, idx):
    lane = lax.broadcasted_iota(jnp.int32, tile.shape, 1)
    return jnp.sum(jnp.where(lane == idx, tile, 0.0), axis=-1, keepdims=True)


ONES_ROWS = 16
FOX_HEADS_PER_STEP = 8


def _fox_kernel(q_ref, k_ref, v_ref, cq_ref, ck_ref, gq_ref, gk_ref, o_ref, ka_ref, vt_ref,
                qat_ref, s_ref, p_ref, acc_ref, *, tq, tk):
    hg = pl.program_id(1)
    qi = pl.program_id(2)
    n_kt = vt_ref.shape[1]
    nh = ka_ref.shape[0]
    heads = range(nh)

    @pl.when(qi == 0)
    def _():
        gk = gk_ref[...]
        for hh in heads:
            sl = slice(hh * HEAD_DIM, (hh + 1) * HEAD_DIM)
            ka_ref[hh, :, :HEAD_DIM] = _headnorm(k_ref[0, :, sl], gk).astype(BF16)
            ck = _pick_lane(ck_ref[0], nh * hg + hh) * LOG2E
            ka_ref[hh, :, HEAD_DIM:] = _gate_columns(ck, lead=False)
        for pr in range(nh // 2):
            vt = v_ref[0, :, pr * PAIR:(pr + 1) * PAIR].astype(F32).T.astype(BF16)
            for hh in (2 * pr, 2 * pr + 1):
                rows = slice((hh % 2) * HEAD_DIM, (hh % 2 + 1) * HEAD_DIM)
                for jj in range(n_kt):
                    vt_ref[hh, jj, :HEAD_DIM] = vt[rows, jj * tk:(jj + 1) * tk]
                    vt_ref[hh, jj, HEAD_DIM:] = jnp.ones((ONES_ROWS, tk), BF16)

    gq = gq_ref[...] * (ATTN_SCALE * LOG2E)
    for hh in heads:
        sl = slice(hh * HEAD_DIM, (hh + 1) * HEAD_DIM)
        qn = _headnorm(q_ref[0, :, sl], gq)
        cq = _pick_lane(cq_ref[0], nh * hg + hh) * LOG2E
        qa = jnp.concatenate([qn, _gate_columns(cq, lead=True).astype(F32)], axis=1)
        qat_ref[hh] = qa.T.astype(BF16)
    key = lax.broadcasted_iota(jnp.int32, (tk, tq), 0)
    qry = lax.broadcasted_iota(jnp.int32, (tk, tq), 1)
    ahead = key - qry

    def scores(t, hh, diagonal):
        start = pl.multiple_of(t * tk, tk)
        st = _dot(ka_ref[hh, pl.ds(start, tk), :], qat_ref[hh])
        if diagonal is None:
            st = jnp.where(ahead > jnp.where(t == qi, 0, tk), NEG_INF, st)
        elif diagonal:
            st = jnp.where(ahead > 0, NEG_INF, st)
        s_ref[hh] = st
        return jnp.max(st, axis=0, keepdims=True)

    def pv(t, hh, alpha_prev):
        acc_ref[hh] = alpha_prev * acc_ref[hh] + _dot(vt_ref[hh, t], p_ref[hh])

    def trip(i, carry, next_diagonal):
        out = []
        for hh in heads:
            m, alpha_prev, mx = carry[hh]
            pv(jnp.maximum(i - 1, 0), hh, alpha_prev)
            m_new = jnp.maximum(m, mx)
            alpha = jnp.exp2(m - m_new)
            p_ref[hh] = jnp.exp2(s_ref[hh] - m_new).astype(BF16)
            if next_diagonal is not None:
                mx = scores(i + 1, hh, next_diagonal)
            out.append((m_new, alpha, mx))
        return tuple(out)

    p_ref[...] = jnp.zeros_like(p_ref)
    acc_ref[...] = jnp.zeros_like(acc_ref)
    carry = tuple((jnp.full((1, tq), NEG_INF, F32), jnp.ones((1, tq), F32),
                   scores(0, hh, None)) for hh in heads)
    carry = lax.fori_loop(0, jnp.maximum(qi - 1, 0),
                          functools.partial(trip, next_diagonal=False), carry)
    carry = lax.fori_loop(jnp.maximum(qi - 1, 0), qi,
                          functools.partial(trip, next_diagonal=True), carry)
    carry = trip(qi, carry, None)
    for pr in range(nh // 2):
        o_t = []
        for hh in (2 * pr, 2 * pr + 1):
            pv(qi, hh, carry[hh][1])
            acc = acc_ref[hh]
            o_t.append(acc[:HEAD_DIM] / acc[HEAD_DIM:HEAD_DIM + 1])
        o_ref[0, :, pr * PAIR:(pr + 1) * PAIR] = (
            jnp.concatenate(o_t, axis=0).T.astype(BF16))


def _fox_attention(qkv, cum, gq, gk, tq=256):
    b, s, _ = qkv.shape
    tk = tq
    nq = s // tq
    nh = FOX_HEADS_PER_STEP
    groups = N_HEADS // nh
    gw = nh * HEAD_DIM
    vrows = HEAD_DIM + ONES_ROWS
    return pl.pallas_call(
        functools.partial(_fox_kernel, tq=tq, tk=tk),
        out_shape=jax.ShapeDtypeStruct((b, s, N_HEADS * HEAD_DIM), BF16),
        grid=(b, groups, nq),
        in_specs=[
            pl.BlockSpec((1, tq, gw), lambda bi, hg, qi: (bi, qi, hg)),
            pl.BlockSpec((1, s, gw), lambda bi, hg, qi: (bi, 0, groups + hg)),
            pl.BlockSpec((1, s, gw), lambda bi, hg, qi: (bi, 0, 2 * groups + hg)),
            pl.BlockSpec((1, tq, LANES), lambda bi, hg, qi: (bi, qi, 0)),
            pl.BlockSpec((1, s, LANES), lambda bi, hg, qi: (bi, 0, 0)),
            pl.BlockSpec((1, HEAD_DIM), lambda bi, hg, qi: (0, 0)),
            pl.BlockSpec((1, HEAD_DIM), lambda bi, hg, qi: (0, 0)),
        ],
        out_specs=pl.BlockSpec((1, tq, gw), lambda bi, hg, qi: (bi, qi, hg)),
        scratch_shapes=[pltpu.VMEM((nh, s, 2 * HEAD_DIM), BF16),
                        pltpu.VMEM((nh, s // tk, vrows, tk), BF16),
                        pltpu.VMEM((nh, 2 * HEAD_DIM, tq), BF16),
                        pltpu.VMEM((nh, tk, tq), F32),
                        pltpu.VMEM((nh, tk, tq), BF16),
                        pltpu.VMEM((nh, vrows, tq), F32)],
        compiler_params=_params(("arbitrary", "arbitrary", "arbitrary")),
        name="fox_attn",
    )(qkv, qkv, qkv, cum, cum, gq.reshape(1, HEAD_DIM), gk.reshape(1, HEAD_DIM))


ROUTER_LANES = LANES


def _route(logits):
    lane = lax.broadcasted_iota(jnp.int32, logits.shape, 1)
    lane_f = lane.astype(F32)
    big = float(ROUTER_LANES)
    is_g = lane < N_GROUPS
    gl = jnp.where(is_g, logits, -jnp.inf)
    gmax = jnp.max(gl, axis=-1, keepdims=True)
    gsum = jnp.sum(jnp.where(is_g, jnp.exp(logits - gmax), 0.0), axis=-1, keepdims=True)
    g_w = 1.0 / gsum
    gidx = jnp.min(jnp.where(gl == gmax, lane_f, big), axis=-1, keepdims=True)
    lo = N_GROUPS + EXPERTS_PER_GROUP * gidx
    in_grp = (lane_f >= lo) & (lane_f < lo + EXPERTS_PER_GROUP)
    el = jnp.where(in_grp, logits, -jnp.inf)
    t1 = jnp.max(el, axis=-1, keepdims=True)
    i1 = jnp.min(jnp.where(el == t1, lane_f, big), axis=-1, keepdims=True)
    el2 = jnp.where(lane_f == i1, -jnp.inf, el)
    t2 = jnp.max(el2, axis=-1, keepdims=True)
    i2 = jnp.min(jnp.where(el2 == t2, lane_f, big), axis=-1, keepdims=True)
    e2 = jnp.exp(t2 - t1)
    w1 = g_w / (1.0 + e2)
    w2 = w1 * e2
    gates = jnp.where(lane_f == i1, w1, jnp.where(lane_f == i2, w2, 0.0))
    return jnp.where(lane == 0, gidx, gates)


MOE_TILE = 512
SUBLANES = 8
ROUTED_COLS = D_MODEL + LANES


def _moe_route_kernel(x_ref, gam_ref, sc_ref, sh_ref, wr_ref, br_ref, tri_ref, o_ref, meta_ref,
                      cnt_ref):
    h = _modnorm(x_ref[0], gam_ref[...], sc_ref[0], sh_ref[0])
    h_hi = h.astype(BF16)
    h_lo = (h - h_hi.astype(F32)).astype(BF16)
    wr = wr_ref[...]
    w_hi = wr.astype(BF16)
    w_lo = (wr - w_hi.astype(F32)).astype(BF16)
    logits = _dot(h_hi, w_hi) + _dot(h_hi, w_lo) + _dot(h_lo, w_hi) + br_ref[...]
    route = _route(logits)
    tm = route.shape[0]

    @pl.when(pl.program_id(0) == 0)
    def _():
        cnt_ref[...] = jnp.zeros_like(cnt_ref)

    lane = lax.broadcasted_iota(jnp.int32, route.shape, 1)
    member = jnp.where((lane.astype(F32) == route[:, 0:1]) & (lane < N_GROUPS), 1.0, 0.0)
    incl = _dot(tri_ref[...], member.astype(BF16))
    rank = jnp.sum(member * (incl - member + cnt_ref[...]), axis=-1, keepdims=True)
    cnt_ref[...] = cnt_ref[...] + incl[tm - 1:tm, :]
    meta = jnp.where(lane == 0, route[:, 0:1], jnp.where(lane == 1, rank, 0.0))
    meta_ref[...] = meta.T[:SUBLANES, :]

    o_ref[:, :D_MODEL] = h
    o_ref[:, D_MODEL:] = route


def _row_copy(src, src_row, dst, dst_row, sem):
    return pltpu.make_async_copy(src.at[pl.ds(src_row, 1), :], dst.at[pl.ds(dst_row, 1), :], sem)


def _moe_expert_kernel(tg_ref, nt_ref, nv_ref, dest_ref, hx_hbm, wg_ref, wu_ref, wd_ref, e_ref,
                       y_hbm, buf, ybuf, src_ref, gsem, ssem):
    t = pl.program_id(0)
    n_used = nt_ref[0]
    slot = t % 2
    rows = buf.shape[1]
    unroll = 8

    @pl.when(t == 0)
    def _():
        def clear(p, c):
            src_ref[p] = 0
            return c
        lax.fori_loop(0, src_ref.shape[0], clear, 0, unroll=unroll)

        def invert(tok, c):
            src_ref[dest_ref[tok]] = tok
            return c
        lax.fori_loop(0, dest_ref.shape[0], invert, 0, unroll=unroll)

    def gather_start(tile, sl):
        def body(r, c):
            _row_copy(hx_hbm, src_ref[tile * rows + r], buf.at[sl], r, gsem.at[sl]).start()
            return c
        lax.fori_loop(0, rows, body, 0, unroll=unroll)

    def gather_wait(sl):
        def body(r, c):
            _row_copy(hx_hbm, 0, buf.at[sl], r, gsem.at[sl]).wait()
            return c
        lax.fori_loop(0, rows, body, 0, unroll=unroll)

    def valid_rows_loop(tile, body):
        n_valid = nv_ref[tile]
        chunks = n_valid // unroll

        def chunk(k, c):
            for i in range(unroll):
                body(k * unroll + i)
            return c
        lax.fori_loop(0, chunks, chunk, 0)

        def single(r, c):
            body(r)
            return c
        lax.fori_loop(chunks * unroll, n_valid, single, 0)

    def scatter_start(tile, sl):
        valid_rows_loop(tile, lambda r: _row_copy(
            ybuf.at[sl], r, y_hbm, src_ref[tile * rows + r], ssem.at[sl]).start(priority=1))

    def scatter_wait(tile, sl):
        valid_rows_loop(tile, lambda r: _row_copy(ybuf.at[sl], r, y_hbm, 0, ssem.at[sl]).wait())

    @pl.when(t == 0)
    def _():
        gather_start(0, 0)

    @pl.when(t + 1 < n_used)
    def _():
        gather_start(t + 1, 1 - slot)

    @pl.when(t < n_used)
    def _():
        gather_wait(slot)
        h = buf[slot, :, :D_MODEL].astype(BF16)
        gates = buf[slot, :, D_MODEL:]
        g_hi = gates.astype(BF16)
        g_lo = (gates - g_hi.astype(F32)).astype(BF16)
        y = None
        for e in range(EXPERTS_PER_GROUP):
            a = _dot(h, wg_ref[e])
            u = _dot(h, wu_ref[e])
            spread = e_ref[0, :, e * EXPERT_FF:(e + 1) * EXPERT_FF]
            gexp = _dot(g_hi, spread) + _dot(g_lo, spread)
            hid = a * (1.0 / (1.0 + jnp.exp(-a))) * u * gexp
            part = _dot(hid.astype(BF16), wd_ref[e])
            y = part if y is None else y + part

        @pl.when(t >= 2)
        def _():
            scatter_wait(t - 2, slot)

        ybuf[slot] = y
        scatter_start(t, slot)

    @pl.when(t == pl.num_programs(0) - 1)
    def _():
        for back in (2, 1):
            scatter_wait(n_used - back, (n_used - back) % 2)


def _moe_combine_kernel(x_ref, g_ref, y_ref, o_ref):
    tm = x_ref.shape[1]
    o_ref[0] = x_ref[0] + g_ref[0] * y_ref[...]


def _moe(x, gamma, sc, sh, gate, w_router, b_router, w_gate, w_up, w_down):
    b, s, d = x.shape
    n = b * s
    tm = PERM_TILE
    spt = s // tm
    rows = MOE_TILE
    n_tiles = n // rows + N_GROUPS
    p_rows = n_tiles * rows
    bmap = lambda i: (i // spt, 0, 0)
    tri = (jnp.arange(tm)[:, None] >= jnp.arange(tm)[None, :]).astype(BF16)
    hx, meta = pl.pallas_call(
        _moe_route_kernel,
        out_shape=(jax.ShapeDtypeStruct((n, ROUTED_COLS), F32),
                   jax.ShapeDtypeStruct((SUBLANES, n), F32)),
        grid=(b * spt,),
        in_specs=[
            pl.BlockSpec((1, tm, d), lambda i: (i // spt, i % spt, 0)),
            pl.BlockSpec((1, d), lambda i: (0, 0)),
            pl.BlockSpec((1, 1, d), bmap),
            pl.BlockSpec((1, 1, d), bmap),
            pl.BlockSpec((d, ROUTER_LANES), lambda i: (0, 0)),
            pl.BlockSpec((1, ROUTER_LANES), lambda i: (0, 0)),
            pl.BlockSpec((tm, tm), lambda i: (0, 0)),
        ],
        out_specs=(pl.BlockSpec((tm, ROUTED_COLS), lambda i: (i, 0)),
                   pl.BlockSpec((SUBLANES, tm), lambda i: (0, i))),
        scratch_shapes=[pltpu.VMEM((1, LANES), F32)],
        compiler_params=_params(("arbitrary",)),
        name="moe_route",
    )(x, gamma.reshape(1, d), sc, sh, w_router, b_router, tri)

    gidx = meta[0].astype(jnp.int32)
    rank = meta[1].astype(jnp.int32)
    onehot = (gidx[:, None] == jnp.arange(N_GROUPS)[None, :]).astype(jnp.int32)
    count = jnp.sum(onehot, axis=0)
    padded = (count + rows - 1) // rows * rows
    ends = jnp.cumsum(padded)
    dest = jnp.sum(onehot * (ends - padded)[None, :], axis=1) + rank
    n_used = (ends[-1] // rows).reshape(1).astype(jnp.int32)
    tile_start = jnp.arange(n_tiles, dtype=jnp.int32) * rows
    tile_group = jnp.minimum(jnp.sum(ends[None, :] <= tile_start[:, None], axis=1),
                             N_GROUPS - 1).astype(jnp.int32)
    seg_valid_end = ends - padded + count
    n_valid = jnp.clip(jnp.sum(jnp.where(jnp.arange(N_GROUPS)[None, :] == tile_group[:, None],
                                         seg_valid_end[None, :], 0), axis=1) - tile_start,
                       0, rows).astype(jnp.int32)

    ff = EXPERTS_PER_GROUP * EXPERT_FF
    lanes = jnp.arange(ROUTER_LANES)[None, :, None]
    expert = N_GROUPS + EXPERTS_PER_GROUP * jnp.arange(N_GROUPS)[:, None, None] \
        + jnp.arange(ff)[None, None, :] // EXPERT_FF
    expand = (lanes == expert).astype(BF16)
    wmap = lambda t, tg, nt, nv, ds: (tg[t], 0, 0)
    y_tok = pl.pallas_call(
        _moe_expert_kernel,
        out_shape=jax.ShapeDtypeStruct((n, d), F32),
        grid_spec=pltpu.PrefetchScalarGridSpec(
            num_scalar_prefetch=4,
            grid=(n_tiles,),
            in_specs=[
                pl.BlockSpec(memory_space=pl.ANY),
                pl.BlockSpec((EXPERTS_PER_GROUP, d, EXPERT_FF), wmap),
                pl.BlockSpec((EXPERTS_PER_GROUP, d, EXPERT_FF), wmap),
                pl.BlockSpec((EXPERTS_PER_GROUP, EXPERT_FF, d), wmap),
                pl.BlockSpec((1, ROUTER_LANES, ff), wmap),
            ],
            out_specs=pl.BlockSpec(memory_space=pl.ANY),
            scratch_shapes=[pltpu.VMEM((2, rows, ROUTED_COLS), F32),
                            pltpu.VMEM((2, rows, d), F32),
                            pltpu.SMEM((p_rows,), jnp.int32),
                            pltpu.SemaphoreType.DMA((2,)),
                            pltpu.SemaphoreType.DMA((2,))],
        ),
        compiler_params=_params(("arbitrary",)),
        name="moe_experts",
    )(tile_group, n_used, n_valid, dest, hx, w_gate, w_up, w_down, expand)

    return pl.pallas_call(
        _moe_combine_kernel,
        out_shape=jax.ShapeDtypeStruct((b, s, d), F32),
        grid=(b * spt,),
        in_specs=[pl.BlockSpec((1, tm, d), lambda i: (i // spt, i % spt, 0)),
                  pl.BlockSpec((1, 1, d), bmap),
                  pl.BlockSpec((tm, d), lambda i: (i, 0))],
        out_specs=pl.BlockSpec((1, tm, d), lambda i: (i // spt, i % spt, 0)),
        compiler_params=_params(("arbitrary",)),
        name="moe_combine",
    )(x, gate, y_tok)


def kernel(x, c, w_ada, b_ada, norm_mix, norm_ffn, rel_bias, a_w_in, a_w_out, a_q_norm,
           a_k_norm, b_w_in, b_f_bias, b_w_out, b_q_norm, b_k_norm, router_group_w,
           router_group_b, router_expert_w, router_expert_b, w_gate, w_up, w_down):
    b, s, d = x.shape
    depth = w_ada.shape[0]
    mod = _ada(c, w_ada, b_ada).reshape(depth, b, 6, 1, d)
    pad_r = ROUTER_LANES - N_GROUPS - N_EXPERTS
    for i in range(depth):
        sh_m, sc_m, g_m, sh_f, sc_f, g_f = (mod[i, :, k] for k in range(6))
        j = i // 2
        if i % 2 == 0:
            proj = _normproj_a(x, norm_mix[i], sc_m, sh_m, a_w_in[j].astype(BF16))
            outs, lses = [], []
            for g, (window, dil) in enumerate(A_DILATED):
                tab = rel_bias[:, g * N_HEADS:(g + 1) * N_HEADS]
                o, l = _attn_a_group(proj, g, tab, a_q_norm[j], a_k_norm[j], window, dil)
                outs.append(o)
                lses.append(l)
            x = _merge_out(outs, lses, x, g_m, a_w_out[j].astype(BF16))
        else:
            n_qkv = 3 * N_HEADS * HEAD_DIM
            w_in = b_w_in[j]
            w_f = jnp.pad(w_in[:, n_qkv:], ((0, 0), (0, LANES - N_HEADS))).astype(BF16)
            qkv, fg = _normproj(x, norm_mix[i], sc_m, sh_m, w_in[:, :n_qkv].astype(BF16), w_f)
            cum = _cum_forget(fg, b_f_bias[j])
            o = _fox_attention(qkv, cum, b_q_norm[j], b_k_norm[j])
            x = _out_proj(o, x, g_m, b_w_out[j].astype(BF16))
        w_router = jnp.pad(jnp.concatenate([router_group_w[i], router_expert_w[i]], axis=1),
                           ((0, 0), (0, pad_r)))
        b_router = jnp.pad(jnp.concatenate([router_group_b[i], router_expert_b[i]]),
                           (0, pad_r)).reshape(1, ROUTER_LANES)
        x = _moe(x, norm_ffn[i], sc_f, sh_f, g_f, w_router, b_router,
                 w_gate[i].astype(BF16), w_up[i].astype(BF16), w_down[i].astype(BF16))
    return x
```

```python
import functools
import math

import jax
import jax.numpy as jnp
import numpy as np
from jax import lax
from jax.experimental import pallas as pl
from jax.experimental.pallas import tpu as pltpu

F32 = jnp.float32
BF16 = jnp.bfloat16

D_MODEL = 1024
HEAD_DIM = 64
N_HEADS = 16
ATTN_SCALE = HEAD_DIM ** -0.5
A_DILATED = ((128, 1), (512, 4), (2048, 16))
Q_BLOCK = 128
REL_BUCKETS = 32
REL_MAX_DIST = 2048
N_GROUPS = 4
EXPERTS_PER_GROUP = 4
N_EXPERTS = 16
EXPERT_FF = 256
EPS = 1e-6
NEG_INF = -1e30

LANES = 128
VMEM_LIMIT = 48 * 1024 * 1024
LOG2E = 1.4426950408889634
LN2 = 0.6931471805599453
PERM_TILE = 1024
PAIR = 2 * HEAD_DIM
N_PAIRS = N_HEADS // 2


def _params(sem):
    return pltpu.CompilerParams(dimension_semantics=sem, vmem_limit_bytes=VMEM_LIMIT)


def _split3(x):
    hi = x.astype(BF16)
    r1 = x - hi.astype(F32)
    mid = r1.astype(BF16)
    lo = (r1 - mid.astype(F32)).astype(BF16)
    return hi, mid, lo


def _dot(a, b):
    return jnp.dot(a, b, preferred_element_type=F32)


def _dot_nt(a, b):
    return lax.dot_general(a, b, (((1,), (1,)), ((), ())), preferred_element_type=F32)


def _ada_kernel(c_ref, w_ref, b_ref, o_ref):
    c = c_ref[...]
    a = c * (1.0 / (1.0 + jnp.exp(-c)))
    o_ref[0] = _dot(a.astype(BF16), w_ref[0].astype(BF16)) + b_ref[0]


def _ada(c, w_ada, b_ada):
    depth, d, n6 = w_ada.shape
    b = c.shape[0]
    rows = 8
    c_pad = jnp.pad(c, ((0, rows - b), (0, 0)))
    tn = 1536
    out = pl.pallas_call(
        _ada_kernel,
        out_shape=jax.ShapeDtypeStruct((depth, rows, n6), F32),
        grid=(depth, n6 // tn),
        in_specs=[
            pl.BlockSpec((rows, d), lambda i, j: (0, 0)),
            pl.BlockSpec((1, d, tn), lambda i, j: (i, 0, j)),
            pl.BlockSpec((1, 1, tn), lambda i, j: (i, 0, j)),
        ],
        out_specs=pl.BlockSpec((1, rows, tn), lambda i, j: (i, 0, j)),
        compiler_params=_params(("arbitrary", "arbitrary")),
        name="ada_mod",
    )(c_pad, w_ada, b_ada.reshape(depth, 1, n6))
    return out[:, :b]


def _modnorm(x, gamma, sc, sh):
    ms = jnp.mean(x * x, axis=-1, keepdims=True)
    y = x * lax.rsqrt(ms + EPS) * gamma
    return y * (1.0 + sc) + sh


def _normproj_kernel(x_ref, g_ref, sc_ref, sh_ref, w_ref, o_ref, h_ref):
    @pl.when(pl.program_id(1) == 0)
    def _():
        h_ref[...] = _modnorm(x_ref[0], g_ref[...], sc_ref[0], sh_ref[0]).astype(BF16)

    o_ref[0] = _dot(h_ref[...], w_ref[...]).astype(BF16)


def _normproj_gate_kernel(x_ref, g_ref, sc_ref, sh_ref, w_ref, wf_ref, o_ref, f_ref, h_ref):
    @pl.when(pl.program_id(1) == 0)
    def _():
        h = _modnorm(x_ref[0], g_ref[...], sc_ref[0], sh_ref[0]).astype(BF16)
        h_ref[...] = h
        f_ref[0] = _dot(h, wf_ref[...])

    o_ref[0] = _dot(h_ref[...], w_ref[...]).astype(BF16)


def _normproj(x, gamma, sc, sh, w, w_f=None, tm=1024, tn=1536):
    b, s, d = x.shape
    ncols = w.shape[1]
    spt = s // tm
    grid = (b * spt, ncols // tn)
    xmap = lambda i, j: (i // spt, i % spt, 0)
    bmap = lambda i, j: (i // spt, 0, 0)
    in_specs = [
        pl.BlockSpec((1, tm, d), xmap),
        pl.BlockSpec((1, d), lambda i, j: (0, 0)),
        pl.BlockSpec((1, 1, d), bmap),
        pl.BlockSpec((1, 1, d), bmap),
        pl.BlockSpec((d, tn), lambda i, j: (0, j)),
    ]
    o_spec = pl.BlockSpec((1, tm, tn), lambda i, j: (i // spt, i % spt, j))
    scratch = [pltpu.VMEM((tm, d), BF16)]
    if w_f is None:
        return pl.pallas_call(
            _normproj_kernel,
            out_shape=jax.ShapeDtypeStruct((b, s, ncols), BF16),
            grid=grid, in_specs=in_specs, out_specs=o_spec, scratch_shapes=scratch,
            compiler_params=_params(("arbitrary", "arbitrary")),
            name="normproj",
        )(x, gamma.reshape(1, d), sc, sh, w)
    in_specs.append(pl.BlockSpec((d, LANES), lambda i, j: (0, 0)))
    f_spec = pl.BlockSpec((1, tm, LANES), xmap)
    return pl.pallas_call(
        _normproj_gate_kernel,
        out_shape=(jax.ShapeDtypeStruct((b, s, ncols), BF16),
                   jax.ShapeDtypeStruct((b, s, LANES), F32)),
        grid=grid, in_specs=in_specs, out_specs=(o_spec, f_spec), scratch_shapes=scratch,
        compiler_params=_params(("arbitrary", "arbitrary")),
        name="normproj_gate",
    )(x, gamma.reshape(1, d), sc, sh, w, w_f)


def _normproj_a_kernel(x_ref, g_ref, sc_ref, sh_ref, w_ref, o_ref, hf_ref, hp_ref):
    tm = x_ref.shape[1]
    n_slab = x_ref.shape[2] // LANES

    @pl.when(pl.program_id(1) == 0)
    def _():
        h = _modnorm(x_ref[0], g_ref[...], sc_ref[0], sh_ref[0])
        hp_ref[0] = h.astype(BF16)
        for c in range(n_slab):
            hf_ref[c] = h[:, c * LANES:(c + 1) * LANES]
        for g, (_, r) in enumerate(A_DILATED):
            if r == 1:
                continue
            n = tm // r
            for jr in range(r):
                for c in range(n_slab):
                    hp_ref[g, jr * n:(jr + 1) * n, c * LANES:(c + 1) * LANES] = (
                        hf_ref[c, pl.ds(jr, n, stride=r), :].astype(BF16))

    group = (pl.program_id(1) * w_ref.shape[1]) // (3 * N_HEADS * HEAD_DIM)
    o_ref[0] = _dot(hp_ref[group], w_ref[...]).astype(BF16)


def _normproj_a(x, gamma, sc, sh, w, tn=1536):
    b, s, d = x.shape
    tm = PERM_TILE
    ncols = w.shape[1]
    spt = s // tm
    bmap = lambda i, j: (i // spt, 0, 0)
    return pl.pallas_call(
        _normproj_a_kernel,
        out_shape=jax.ShapeDtypeStruct((b, s, ncols), BF16),
        grid=(b * spt, ncols // tn),
        in_specs=[
            pl.BlockSpec((1, tm, d), lambda i, j: (i // spt, i % spt, 0)),
            pl.BlockSpec((1, d), lambda i, j: (0, 0)),
            pl.BlockSpec((1, 1, d), bmap),
            pl.BlockSpec((1, 1, d), bmap),
            pl.BlockSpec((d, tn), lambda i, j: (0, j)),
        ],
        out_specs=pl.BlockSpec((1, tm, tn), lambda i, j: (i // spt, i % spt, j)),
        scratch_shapes=[pltpu.VMEM((d // LANES, tm, LANES), F32),
                        pltpu.VMEM((len(A_DILATED), tm, d), BF16)],
        compiler_params=_params(("arbitrary", "arbitrary")),
        name="normproj_a",
    )(x, gamma.reshape(1, d), sc, sh, w)


def _headnorm(t, gain):
    t = t.astype(F32)
    ms = jnp.mean(t * t, axis=-1, keepdims=True)
    return t * lax.rsqrt(ms + EPS) * gain


def _pairnorm(t, gain2, even):
    t = t.astype(F32)
    sq = t * t
    tot = jnp.sum(sq, axis=-1, keepdims=True)
    s_even = jnp.sum(jnp.where(even, sq, 0.0), axis=-1, keepdims=True)
    inv = jnp.where(even, lax.rsqrt(s_even * (1.0 / HEAD_DIM) + EPS),
                    lax.rsqrt((tot - s_even) * (1.0 / HEAD_DIM) + EPS))
    return t * inv * gain2


def _rows(ref, cols=slice(None)):
    if len(ref.shape) == 2:
        return ref[:, cols]
    return jnp.concatenate([ref[i, :, cols] for i in range(ref.shape[0])], axis=0)


def _store_rows(ref, cols, val):
    if len(ref.shape) == 2:
        ref[:, cols] = val
    else:
        n = ref.shape[1]
        for i in range(ref.shape[0]):
            ref[i, :, cols] = val[i * n:(i + 1) * n]


def _attn_a_kernel(q_ref, k_ref, v_ref, bias_ref, gq_ref, gk_ref, o_ref, lse_ref,
                   kn_ref, vv_ref):
    ub = pl.program_id(2)
    slot = ub % 2
    prev = 1 - slot

    @pl.when(ub == 0)
    def _():
        kn_ref[1] = jnp.zeros(kn_ref.shape[1:], BF16)
        vv_ref[1] = jnp.zeros(vv_ref.shape[1:], BF16)

    first = jnp.minimum(ub, 1)
    lane = lax.broadcasted_iota(jnp.int32, (Q_BLOCK, PAIR), 1)
    even = lane < HEAD_DIM
    lane_row = lax.broadcasted_iota(jnp.int32, (1, PAIR), 1)
    keep_even = jnp.where(lane_row < HEAD_DIM, 1.0, 0.0).astype(BF16)
    keep_odd = jnp.where(lane_row < HEAD_DIM, 0.0, 1.0).astype(BF16)
    gq2 = gq_ref[...] * (ATTN_SCALE * LOG2E)
    gk2 = gk_ref[...]
    for hp in range(N_PAIRS):
        cs = slice(hp * PAIR, (hp + 1) * PAIR)
        kn_ref[slot, :, cs] = _pairnorm(_rows(k_ref, cs), gk2, even).astype(BF16)
    vv_ref[slot] = _rows(v_ref)
    lse_tile = jnp.zeros((Q_BLOCK, LANES), F32)
    for hp in range(N_PAIRS):
        cs = slice(hp * PAIR, (hp + 1) * PAIR)
        qn = _pairnorm(_rows(q_ref, cs), gq2, even)
        qq = jnp.concatenate([jnp.where(even, qn, 0.0), jnp.where(even, 0.0, qn)],
                             axis=0).astype(BF16)
        kcat = jnp.concatenate([kn_ref[prev, :, cs], kn_ref[slot, :, cs]], axis=0)
        s = _dot_nt(qq, kcat) + bias_ref[first, hp]
        m = jnp.max(s, axis=-1, keepdims=True)
        p = jnp.exp2(s - m)
        l = jnp.sum(p, axis=-1, keepdims=True)
        pb = p.astype(BF16)
        vcat = jnp.concatenate([vv_ref[prev, :, cs], vv_ref[slot, :, cs]], axis=0)
        o = (_dot(pb[:Q_BLOCK], vcat * keep_even)
             + _dot(pb[Q_BLOCK:], vcat * keep_odd))
        inv = 1.0 / l
        o = o * jnp.where(even, inv[:Q_BLOCK], inv[Q_BLOCK:])
        _store_rows(o_ref, cs, o.astype(BF16))
        lse = m * LN2 + jnp.log(l)
        lse_tile = jnp.where(lane == 2 * hp, lse[:Q_BLOCK],
                             jnp.where(lane == 2 * hp + 1, lse[Q_BLOCK:], lse_tile))
    _store_rows(lse_ref, slice(None), lse_tile)


def _t5_bucket(dist):
    max_exact = REL_BUCKETS // 2
    d = np.maximum(dist, 0)
    large = max_exact + (np.log(np.maximum(d, 1).astype(np.float32) / max_exact)
                         / math.log(REL_MAX_DIST / max_exact)
                         * (REL_BUCKETS - max_exact)).astype(np.int32)
    large = np.minimum(large, REL_BUCKETS - 1)
    return np.where(d < max_exact, d, large)


def _bias_kernel(tab_ref, onehot_ref, o_ref):
    hi, mid, lo = _split3(tab_ref[...])
    oh = onehot_ref[...]
    o_ref[...] = (_dot(hi, oh) + _dot(mid, oh) + _dot(lo, oh)) * LOG2E


def _bias_tables(rel_bias_g, window, dilation):
    qi = np.arange(Q_BLOCK)[:, None]
    kj = np.arange(2 * Q_BLOCK)[None, :]
    dist = qi + Q_BLOCK - kj
    bucket = _t5_bucket(dist * dilation).reshape(1, -1)
    onehot = jnp.asarray(bucket == np.arange(REL_BUCKETS)[:, None], BF16)
    npos = onehot.shape[1]
    chunk = 4096
    bias = pl.pallas_call(
        _bias_kernel,
        out_shape=jax.ShapeDtypeStruct((N_HEADS, npos), F32),
        grid=(npos // chunk,),
        in_specs=[pl.BlockSpec((N_HEADS, REL_BUCKETS), lambda i: (0, 0)),
                  pl.BlockSpec((REL_BUCKETS, chunk), lambda i: (0, i))],
        out_specs=pl.BlockSpec((N_HEADS, chunk), lambda i: (0, i)),
        compiler_params=_params(("arbitrary",)),
        name="rel_bias_table",
    )(rel_bias_g.T.astype(F32), onehot)
    bias = bias.reshape(N_PAIRS, 2 * Q_BLOCK, 2 * Q_BLOCK)
    band = (dist >= 0) & (dist <= window // dilation)
    first = band & (qi - dist >= 0)
    band = np.concatenate([band, band], axis=0)[None]
    first = np.concatenate([first, first], axis=0)[None]
    return jnp.stack([jnp.where(first, bias, NEG_INF), jnp.where(band, bias, NEG_INF)], axis=0)


def _attn_a_group(proj, g, rel_bias_g, gq, gk, window, dilation):
    b, s, cols = proj.shape
    r = dilation
    nt = s // PERM_TILE
    n_t = PERM_TILE // r
    nb = s // r // Q_BLOCK
    gw = N_HEADS * HEAD_DIM
    tables = _bias_tables(rel_bias_g, window, dilation)
    base = g * 3
    if n_t >= Q_BLOCK:
        bpt = n_t // Q_BLOCK
        blk = lambda w: (None, None, None, Q_BLOCK, w)
        imap = lambda col: (lambda bi, jr, ub: (bi, ub // bpt, jr, ub % bpt, col))
    else:
        blk = lambda w: (None, Q_BLOCK // n_t, None, n_t, w)
        imap = lambda col: (lambda bi, jr, ub: (bi, ub, jr, 0, col))
    pv = proj.reshape(b, nt, r, n_t, cols)
    gains = lambda t: jnp.concatenate([t, t]).reshape(1, PAIR)
    o, lse = pl.pallas_call(
        _attn_a_kernel,
        out_shape=(jax.ShapeDtypeStruct((b, nt, r, n_t, gw), BF16),
                   jax.ShapeDtypeStruct((b, nt, r, n_t, LANES), F32)),
        grid=(b, r, nb),
        in_specs=[
            pl.BlockSpec(blk(gw), imap(base)),
            pl.BlockSpec(blk(gw), imap(base + 1)),
            pl.BlockSpec(blk(gw), imap(base + 2)),
            pl.BlockSpec(tables.shape, lambda bi, jr, ub: (0, 0, 0, 0)),
            pl.BlockSpec((1, PAIR), lambda bi, jr, ub: (0, 0)),
            pl.BlockSpec((1, PAIR), lambda bi, jr, ub: (0, 0)),
        ],
        out_specs=(pl.BlockSpec(blk(gw), imap(0)), pl.BlockSpec(blk(LANES), imap(0))),
        scratch_shapes=[pltpu.VMEM((2, Q_BLOCK, gw), BF16), pltpu.VMEM((2, Q_BLOCK, gw), BF16)],
        compiler_params=_params(("arbitrary", "arbitrary", "arbitrary")),
        name=f"attn_a_g{g}",
    )(pv, pv, pv, tables, gains(gq), gains(gk))
    return o.reshape(b, s, gw), lse.reshape(b, s, LANES)


MERGE_CHUNK = 128


def _merge_out_kernel(o0_ref, o1_ref, o2_ref, l0_ref, l1_ref, l2_ref, x_ref, g_ref, w_ref,
                      e_ref, out_ref, lun_ref, oun_ref, om_ref):
    tm = o0_ref.shape[1]
    n_slab = o0_ref.shape[2] // LANES

    @pl.when(pl.program_id(1) == 0)
    def _():
        for gi, (o_ref, l_ref) in enumerate(((o1_ref, l1_ref), (o2_ref, l2_ref))):
            r = A_DILATED[gi + 1][1]
            n = tm // r
            for jr in range(r):
                rows = slice(jr * n, (jr + 1) * n)
                lun_ref[gi, pl.ds(jr, n, stride=r), :] = l_ref[0, rows, :]
                for c in range(n_slab):
                    oun_ref[gi, c, pl.ds(jr, n, stride=r), :] = (
                        o_ref[0, rows, c * LANES:(c + 1) * LANES].astype(F32))
        expand = e_ref[...]

        def widen(w):
            hi = w.astype(BF16)
            lo = (w - hi.astype(F32)).astype(BF16)
            return _dot(hi, expand) + _dot(lo, expand)

        for k in range(tm // MERGE_CHUNK):
            rs = slice(k * MERGE_CHUNK, (k + 1) * MERGE_CHUNK)
            l0, l1, l2 = l0_ref[0, rs, :], lun_ref[0, rs, :], lun_ref[1, rs, :]
            m = jnp.maximum(jnp.maximum(l0, l1), l2)
            e0, e1, e2 = jnp.exp(l0 - m), jnp.exp(l1 - m), jnp.exp(l2 - m)
            inv = 1.0 / (e0 + e1 + e2)
            o1 = jnp.concatenate([oun_ref[0, c, rs, :] for c in range(n_slab)], axis=1)
            o2 = jnp.concatenate([oun_ref[1, c, rs, :] for c in range(n_slab)], axis=1)
            o = (widen(e0 * inv) * o0_ref[0, rs, :].astype(F32)
                 + widen(e1 * inv) * o1 + widen(e2 * inv) * o2)
            om_ref[rs, :] = o.astype(BF16)

    out_ref[0] = x_ref[0] + g_ref[0] * _dot(om_ref[...], w_ref[...])


def _merge_out(outs, lses, x, gate, w_out, tn=512):
    b, s, d = x.shape
    tm = PERM_TILE
    spt = s // tm
    rmap = lambda i, j: (i // spt, i % spt, 0)
    cmap = lambda i, j: (i // spt, i % spt, j)
    expand = ((jnp.arange(d)[None, :] // HEAD_DIM == jnp.arange(LANES)[:, None])
              & (jnp.arange(LANES)[:, None] < N_HEADS)).astype(BF16)
    o_spec = pl.BlockSpec((1, tm, d), rmap)
    l_spec = pl.BlockSpec((1, tm, LANES), rmap)
    return pl.pallas_call(
        _merge_out_kernel,
        out_shape=jax.ShapeDtypeStruct((b, s, d), F32),
        grid=(b * spt, d // tn),
        in_specs=[o_spec, o_spec, o_spec, l_spec, l_spec, l_spec,
                  pl.BlockSpec((1, tm, tn), cmap),
                  pl.BlockSpec((1, 1, tn), lambda i, j: (i // spt, 0, j)),
                  pl.BlockSpec((d, tn), lambda i, j: (0, j)),
                  pl.BlockSpec((LANES, d), lambda i, j: (0, 0))],
        out_specs=pl.BlockSpec((1, tm, tn), cmap),
        scratch_shapes=[pltpu.VMEM((2, tm, LANES), F32),
                        pltpu.VMEM((2, d // LANES, tm, LANES), F32),
                        pltpu.VMEM((tm, d), BF16)],
        compiler_params=_params(("arbitrary", "arbitrary")),
        name="merge_out",
    )(*outs, *lses, x, gate, w_out, expand)


def _out_kernel(o_ref, x_ref, g_ref, w_ref, out_ref):
    out_ref[0] = x_ref[0] + g_ref[0] * _dot(o_ref[0], w_ref[...])


def _out_proj(o, x, gate, w_out, tm=512):
    b, s, d = x.shape
    spt = s // tm
    xmap = lambda i: (i // spt, i % spt, 0)
    return pl.pallas_call(
        _out_kernel,
        out_shape=jax.ShapeDtypeStruct((b, s, d), F32),
        grid=(b * spt,),
        in_specs=[pl.BlockSpec((1, tm, d), xmap),
                  pl.BlockSpec((1, tm, d), xmap),
                  pl.BlockSpec((1, 1, d), lambda i: (i // spt, 0, 0)),
                  pl.BlockSpec((d, d), lambda i: (0, 0))],
        out_specs=pl.BlockSpec((1, tm, d), xmap),
        compiler_params=_params(("arbitrary",)),
        name="out_proj",
    )(o, x, gate, w_out)


def _cum_kernel(f_ref, fb_ref, tri_ref, o_ref, carry_ref):
    @pl.when(pl.program_id(1) == 0)
    def _():
        carry_ref[...] = jnp.zeros_like(carry_ref)

    z = f_ref[0] + fb_ref[...]
    logf = jnp.minimum(z, 0.0) - jnp.log(1.0 + jnp.exp(-jnp.abs(z)))
    hi, mid, lo = _split3(logf)
    tri = tri_ref[...]
    cum = _dot(tri, hi) + _dot(tri, mid) + _dot(tri, lo) + carry_ref[...]
    o_ref[0] = cum
    carry_ref[...] = cum[-1:, :]


def _cum_forget(fg, f_bias, tc=512):
    b, s, w = fg.shape
    fb = jnp.pad(f_bias, (0, w - f_bias.shape[0])).reshape(1, w)
    tri = (jnp.arange(tc)[:, None] >= jnp.arange(tc)[None, :]).astype(BF16)
    return pl.pallas_call(
        _cum_kernel,
        out_shape=jax.ShapeDtypeStruct((b, s, w), F32),
        grid=(b, s // tc),
        in_specs=[pl.BlockSpec((1, tc, w), lambda i, j: (i, j, 0)),
                  pl.BlockSpec((1, w), lambda i, j: (0, 0)),
                  pl.BlockSpec((tc, tc), lambda i, j: (0, 0))],
        out_specs=pl.BlockSpec((1, tc, w), lambda i, j: (i, j, 0)),
        scratch_shapes=[pltpu.VMEM((1, w), F32)],
        compiler_params=_params(("arbitrary", "arbitrary")),
        name="cum_forget",
    )(fg, fb, tri)


ONES_ROWS = 16
FOX_HEADS_PER_STEP = 8
GATE_ROWS = HEAD_DIM // FOX_HEADS_PER_STEP
assert GATE_ROWS >= 6


def _fox_kernel(q_ref, k_ref, v_ref, cq_ref, ck_ref, gq_ref, gk_ref, o_ref, ka_ref, vt_ref,
                qat_ref, cqt_ref, ckt_ref, s_ref, p_ref, acc_ref, *, tq, tk):
    hg = pl.program_id(1)
    qi = pl.program_id(2)
    n_kt = vt_ref.shape[1]
    nh = ka_ref.shape[0]
    heads = range(nh)

    @pl.when(qi == 0)
    def _():
        gk = gk_ref[...]
        ckt_ref[...] = (ck_ref[0] * LOG2E).T
        hi, mid, lo = (-p.astype(F32) for p in _split3(ckt_ref[pl.ds(nh * hg, nh), :]))
        s_len = ckt_ref.shape[1]
        blocks = [jnp.zeros((HEAD_DIM, s_len), F32)]
        for hh in heads:
            blocks += [jnp.ones((3, s_len), F32), hi[hh:hh + 1], mid[hh:hh + 1], lo[hh:hh + 1],
                       jnp.zeros((GATE_ROWS - 6, s_len), F32)]
        gate_cols = jnp.concatenate(blocks, axis=0).T.astype(BF16)
        for hh in heads:
            sl = slice(hh * HEAD_DIM, (hh + 1) * HEAD_DIM)
            ka_ref[hh, :, :HEAD_DIM] = _headnorm(k_ref[0, :, sl], gk).astype(BF16)
            ka_ref[hh, :, HEAD_DIM:] = gate_cols[:, HEAD_DIM:]
        for pr in range(nh // 2):
            vt = v_ref[0, :, pr * PAIR:(pr + 1) * PAIR].astype(F32).T.astype(BF16)
            for hh in (2 * pr, 2 * pr + 1):
                rows = slice((hh % 2) * HEAD_DIM, (hh % 2 + 1) * HEAD_DIM)
                for jj in range(n_kt):
                    vt_ref[hh, jj, :HEAD_DIM] = vt[rows, jj * tk:(jj + 1) * tk]
                    vt_ref[hh, jj, HEAD_DIM:] = jnp.ones((ONES_ROWS, tk), BF16)

    even = lax.broadcasted_iota(jnp.int32, (tq, PAIR), 1) < HEAD_DIM
    gq2 = gq_ref[...] * (ATTN_SCALE * LOG2E)
    cqt_ref[...] = (cq_ref[0] * LOG2E).T
    hi, mid, lo = (p.astype(F32) for p in _split3(cqt_ref[pl.ds(nh * hg, nh), :]))
    for pr in range(nh // 2):
        q_t = _pairnorm(q_ref[0, :, pr * PAIR:(pr + 1) * PAIR], gq2, even).T
        for hh in (2 * pr, 2 * pr + 1):
            r0 = (hh % 2) * HEAD_DIM
            qat_ref[hh, :HEAD_DIM] = q_t[r0:r0 + HEAD_DIM].astype(BF16)
            pieces = [hi[hh:hh + 1], mid[hh:hh + 1], lo[hh:hh + 1], jnp.ones((3, tq), F32)]
            before, after = hh * GATE_ROWS, HEAD_DIM - (hh + 1) * GATE_ROWS + GATE_ROWS - 6
            if before:
                pieces.insert(0, jnp.zeros((before, tq), F32))
            pieces.append(jnp.zeros((after, tq), F32))
            qat_ref[hh, HEAD_DIM:] = jnp.concatenate(pieces, axis=0).astype(BF16)
    key = lax.broadcasted_iota(jnp.int32, (tk, tq), 0)
    qry = lax.broadcasted_iota(jnp.int32, (tk, tq), 1)
    ahead = key - qry

    def scores(t, hh, diagonal):
        start = pl.multiple_of(t * tk, tk)
        st = _dot(ka_ref[hh, pl.ds(start, tk), :], qat_ref[hh])
        if diagonal is None:
            st = jnp.where(ahead > jnp.where(t == qi, 0, tk), NEG_INF, st)
        elif diagonal:
            st = jnp.where(ahead > 0, NEG_INF, st)
        s_ref[hh] = st
        return jnp.max(st, axis=0, keepdims=True)

    def pv(t, hh, alpha_prev):
        acc_ref[hh] = alpha_prev * acc_ref[hh] + _dot(vt_ref[hh, t], p_ref[hh])

    def trip(i, carry, next_diagonal):
        out = []
        for hh in heads:
            m, alpha_prev, mx = carry[hh]
            pv(jnp.maximum(i - 1, 0), hh, alpha_prev)
            m_new = jnp.maximum(m, mx)
            alpha = jnp.exp2(m - m_new)
            p_ref[hh] = jnp.exp2(s_ref[hh] - m_new).astype(BF16)
            if next_diagonal is not None:
                mx = scores(i + 1, hh, next_diagonal)
            out.append((m_new, alpha, mx))
        return tuple(out)

    p_ref[...] = jnp.zeros_like(p_ref)
    acc_ref[...] = jnp.zeros_like(acc_ref)
    carry = tuple((jnp.full((1, tq), NEG_INF, F32), jnp.ones((1, tq), F32),
                   scores(0, hh, None)) for hh in heads)
    carry = lax.fori_loop(0, jnp.maximum(qi - 1, 0),
                          functools.partial(trip, next_diagonal=False), carry)
    carry = lax.fori_loop(jnp.maximum(qi - 1, 0), qi,
                          functools.partial(trip, next_diagonal=True), carry)
    carry = trip(qi, carry, None)
    for pr in range(nh // 2):
        o_t = []
        for hh in (2 * pr, 2 * pr + 1):
            pv(qi, hh, carry[hh][1])
            acc = acc_ref[hh]
            o_t.append(acc[:HEAD_DIM] / acc[HEAD_DIM:HEAD_DIM + 1])
        o_ref[0, :, pr * PAIR:(pr + 1) * PAIR] = (
            jnp.concatenate(o_t, axis=0).T.astype(BF16))


def _fox_attention(qkv, cum, gq, gk, tq=256):
    b, s, _ = qkv.shape
    tk = tq
    nq = s // tq
    nh = FOX_HEADS_PER_STEP
    groups = N_HEADS // nh
    gw = nh * HEAD_DIM
    vrows = HEAD_DIM + ONES_ROWS
    return pl.pallas_call(
        functools.partial(_fox_kernel, tq=tq, tk=tk),
        out_shape=jax.ShapeDtypeStruct((b, s, N_HEADS * HEAD_DIM), BF16),
        grid=(b, groups, nq),
        in_specs=[
            pl.BlockSpec((1, tq, gw), lambda bi, hg, qi: (bi, qi, hg)),
            pl.BlockSpec((1, s, gw), lambda bi, hg, qi: (bi, 0, groups + hg)),
            pl.BlockSpec((1, s, gw), lambda bi, hg, qi: (bi, 0, 2 * groups + hg)),
            pl.BlockSpec((1, tq, LANES), lambda bi, hg, qi: (bi, qi, 0)),
            pl.BlockSpec((1, s, LANES), lambda bi, hg, qi: (bi, 0, 0)),
            pl.BlockSpec((1, PAIR), lambda bi, hg, qi: (0, 0)),
            pl.BlockSpec((1, HEAD_DIM), lambda bi, hg, qi: (0, 0)),
        ],
        out_specs=pl.BlockSpec((1, tq, gw), lambda bi, hg, qi: (bi, qi, hg)),
        scratch_shapes=[pltpu.VMEM((nh, s, 2 * HEAD_DIM), BF16),
                        pltpu.VMEM((nh, s // tk, vrows, tk), BF16),
                        pltpu.VMEM((nh, 2 * HEAD_DIM, tq), BF16),
                        pltpu.VMEM((LANES, tq), F32),
                        pltpu.VMEM((LANES, s), F32),
                        pltpu.VMEM((nh, tk, tq), F32),
                        pltpu.VMEM((nh, tk, tq), BF16),
                        pltpu.VMEM((nh, vrows, tq), F32)],
        compiler_params=_params(("arbitrary", "arbitrary", "arbitrary")),
        name="fox_attn",
    )(qkv, qkv, qkv, cum, cum, jnp.concatenate([gq, gq]).reshape(1, PAIR),
      gk.reshape(1, HEAD_DIM))


ROUTER_LANES = LANES


def _route(logits):
    lane = lax.broadcasted_iota(jnp.int32, logits.shape, 1)
    lane_f = lane.astype(F32)
    big = float(ROUTER_LANES)
    is_g = lane < N_GROUPS
    gl = jnp.where(is_g, logits, -jnp.inf)
    gmax = jnp.max(gl, axis=-1, keepdims=True)
    gsum = jnp.sum(jnp.where(is_g, jnp.exp(logits - gmax), 0.0), axis=-1, keepdims=True)
    g_w = 1.0 / gsum
    gidx = jnp.min(jnp.where(gl == gmax, lane_f, big), axis=-1, keepdims=True)
    lo = N_GROUPS + EXPERTS_PER_GROUP * gidx
    in_grp = (lane_f >= lo) & (lane_f < lo + EXPERTS_PER_GROUP)
    el = jnp.where(in_grp, logits, -jnp.inf)
    t1 = jnp.max(el, axis=-1, keepdims=True)
    i1 = jnp.min(jnp.where(el == t1, lane_f, big), axis=-1, keepdims=True)
    el2 = jnp.where(lane_f == i1, -jnp.inf, el)
    t2 = jnp.max(el2, axis=-1, keepdims=True)
    i2 = jnp.min(jnp.where(el2 == t2, lane_f, big), axis=-1, keepdims=True)
    e2 = jnp.exp(t2 - t1)
    w1 = g_w / (1.0 + e2)
    w2 = w1 * e2
    gates = jnp.where(lane_f == i1, w1, jnp.where(lane_f == i2, w2, 0.0))
    return jnp.where(lane == 0, gidx, gates)


MOE_TILE = 512
SUBLANES = 8
ROUTED_COLS = D_MODEL + LANES


def _moe_route_kernel(x_ref, gam_ref, sc_ref, sh_ref, wr_ref, br_ref, tri_ref, o_ref, meta_ref,
                      cnt_ref):
    h = _modnorm(x_ref[0], gam_ref[...], sc_ref[0], sh_ref[0])
    h_hi = h.astype(BF16)
    h_lo = (h - h_hi.astype(F32)).astype(BF16)
    wr = wr_ref[...]
    w_hi = wr.astype(BF16)
    w_lo = (wr - w_hi.astype(F32)).astype(BF16)
    logits = _dot(h_hi, w_hi) + _dot(h_hi, w_lo) + _dot(h_lo, w_hi) + br_ref[...]
    route = _route(logits)
    tm = route.shape[0]

    @pl.when(pl.program_id(0) == 0)
    def _():
        cnt_ref[...] = jnp.zeros_like(cnt_ref)

    lane = lax.broadcasted_iota(jnp.int32, route.shape, 1)
    member = jnp.where((lane.astype(F32) == route[:, 0:1]) & (lane < N_GROUPS), 1.0, 0.0)
    incl = _dot(tri_ref[...], member.astype(BF16))
    rank = jnp.sum(member * (incl - member + cnt_ref[...]), axis=-1, keepdims=True)
    cnt_ref[...] = cnt_ref[...] + incl[tm - 1:tm, :]
    meta = jnp.where(lane == 0, route[:, 0:1], jnp.where(lane == 1, rank, 0.0))
    meta_ref[...] = meta.T[:SUBLANES, :]

    o_ref[:, :D_MODEL] = h
    o_ref[:, D_MODEL:] = route


def _row_copy(src, src_row, dst, dst_row, sem):
    return pltpu.make_async_copy(src.at[pl.ds(src_row, 1), :], dst.at[pl.ds(dst_row, 1), :], sem)


def _moe_expert_kernel(tg_ref, nt_ref, nv_ref, dest_ref, hx_hbm, wg_ref, wu_ref, wd_ref, e_ref,
                       y_hbm, buf, ybuf, src_ref, gsem, ssem):
    t = pl.program_id(0)
    n_used = nt_ref[0]
    slot = t % 2
    rows = buf.shape[1]
    unroll = 8

    @pl.when(t == 0)
    def _():
        def clear(p, c):
            src_ref[p] = 0
            return c
        lax.fori_loop(0, src_ref.shape[0], clear, 0, unroll=unroll)

        def invert(tok, c):
            src_ref[dest_ref[tok]] = tok
            return c
        lax.fori_loop(0, dest_ref.shape[0], invert, 0, unroll=unroll)

    def gather_start(tile, sl):
        def body(r, c):
            _row_copy(hx_hbm, src_ref[tile * rows + r], buf.at[sl], r, gsem.at[sl]).start()
            return c
        lax.fori_loop(0, rows, body, 0, unroll=unroll)

    def gather_wait(sl):
        def body(r, c):
            _row_copy(hx_hbm, 0, buf.at[sl], r, gsem.at[sl]).wait()
            return c
        lax.fori_loop(0, rows, body, 0, unroll=unroll)

    def valid_rows_loop(tile, body):
        n_valid = nv_ref[tile]
        chunks = n_valid // unroll

        def chunk(k, c):
            for i in range(unroll):
                body(k * unroll + i)
            return c
        lax.fori_loop(0, chunks, chunk, 0)

        def single(r, c):
            body(r)
            return c
        lax.fori_loop(chunks * unroll, n_valid, single, 0)

    def scatter_start(tile, sl):
        valid_rows_loop(tile, lambda r: _row_copy(
            ybuf.at[sl], r, y_hbm, src_ref[tile * rows + r], ssem.at[sl]).start(priority=1))

    def scatter_wait(tile, sl):
        valid_rows_loop(tile, lambda r: _row_copy(ybuf.at[sl], r, y_hbm, 0, ssem.at[sl]).wait())

    @pl.when(t == 0)
    def _():
        gather_start(0, 0)

    @pl.when(t + 1 < n_used)
    def _():
        gather_start(t + 1, 1 - slot)

    @pl.when(t < n_used)
    def _():
        gather_wait(slot)
        h = buf[slot, :, :D_MODEL].astype(BF16)
        gates = buf[slot, :, D_MODEL:]
        g_hi = gates.astype(BF16)
        g_lo = (gates - g_hi.astype(F32)).astype(BF16)
        y = None
        for e in range(EXPERTS_PER_GROUP):
            a = _dot(h, wg_ref[e])
            u = _dot(h, wu_ref[e])
            spread = e_ref[0, :, e * EXPERT_FF:(e + 1) * EXPERT_FF]
            gexp = _dot(g_hi, spread) + _dot(g_lo, spread)
            hid = a * (1.0 / (1.0 + jnp.exp(-a))) * u * gexp
            part = _dot(hid.astype(BF16), wd_ref[e])
            y = part if y is None else y + part

        @pl.when(t >= 2)
        def _():
            scatter_wait(t - 2, slot)

        ybuf[slot] = y
        scatter_start(t, slot)

    @pl.when(t == pl.num_programs(0) - 1)
    def _():
        for back in (2, 1):
            scatter_wait(n_used - back, (n_used - back) % 2)


def _moe_combine_kernel(x_ref, g_ref, y_ref, o_ref):
    tm = x_ref.shape[1]
    o_ref[0] = x_ref[0] + g_ref[0] * y_ref[...]


def _moe(x, gamma, sc, sh, gate, w_router, b_router, w_gate, w_up, w_down):
    b, s, d = x.shape
    n = b * s
    tm = PERM_TILE
    spt = s // tm
    rows = MOE_TILE
    n_tiles = n // rows + N_GROUPS
    p_rows = n_tiles * rows
    bmap = lambda i: (i // spt, 0, 0)
    tri = (jnp.arange(tm)[:, None] >= jnp.arange(tm)[None, :]).astype(BF16)
    hx, meta = pl.pallas_call(
        _moe_route_kernel,
        out_shape=(jax.ShapeDtypeStruct((n, ROUTED_COLS), F32),
                   jax.ShapeDtypeStruct((SUBLANES, n), F32)),
        grid=(b * spt,),
        in_specs=[
            pl.BlockSpec((1, tm, d), lambda i: (i // spt, i % spt, 0)),
            pl.BlockSpec((1, d), lambda i: (0, 0)),
            pl.BlockSpec((1, 1, d), bmap),
            pl.BlockSpec((1, 1, d), bmap),
            pl.BlockSpec((d, ROUTER_LANES), lambda i: (0, 0)),
            pl.BlockSpec((1, ROUTER_LANES), lambda i: (0, 0)),
            pl.BlockSpec((tm, tm), lambda i: (0, 0)),
        ],
        out_specs=(pl.BlockSpec((tm, ROUTED_COLS), lambda i: (i, 0)),
                   pl.BlockSpec((SUBLANES, tm), lambda i: (0, i))),
        scratch_shapes=[pltpu.VMEM((1, LANES), F32)],
        compiler_params=_params(("arbitrary",)),
        name="moe_route",
    )(x, gamma.reshape(1, d), sc, sh, w_router, b_router, tri)

    gidx = meta[0].astype(jnp.int32)
    rank = meta[1].astype(jnp.int32)
    onehot = (gidx[:, None] == jnp.arange(N_GROUPS)[None, :]).astype(jnp.int32)
    count = jnp.sum(onehot, axis=0)
    padded = (count + rows - 1) // rows * rows
    ends = jnp.cumsum(padded)
    dest = jnp.sum(onehot * (ends - padded)[None, :], axis=1) + rank
    n_used = (ends[-1] // rows).reshape(1).astype(jnp.int32)
    tile_start = jnp.arange(n_tiles, dtype=jnp.int32) * rows
    tile_group = jnp.minimum(jnp.sum(ends[None, :] <= tile_start[:, None], axis=1),
                             N_GROUPS - 1).astype(jnp.int32)
    seg_valid_end = ends - padded + count
    n_valid = jnp.clip(jnp.sum(jnp.where(jnp.arange(N_GROUPS)[None, :] == tile_group[:, None],
                                         seg_valid_end[None, :], 0), axis=1) - tile_start,
                       0, rows).astype(jnp.int32)

    ff = EXPERTS_PER_GROUP * EXPERT_FF
    lanes = jnp.arange(ROUTER_LANES)[None, :, None]
    expert = N_GROUPS + EXPERTS_PER_GROUP * jnp.arange(N_GROUPS)[:, None, None] \
        + jnp.arange(ff)[None, None, :] // EXPERT_FF
    expand = (lanes == expert).astype(BF16)
    wmap = lambda t, tg, nt, nv, ds: (tg[t], 0, 0)
    y_tok = pl.pallas_call(
        _moe_expert_kernel,
        out_shape=jax.ShapeDtypeStruct((n, d), F32),
        grid_spec=pltpu.PrefetchScalarGridSpec(
            num_scalar_prefetch=4,
            grid=(n_tiles,),
            in_specs=[
                pl.BlockSpec(memory_space=pl.ANY),
                pl.BlockSpec((EXPERTS_PER_GROUP, d, EXPERT_FF), wmap),
                pl.BlockSpec((EXPERTS_PER_GROUP, d, EXPERT_FF), wmap),
                pl.BlockSpec((EXPERTS_PER_GROUP, EXPERT_FF, d), wmap),
                pl.BlockSpec((1, ROUTER_LANES, ff), wmap),
            ],
            out_specs=pl.BlockSpec(memory_space=pl.ANY),
            scratch_shapes=[pltpu.VMEM((2, rows, ROUTED_COLS), F32),
                            pltpu.VMEM((2, rows, d), F32),
                            pltpu.SMEM((p_rows,), jnp.int32),
                            pltpu.SemaphoreType.DMA((2,)),
                            pltpu.SemaphoreType.DMA((2,))],
        ),
        compiler_params=_params(("arbitrary",)),
        name="moe_experts",
    )(tile_group, n_used, n_valid, dest, hx, w_gate, w_up, w_down, expand)

    return pl.pallas_call(
        _moe_combine_kernel,
        out_shape=jax.ShapeDtypeStruct((b, s, d), F32),
        grid=(b * spt,),
        in_specs=[pl.BlockSpec((1, tm, d), lambda i: (i // spt, i % spt, 0)),
                  pl.BlockSpec((1, 1, d), bmap),
                  pl.BlockSpec((tm, d), lambda i: (i, 0))],
        out_specs=pl.BlockSpec((1, tm, d), lambda i: (i // spt, i % spt, 0)),
        compiler_params=_params(("arbitrary",)),
        name="moe_combine",
    )(x, gate, y_tok)


def kernel(x, c, w_ada, b_ada, norm_mix, norm_ffn, rel_bias, a_w_in, a_w_out, a_q_norm,
           a_k_norm, b_w_in, b_f_bias, b_w_out, b_q_norm, b_k_norm, router_group_w,
           router_group_b, router_expert_w, router_expert_b, w_gate, w_up, w_down):
    b, s, d = x.shape
    depth = w_ada.shape[0]
    mod = _ada(c, w_ada, b_ada).reshape(depth, b, 6, 1, d)
    pad_r = ROUTER_LANES - N_GROUPS - N_EXPERTS
    for i in range(depth):
        sh_m, sc_m, g_m, sh_f, sc_f, g_f = (mod[i, :, k] for k in range(6))
        j = i // 2
        if i % 2 == 0:
            proj = _normproj_a(x, norm_mix[i], sc_m, sh_m, a_w_in[j].astype(BF16))
            outs, lses = [], []
            for g, (window, dil) in enumerate(A_DILATED):
                tab = rel_bias[:, g * N_HEADS:(g + 1) * N_HEADS]
                o, l = _attn_a_group(proj, g, tab, a_q_norm[j], a_k_norm[j], window, dil)
                outs.append(o)
                lses.append(l)
            x = _merge_out(outs, lses, x, g_m, a_w_out[j].astype(BF16))
        else:
            n_qkv = 3 * N_HEADS * HEAD_DIM
            w_in = b_w_in[j]
            w_f = jnp.pad(w_in[:, n_qkv:], ((0, 0), (0, LANES - N_HEADS))).astype(BF16)
            qkv, fg = _normproj(x, norm_mix[i], sc_m, sh_m, w_in[:, :n_qkv].astype(BF16), w_f)
            cum = _cum_forget(fg, b_f_bias[j])
            o = _fox_attention(qkv, cum, b_q_norm[j], b_k_norm[j])
            x = _out_proj(o, x, g_m, b_w_out[j].astype(BF16))
        w_router = jnp.pad(jnp.concatenate([router_group_w[i], router_expert_w[i]], axis=1),
                           ((0, 0), (0, pad_r)))
        b_router = jnp.pad(jnp.concatenate([router_group_b[i], router_expert_b[i]]),
                           (0, pad_r)).reshape(1, ROUTER_LANES)
        x = _moe(x, norm_ffn[i], sc_f, sh_f, g_f, w_router, b_router,
                 w_gate[i].astype(BF16), w_up[i].astype(BF16), w_down[i].astype(BF16))
    return x
```

```python
import functools
import math

import jax
import jax.numpy as jnp
import numpy as np
from jax import lax
from jax.experimental import pallas as pl
from jax.experimental.pallas import tpu as pltpu

F32 = jnp.float32
BF16 = jnp.bfloat16

D_MODEL = 1024
HEAD_DIM = 64
N_HEADS = 16
ATTN_SCALE = HEAD_DIM ** -0.5
A_DILATED = ((128, 1), (512, 4), (2048, 16))
Q_BLOCK = 128
REL_BUCKETS = 32
REL_MAX_DIST = 2048
N_GROUPS = 4
EXPERTS_PER_GROUP = 4
N_EXPERTS = 16
EXPERT_FF = 256
EPS = 1e-6
NEG_INF = -1e30

LANES = 128
VMEM_LIMIT = 48 * 1024 * 1024
LOG2E = 1.4426950408889634
LN2 = 0.6931471805599453
PERM_TILE = 1024
PAIR = 2 * HEAD_DIM
N_PAIRS = N_HEADS // 2


def _params(sem):
    return pltpu.CompilerParams(dimension_semantics=sem, vmem_limit_bytes=VMEM_LIMIT)


def _split3(x):
    hi = x.astype(BF16)
    r1 = x - hi.astype(F32)
    mid = r1.astype(BF16)
    lo = (r1 - mid.astype(F32)).astype(BF16)
    return hi, mid, lo


def _dot(a, b):
    return jnp.dot(a, b, preferred_element_type=F32)


def _dot_nt(a, b):
    return lax.dot_general(a, b, (((1,), (1,)), ((), ())), preferred_element_type=F32)


def _ada_kernel(c_ref, w_ref, b_ref, o_ref):
    c = c_ref[...]
    a = c * (1.0 / (1.0 + jnp.exp(-c)))
    o_ref[0] = _dot(a.astype(BF16), w_ref[0].astype(BF16)) + b_ref[0]


def _ada(c, w_ada, b_ada):
    depth, d, n6 = w_ada.shape
    b = c.shape[0]
    rows = 8
    c_pad = jnp.pad(c, ((0, rows - b), (0, 0)))
    tn = 1536
    out = pl.pallas_call(
        _ada_kernel,
        out_shape=jax.ShapeDtypeStruct((depth, rows, n6), F32),
        grid=(depth, n6 // tn),
        in_specs=[
            pl.BlockSpec((rows, d), lambda i, j: (0, 0)),
            pl.BlockSpec((1, d, tn), lambda i, j: (i, 0, j)),
            pl.BlockSpec((1, 1, tn), lambda i, j: (i, 0, j)),
        ],
        out_specs=pl.BlockSpec((1, rows, tn), lambda i, j: (i, 0, j)),
        compiler_params=_params(("arbitrary", "arbitrary")),
        name="ada_mod",
    )(c_pad, w_ada, b_ada.reshape(depth, 1, n6))
    return out[:, :b]


def _modnorm(x, gamma, sc, sh):
    ms = jnp.mean(x * x, axis=-1, keepdims=True)
    y = x * lax.rsqrt(ms + EPS) * gamma
    return y * (1.0 + sc) + sh


def _normproj_kernel(x_ref, g_ref, sc_ref, sh_ref, w_ref, o_ref, h_ref):
    @pl.when(pl.program_id(1) == 0)
    def _():
        h_ref[...] = _modnorm(x_ref[0], g_ref[...], sc_ref[0], sh_ref[0]).astype(BF16)

    o_ref[0] = _dot(h_ref[...], w_ref[...]).astype(BF16)


def _normproj_gate_kernel(x_ref, g_ref, sc_ref, sh_ref, w_ref, wf_ref, o_ref, f_ref, h_ref):
    @pl.when(pl.program_id(1) == 0)
    def _():
        h = _modnorm(x_ref[0], g_ref[...], sc_ref[0], sh_ref[0]).astype(BF16)
        h_ref[...] = h
        f_ref[0] = _dot(h, wf_ref[...])

    o_ref[0] = _dot(h_ref[...], w_ref[...]).astype(BF16)


def _normproj(x, gamma, sc, sh, w, w_f=None, tm=1024, tn=1536):
    b, s, d = x.shape
    ncols = w.shape[1]
    spt = s // tm
    grid = (b * spt, ncols // tn)
    xmap = lambda i, j: (i // spt, i % spt, 0)
    bmap = lambda i, j: (i // spt, 0, 0)
    in_specs = [
        pl.BlockSpec((1, tm, d), xmap),
        pl.BlockSpec((1, d), lambda i, j: (0, 0)),
        pl.BlockSpec((1, 1, d), bmap),
        pl.BlockSpec((1, 1, d), bmap),
        pl.BlockSpec((d, tn), lambda i, j: (0, j)),
    ]
    o_spec = pl.BlockSpec((1, tm, tn), lambda i, j: (i // spt, i % spt, j))
    scratch = [pltpu.VMEM((tm, d), BF16)]
    if w_f is None:
        return pl.pallas_call(
            _normproj_kernel,
            out_shape=jax.ShapeDtypeStruct((b, s, ncols), BF16),
            grid=grid, in_specs=in_specs, out_specs=o_spec, scratch_shapes=scratch,
            compiler_params=_params(("arbitrary", "arbitrary")),
            name="normproj",
        )(x, gamma.reshape(1, d), sc, sh, w)
    in_specs.append(pl.BlockSpec((d, LANES), lambda i, j: (0, 0)))
    f_spec = pl.BlockSpec((1, tm, LANES), xmap)
    return pl.pallas_call(
        _normproj_gate_kernel,
        out_shape=(jax.ShapeDtypeStruct((b, s, ncols), BF16),
                   jax.ShapeDtypeStruct((b, s, LANES), F32)),
        grid=grid, in_specs=in_specs, out_specs=(o_spec, f_spec), scratch_shapes=scratch,
        compiler_params=_params(("arbitrary", "arbitrary")),
        name="normproj_gate",
    )(x, gamma.reshape(1, d), sc, sh, w, w_f)


def _normproj_a_kernel(x_ref, g_ref, sc_ref, sh_ref, w_ref, o_ref, hf_ref, hp_ref):
    tm = x_ref.shape[1]
    n_slab = x_ref.shape[2] // LANES

    @pl.when(pl.program_id(1) == 0)
    def _():
        h = _modnorm(x_ref[0], g_ref[...], sc_ref[0], sh_ref[0])
        hp_ref[0] = h.astype(BF16)
        for c in range(n_slab):
            hf_ref[c] = h[:, c * LANES:(c + 1) * LANES]
        for g, (_, r) in enumerate(A_DILATED):
            if r == 1:
                continue
            n = tm // r
            for jr in range(r):
                for c in range(n_slab):
                    hp_ref[g, jr * n:(jr + 1) * n, c * LANES:(c + 1) * LANES] = (
                        hf_ref[c, pl.ds(jr, n, stride=r), :].astype(BF16))

    group = (pl.program_id(1) * w_ref.shape[1]) // (3 * N_HEADS * HEAD_DIM)
    o_ref[0] = _dot(hp_ref[group], w_ref[...]).astype(BF16)


def _normproj_a(x, gamma, sc, sh, w, tn=1536):
    b, s, d = x.shape
    tm = PERM_TILE
    ncols = w.shape[1]
    spt = s // tm
    bmap = lambda i, j: (i // spt, 0, 0)
    return pl.pallas_call(
        _normproj_a_kernel,
        out_shape=jax.ShapeDtypeStruct((b, s, ncols), BF16),
        grid=(b * spt, ncols // tn),
        in_specs=[
            pl.BlockSpec((1, tm, d), lambda i, j: (i // spt, i % spt, 0)),
            pl.BlockSpec((1, d), lambda i, j: (0, 0)),
            pl.BlockSpec((1, 1, d), bmap),
            pl.BlockSpec((1, 1, d), bmap),
            pl.BlockSpec((d, tn), lambda i, j: (0, j)),
        ],
        out_specs=pl.BlockSpec((1, tm, tn), lambda i, j: (i // spt, i % spt, j)),
        scratch_shapes=[pltpu.VMEM((d // LANES, tm, LANES), F32),
                        pltpu.VMEM((len(A_DILATED), tm, d), BF16)],
        compiler_params=_params(("arbitrary", "arbitrary")),
        name="normproj_a",
    )(x, gamma.reshape(1, d), sc, sh, w)


def _headnorm(t, gain):
    t = t.astype(F32)
    ms = jnp.mean(t * t, axis=-1, keepdims=True)
    return t * lax.rsqrt(ms + EPS) * gain


def _pairnorm(t, gain2, even):
    t = t.astype(F32)
    sq = t * t
    tot = jnp.sum(sq, axis=-1, keepdims=True)
    s_even = jnp.sum(jnp.where(even, sq, 0.0), axis=-1, keepdims=True)
    inv = jnp.where(even, lax.rsqrt(s_even * (1.0 / HEAD_DIM) + EPS),
                    lax.rsqrt((tot - s_even) * (1.0 / HEAD_DIM) + EPS))
    return t * inv * gain2


def _rows(ref, cols=slice(None)):
    if len(ref.shape) == 2:
        return ref[:, cols]
    return jnp.concatenate([ref[i, :, cols] for i in range(ref.shape[0])], axis=0)


def _store_rows(ref, cols, val):
    if len(ref.shape) == 2:
        ref[:, cols] = val
    else:
        n = ref.shape[1]
        for i in range(ref.shape[0]):
            ref[i, :, cols] = val[i * n:(i + 1) * n]


def _attn_a_kernel(q_ref, k_ref, v_ref, bias_ref, gq_ref, gk_ref, o_ref, lse_ref,
                   kn_ref, vv_ref):
    ub = pl.program_id(2)
    slot = ub % 2
    prev = 1 - slot

    @pl.when(ub == 0)
    def _():
        kn_ref[1] = jnp.zeros(kn_ref.shape[1:], BF16)
        vv_ref[1] = jnp.zeros(vv_ref.shape[1:], BF16)

    first = jnp.minimum(ub, 1)
    lane = lax.broadcasted_iota(jnp.int32, (Q_BLOCK, PAIR), 1)
    even = lane < HEAD_DIM
    lane_row = lax.broadcasted_iota(jnp.int32, (1, PAIR), 1)
    keep_even = jnp.where(lane_row < HEAD_DIM, 1.0, 0.0).astype(BF16)
    keep_odd = jnp.where(lane_row < HEAD_DIM, 0.0, 1.0).astype(BF16)
    gq2 = gq_ref[...] * (ATTN_SCALE * LOG2E)
    gk2 = gk_ref[...]
    for hp in range(N_PAIRS):
        cs = slice(hp * PAIR, (hp + 1) * PAIR)
        kn_ref[slot, :, cs] = _pairnorm(_rows(k_ref, cs), gk2, even).astype(BF16)
    vv_ref[slot] = _rows(v_ref)
    lse_tile = jnp.zeros((Q_BLOCK, LANES), F32)
    for hp in range(N_PAIRS):
        cs = slice(hp * PAIR, (hp + 1) * PAIR)
        qn = _pairnorm(_rows(q_ref, cs), gq2, even)
        qq = jnp.concatenate([jnp.where(even, qn, 0.0), jnp.where(even, 0.0, qn)],
                             axis=0).astype(BF16)
        kcat = jnp.concatenate([kn_ref[prev, :, cs], kn_ref[slot, :, cs]], axis=0)
        s = _dot_nt(qq, kcat) + bias_ref[first, hp]
        m = jnp.max(s, axis=-1, keepdims=True)
        p = jnp.exp2(s - m)
        l = jnp.sum(p, axis=-1, keepdims=True)
        pb = p.astype(BF16)
        vcat = jnp.concatenate([vv_ref[prev, :, cs], vv_ref[slot, :, cs]], axis=0)
        o = (_dot(pb[:Q_BLOCK], vcat * keep_even)
             + _dot(pb[Q_BLOCK:], vcat * keep_odd))
        inv = 1.0 / l
        o = o * jnp.where(even, inv[:Q_BLOCK], inv[Q_BLOCK:])
        _store_rows(o_ref, cs, o.astype(BF16))
        lse = m * LN2 + jnp.log(l)
        lse_tile = jnp.where(lane == 2 * hp, lse[:Q_BLOCK],
                             jnp.where(lane == 2 * hp + 1, lse[Q_BLOCK:], lse_tile))
    _store_rows(lse_ref, slice(None), lse_tile)


def _t5_bucket(dist):
    max_exact = REL_BUCKETS // 2
    d = np.maximum(dist, 0)
    large = max_exact + (np.log(np.maximum(d, 1).astype(np.float32) / max_exact)
                         / math.log(REL_MAX_DIST / max_exact)
                         * (REL_BUCKETS - max_exact)).astype(np.int32)
    large = np.minimum(large, REL_BUCKETS - 1)
    return np.where(d < max_exact, d, large)


def _bias_kernel(tab_ref, onehot_ref, o_ref):
    hi, mid, lo = _split3(tab_ref[...])
    oh = onehot_ref[...]
    o_ref[...] = (_dot(hi, oh) + _dot(mid, oh) + _dot(lo, oh)) * LOG2E


def _bias_tables(rel_bias_g, window, dilation):
    qi = np.arange(Q_BLOCK)[:, None]
    kj = np.arange(2 * Q_BLOCK)[None, :]
    dist = qi + Q_BLOCK - kj
    bucket = _t5_bucket(dist * dilation).reshape(1, -1)
    onehot = jnp.asarray(bucket == np.arange(REL_BUCKETS)[:, None], BF16)
    npos = onehot.shape[1]
    chunk = 4096
    bias = pl.pallas_call(
        _bias_kernel,
        out_shape=jax.ShapeDtypeStruct((N_HEADS, npos), F32),
        grid=(npos // chunk,),
        in_specs=[pl.BlockSpec((N_HEADS, REL_BUCKETS), lambda i: (0, 0)),
                  pl.BlockSpec((REL_BUCKETS, chunk), lambda i: (0, i))],
        out_specs=pl.BlockSpec((N_HEADS, chunk), lambda i: (0, i)),
        compiler_params=_params(("arbitrary",)),
        name="rel_bias_table",
    )(rel_bias_g.T.astype(F32), onehot)
    bias = bias.reshape(N_PAIRS, 2 * Q_BLOCK, 2 * Q_BLOCK)
    band = (dist >= 0) & (dist <= window // dilation)
    first = band & (qi - dist >= 0)
    band = np.concatenate([band, band], axis=0)[None]
    first = np.concatenate([first, first], axis=0)[None]
    return jnp.stack([jnp.where(first, bias, NEG_INF), jnp.where(band, bias, NEG_INF)], axis=0)


def _attn_a_group(proj, g, rel_bias_g, gq, gk, window, dilation):
    b, s, cols = proj.shape
    r = dilation
    nt = s // PERM_TILE
    n_t = PERM_TILE // r
    nb = s // r // Q_BLOCK
    gw = N_HEADS * HEAD_DIM
    tables = _bias_tables(rel_bias_g, window, dilation)
    base = g * 3
    if n_t >= Q_BLOCK:
        bpt = n_t // Q_BLOCK
        blk = lambda w: (None, None, None, Q_BLOCK, w)
        imap = lambda col: (lambda bi, jr, ub: (bi, ub // bpt, jr, ub % bpt, col))
    else:
        blk = lambda w: (None, Q_BLOCK // n_t, None, n_t, w)
        imap = lambda col: (lambda bi, jr, ub: (bi, ub, jr, 0, col))
    pv = proj.reshape(b, nt, r, n_t, cols)
    gains = lambda t: jnp.concatenate([t, t]).reshape(1, PAIR)
    o, lse = pl.pallas_call(
        _attn_a_kernel,
        out_shape=(jax.ShapeDtypeStruct((b, nt, r, n_t, gw), BF16),
                   jax.ShapeDtypeStruct((b, nt, r, n_t, LANES), F32)),
        grid=(b, r, nb),
        in_specs=[
            pl.BlockSpec(blk(gw), imap(base)),
            pl.BlockSpec(blk(gw), imap(base + 1)),
            pl.BlockSpec(blk(gw), imap(base + 2)),
            pl.BlockSpec(tables.shape, lambda bi, jr, ub: (0, 0, 0, 0)),
            pl.BlockSpec((1, PAIR), lambda bi, jr, ub: (0, 0)),
            pl.BlockSpec((1, PAIR), lambda bi, jr, ub: (0, 0)),
        ],
        out_specs=(pl.BlockSpec(blk(gw), imap(0)), pl.BlockSpec(blk(LANES), imap(0))),
        scratch_shapes=[pltpu.VMEM((2, Q_BLOCK, gw), BF16), pltpu.VMEM((2, Q_BLOCK, gw), BF16)],
        compiler_params=_params(("arbitrary", "arbitrary", "arbitrary")),
        name=f"attn_a_g{g}",
    )(pv, pv, pv, tables, gains(gq), gains(gk))
    return o.reshape(b, s, gw), lse.reshape(b, s, LANES)


MERGE_CHUNK = 128


def _merge_out_kernel(o0_ref, o1_ref, o2_ref, l0_ref, l1_ref, l2_ref, x_ref, g_ref, w_ref,
                      e_ref, out_ref, lun_ref, oun_ref, om_ref):
    tm = o0_ref.shape[1]
    n_slab = o0_ref.shape[2] // LANES

    @pl.when(pl.program_id(1) == 0)
    def _():
        for gi, (o_ref, l_ref) in enumerate(((o1_ref, l1_ref), (o2_ref, l2_ref))):
            r = A_DILATED[gi + 1][1]
            n = tm // r
            for jr in range(r):
                rows = slice(jr * n, (jr + 1) * n)
                lun_ref[gi, pl.ds(jr, n, stride=r), :] = l_ref[0, rows, :]
                for c in range(n_slab):
                    oun_ref[gi, c, pl.ds(jr, n, stride=r), :] = (
                        o_ref[0, rows, c * LANES:(c + 1) * LANES].astype(F32))
        expand = e_ref[...]

        def widen(w):
            hi = w.astype(BF16)
            lo = (w - hi.astype(F32)).astype(BF16)
            return _dot(hi, expand) + _dot(lo, expand)

        for k in range(tm // MERGE_CHUNK):
            rs = slice(k * MERGE_CHUNK, (k + 1) * MERGE_CHUNK)
            l0, l1, l2 = l0_ref[0, rs, :], lun_ref[0, rs, :], lun_ref[1, rs, :]
            m = jnp.maximum(jnp.maximum(l0, l1), l2)
            e0, e1, e2 = jnp.exp(l0 - m), jnp.exp(l1 - m), jnp.exp(l2 - m)
            inv = 1.0 / (e0 + e1 + e2)
            o1 = jnp.concatenate([oun_ref[0, c, rs, :] for c in range(n_slab)], axis=1)
            o2 = jnp.concatenate([oun_ref[1, c, rs, :] for c in range(n_slab)], axis=1)
            o = (widen(e0 * inv) * o0_ref[0, rs, :].astype(F32)
                 + widen(e1 * inv) * o1 + widen(e2 * inv) * o2)
            om_ref[rs, :] = o.astype(BF16)

    out_ref[0] = x_ref[0] + g_ref[0] * _dot(om_ref[...], w_ref[...])


def _merge_out(outs, lses, x, gate, w_out, tn=512):
    b, s, d = x.shape
    tm = PERM_TILE
    spt = s // tm
    rmap = lambda i, j: (i // spt, i % spt, 0)
    cmap = lambda i, j: (i // spt, i % spt, j)
    expand = ((jnp.arange(d)[None, :] // HEAD_DIM == jnp.arange(LANES)[:, None])
              & (jnp.arange(LANES)[:, None] < N_HEADS)).astype(BF16)
    o_spec = pl.BlockSpec((1, tm, d), rmap)
    l_spec = pl.BlockSpec((1, tm, LANES), rmap)
    return pl.pallas_call(
        _merge_out_kernel,
        out_shape=jax.ShapeDtypeStruct((b, s, d), F32),
        grid=(b * spt, d // tn),
        in_specs=[o_spec, o_spec, o_spec, l_spec, l_spec, l_spec,
                  pl.BlockSpec((1, tm, tn), cmap),
                  pl.BlockSpec((1, 1, tn), lambda i, j: (i // spt, 0, j)),
                  pl.BlockSpec((d, tn), lambda i, j: (0, j)),
                  pl.BlockSpec((LANES, d), lambda i, j: (0, 0))],
        out_specs=pl.BlockSpec((1, tm, tn), cmap),
        scratch_shapes=[pltpu.VMEM((2, tm, LANES), F32),
                        pltpu.VMEM((2, d // LANES, tm, LANES), F32),
                        pltpu.VMEM((tm, d), BF16)],
        compiler_params=_params(("arbitrary", "arbitrary")),
        name="merge_out",
    )(*outs, *lses, x, gate, w_out, expand)


def _out_kernel(o_ref, x_ref, g_ref, w_ref, out_ref):
    out_ref[0] = x_ref[0] + g_ref[0] * _dot(o_ref[0], w_ref[...])


def _out_proj(o, x, gate, w_out, tm=512):
    b, s, d = x.shape
    spt = s // tm
    xmap = lambda i: (i // spt, i % spt, 0)
    return pl.pallas_call(
        _out_kernel,
        out_shape=jax.ShapeDtypeStruct((b, s, d), F32),
        grid=(b * spt,),
        in_specs=[pl.BlockSpec((1, tm, d), xmap),
                  pl.BlockSpec((1, tm, d), xmap),
                  pl.BlockSpec((1, 1, d), lambda i: (i // spt, 0, 0)),
                  pl.BlockSpec((d, d), lambda i: (0, 0))],
        out_specs=pl.BlockSpec((1, tm, d), xmap),
        compiler_params=_params(("arbitrary",)),
        name="out_proj",
    )(o, x, gate, w_out)


def _cum_kernel(f_ref, fb_ref, tri_ref, o_ref, carry_ref):
    @pl.when(pl.program_id(1) == 0)
    def _():
        carry_ref[...] = jnp.zeros_like(carry_ref)

    z = f_ref[0] + fb_ref[...]
    logf = jnp.minimum(z, 0.0) - jnp.log(1.0 + jnp.exp(-jnp.abs(z)))
    hi, mid, lo = _split3(logf)
    tri = tri_ref[...]
    cum = _dot(tri, hi) + _dot(tri, mid) + _dot(tri, lo) + carry_ref[...]
    o_ref[0] = cum
    carry_ref[...] = cum[-1:, :]


def _cum_forget(fg, f_bias, tc=512):
    b, s, w = fg.shape
    fb = jnp.pad(f_bias, (0, w - f_bias.shape[0])).reshape(1, w)
    tri = (jnp.arange(tc)[:, None] >= jnp.arange(tc)[None, :]).astype(BF16)
    return pl.pallas_call(
        _cum_kernel,
        out_shape=jax.ShapeDtypeStruct((b, s, w), F32),
        grid=(b, s // tc),
        in_specs=[pl.BlockSpec((1, tc, w), lambda i, j: (i, j, 0)),
                  pl.BlockSpec((1, w), lambda i, j: (0, 0)),
                  pl.BlockSpec((tc, tc), lambda i, j: (0, 0))],
        out_specs=pl.BlockSpec((1, tc, w), lambda i, j: (i, j, 0)),
        scratch_shapes=[pltpu.VMEM((1, w), F32)],
        compiler_params=_params(("arbitrary", "arbitrary")),
        name="cum_forget",
    )(fg, fb, tri)


ONES_ROWS = 16
FOX_HEADS_PER_STEP = 8
GATE_ROWS = HEAD_DIM // FOX_HEADS_PER_STEP
assert GATE_ROWS >= 6


def _fox_kernel(q_ref, k_ref, v_ref, cq_ref, ck_ref, gq_ref, gk_ref, o_ref, ka_ref, vt_ref,
                qat_ref, cqt_ref, ckt_ref, s_ref, p_ref, acc_ref, *, tq, tk):
    hg = pl.program_id(1)
    qi = pl.program_id(2)
    n_kt = vt_ref.shape[1]
    nh = ka_ref.shape[0]
    heads = range(nh)

    @pl.when(qi == 0)
    def _():
        gk = gk_ref[...]
        ckt_ref[...] = (ck_ref[0] * LOG2E).T
        hi, mid, lo = (-p.astype(F32) for p in _split3(ckt_ref[pl.ds(nh * hg, nh), :]))
        s_len = ckt_ref.shape[1]
        blocks = [jnp.zeros((HEAD_DIM, s_len), F32)]
        for hh in heads:
            blocks += [jnp.ones((3, s_len), F32), hi[hh:hh + 1], mid[hh:hh + 1], lo[hh:hh + 1],
                       jnp.zeros((GATE_ROWS - 6, s_len), F32)]
        gate_cols = jnp.concatenate(blocks, axis=0).T.astype(BF16)
        for hh in heads:
            sl = slice(hh * HEAD_DIM, (hh + 1) * HEAD_DIM)
            ka_ref[hh, :, :HEAD_DIM] = _headnorm(k_ref[0, :, sl], gk).astype(BF16)
            ka_ref[hh, :, HEAD_DIM:] = gate_cols[:, HEAD_DIM:]
        for pr in range(nh // 2):
            vt = v_ref[0, :, pr * PAIR:(pr + 1) * PAIR].astype(F32).T.astype(BF16)
            for hh in (2 * pr, 2 * pr + 1):
                rows = slice((hh % 2) * HEAD_DIM, (hh % 2 + 1) * HEAD_DIM)
                for jj in range(n_kt):
                    vt_ref[hh, jj, :HEAD_DIM] = vt[rows, jj * tk:(jj + 1) * tk]
                    vt_ref[hh, jj, HEAD_DIM:] = jnp.ones((ONES_ROWS, tk), BF16)

    even = lax.broadcasted_iota(jnp.int32, (tq, PAIR), 1) < HEAD_DIM
    gq2 = gq_ref[...] * (ATTN_SCALE * LOG2E)
    cqt_ref[...] = (cq_ref[0] * LOG2E).T
    hi, mid, lo = (p.astype(F32) for p in _split3(cqt_ref[pl.ds(nh * hg, nh), :]))
    for pr in range(nh // 2):
        q_t = _pairnorm(q_ref[0, :, pr * PAIR:(pr + 1) * PAIR], gq2, even).T
        for hh in (2 * pr, 2 * pr + 1):
            r0 = (hh % 2) * HEAD_DIM
            qat_ref[hh, :HEAD_DIM] = q_t[r0:r0 + HEAD_DIM].astype(BF16)
            pieces = [hi[hh:hh + 1], mid[hh:hh + 1], lo[hh:hh + 1], jnp.ones((3, tq), F32)]
            before, after = hh * GATE_ROWS, HEAD_DIM - (hh + 1) * GATE_ROWS + GATE_ROWS - 6
            if before:
                pieces.insert(0, jnp.zeros((before, tq), F32))
            pieces.append(jnp.zeros((after, tq), F32))
            qat_ref[hh, HEAD_DIM:] = jnp.concatenate(pieces, axis=0).astype(BF16)
    key = lax.broadcasted_iota(jnp.int32, (tk, tq), 0)
    qry = lax.broadcasted_iota(jnp.int32, (tk, tq), 1)
    ahead = key - qry

    def scores(t, hh, diagonal):
        start = pl.multiple_of(t * tk, tk)
        st = _dot(ka_ref[hh, pl.ds(start, tk), :], qat_ref[hh])
        if diagonal is None:
            st = jnp.where(ahead > jnp.where(t == qi, 0, tk), NEG_INF, st)
        elif diagonal:
            st = jnp.where(ahead > 0, NEG_INF, st)
        s_ref[hh] = st
        return jnp.max(st, axis=0, keepdims=True)

    def pv(t, hh, alpha_prev):
        acc_ref[hh] = alpha_prev * acc_ref[hh] + _dot(vt_ref[hh, t], p_ref[hh])

    def trip(i, carry, next_diagonal):
        out = []
        for hh in heads:
            m, alpha_prev, mx = carry[hh]
            pv(jnp.maximum(i - 1, 0), hh, alpha_prev)
            m_new = jnp.maximum(m, mx)
            alpha = jnp.exp2(m - m_new)
            p_ref[hh] = jnp.exp2(s_ref[hh] - m_new).astype(BF16)
            if next_diagonal is not None:
                mx = scores(i + 1, hh, next_diagonal)
            out.append((m_new, alpha, mx))
        return tuple(out)

    p_ref[...] = jnp.zeros_like(p_ref)
    acc_ref[...] = jnp.zeros_like(acc_ref)
    carry = tuple((jnp.full((1, tq), NEG_INF, F32), jnp.ones((1, tq), F32),
                   scores(0, hh, None)) for hh in heads)
    carry = lax.fori_loop(0, jnp.maximum(qi - 1, 0),
                          functools.partial(trip, next_diagonal=False), carry)
    carry = lax.fori_loop(jnp.maximum(qi - 1, 0), qi,
                          functools.partial(trip, next_diagonal=True), carry)
    carry = trip(qi, carry, None)
    for pr in range(nh // 2):
        o_t = []
        for hh in (2 * pr, 2 * pr + 1):
            pv(qi, hh, carry[hh][1])
            acc = acc_ref[hh]
            o_t.append(acc[:HEAD_DIM] / acc[HEAD_DIM:HEAD_DIM + 1])
        o_ref[0, :, pr * PAIR:(pr + 1) * PAIR] = (
            jnp.concatenate(o_t, axis=0).T.astype(BF16))


def _fox_attention(qkv, cum, gq, gk, tq=256):
    b, s, _ = qkv.shape
    tk = tq
    nq = s // tq
    nh = FOX_HEADS_PER_STEP
    groups = N_HEADS // nh
    gw = nh * HEAD_DIM
    vrows = HEAD_DIM + ONES_ROWS
    return pl.pallas_call(
        functools.partial(_fox_kernel, tq=tq, tk=tk),
        out_shape=jax.ShapeDtypeStruct((b, s, N_HEADS * HEAD_DIM), BF16),
        grid=(b, groups, nq),
        in_specs=[
            pl.BlockSpec((1, tq, gw), lambda bi, hg, qi: (bi, qi, hg)),
            pl.BlockSpec((1, s, gw), lambda bi, hg, qi: (bi, 0, groups + hg)),
            pl.BlockSpec((1, s, gw), lambda bi, hg, qi: (bi, 0, 2 * groups + hg)),
            pl.BlockSpec((1, tq, LANES), lambda bi, hg, qi: (bi, qi, 0)),
            pl.BlockSpec((1, s, LANES), lambda bi, hg, qi: (bi, 0, 0)),
            pl.BlockSpec((1, PAIR), lambda bi, hg, qi: (0, 0)),
            pl.BlockSpec((1, HEAD_DIM), lambda bi, hg, qi: (0, 0)),
        ],
        out_specs=pl.BlockSpec((1, tq, gw), lambda bi, hg, qi: (bi, qi, hg)),
        scratch_shapes=[pltpu.VMEM((nh, s, 2 * HEAD_DIM), BF16),
                        pltpu.VMEM((nh, s // tk, vrows, tk), BF16),
                        pltpu.VMEM((nh, 2 * HEAD_DIM, tq), BF16),
                        pltpu.VMEM((LANES, tq), F32),
                        pltpu.VMEM((LANES, s), F32),
                        pltpu.VMEM((nh, tk, tq), F32),
                        pltpu.VMEM((nh, tk, tq), BF16),
                        pltpu.VMEM((nh, vrows, tq), F32)],
        compiler_params=_params(("arbitrary", "arbitrary", "arbitrary")),
        name="fox_attn",
    )(qkv, qkv, qkv, cum, cum, jnp.concatenate([gq, gq]).reshape(1, PAIR),
      gk.reshape(1, HEAD_DIM))


ROUTER_LANES = LANES


def _route(logits):
    lane = lax.broadcasted_iota(jnp.int32, logits.shape, 1)
    lane_f = lane.astype(F32)
    big = float(ROUTER_LANES)
    is_g = lane < N_GROUPS
    gl = jnp.where(is_g, logits, -jnp.inf)
    gmax = jnp.max(gl, axis=-1, keepdims=True)
    gsum = jnp.sum(jnp.where(is_g, jnp.exp(logits - gmax), 0.0), axis=-1, keepdims=True)
    g_w = 1.0 / gsum
    gidx = jnp.min(jnp.where(gl == gmax, lane_f, big), axis=-1, keepdims=True)
    lo = N_GROUPS + EXPERTS_PER_GROUP * gidx
    in_grp = (lane_f >= lo) & (lane_f < lo + EXPERTS_PER_GROUP)
    el = jnp.where(in_grp, logits, -jnp.inf)
    t1 = jnp.max(el, axis=-1, keepdims=True)
    i1 = jnp.min(jnp.where(el == t1, lane_f, big), axis=-1, keepdims=True)
    el2 = jnp.where(lane_f == i1, -jnp.inf, el)
    t2 = jnp.max(el2, axis=-1, keepdims=True)
    i2 = jnp.min(jnp.where(el2 == t2, lane_f, big), axis=-1, keepdims=True)
    e2 = jnp.exp(t2 - t1)
    w1 = g_w / (1.0 + e2)
    w2 = w1 * e2
    gates = jnp.where(lane_f == i1, w1, jnp.where(lane_f == i2, w2, 0.0))
    return jnp.where(lane == 0, gidx, gates)


MOE_TILE = 512
SUBLANES = 8
ROUTED_COLS = D_MODEL + LANES


def _moe_route_kernel(x_ref, gam_ref, sc_ref, sh_ref, wr_ref, br_ref, tri_ref, o_ref, meta_ref,
                      cnt_ref):
    h = _modnorm(x_ref[0], gam_ref[...], sc_ref[0], sh_ref[0])
    h_hi = h.astype(BF16)
    h_lo = (h - h_hi.astype(F32)).astype(BF16)
    wr = wr_ref[...]
    w_hi = wr.astype(BF16)
    w_lo = (wr - w_hi.astype(F32)).astype(BF16)
    logits = _dot(h_hi, w_hi) + _dot(h_hi, w_lo) + _dot(h_lo, w_hi) + br_ref[...]
    route = _route(logits)
    tm = route.shape[0]

    @pl.when(pl.program_id(0) == 0)
    def _():
        cnt_ref[...] = jnp.zeros_like(cnt_ref)

    lane = lax.broadcasted_iota(jnp.int32, route.shape, 1)
    member = jnp.where((lane.astype(F32) == route[:, 0:1]) & (lane < N_GROUPS), 1.0, 0.0)
    incl = _dot(tri_ref[...], member.astype(BF16))
    rank = jnp.sum(member * (incl - member + cnt_ref[...]), axis=-1, keepdims=True)
    cnt_ref[...] = cnt_ref[...] + incl[tm - 1:tm, :]
    meta = jnp.where(lane == 0, route[:, 0:1], jnp.where(lane == 1, rank, 0.0))
    meta_ref[...] = meta.T[:SUBLANES, :]

    o_ref[:, :D_MODEL] = h
    o_ref[:, D_MODEL:] = route


def _row_copy(src, src_row, dst, dst_row, sem):
    return pltpu.make_async_copy(src.at[pl.ds(src_row, 1), :], dst.at[pl.ds(dst_row, 1), :], sem)


D_ROWS = D_MODEL // LANES


def _tile_copy(src, src_tok, dst, dst_tok, sem):
    s0 = pl.multiple_of(src_tok * D_ROWS, D_ROWS)
    d0 = pl.multiple_of(dst_tok * D_ROWS, D_ROWS)
    return pltpu.make_async_copy(src.at[pl.ds(s0, D_ROWS), :], dst.at[pl.ds(d0, D_ROWS), :], sem)


def _moe_expert_kernel(tg_ref, nt_ref, nv_ref, dest_ref, hx_hbm, wg_ref, wu_ref, wd_ref, e_ref,
                       y_hbm, buf, ybuf, wg_bf, wu_bf, wd_bf, src_ref, gsem, ssem):
    t = pl.program_id(0)
    n_used = nt_ref[0]
    slot = t % 2
    rows = buf.shape[1]
    unroll = 8

    @pl.when(t == 0)
    def _():
        def clear(p, c):
            src_ref[p] = 0
            return c
        lax.fori_loop(0, src_ref.shape[0], clear, 0, unroll=unroll)

        def invert(tok, c):
            src_ref[dest_ref[tok]] = tok
            return c
        lax.fori_loop(0, dest_ref.shape[0], invert, 0, unroll=unroll)

    def gather_start(tile, sl):
        def body(r, c):
            _row_copy(hx_hbm, src_ref[tile * rows + r], buf.at[sl], r, gsem.at[sl]).start()
            return c
        lax.fori_loop(0, rows, body, 0, unroll=unroll)

    def gather_wait(sl):
        def body(r, c):
            _row_copy(hx_hbm, 0, buf.at[sl], r, gsem.at[sl]).wait()
            return c
        lax.fori_loop(0, rows, body, 0, unroll=unroll)

    def valid_rows_loop(tile, body):
        n_valid = nv_ref[tile]
        chunks = n_valid // unroll

        def chunk(k, c):
            for i in range(unroll):
                body(k * unroll + i)
            return c
        lax.fori_loop(0, chunks, chunk, 0)

        def single(r, c):
            body(r)
            return c
        lax.fori_loop(chunks * unroll, n_valid, single, 0)

    def scatter_start(tile, sl):
        valid_rows_loop(tile, lambda r: _tile_copy(
            ybuf.at[sl], r, y_hbm, src_ref[tile * rows + r], ssem.at[sl]).start(priority=1))

    def scatter_wait(tile, sl):
        valid_rows_loop(tile, lambda r: _tile_copy(ybuf.at[sl], r, y_hbm, 0, ssem.at[sl]).wait())

    @pl.when(t == 0)
    def _():
        gather_start(0, 0)

    @pl.when(t + 1 < n_used)
    def _():
        gather_start(t + 1, 1 - slot)

    @pl.when((t == 0) | (tg_ref[t] != tg_ref[jnp.maximum(t - 1, 0)]))
    def _():
        wg_bf[...] = wg_ref[...].astype(BF16)
        wu_bf[...] = wu_ref[...].astype(BF16)
        wd_bf[...] = wd_ref[...].astype(BF16)

    @pl.when(t < n_used)
    def _():
        gather_wait(slot)
        h = buf[slot, :, :D_MODEL].astype(BF16)
        gates = buf[slot, :, D_MODEL:]
        g_hi = gates.astype(BF16)
        g_lo = (gates - g_hi.astype(F32)).astype(BF16)
        y = None
        for e in range(EXPERTS_PER_GROUP):
            a = _dot(h, wg_bf[e])
            u = _dot(h, wu_bf[e])
            spread = e_ref[0, :, e * EXPERT_FF:(e + 1) * EXPERT_FF]
            gexp = _dot(g_hi, spread) + _dot(g_lo, spread)
            hid = a * (1.0 / (1.0 + jnp.exp(-a))) * u * gexp
            part = _dot(hid.astype(BF16), wd_bf[e])
            y = part if y is None else y + part

        @pl.when(t >= 2)
        def _():
            scatter_wait(t - 2, slot)

        for c in range(D_ROWS):
            ybuf[slot, pl.ds(c, rows, stride=D_ROWS), :] = y[:, c * LANES:(c + 1) * LANES]
        scatter_start(t, slot)

    @pl.when(t == pl.num_programs(0) - 1)
    def _():
        for back in (2, 1):
            scatter_wait(n_used - back, (n_used - back) % 2)


def _moe_combine_kernel(x_ref, g_ref, y_ref, o_ref):
    tm = x_ref.shape[1]
    tm = x_ref.shape[1]
    y = jnp.concatenate([y_ref[pl.ds(c, tm, stride=D_ROWS), :] for c in range(D_ROWS)], axis=1)
    o_ref[0] = x_ref[0] + g_ref[0] * y


def _moe(x, gamma, sc, sh, gate, w_router, b_router, w_gate, w_up, w_down):
    b, s, d = x.shape
    n = b * s
    tm = PERM_TILE
    spt = s // tm
    rows = MOE_TILE
    n_tiles = n // rows + N_GROUPS
    p_rows = n_tiles * rows
    bmap = lambda i: (i // spt, 0, 0)
    tri = (jnp.arange(tm)[:, None] >= jnp.arange(tm)[None, :]).astype(BF16)
    hx, meta = pl.pallas_call(
        _moe_route_kernel,
        out_shape=(jax.ShapeDtypeStruct((n, ROUTED_COLS), F32),
                   jax.ShapeDtypeStruct((SUBLANES, n), F32)),
        grid=(b * spt,),
        in_specs=[
            pl.BlockSpec((1, tm, d), lambda i: (i // spt, i % spt, 0)),
            pl.BlockSpec((1, d), lambda i: (0, 0)),
            pl.BlockSpec((1, 1, d), bmap),
            pl.BlockSpec((1, 1, d), bmap),
            pl.BlockSpec((d, ROUTER_LANES), lambda i: (0, 0)),
            pl.BlockSpec((1, ROUTER_LANES), lambda i: (0, 0)),
            pl.BlockSpec((tm, tm), lambda i: (0, 0)),
        ],
        out_specs=(pl.BlockSpec((tm, ROUTED_COLS), lambda i: (i, 0)),
                   pl.BlockSpec((SUBLANES, tm), lambda i: (0, i))),
        scratch_shapes=[pltpu.VMEM((1, LANES), F32)],
        compiler_params=_params(("arbitrary",)),
        name="moe_route",
    )(x, gamma.reshape(1, d), sc, sh, w_router, b_router, tri)

    gidx = meta[0].astype(jnp.int32)
    rank = meta[1].astype(jnp.int32)
    onehot = (gidx[:, None] == jnp.arange(N_GROUPS)[None, :]).astype(jnp.int32)
    count = jnp.sum(onehot, axis=0)
    padded = (count + rows - 1) // rows * rows
    ends = jnp.cumsum(padded)
    dest = jnp.sum(onehot * (ends - padded)[None, :], axis=1) + rank
    n_used = (ends[-1] // rows).reshape(1).astype(jnp.int32)
    tile_start = jnp.arange(n_tiles, dtype=jnp.int32) * rows
    tile_group = jnp.minimum(jnp.sum(ends[None, :] <= tile_start[:, None], axis=1),
                             N_GROUPS - 1).astype(jnp.int32)
    seg_valid_end = ends - padded + count
    n_valid = jnp.clip(jnp.sum(jnp.where(jnp.arange(N_GROUPS)[None, :] == tile_group[:, None],
                                         seg_valid_end[None, :], 0), axis=1) - tile_start,
                       0, rows).astype(jnp.int32)

    ff = EXPERTS_PER_GROUP * EXPERT_FF
    lanes = jnp.arange(ROUTER_LANES)[None, :, None]
    expert = N_GROUPS + EXPERTS_PER_GROUP * jnp.arange(N_GROUPS)[:, None, None] \
        + jnp.arange(ff)[None, None, :] // EXPERT_FF
    expand = (lanes == expert).astype(BF16)
    wmap = lambda t, tg, nt, nv, ds: (tg[t], 0, 0)
    y_tok = pl.pallas_call(
        _moe_expert_kernel,
        out_shape=jax.ShapeDtypeStruct((n * D_ROWS, LANES), F32),
        grid_spec=pltpu.PrefetchScalarGridSpec(
            num_scalar_prefetch=4,
            grid=(n_tiles,),
            in_specs=[
                pl.BlockSpec(memory_space=pl.ANY),
                pl.BlockSpec((EXPERTS_PER_GROUP, d, EXPERT_FF), wmap),
                pl.BlockSpec((EXPERTS_PER_GROUP, d, EXPERT_FF), wmap),
                pl.BlockSpec((EXPERTS_PER_GROUP, EXPERT_FF, d), wmap),
                pl.BlockSpec((1, ROUTER_LANES, ff), wmap),
            ],
            out_specs=pl.BlockSpec(memory_space=pl.ANY),
            scratch_shapes=[pltpu.VMEM((2, rows, ROUTED_COLS), F32),
                            pltpu.VMEM((2, rows * D_ROWS, LANES), F32),
                            pltpu.VMEM((EXPERTS_PER_GROUP, d, EXPERT_FF), BF16),
                            pltpu.VMEM((EXPERTS_PER_GROUP, d, EXPERT_FF), BF16),
                            pltpu.VMEM((EXPERTS_PER_GROUP, EXPERT_FF, d), BF16),
                            pltpu.SMEM((p_rows,), jnp.int32),
                            pltpu.SemaphoreType.DMA((2,)),
                            pltpu.SemaphoreType.DMA((2,))],
        ),
        compiler_params=_params(("arbitrary",)),
        name="moe_experts",
    )(tile_group, n_used, n_valid, dest, hx, w_gate, w_up, w_down, expand)

    return pl.pallas_call(
        _moe_combine_kernel,
        out_shape=jax.ShapeDtypeStruct((b, s, d), F32),
        grid=(b * spt,),
        in_specs=[pl.BlockSpec((1, tm, d), lambda i: (i // spt, i % spt, 0)),
                  pl.BlockSpec((1, 1, d), bmap),
                  pl.BlockSpec((tm * D_ROWS, LANES), lambda i: (i, 0))],
        out_specs=pl.BlockSpec((1, tm, d), lambda i: (i // spt, i % spt, 0)),
        compiler_params=_params(("arbitrary",)),
        name="moe_combine",
    )(x, gate, y_tok)


def kernel(x, c, w_ada, b_ada, norm_mix, norm_ffn, rel_bias, a_w_in, a_w_out, a_q_norm,
           a_k_norm, b_w_in, b_f_bias, b_w_out, b_q_norm, b_k_norm, router_group_w,
           router_group_b, router_expert_w, router_expert_b, w_gate, w_up, w_down):
    b, s, d = x.shape
    depth = w_ada.shape[0]
    mod = _ada(c, w_ada, b_ada).reshape(depth, b, 6, 1, d)
    pad_r = ROUTER_LANES - N_GROUPS - N_EXPERTS
    for i in range(depth):
        sh_m, sc_m, g_m, sh_f, sc_f, g_f = (mod[i, :, k] for k in range(6))
        j = i // 2
        if i % 2 == 0:
            proj = _normproj_a(x, norm_mix[i], sc_m, sh_m, a_w_in[j].astype(BF16))
            outs, lses = [], []
            for g, (window, dil) in enumerate(A_DILATED):
                tab = rel_bias[:, g * N_HEADS:(g + 1) * N_HEADS]
                o, l = _attn_a_group(proj, g, tab, a_q_norm[j], a_k_norm[j], window, dil)
                outs.append(o)
                lses.append(l)
            x = _merge_out(outs, lses, x, g_m, a_w_out[j].astype(BF16))
        else:
            n_qkv = 3 * N_HEADS * HEAD_DIM
            w_in = b_w_in[j]
            w_f = jnp.pad(w_in[:, n_qkv:], ((0, 0), (0, LANES - N_HEADS))).astype(BF16)
            qkv, fg = _normproj(x, norm_mix[i], sc_m, sh_m, w_in[:, :n_qkv].astype(BF16), w_f)
            cum = _cum_forget(fg, b_f_bias[j])
            o = _fox_attention(qkv, cum, b_q_norm[j], b_k_norm[j])
            x = _out_proj(o, x, g_m, b_w_out[j].astype(BF16))
        w_router = jnp.pad(jnp.concatenate([router_group_w[i], router_expert_w[i]], axis=1),
                           ((0, 0), (0, pad_r)))
        b_router = jnp.pad(jnp.concatenate([router_group_b[i], router_expert_b[i]]),
                           (0, pad_r)).reshape(1, ROUTER_LANES)
        x = _moe(x, norm_ffn[i], sc_f, sh_f, g_f, w_router, b_router,
                 w_gate[i], w_up[i], w_down[i])
    return x
```

```python
import functools
import math

import jax
import jax.numpy as jnp
import numpy as np
from jax import lax
from jax.experimental import pallas as pl
from jax.experimental.pallas import tpu as pltpu

F32 = jnp.float32
BF16 = jnp.bfloat16

D_MODEL = 1024
HEAD_DIM = 64
N_HEADS = 16
ATTN_SCALE = HEAD_DIM ** -0.5
A_DILATED = ((128, 1), (512, 4), (2048, 16))
Q_BLOCK = 128
REL_BUCKETS = 32
REL_MAX_DIST = 2048
N_GROUPS = 4
EXPERTS_PER_GROUP = 4
N_EXPERTS = 16
EXPERT_FF = 256
EPS = 1e-6
NEG_INF = -1e30

LANES = 128
VMEM_LIMIT = 48 * 1024 * 1024
LOG2E = 1.4426950408889634
LN2 = 0.6931471805599453
PERM_TILE = 1024
PAIR = 2 * HEAD_DIM
N_PAIRS = N_HEADS // 2


def _params(sem):
    return pltpu.CompilerParams(dimension_semantics=sem, vmem_limit_bytes=VMEM_LIMIT)


def _split3(x):
    hi = x.astype(BF16)
    r1 = x - hi.astype(F32)
    mid = r1.astype(BF16)
    lo = (r1 - mid.astype(F32)).astype(BF16)
    return hi, mid, lo


def _dot(a, b):
    return jnp.dot(a, b, preferred_element_type=F32)


def _dot_nt(a, b):
    return lax.dot_general(a, b, (((1,), (1,)), ((), ())), preferred_element_type=F32)


def _ada_kernel(c_ref, w_ref, b_ref, o_ref):
    c = c_ref[...]
    a = c * (1.0 / (1.0 + jnp.exp(-c)))
    o_ref[0] = _dot(a.astype(BF16), w_ref[0].astype(BF16)) + b_ref[0]


def _ada(c, w_ada, b_ada):
    depth, d, n6 = w_ada.shape
    b = c.shape[0]
    rows = 8
    c_pad = jnp.pad(c, ((0, rows - b), (0, 0)))
    tn = 1536
    out = pl.pallas_call(
        _ada_kernel,
        out_shape=jax.ShapeDtypeStruct((depth, rows, n6), F32),
        grid=(depth, n6 // tn),
        in_specs=[
            pl.BlockSpec((rows, d), lambda i, j: (0, 0)),
            pl.BlockSpec((1, d, tn), lambda i, j: (i, 0, j)),
            pl.BlockSpec((1, 1, tn), lambda i, j: (i, 0, j)),
        ],
        out_specs=pl.BlockSpec((1, rows, tn), lambda i, j: (i, 0, j)),
        compiler_params=_params(("arbitrary", "arbitrary")),
        name="ada_mod",
    )(c_pad, w_ada, b_ada.reshape(depth, 1, n6))
    return out[:, :b]


def _modnorm(x, gamma, sc, sh):
    ms = jnp.mean(x * x, axis=-1, keepdims=True)
    y = x * lax.rsqrt(ms + EPS) * gamma
    return y * (1.0 + sc) + sh


def _normproj_kernel(x_ref, g_ref, sc_ref, sh_ref, w_ref, o_ref, h_ref):
    @pl.when(pl.program_id(1) == 0)
    def _():
        h_ref[...] = _modnorm(x_ref[0], g_ref[...], sc_ref[0], sh_ref[0]).astype(BF16)

    o_ref[0] = _dot(h_ref[...], w_ref[...]).astype(BF16)


def _normproj_gate_kernel(x_ref, g_ref, sc_ref, sh_ref, w_ref, wf_ref, o_ref, f_ref, h_ref):
    @pl.when(pl.program_id(1) == 0)
    def _():
        h = _modnorm(x_ref[0], g_ref[...], sc_ref[0], sh_ref[0]).astype(BF16)
        h_ref[...] = h
        f_ref[0] = _dot(h, wf_ref[...])

    o_ref[0] = _dot(h_ref[...], w_ref[...]).astype(BF16)


def _normproj(x, gamma, sc, sh, w, w_f=None, tm=1024, tn=1536):
    b, s, d = x.shape
    ncols = w.shape[1]
    spt = s // tm
    grid = (b * spt, ncols // tn)
    xmap = lambda i, j: (i // spt, i % spt, 0)
    bmap = lambda i, j: (i // spt, 0, 0)
    in_specs = [
        pl.BlockSpec((1, tm, d), xmap),
        pl.BlockSpec((1, d), lambda i, j: (0, 0)),
        pl.BlockSpec((1, 1, d), bmap),
        pl.BlockSpec((1, 1, d), bmap),
        pl.BlockSpec((d, tn), lambda i, j: (0, j)),
    ]
    o_spec = pl.BlockSpec((1, tm, tn), lambda i, j: (i // spt, i % spt, j))
    scratch = [pltpu.VMEM((tm, d), BF16)]
    if w_f is None:
        return pl.pallas_call(
            _normproj_kernel,
            out_shape=jax.ShapeDtypeStruct((b, s, ncols), BF16),
            grid=grid, in_specs=in_specs, out_specs=o_spec, scratch_shapes=scratch,
            compiler_params=_params(("arbitrary", "arbitrary")),
            name="normproj",
        )(x, gamma.reshape(1, d), sc, sh, w)
    in_specs.append(pl.BlockSpec((d, LANES), lambda i, j: (0, 0)))
    f_spec = pl.BlockSpec((1, tm, LANES), xmap)
    return pl.pallas_call(
        _normproj_gate_kernel,
        out_shape=(jax.ShapeDtypeStruct((b, s, ncols), BF16),
                   jax.ShapeDtypeStruct((b, s, LANES), F32)),
        grid=grid, in_specs=in_specs, out_specs=(o_spec, f_spec), scratch_shapes=scratch,
        compiler_params=_params(("arbitrary", "arbitrary")),
        name="normproj_gate",
    )(x, gamma.reshape(1, d), sc, sh, w, w_f)


def _normproj_a_kernel(x_ref, g_ref, sc_ref, sh_ref, w_ref, o_ref, hf_ref, hp_ref):
    tm = x_ref.shape[1]
    n_slab = x_ref.shape[2] // LANES

    @pl.when(pl.program_id(1) == 0)
    def _():
        h = _modnorm(x_ref[0], g_ref[...], sc_ref[0], sh_ref[0])
        hp_ref[0] = h.astype(BF16)
        for c in range(n_slab):
            hf_ref[c] = h[:, c * LANES:(c + 1) * LANES]
        for g, (_, r) in enumerate(A_DILATED):
            if r == 1:
                continue
            n = tm // r
            for jr in range(r):
                for c in range(n_slab):
                    hp_ref[g, jr * n:(jr + 1) * n, c * LANES:(c + 1) * LANES] = (
                        hf_ref[c, pl.ds(jr, n, stride=r), :].astype(BF16))

    group = (pl.program_id(1) * w_ref.shape[1]) // (3 * N_HEADS * HEAD_DIM)
    o_ref[0] = _dot(hp_ref[group], w_ref[...]).astype(BF16)


def _normproj_a(x, gamma, sc, sh, w, tn=1536):
    b, s, d = x.shape
    tm = PERM_TILE
    ncols = w.shape[1]
    spt = s // tm
    bmap = lambda i, j: (i // spt, 0, 0)
    return pl.pallas_call(
        _normproj_a_kernel,
        out_shape=jax.ShapeDtypeStruct((b, s, ncols), BF16),
        grid=(b * spt, ncols // tn),
        in_specs=[
            pl.BlockSpec((1, tm, d), lambda i, j: (i // spt, i % spt, 0)),
            pl.BlockSpec((1, d), lambda i, j: (0, 0)),
            pl.BlockSpec((1, 1, d), bmap),
            pl.BlockSpec((1, 1, d), bmap),
            pl.BlockSpec((d, tn), lambda i, j: (0, j)),
        ],
        out_specs=pl.BlockSpec((1, tm, tn), lambda i, j: (i // spt, i % spt, j)),
        scratch_shapes=[pltpu.VMEM((d // LANES, tm, LANES), F32),
                        pltpu.VMEM((len(A_DILATED), tm, d), BF16)],
        compiler_params=_params(("arbitrary", "arbitrary")),
        name="normproj_a",
    )(x, gamma.reshape(1, d), sc, sh, w)


def _headnorm(t, gain):
    t = t.astype(F32)
    ms = jnp.mean(t * t, axis=-1, keepdims=True)
    return t * lax.rsqrt(ms + EPS) * gain


def _pairnorm(t, gain2, even):
    t = t.astype(F32)
    sq = t * t
    tot = jnp.sum(sq, axis=-1, keepdims=True)
    s_even = jnp.sum(jnp.where(even, sq, 0.0), axis=-1, keepdims=True)
    inv = jnp.where(even, lax.rsqrt(s_even * (1.0 / HEAD_DIM) + EPS),
                    lax.rsqrt((tot - s_even) * (1.0 / HEAD_DIM) + EPS))
    return t * inv * gain2


def _rows(ref, cols=slice(None)):
    if len(ref.shape) == 2:
        return ref[:, cols]
    return jnp.concatenate([ref[i, :, cols] for i in range(ref.shape[0])], axis=0)


def _store_rows(ref, cols, val):
    if len(ref.shape) == 2:
        ref[:, cols] = val
    else:
        n = ref.shape[1]
        for i in range(ref.shape[0]):
            ref[i, :, cols] = val[i * n:(i + 1) * n]


def _attn_a_kernel(q_ref, k_ref, v_ref, bias_ref, gq_ref, gk_ref, o_ref, lse_ref,
                   kn_ref, vv_ref):
    ub = pl.program_id(2)
    slot = ub % 2
    prev = 1 - slot

    @pl.when(ub == 0)
    def _():
        kn_ref[1] = jnp.zeros(kn_ref.shape[1:], BF16)
        vv_ref[1] = jnp.zeros(vv_ref.shape[1:], BF16)

    first = jnp.minimum(ub, 1)
    lane = lax.broadcasted_iota(jnp.int32, (Q_BLOCK, PAIR), 1)
    even = lane < HEAD_DIM
    lane_row = lax.broadcasted_iota(jnp.int32, (1, PAIR), 1)
    keep_even = jnp.where(lane_row < HEAD_DIM, 1.0, 0.0).astype(BF16)
    keep_odd = jnp.where(lane_row < HEAD_DIM, 0.0, 1.0).astype(BF16)
    gq2 = gq_ref[...] * (ATTN_SCALE * LOG2E)
    gk2 = gk_ref[...]
    for hp in range(N_PAIRS):
        cs = slice(hp * PAIR, (hp + 1) * PAIR)
        kn_ref[slot, :, cs] = _pairnorm(_rows(k_ref, cs), gk2, even).astype(BF16)
    vv_ref[slot] = _rows(v_ref)
    lse_tile = jnp.zeros((Q_BLOCK, LANES), F32)
    for hp in range(N_PAIRS):
        cs = slice(hp * PAIR, (hp + 1) * PAIR)
        qn = _pairnorm(_rows(q_ref, cs), gq2, even)
        qq = jnp.concatenate([jnp.where(even, qn, 0.0), jnp.where(even, 0.0, qn)],
                             axis=0).astype(BF16)
        kcat = jnp.concatenate([kn_ref[prev, :, cs], kn_ref[slot, :, cs]], axis=0)
        s = _dot_nt(qq, kcat) + bias_ref[first, hp]
        m = jnp.max(s, axis=-1, keepdims=True)
        p = jnp.exp2(s - m)
        l = jnp.sum(p, axis=-1, keepdims=True)
        pb = p.astype(BF16)
        vcat = jnp.concatenate([vv_ref[prev, :, cs], vv_ref[slot, :, cs]], axis=0)
        o = (_dot(pb[:Q_BLOCK], vcat * keep_even)
             + _dot(pb[Q_BLOCK:], vcat * keep_odd))
        inv = 1.0 / l
        o = o * jnp.where(even, inv[:Q_BLOCK], inv[Q_BLOCK:])
        _store_rows(o_ref, cs, o.astype(BF16))
        lse = m * LN2 + jnp.log(l)
        lse_tile = jnp.where(lane == 2 * hp, lse[:Q_BLOCK],
                             jnp.where(lane == 2 * hp + 1, lse[Q_BLOCK:], lse_tile))
    _store_rows(lse_ref, slice(None), lse_tile)


def _t5_bucket(dist):
    max_exact = REL_BUCKETS // 2
    d = np.maximum(dist, 0)
    large = max_exact + (np.log(np.maximum(d, 1).astype(np.float32) / max_exact)
                         / math.log(REL_MAX_DIST / max_exact)
                         * (REL_BUCKETS - max_exact)).astype(np.int32)
    large = np.minimum(large, REL_BUCKETS - 1)
    return np.where(d < max_exact, d, large)


def _bias_kernel(tab_ref, onehot_ref, o_ref):
    hi, mid, lo = _split3(tab_ref[...])
    oh = onehot_ref[...]
    o_ref[...] = (_dot(hi, oh) + _dot(mid, oh) + _dot(lo, oh)) * LOG2E


def _bias_tables(rel_bias_g, window, dilation):
    qi = np.arange(Q_BLOCK)[:, None]
    kj = np.arange(2 * Q_BLOCK)[None, :]
    dist = qi + Q_BLOCK - kj
    bucket = _t5_bucket(dist * dilation).reshape(1, -1)
    onehot = jnp.asarray(bucket == np.arange(REL_BUCKETS)[:, None], BF16)
    npos = onehot.shape[1]
    chunk = 4096
    bias = pl.pallas_call(
        _bias_kernel,
        out_shape=jax.ShapeDtypeStruct((N_HEADS, npos), F32),
        grid=(npos // chunk,),
        in_specs=[pl.BlockSpec((N_HEADS, REL_BUCKETS), lambda i: (0, 0)),
                  pl.BlockSpec((REL_BUCKETS, chunk), lambda i: (0, i))],
        out_specs=pl.BlockSpec((N_HEADS, chunk), lambda i: (0, i)),
        compiler_params=_params(("arbitrary",)),
        name="rel_bias_table",
    )(rel_bias_g.T.astype(F32), onehot)
    bias = bias.reshape(N_PAIRS, 2 * Q_BLOCK, 2 * Q_BLOCK)
    band = (dist >= 0) & (dist <= window // dilation)
    first = band & (qi - dist >= 0)
    band = np.concatenate([band, band], axis=0)[None]
    first = np.concatenate([first, first], axis=0)[None]
    return jnp.stack([jnp.where(first, bias, NEG_INF), jnp.where(band, bias, NEG_INF)], axis=0)


def _attn_a_group(proj, g, rel_bias_g, gq, gk, window, dilation):
    b, s, cols = proj.shape
    r = dilation
    nt = s // PERM_TILE
    n_t = PERM_TILE // r
    nb = s // r // Q_BLOCK
    gw = N_HEADS * HEAD_DIM
    tables = _bias_tables(rel_bias_g, window, dilation)
    base = g * 3
    if n_t >= Q_BLOCK:
        bpt = n_t // Q_BLOCK
        blk = lambda w: (None, None, None, Q_BLOCK, w)
        imap = lambda col: (lambda bi, jr, ub: (bi, ub // bpt, jr, ub % bpt, col))
    else:
        blk = lambda w: (None, Q_BLOCK // n_t, None, n_t, w)
        imap = lambda col: (lambda bi, jr, ub: (bi, ub, jr, 0, col))
    pv = proj.reshape(b, nt, r, n_t, cols)
    gains = lambda t: jnp.concatenate([t, t]).reshape(1, PAIR)
    o, lse = pl.pallas_call(
        _attn_a_kernel,
        out_shape=(jax.ShapeDtypeStruct((b, nt, r, n_t, gw), BF16),
                   jax.ShapeDtypeStruct((b, nt, r, n_t, LANES), F32)),
        grid=(b, r, nb),
        in_specs=[
            pl.BlockSpec(blk(gw), imap(base)),
            pl.BlockSpec(blk(gw), imap(base + 1)),
            pl.BlockSpec(blk(gw), imap(base + 2)),
            pl.BlockSpec(tables.shape, lambda bi, jr, ub: (0, 0, 0, 0)),
            pl.BlockSpec((1, PAIR), lambda bi, jr, ub: (0, 0)),
            pl.BlockSpec((1, PAIR), lambda bi, jr, ub: (0, 0)),
        ],
        out_specs=(pl.BlockSpec(blk(gw), imap(0)), pl.BlockSpec(blk(LANES), imap(0))),
        scratch_shapes=[pltpu.VMEM((2, Q_BLOCK, gw), BF16), pltpu.VMEM((2, Q_BLOCK, gw), BF16)],
        compiler_params=_params(("arbitrary", "arbitrary", "arbitrary")),
        name=f"attn_a_g{g}",
    )(pv, pv, pv, tables, gains(gq), gains(gk))
    return o.reshape(b, s, gw), lse.reshape(b, s, LANES)


MERGE_CHUNK = 128


def _merge_out_kernel(o0_ref, o1_ref, o2_ref, l0_ref, l1_ref, l2_ref, x_ref, g_ref, w_ref,
                      e_ref, out_ref, lun_ref, oun_ref, om_ref):
    tm = o0_ref.shape[1]
    n_slab = o0_ref.shape[2] // LANES

    @pl.when(pl.program_id(1) == 0)
    def _():
        for gi, (o_ref, l_ref) in enumerate(((o1_ref, l1_ref), (o2_ref, l2_ref))):
            r = A_DILATED[gi + 1][1]
            n = tm // r
            for jr in range(r):
                rows = slice(jr * n, (jr + 1) * n)
                lun_ref[gi, pl.ds(jr, n, stride=r), :] = l_ref[0, rows, :]
                for c in range(n_slab):
                    oun_ref[gi, c, pl.ds(jr, n, stride=r), :] = (
                        o_ref[0, rows, c * LANES:(c + 1) * LANES].astype(F32))
        expand = e_ref[...]

        def widen(w):
            hi = w.astype(BF16)
            lo = (w - hi.astype(F32)).astype(BF16)
            return _dot(hi, expand) + _dot(lo, expand)

        for k in range(tm // MERGE_CHUNK):
            rs = slice(k * MERGE_CHUNK, (k + 1) * MERGE_CHUNK)
            l0, l1, l2 = l0_ref[0, rs, :], lun_ref[0, rs, :], lun_ref[1, rs, :]
            m = jnp.maximum(jnp.maximum(l0, l1), l2)
            e0, e1, e2 = jnp.exp(l0 - m), jnp.exp(l1 - m), jnp.exp(l2 - m)
            inv = 1.0 / (e0 + e1 + e2)
            o1 = jnp.concatenate([oun_ref[0, c, rs, :] for c in range(n_slab)], axis=1)
            o2 = jnp.concatenate([oun_ref[1, c, rs, :] for c in range(n_slab)], axis=1)
            o = (widen(e0 * inv) * o0_ref[0, rs, :].astype(F32)
                 + widen(e1 * inv) * o1 + widen(e2 * inv) * o2)
            om_ref[rs, :] = o.astype(BF16)

    out_ref[0] = x_ref[0] + g_ref[0] * _dot(om_ref[...], w_ref[...])


def _merge_out(outs, lses, x, gate, w_out, tn=512):
    b, s, d = x.shape
    tm = PERM_TILE
    spt = s // tm
    rmap = lambda i, j: (i // spt, i % spt, 0)
    cmap = lambda i, j: (i // spt, i % spt, j)
    expand = ((jnp.arange(d)[None, :] // HEAD_DIM == jnp.arange(LANES)[:, None])
              & (jnp.arange(LANES)[:, None] < N_HEADS)).astype(BF16)
    o_spec = pl.BlockSpec((1, tm, d), rmap)
    l_spec = pl.BlockSpec((1, tm, LANES), rmap)
    return pl.pallas_call(
        _merge_out_kernel,
        out_shape=jax.ShapeDtypeStruct((b, s, d), F32),
        grid=(b * spt, d // tn),
        in_specs=[o_spec, o_spec, o_spec, l_spec, l_spec, l_spec,
                  pl.BlockSpec((1, tm, tn), cmap),
                  pl.BlockSpec((1, 1, tn), lambda i, j: (i // spt, 0, j)),
                  pl.BlockSpec((d, tn), lambda i, j: (0, j)),
                  pl.BlockSpec((LANES, d), lambda i, j: (0, 0))],
        out_specs=pl.BlockSpec((1, tm, tn), cmap),
        scratch_shapes=[pltpu.VMEM((2, tm, LANES), F32),
                        pltpu.VMEM((2, d // LANES, tm, LANES), F32),
                        pltpu.VMEM((tm, d), BF16)],
        compiler_params=_params(("arbitrary", "arbitrary")),
        name="merge_out",
    )(*outs, *lses, x, gate, w_out, expand)


def _out_kernel(o_ref, x_ref, g_ref, w_ref, out_ref):
    out_ref[0] = x_ref[0] + g_ref[0] * _dot(o_ref[0], w_ref[...])


def _out_proj(o, x, gate, w_out, tm=512):
    b, s, d = x.shape
    spt = s // tm
    xmap = lambda i: (i // spt, i % spt, 0)
    return pl.pallas_call(
        _out_kernel,
        out_shape=jax.ShapeDtypeStruct((b, s, d), F32),
        grid=(b * spt,),
        in_specs=[pl.BlockSpec((1, tm, d), xmap),
                  pl.BlockSpec((1, tm, d), xmap),
                  pl.BlockSpec((1, 1, d), lambda i: (i // spt, 0, 0)),
                  pl.BlockSpec((d, d), lambda i: (0, 0))],
        out_specs=pl.BlockSpec((1, tm, d), xmap),
        compiler_params=_params(("arbitrary",)),
        name="out_proj",
    )(o, x, gate, w_out)


def _cum_kernel(f_ref, fb_ref, tri_ref, o_ref, carry_ref):
    @pl.when(pl.program_id(1) == 0)
    def _():
        carry_ref[...] = jnp.zeros_like(carry_ref)

    z = f_ref[0] + fb_ref[...]
    logf = jnp.minimum(z, 0.0) - jnp.log(1.0 + jnp.exp(-jnp.abs(z)))
    hi, mid, lo = _split3(logf)
    tri = tri_ref[...]
    cum = _dot(tri, hi) + _dot(tri, mid) + _dot(tri, lo) + carry_ref[...]
    o_ref[0] = cum
    carry_ref[...] = cum[-1:, :]


def _cum_forget(fg, f_bias, tc=512):
    b, s, w = fg.shape
    fb = jnp.pad(f_bias, (0, w - f_bias.shape[0])).reshape(1, w)
    tri = (jnp.arange(tc)[:, None] >= jnp.arange(tc)[None, :]).astype(BF16)
    return pl.pallas_call(
        _cum_kernel,
        out_shape=jax.ShapeDtypeStruct((b, s, w), F32),
        grid=(b, s // tc),
        in_specs=[pl.BlockSpec((1, tc, w), lambda i, j: (i, j, 0)),
                  pl.BlockSpec((1, w), lambda i, j: (0, 0)),
                  pl.BlockSpec((tc, tc), lambda i, j: (0, 0))],
        out_specs=pl.BlockSpec((1, tc, w), lambda i, j: (i, j, 0)),
        scratch_shapes=[pltpu.VMEM((1, w), F32)],
        compiler_params=_params(("arbitrary", "arbitrary")),
        name="cum_forget",
    )(fg, fb, tri)


ONES_ROWS = 16
FOX_HEADS_PER_STEP = 8
GATE_ROWS = HEAD_DIM // FOX_HEADS_PER_STEP
assert GATE_ROWS >= 6


def _fox_kernel(q_ref, k_ref, v_ref, cq_ref, ck_ref, gq_ref, gk_ref, o_ref, ka_ref, vt_ref,
                qat_ref, cqt_ref, ckt_ref, s_ref, p_ref, acc_ref, *, tq, tk):
    hg = pl.program_id(1)
    qi = pl.program_id(2)
    n_kt = vt_ref.shape[1]
    nh = ka_ref.shape[0]
    heads = range(nh)

    @pl.when(qi == 0)
    def _():
        gk = gk_ref[...]
        ckt_ref[...] = (ck_ref[0] * LOG2E).T
        hi, mid, lo = (-p.astype(F32) for p in _split3(ckt_ref[pl.ds(nh * hg, nh), :]))
        s_len = ckt_ref.shape[1]
        blocks = [jnp.zeros((HEAD_DIM, s_len), F32)]
        for hh in heads:
            blocks += [jnp.ones((3, s_len), F32), hi[hh:hh + 1], mid[hh:hh + 1], lo[hh:hh + 1],
                       jnp.zeros((GATE_ROWS - 6, s_len), F32)]
        gate_cols = jnp.concatenate(blocks, axis=0).T.astype(BF16)
        for hh in heads:
            sl = slice(hh * HEAD_DIM, (hh + 1) * HEAD_DIM)
            ka_ref[hh, :, :HEAD_DIM] = _headnorm(k_ref[0, :, sl], gk).astype(BF16)
            ka_ref[hh, :, HEAD_DIM:] = gate_cols[:, HEAD_DIM:]
        for pr in range(nh // 2):
            vt = v_ref[0, :, pr * PAIR:(pr + 1) * PAIR].astype(F32).T.astype(BF16)
            for hh in (2 * pr, 2 * pr + 1):
                rows = slice((hh % 2) * HEAD_DIM, (hh % 2 + 1) * HEAD_DIM)
                for jj in range(n_kt):
                    vt_ref[hh, jj, :HEAD_DIM] = vt[rows, jj * tk:(jj + 1) * tk]
                    vt_ref[hh, jj, HEAD_DIM:] = jnp.ones((ONES_ROWS, tk), BF16)

    even = lax.broadcasted_iota(jnp.int32, (tq, PAIR), 1) < HEAD_DIM
    gq2 = gq_ref[...] * (ATTN_SCALE * LOG2E)
    cqt_ref[...] = (cq_ref[0] * LOG2E).T
    hi, mid, lo = (p.astype(F32) for p in _split3(cqt_ref[pl.ds(nh * hg, nh), :]))
    for pr in range(nh // 2):
        q_t = _pairnorm(q_ref[0, :, pr * PAIR:(pr + 1) * PAIR], gq2, even).T
        for hh in (2 * pr, 2 * pr + 1):
            r0 = (hh % 2) * HEAD_DIM
            qat_ref[hh, :HEAD_DIM] = q_t[r0:r0 + HEAD_DIM].astype(BF16)
            pieces = [hi[hh:hh + 1], mid[hh:hh + 1], lo[hh:hh + 1], jnp.ones((3, tq), F32)]
            before, after = hh * GATE_ROWS, HEAD_DIM - (hh + 1) * GATE_ROWS + GATE_ROWS - 6
            if before:
                pieces.insert(0, jnp.zeros((before, tq), F32))
            pieces.append(jnp.zeros((after, tq), F32))
            qat_ref[hh, HEAD_DIM:] = jnp.concatenate(pieces, axis=0).astype(BF16)
    key = lax.broadcasted_iota(jnp.int32, (tk, tq), 0)
    qry = lax.broadcasted_iota(jnp.int32, (tk, tq), 1)
    ahead = key - qry

    def scores(t, hh, diagonal):
        start = pl.multiple_of(t * tk, tk)
        st = _dot(ka_ref[hh, pl.ds(start, tk), :], qat_ref[hh])
        if diagonal is None:
            st = jnp.where(ahead > jnp.where(t == qi, 0, tk), NEG_INF, st)
        elif diagonal:
            st = jnp.where(ahead > 0, NEG_INF, st)
        s_ref[hh] = st
        return jnp.max(st, axis=0, keepdims=True)

    def pv(t, hh, alpha_prev):
        acc_ref[hh] = alpha_prev * acc_ref[hh] + _dot(vt_ref[hh, t], p_ref[hh])

    def trip(i, carry, next_diagonal):
        out = []
        for hh in heads:
            m, alpha_prev, mx = carry[hh]
            pv(jnp.maximum(i - 1, 0), hh, alpha_prev)
            m_new = jnp.maximum(m, mx)
            alpha = jnp.exp2(m - m_new)
            p_ref[hh] = jnp.exp2(s_ref[hh] - m_new).astype(BF16)
            if next_diagonal is not None:
                mx = scores(i + 1, hh, next_diagonal)
            out.append((m_new, alpha, mx))
        return tuple(out)

    p_ref[...] = jnp.zeros_like(p_ref)
    acc_ref[...] = jnp.zeros_like(acc_ref)
    carry = tuple((jnp.full((1, tq), NEG_INF, F32), jnp.ones((1, tq), F32),
                   scores(0, hh, None)) for hh in heads)
    carry = lax.fori_loop(0, jnp.maximum(qi - 1, 0),
                          functools.partial(trip, next_diagonal=False), carry)
    carry = lax.fori_loop(jnp.maximum(qi - 1, 0), qi,
                          functools.partial(trip, next_diagonal=True), carry)
    carry = trip(qi, carry, None)
    for pr in range(nh // 2):
        o_t = []
        for hh in (2 * pr, 2 * pr + 1):
            pv(qi, hh, carry[hh][1])
            acc = acc_ref[hh]
            o_t.append(acc[:HEAD_DIM] / acc[HEAD_DIM:HEAD_DIM + 1])
        o_ref[0, :, pr * PAIR:(pr + 1) * PAIR] = (
            jnp.concatenate(o_t, axis=0).T.astype(BF16))


def _fox_attention(qkv, cum, gq, gk, tq=256):
    b, s, _ = qkv.shape
    tk = tq
    nq = s // tq
    nh = FOX_HEADS_PER_STEP
    groups = N_HEADS // nh
    gw = nh * HEAD_DIM
    vrows = HEAD_DIM + ONES_ROWS
    return pl.pallas_call(
        functools.partial(_fox_kernel, tq=tq, tk=tk),
        out_shape=jax.ShapeDtypeStruct((b, s, N_HEADS * HEAD_DIM), BF16),
        grid=(b, groups, nq),
        in_specs=[
            pl.BlockSpec((1, tq, gw), lambda bi, hg, qi: (bi, qi, hg)),
            pl.BlockSpec((1, s, gw), lambda bi, hg, qi: (bi, 0, groups + hg)),
            pl.BlockSpec((1, s, gw), lambda bi, hg, qi: (bi, 0, 2 * groups + hg)),
            pl.BlockSpec((1, tq, LANES), lambda bi, hg, qi: (bi, qi, 0)),
            pl.BlockSpec((1, s, LANES), lambda bi, hg, qi: (bi, 0, 0)),
            pl.BlockSpec((1, PAIR), lambda bi, hg, qi: (0, 0)),
            pl.BlockSpec((1, HEAD_DIM), lambda bi, hg, qi: (0, 0)),
        ],
        out_specs=pl.BlockSpec((1, tq, gw), lambda bi, hg, qi: (bi, qi, hg)),
        scratch_shapes=[pltpu.VMEM((nh, s, 2 * HEAD_DIM), BF16),
                        pltpu.VMEM((nh, s // tk, vrows, tk), BF16),
                        pltpu.VMEM((nh, 2 * HEAD_DIM, tq), BF16),
                        pltpu.VMEM((LANES, tq), F32),
                        pltpu.VMEM((LANES, s), F32),
                        pltpu.VMEM((nh, tk, tq), F32),
                        pltpu.VMEM((nh, tk, tq), BF16),
                        pltpu.VMEM((nh, vrows, tq), F32)],
        compiler_params=_params(("arbitrary", "arbitrary", "arbitrary")),
        name="fox_attn",
    )(qkv, qkv, qkv, cum, cum, jnp.concatenate([gq, gq]).reshape(1, PAIR),
      gk.reshape(1, HEAD_DIM))


ROUTER_LANES = LANES


def _route(logits):
    lane = lax.broadcasted_iota(jnp.int32, logits.shape, 1)
    lane_f = lane.astype(F32)
    big = float(ROUTER_LANES)
    is_g = lane < N_GROUPS
    gl = jnp.where(is_g, logits, -jnp.inf)
    gmax = jnp.max(gl, axis=-1, keepdims=True)
    gsum = jnp.sum(jnp.where(is_g, jnp.exp(logits - gmax), 0.0), axis=-1, keepdims=True)
    g_w = 1.0 / gsum
    gidx = jnp.min(jnp.where(gl == gmax, lane_f, big), axis=-1, keepdims=True)
    lo = N_GROUPS + EXPERTS_PER_GROUP * gidx
    in_grp = (lane_f >= lo) & (lane_f < lo + EXPERTS_PER_GROUP)
    el = jnp.where(in_grp, logits, -jnp.inf)
    t1 = jnp.max(el, axis=-1, keepdims=True)
    i1 = jnp.min(jnp.where(el == t1, lane_f, big), axis=-1, keepdims=True)
    el2 = jnp.where(lane_f == i1, -jnp.inf, el)
    t2 = jnp.max(el2, axis=-1, keepdims=True)
    i2 = jnp.min(jnp.where(el2 == t2, lane_f, big), axis=-1, keepdims=True)
    e2 = jnp.exp(t2 - t1)
    w1 = g_w / (1.0 + e2)
    w2 = w1 * e2
    gates = jnp.where(lane_f == i1, w1, jnp.where(lane_f == i2, w2, 0.0))
    return jnp.where(lane == 0, gidx, gates)


MOE_TILE = 512
SUBLANES = 8
ROUTED_COLS = D_MODEL + LANES


def _moe_route_kernel(x_ref, gam_ref, sc_ref, sh_ref, wr_ref, br_ref, tri_ref, o_ref, meta_ref,
                      cnt_ref):
    h = _modnorm(x_ref[0], gam_ref[...], sc_ref[0], sh_ref[0])
    h_hi = h.astype(BF16)
    h_lo = (h - h_hi.astype(F32)).astype(BF16)
    wr = wr_ref[...]
    w_hi = wr.astype(BF16)
    w_lo = (wr - w_hi.astype(F32)).astype(BF16)
    logits = _dot(h_hi, w_hi) + _dot(h_hi, w_lo) + _dot(h_lo, w_hi) + br_ref[...]
    route = _route(logits)
    tm = route.shape[0]

    @pl.when(pl.program_id(0) == 0)
    def _():
        cnt_ref[...] = jnp.zeros_like(cnt_ref)

    lane = lax.broadcasted_iota(jnp.int32, route.shape, 1)
    member = jnp.where((lane.astype(F32) == route[:, 0:1]) & (lane < N_GROUPS), 1.0, 0.0)
    incl = _dot(tri_ref[...], member.astype(BF16))
    rank = jnp.sum(member * (incl - member + cnt_ref[...]), axis=-1, keepdims=True)
    cnt_ref[...] = cnt_ref[...] + incl[tm - 1:tm, :]
    meta = jnp.where(lane == 0, route[:, 0:1], jnp.where(lane == 1, rank, 0.0))
    meta_ref[...] = meta.T[:SUBLANES, :]

    o_ref[:, :D_MODEL] = h
    o_ref[:, D_MODEL:] = route


def _row_copy(src, src_row, dst, dst_row, sem):
    return pltpu.make_async_copy(src.at[pl.ds(src_row, 1), :], dst.at[pl.ds(dst_row, 1), :], sem)


D_ROWS = D_MODEL // LANES


def _tile_copy(src, src_tok, dst, dst_tok, sem):
    s0 = pl.multiple_of(src_tok * D_ROWS, D_ROWS)
    d0 = pl.multiple_of(dst_tok * D_ROWS, D_ROWS)
    return pltpu.make_async_copy(src.at[pl.ds(s0, D_ROWS), :], dst.at[pl.ds(d0, D_ROWS), :], sem)


def _moe_expert_kernel(tg_ref, nt_ref, nv_ref, dest_ref, hx_hbm, wg_ref, wu_ref, wd_ref, e_ref,
                       y_hbm, buf, ybuf, wg_bf, wu_bf, wd_bf, src_ref, gsem, ssem):
    t = pl.program_id(0)
    n_used = nt_ref[0]
    slot = t % 2
    rows = buf.shape[1]
    unroll = 8

    @pl.when(t == 0)
    def _():
        def clear(p, c):
            src_ref[p] = 0
            return c
        lax.fori_loop(0, src_ref.shape[0], clear, 0, unroll=unroll)

        def invert(tok, c):
            src_ref[dest_ref[tok]] = tok
            return c
        lax.fori_loop(0, dest_ref.shape[0], invert, 0, unroll=unroll)

    def gather_start(tile, sl):
        def body(r, c):
            _row_copy(hx_hbm, src_ref[tile * rows + r], buf.at[sl], r, gsem.at[sl]).start()
            return c
        lax.fori_loop(0, rows, body, 0, unroll=unroll)

    def gather_wait(sl):
        def body(r, c):
            _row_copy(hx_hbm, 0, buf.at[sl], r, gsem.at[sl]).wait()
            return c
        lax.fori_loop(0, rows, body, 0, unroll=unroll)

    def valid_rows_loop(tile, body):
        n_valid = nv_ref[tile]
        chunks = n_valid // unroll

        def chunk(k, c):
            for i in range(unroll):
                body(k * unroll + i)
            return c
        lax.fori_loop(0, chunks, chunk, 0)

        def single(r, c):
            body(r)
            return c
        lax.fori_loop(chunks * unroll, n_valid, single, 0)

    def scatter_start(tile, sl):
        valid_rows_loop(tile, lambda r: _tile_copy(
            ybuf.at[sl], r, y_hbm, src_ref[tile * rows + r], ssem.at[sl]).start(priority=1))

    def scatter_wait(tile, sl):
        valid_rows_loop(tile, lambda r: _tile_copy(ybuf.at[sl], r, y_hbm, 0, ssem.at[sl]).wait())

    @pl.when(t == 0)
    def _():
        gather_start(0, 0)

    @pl.when(t + 1 < n_used)
    def _():
        gather_start(t + 1, 1 - slot)

    @pl.when((t == 0) | (tg_ref[t] != tg_ref[jnp.maximum(t - 1, 0)]))
    def _():
        wg_bf[...] = wg_ref[...].astype(BF16)
        wu_bf[...] = wu_ref[...].astype(BF16)
        wd_bf[...] = wd_ref[...].astype(BF16)

    @pl.when(t < n_used)
    def _():
        gather_wait(slot)
        h = buf[slot, :, :D_MODEL].astype(BF16)
        gates = buf[slot, :, D_MODEL:]
        g_hi = gates.astype(BF16)
        g_lo = (gates - g_hi.astype(F32)).astype(BF16)
        y = None
        for e in range(EXPERTS_PER_GROUP):
            a = _dot(h, wg_bf[e])
            u = _dot(h, wu_bf[e])
            spread = e_ref[0, :, e * EXPERT_FF:(e + 1) * EXPERT_FF]
            gexp = _dot(g_hi, spread) + _dot(g_lo, spread)
            hid = a * (1.0 / (1.0 + jnp.exp(-a))) * u * gexp
            part = _dot(hid.astype(BF16), wd_bf[e])
            y = part if y is None else y + part

        @pl.when(t >= 2)
        def _():
            scatter_wait(t - 2, slot)

        for c in range(D_ROWS):
            ybuf[slot, pl.ds(c, rows, stride=D_ROWS), :] = y[:, c * LANES:(c + 1) * LANES]
        scatter_start(t, slot)

    @pl.when(t == pl.num_programs(0) - 1)
    def _():
        for back in (2, 1):
            scatter_wait(n_used - back, (n_used - back) % 2)


def _moe_combine_kernel(x_ref, g_ref, y_ref, o_ref):
    tm = x_ref.shape[1]
    tm = x_ref.shape[1]
    y = jnp.concatenate([y_ref[pl.ds(c, tm, stride=D_ROWS), :] for c in range(D_ROWS)], axis=1)
    o_ref[0] = x_ref[0] + g_ref[0] * y


def _moe(x, gamma, sc, sh, gate, w_router, b_router, w_gate, w_up, w_down, layer):
    b, s, d = x.shape
    n = b * s
    tm = PERM_TILE
    spt = s // tm
    rows = MOE_TILE
    n_tiles = n // rows + N_GROUPS
    p_rows = n_tiles * rows
    bmap = lambda i: (i // spt, 0, 0)
    tri = (jnp.arange(tm)[:, None] >= jnp.arange(tm)[None, :]).astype(BF16)
    hx, meta = pl.pallas_call(
        _moe_route_kernel,
        out_shape=(jax.ShapeDtypeStruct((n, ROUTED_COLS), F32),
                   jax.ShapeDtypeStruct((SUBLANES, n), F32)),
        grid=(b * spt,),
        in_specs=[
            pl.BlockSpec((1, tm, d), lambda i: (i // spt, i % spt, 0)),
            pl.BlockSpec((1, d), lambda i: (0, 0)),
            pl.BlockSpec((1, 1, d), bmap),
            pl.BlockSpec((1, 1, d), bmap),
            pl.BlockSpec((d, ROUTER_LANES), lambda i: (0, 0)),
            pl.BlockSpec((1, ROUTER_LANES), lambda i: (0, 0)),
            pl.BlockSpec((tm, tm), lambda i: (0, 0)),
        ],
        out_specs=(pl.BlockSpec((tm, ROUTED_COLS), lambda i: (i, 0)),
                   pl.BlockSpec((SUBLANES, tm), lambda i: (0, i))),
        scratch_shapes=[pltpu.VMEM((1, LANES), F32)],
        compiler_params=_params(("arbitrary",)),
        name="moe_route",
    )(x, gamma.reshape(1, d), sc, sh, w_router, b_router, tri)

    gidx = meta[0].astype(jnp.int32)
    rank = meta[1].astype(jnp.int32)
    onehot = (gidx[:, None] == jnp.arange(N_GROUPS)[None, :]).astype(jnp.int32)
    count = jnp.sum(onehot, axis=0)
    padded = (count + rows - 1) // rows * rows
    ends = jnp.cumsum(padded)
    dest = jnp.sum(onehot * (ends - padded)[None, :], axis=1) + rank
    n_used = (ends[-1] // rows).reshape(1).astype(jnp.int32)
    tile_start = jnp.arange(n_tiles, dtype=jnp.int32) * rows
    tile_group = jnp.minimum(jnp.sum(ends[None, :] <= tile_start[:, None], axis=1),
                             N_GROUPS - 1).astype(jnp.int32)
    seg_valid_end = ends - padded + count
    n_valid = jnp.clip(jnp.sum(jnp.where(jnp.arange(N_GROUPS)[None, :] == tile_group[:, None],
                                         seg_valid_end[None, :], 0), axis=1) - tile_start,
                       0, rows).astype(jnp.int32)

    ff = EXPERTS_PER_GROUP * EXPERT_FF
    lanes = jnp.arange(ROUTER_LANES)[None, :, None]
    expert = N_GROUPS + EXPERTS_PER_GROUP * jnp.arange(N_GROUPS)[:, None, None] \
        + jnp.arange(ff)[None, None, :] // EXPERT_FF
    expand = (lanes == expert).astype(BF16)
    wmap = lambda t, tg, nt, nv, ds: (tg[t], 0, 0)
    lmap = lambda t, tg, nt, nv, ds: (layer, tg[t], 0, 0)
    y_tok = pl.pallas_call(
        _moe_expert_kernel,
        out_shape=jax.ShapeDtypeStruct((n * D_ROWS, LANES), F32),
        grid_spec=pltpu.PrefetchScalarGridSpec(
            num_scalar_prefetch=4,
            grid=(n_tiles,),
            in_specs=[
                pl.BlockSpec(memory_space=pl.ANY),
                pl.BlockSpec((None, EXPERTS_PER_GROUP, d, EXPERT_FF), lmap),
                pl.BlockSpec((None, EXPERTS_PER_GROUP, d, EXPERT_FF), lmap),
                pl.BlockSpec((None, EXPERTS_PER_GROUP, EXPERT_FF, d), lmap),
                pl.BlockSpec((1, ROUTER_LANES, ff), wmap),
            ],
            out_specs=pl.BlockSpec(memory_space=pl.ANY),
            scratch_shapes=[pltpu.VMEM((2, rows, ROUTED_COLS), F32),
                            pltpu.VMEM((2, rows * D_ROWS, LANES), F32),
                            pltpu.VMEM((EXPERTS_PER_GROUP, d, EXPERT_FF), BF16),
                            pltpu.VMEM((EXPERTS_PER_GROUP, d, EXPERT_FF), BF16),
                            pltpu.VMEM((EXPERTS_PER_GROUP, EXPERT_FF, d), BF16),
                            pltpu.SMEM((p_rows,), jnp.int32),
                            pltpu.SemaphoreType.DMA((2,)),
                            pltpu.SemaphoreType.DMA((2,))],
        ),
        compiler_params=_params(("arbitrary",)),
        name="moe_experts",
    )(tile_group, n_used, n_valid, dest, hx, w_gate, w_up, w_down, expand)

    return pl.pallas_call(
        _moe_combine_kernel,
        out_shape=jax.ShapeDtypeStruct((b, s, d), F32),
        grid=(b * spt,),
        in_specs=[pl.BlockSpec((1, tm, d), lambda i: (i // spt, i % spt, 0)),
                  pl.BlockSpec((1, 1, d), bmap),
                  pl.BlockSpec((tm * D_ROWS, LANES), lambda i: (i, 0))],
        out_specs=pl.BlockSpec((1, tm, d), lambda i: (i // spt, i % spt, 0)),
        compiler_params=_params(("arbitrary",)),
        name="moe_combine",
    )(x, gate, y_tok)


def kernel(x, c, w_ada, b_ada, norm_mix, norm_ffn, rel_bias, a_w_in, a_w_out, a_q_norm,
           a_k_norm, b_w_in, b_f_bias, b_w_out, b_q_norm, b_k_norm, router_group_w,
           router_group_b, router_expert_w, router_expert_b, w_gate, w_up, w_down):
    b, s, d = x.shape
    depth = w_ada.shape[0]
    mod = _ada(c, w_ada, b_ada).reshape(depth, b, 6, 1, d)
    pad_r = ROUTER_LANES - N_GROUPS - N_EXPERTS
    for i in range(depth):
        sh_m, sc_m, g_m, sh_f, sc_f, g_f = (mod[i, :, k] for k in range(6))
        j = i // 2
        if i % 2 == 0:
            proj = _normproj_a(x, norm_mix[i], sc_m, sh_m, a_w_in[j].astype(BF16))
            outs, lses = [], []
            for g, (window, dil) in enumerate(A_DILATED):
                tab = rel_bias[:, g * N_HEADS:(g + 1) * N_HEADS]
                o, l = _attn_a_group(proj, g, tab, a_q_norm[j], a_k_norm[j], window, dil)
                outs.append(o)
                lses.append(l)
            x = _merge_out(outs, lses, x, g_m, a_w_out[j].astype(BF16))
        else:
            n_qkv = 3 * N_HEADS * HEAD_DIM
            w_in = b_w_in[j]
            w_f = jnp.pad(w_in[:, n_qkv:], ((0, 0), (0, LANES - N_HEADS))).astype(BF16)
            qkv, fg = _normproj(x, norm_mix[i], sc_m, sh_m, w_in[:, :n_qkv].astype(BF16), w_f)
            cum = _cum_forget(fg, b_f_bias[j])
            o = _fox_attention(qkv, cum, b_q_norm[j], b_k_norm[j])
            x = _out_proj(o, x, g_m, b_w_out[j].astype(BF16))
        w_router = jnp.pad(jnp.concatenate([router_group_w[i], router_expert_w[i]], axis=1),
                           ((0, 0), (0, pad_r)))
        b_router = jnp.pad(jnp.concatenate([router_group_b[i], router_expert_b[i]]),
                           (0, pad_r)).reshape(1, ROUTER_LANES)
        x = _moe(x, norm_ffn[i], sc_f, sh_f, g_f, w_router, b_router,
                 w_gate, w_up, w_down, i)
    return x
```

```python
import functools
import math

import jax
import jax.numpy as jnp
import numpy as np
from jax import lax
from jax.experimental import pallas as pl
from jax.experimental.pallas import tpu as pltpu

F32 = jnp.float32
BF16 = jnp.bfloat16

D_MODEL = 1024
HEAD_DIM = 64
N_HEADS = 16
ATTN_SCALE = HEAD_DIM ** -0.5
A_DILATED = ((128, 1), (512, 4), (2048, 16))
Q_BLOCK = 128
REL_BUCKETS = 32
REL_MAX_DIST = 2048
N_GROUPS = 4
EXPERTS_PER_GROUP = 4
N_EXPERTS = 16
EXPERT_FF = 256
EPS = 1e-6
NEG_INF = -1e30

LANES = 128
VMEM_LIMIT = 48 * 1024 * 1024
LOG2E = 1.4426950408889634
LN2 = 0.6931471805599453
PERM_TILE = 1024
PAIR = 2 * HEAD_DIM
N_PAIRS = N_HEADS // 2


def _params(sem):
    return pltpu.CompilerParams(dimension_semantics=sem, vmem_limit_bytes=VMEM_LIMIT)


def _split3(x):
    hi = x.astype(BF16)
    r1 = x - hi.astype(F32)
    mid = r1.astype(BF16)
    lo = (r1 - mid.astype(F32)).astype(BF16)
    return hi, mid, lo


def _dot(a, b):
    return jnp.dot(a, b, preferred_element_type=F32)


def _dot_nt(a, b):
    return lax.dot_general(a, b, (((1,), (1,)), ((), ())), preferred_element_type=F32)


def _ada_kernel(c_ref, w_ref, b_ref, o_ref):
    c = c_ref[...]
    a = c * (1.0 / (1.0 + jnp.exp(-c)))
    o_ref[0] = _dot(a.astype(BF16), w_ref[0].astype(BF16)) + b_ref[0]


def _ada(c, w_ada, b_ada):
    depth, d, n6 = w_ada.shape
    b = c.shape[0]
    rows = 8
    c_pad = jnp.pad(c, ((0, rows - b), (0, 0)))
    tn = 1536
    out = pl.pallas_call(
        _ada_kernel,
        out_shape=jax.ShapeDtypeStruct((depth, rows, n6), F32),
        grid=(depth, n6 // tn),
        in_specs=[
            pl.BlockSpec((rows, d), lambda i, j: (0, 0)),
            pl.BlockSpec((1, d, tn), lambda i, j: (i, 0, j)),
            pl.BlockSpec((1, 1, tn), lambda i, j: (i, 0, j)),
        ],
        out_specs=pl.BlockSpec((1, rows, tn), lambda i, j: (i, 0, j)),
        compiler_params=_params(("arbitrary", "arbitrary")),
        name="ada_mod",
    )(c_pad, w_ada, b_ada.reshape(depth, 1, n6))
    return out[:, :b]


def _modnorm(x, gamma, sc, sh):
    ms = jnp.mean(x * x, axis=-1, keepdims=True)
    y = x * lax.rsqrt(ms + EPS) * gamma
    return y * (1.0 + sc) + sh


def _normproj_kernel(x_ref, g_ref, sc_ref, sh_ref, w_ref, o_ref, h_ref):
    @pl.when(pl.program_id(1) == 0)
    def _():
        h_ref[...] = _modnorm(x_ref[0], g_ref[...], sc_ref[0], sh_ref[0]).astype(BF16)

    o_ref[0] = _dot(h_ref[...], w_ref[...]).astype(BF16)


def _normproj_gate_kernel(x_ref, g_ref, sc_ref, sh_ref, w_ref, wf_ref, o_ref, f_ref, h_ref):
    @pl.when(pl.program_id(1) == 0)
    def _():
        h = _modnorm(x_ref[0], g_ref[...], sc_ref[0], sh_ref[0]).astype(BF16)
        h_ref[...] = h
        f_ref[0] = _dot(h, wf_ref[...])

    o_ref[0] = _dot(h_ref[...], w_ref[...]).astype(BF16)


def _normproj(x, gamma, sc, sh, w, w_f=None, tm=1024, tn=1536):
    b, s, d = x.shape
    ncols = w.shape[1]
    spt = s // tm
    grid = (b * spt, ncols // tn)
    xmap = lambda i, j: (i // spt, i % spt, 0)
    bmap = lambda i, j: (i // spt, 0, 0)
    in_specs = [
        pl.BlockSpec((1, tm, d), xmap),
        pl.BlockSpec((1, d), lambda i, j: (0, 0)),
        pl.BlockSpec((1, 1, d), bmap),
        pl.BlockSpec((1, 1, d), bmap),
        pl.BlockSpec((d, tn), lambda i, j: (0, j)),
    ]
    o_spec = pl.BlockSpec((1, tm, tn), lambda i, j: (i // spt, i % spt, j))
    scratch = [pltpu.VMEM((tm, d), BF16)]
    if w_f is None:
        return pl.pallas_call(
            _normproj_kernel,
            out_shape=jax.ShapeDtypeStruct((b, s, ncols), BF16),
            grid=grid, in_specs=in_specs, out_specs=o_spec, scratch_shapes=scratch,
            compiler_params=_params(("arbitrary", "arbitrary")),
            name="normproj",
        )(x, gamma.reshape(1, d), sc, sh, w)
    in_specs.append(pl.BlockSpec((d, LANES), lambda i, j: (0, 0)))
    f_spec = pl.BlockSpec((1, tm, LANES), xmap)
    return pl.pallas_call(
        _normproj_gate_kernel,
        out_shape=(jax.ShapeDtypeStruct((b, s, ncols), BF16),
                   jax.ShapeDtypeStruct((b, s, LANES), F32)),
        grid=grid, in_specs=in_specs, out_specs=(o_spec, f_spec), scratch_shapes=scratch,
        compiler_params=_params(("arbitrary", "arbitrary")),
        name="normproj_gate",
    )(x, gamma.reshape(1, d), sc, sh, w, w_f)


def _normproj_a_kernel(x_ref, g_ref, sc_ref, sh_ref, w_ref, o_ref, hf_ref, hp_ref):
    tm = x_ref.shape[1]
    n_slab = x_ref.shape[2] // LANES

    @pl.when(pl.program_id(1) == 0)
    def _():
        h = _modnorm(x_ref[0], g_ref[...], sc_ref[0], sh_ref[0])
        hp_ref[0] = h.astype(BF16)
        for c in range(n_slab):
            hf_ref[c] = h[:, c * LANES:(c + 1) * LANES]
        for g, (_, r) in enumerate(A_DILATED):
            if r == 1:
                continue
            n = tm // r
            for jr in range(r):
                for c in range(n_slab):
                    hp_ref[g, jr * n:(jr + 1) * n, c * LANES:(c + 1) * LANES] = (
                        hf_ref[c, pl.ds(jr, n, stride=r), :].astype(BF16))

    group = (pl.program_id(1) * w_ref.shape[1]) // (3 * N_HEADS * HEAD_DIM)
    o_ref[0] = _dot(hp_ref[group], w_ref[...]).astype(BF16)


def _normproj_a(x, gamma, sc, sh, w, tn=1536):
    b, s, d = x.shape
    tm = PERM_TILE
    ncols = w.shape[1]
    spt = s // tm
    bmap = lambda i, j: (i // spt, 0, 0)
    return pl.pallas_call(
        _normproj_a_kernel,
        out_shape=jax.ShapeDtypeStruct((b, s, ncols), BF16),
        grid=(b * spt, ncols // tn),
        in_specs=[
            pl.BlockSpec((1, tm, d), lambda i, j: (i // spt, i % spt, 0)),
            pl.BlockSpec((1, d), lambda i, j: (0, 0)),
            pl.BlockSpec((1, 1, d), bmap),
            pl.BlockSpec((1, 1, d), bmap),
            pl.BlockSpec((d, tn), lambda i, j: (0, j)),
        ],
        out_specs=pl.BlockSpec((1, tm, tn), lambda i, j: (i // spt, i % spt, j)),
        scratch_shapes=[pltpu.VMEM((d // LANES, tm, LANES), F32),
                        pltpu.VMEM((len(A_DILATED), tm, d), BF16)],
        compiler_params=_params(("arbitrary", "arbitrary")),
        name="normproj_a",
    )(x, gamma.reshape(1, d), sc, sh, w)


def _headnorm(t, gain):
    t = t.astype(F32)
    ms = jnp.mean(t * t, axis=-1, keepdims=True)
    return t * lax.rsqrt(ms + EPS) * gain


def _pairnorm(t, gain2, even):
    t = t.astype(F32)
    sq = t * t
    tot = jnp.sum(sq, axis=-1, keepdims=True)
    s_even = jnp.sum(jnp.where(even, sq, 0.0), axis=-1, keepdims=True)
    inv = jnp.where(even, lax.rsqrt(s_even * (1.0 / HEAD_DIM) + EPS),
                    lax.rsqrt((tot - s_even) * (1.0 / HEAD_DIM) + EPS))
    return t * inv * gain2


def _rows(ref, cols=slice(None)):
    if len(ref.shape) == 2:
        return ref[:, cols]
    return jnp.concatenate([ref[i, :, cols] for i in range(ref.shape[0])], axis=0)


def _store_rows(ref, cols, val):
    if len(ref.shape) == 2:
        ref[:, cols] = val
    else:
        n = ref.shape[1]
        for i in range(ref.shape[0]):
            ref[i, :, cols] = val[i * n:(i + 1) * n]


def _attn_a_kernel(q_ref, k_ref, v_ref, bias_ref, gq_ref, gk_ref, o_ref, lse_ref,
                   kn_ref, vv_ref):
    ub = pl.program_id(2)
    slot = ub % 2
    prev = 1 - slot

    @pl.when(ub == 0)
    def _():
        kn_ref[1] = jnp.zeros(kn_ref.shape[1:], BF16)
        vv_ref[1] = jnp.zeros(vv_ref.shape[1:], BF16)

    first = jnp.minimum(ub, 1)
    lane = lax.broadcasted_iota(jnp.int32, (Q_BLOCK, PAIR), 1)
    even = lane < HEAD_DIM
    lane_row = lax.broadcasted_iota(jnp.int32, (1, PAIR), 1)
    keep_even = jnp.where(lane_row < HEAD_DIM, 1.0, 0.0).astype(BF16)
    keep_odd = jnp.where(lane_row < HEAD_DIM, 0.0, 1.0).astype(BF16)
    gq2 = gq_ref[...] * (ATTN_SCALE * LOG2E)
    gk2 = gk_ref[...]
    for hp in range(N_PAIRS):
        cs = slice(hp * PAIR, (hp + 1) * PAIR)
        kn_ref[slot, :, cs] = _pairnorm(_rows(k_ref, cs), gk2, even).astype(BF16)
    vv_ref[slot] = _rows(v_ref)
    lse_tile = jnp.zeros((Q_BLOCK, LANES), F32)
    for hp in range(N_PAIRS):
        cs = slice(hp * PAIR, (hp + 1) * PAIR)
        qn = _pairnorm(_rows(q_ref, cs), gq2, even)
        qq = jnp.concatenate([jnp.where(even, qn, 0.0), jnp.where(even, 0.0, qn)],
                             axis=0).astype(BF16)
        kcat = jnp.concatenate([kn_ref[prev, :, cs], kn_ref[slot, :, cs]], axis=0)
        s = _dot_nt(qq, kcat) + bias_ref[first, hp]
        m = jnp.max(s, axis=-1, keepdims=True)
        p = jnp.exp2(s - m)
        l = jnp.sum(p, axis=-1, keepdims=True)
        pb = p.astype(BF16)
        vcat = jnp.concatenate([vv_ref[prev, :, cs], vv_ref[slot, :, cs]], axis=0)
        o = (_dot(pb[:Q_BLOCK], vcat * keep_even)
             + _dot(pb[Q_BLOCK:], vcat * keep_odd))
        inv = 1.0 / l
        o = o * jnp.where(even, inv[:Q_BLOCK], inv[Q_BLOCK:])
        _store_rows(o_ref, cs, o.astype(BF16))
        lse = m * LN2 + jnp.log(l)
        lse_tile = jnp.where(lane == 2 * hp, lse[:Q_BLOCK],
                             jnp.where(lane == 2 * hp + 1, lse[Q_BLOCK:], lse_tile))
    _store_rows(lse_ref, slice(None), lse_tile)


def _t5_bucket(dist):
    max_exact = REL_BUCKETS // 2
    d = np.maximum(dist, 0)
    large = max_exact + (np.log(np.maximum(d, 1).astype(np.float32) / max_exact)
                         / math.log(REL_MAX_DIST / max_exact)
                         * (REL_BUCKETS - max_exact)).astype(np.int32)
    large = np.minimum(large, REL_BUCKETS - 1)
    return np.where(d < max_exact, d, large)


def _bias_kernel(tab_ref, onehot_ref, o_ref):
    hi, mid, lo = _split3(tab_ref[...])
    oh = onehot_ref[...]
    o_ref[...] = (_dot(hi, oh) + _dot(mid, oh) + _dot(lo, oh)) * LOG2E


def _bias_tables(rel_bias_g, window, dilation):
    qi = np.arange(Q_BLOCK)[:, None]
    kj = np.arange(2 * Q_BLOCK)[None, :]
    dist = qi + Q_BLOCK - kj
    bucket = _t5_bucket(dist * dilation).reshape(1, -1)
    onehot = jnp.asarray(bucket == np.arange(REL_BUCKETS)[:, None], BF16)
    npos = onehot.shape[1]
    chunk = 4096
    bias = pl.pallas_call(
        _bias_kernel,
        out_shape=jax.ShapeDtypeStruct((N_HEADS, npos), F32),
        grid=(npos // chunk,),
        in_specs=[pl.BlockSpec((N_HEADS, REL_BUCKETS), lambda i: (0, 0)),
                  pl.BlockSpec((REL_BUCKETS, chunk), lambda i: (0, i))],
        out_specs=pl.BlockSpec((N_HEADS, chunk), lambda i: (0, i)),
        compiler_params=_params(("arbitrary",)),
        name="rel_bias_table",
    )(rel_bias_g.T.astype(F32), onehot)
    bias = bias.reshape(N_PAIRS, 2 * Q_BLOCK, 2 * Q_BLOCK)
    band = (dist >= 0) & (dist <= window // dilation)
    first = band & (qi - dist >= 0)
    band = np.concatenate([band, band], axis=0)[None]
    first = np.concatenate([first, first], axis=0)[None]
    return jnp.stack([jnp.where(first, bias, NEG_INF), jnp.where(band, bias, NEG_INF)], axis=0)


def _attn_a_group(proj, g, rel_bias_g, gq, gk, window, dilation):
    b, s, cols = proj.shape
    r = dilation
    nt = s // PERM_TILE
    n_t = PERM_TILE // r
    nb = s // r // Q_BLOCK
    gw = N_HEADS * HEAD_DIM
    tables = _bias_tables(rel_bias_g, window, dilation)
    base = g * 3
    if n_t >= Q_BLOCK:
        bpt = n_t // Q_BLOCK
        blk = lambda w: (None, None, None, Q_BLOCK, w)
        imap = lambda col: (lambda bi, jr, ub: (bi, ub // bpt, jr, ub % bpt, col))
    else:
        blk = lambda w: (None, Q_BLOCK // n_t, None, n_t, w)
        imap = lambda col: (lambda bi, jr, ub: (bi, ub, jr, 0, col))
    pv = proj.reshape(b, nt, r, n_t, cols)
    gains = lambda t: jnp.concatenate([t, t]).reshape(1, PAIR)
    o, lse = pl.pallas_call(
        _attn_a_kernel,
        out_shape=(jax.ShapeDtypeStruct((b, nt, r, n_t, gw), BF16),
                   jax.ShapeDtypeStruct((b, nt, r, n_t, LANES), F32)),
        grid=(b, r, nb),
        in_specs=[
            pl.BlockSpec(blk(gw), imap(base)),
            pl.BlockSpec(blk(gw), imap(base + 1)),
            pl.BlockSpec(blk(gw), imap(base + 2)),
            pl.BlockSpec(tables.shape, lambda bi, jr, ub: (0, 0, 0, 0)),
            pl.BlockSpec((1, PAIR), lambda bi, jr, ub: (0, 0)),
            pl.BlockSpec((1, PAIR), lambda bi, jr, ub: (0, 0)),
        ],
        out_specs=(pl.BlockSpec(blk(gw), imap(0)), pl.BlockSpec(blk(LANES), imap(0))),
        scratch_shapes=[pltpu.VMEM((2, Q_BLOCK, gw), BF16), pltpu.VMEM((2, Q_BLOCK, gw), BF16)],
        compiler_params=_params(("arbitrary", "arbitrary", "arbitrary")),
        name=f"attn_a_g{g}",
    )(pv, pv, pv, tables, gains(gq), gains(gk))
    return o.reshape(b, s, gw), lse.reshape(b, s, LANES)


MERGE_CHUNK = 128


def _merge_out_kernel(o0_ref, o1_ref, o2_ref, l0_ref, l1_ref, l2_ref, x_ref, g_ref, w_ref,
                      e_ref, out_ref, lun_ref, oun_ref, om_ref):
    tm = o0_ref.shape[1]
    n_slab = o0_ref.shape[2] // LANES

    @pl.when(pl.program_id(1) == 0)
    def _():
        for gi, (o_ref, l_ref) in enumerate(((o1_ref, l1_ref), (o2_ref, l2_ref))):
            r = A_DILATED[gi + 1][1]
            n = tm // r
            for jr in range(r):
                rows = slice(jr * n, (jr + 1) * n)
                lun_ref[gi, pl.ds(jr, n, stride=r), :] = l_ref[0, rows, :]
                for c in range(n_slab):
                    oun_ref[gi, c, pl.ds(jr, n, stride=r), :] = (
                        o_ref[0, rows, c * LANES:(c + 1) * LANES].astype(F32))
        expand = e_ref[...]

        def widen(w):
            hi = w.astype(BF16)
            lo = (w - hi.astype(F32)).astype(BF16)
            return _dot(hi, expand) + _dot(lo, expand)

        for k in range(tm // MERGE_CHUNK):
            rs = slice(k * MERGE_CHUNK, (k + 1) * MERGE_CHUNK)
            l0, l1, l2 = l0_ref[0, rs, :], lun_ref[0, rs, :], lun_ref[1, rs, :]
            m = jnp.maximum(jnp.maximum(l0, l1), l2)
            e0, e1, e2 = jnp.exp(l0 - m), jnp.exp(l1 - m), jnp.exp(l2 - m)
            inv = 1.0 / (e0 + e1 + e2)
            o1 = jnp.concatenate([oun_ref[0, c, rs, :] for c in range(n_slab)], axis=1)
            o2 = jnp.concatenate([oun_ref[1, c, rs, :] for c in range(n_slab)], axis=1)
            o = (widen(e0 * inv) * o0_ref[0, rs, :].astype(F32)
                 + widen(e1 * inv) * o1 + widen(e2 * inv) * o2)
            om_ref[rs, :] = o.astype(BF16)

    out_ref[0] = x_ref[0] + g_ref[0] * _dot(om_ref[...], w_ref[...])


def _merge_out(outs, lses, x, gate, w_out, tn=512):
    b, s, d = x.shape
    tm = PERM_TILE
    spt = s // tm
    rmap = lambda i, j: (i // spt, i % spt, 0)
    cmap = lambda i, j: (i // spt, i % spt, j)
    expand = ((jnp.arange(d)[None, :] // HEAD_DIM == jnp.arange(LANES)[:, None])
              & (jnp.arange(LANES)[:, None] < N_HEADS)).astype(BF16)
    o_spec = pl.BlockSpec((1, tm, d), rmap)
    l_spec = pl.BlockSpec((1, tm, LANES), rmap)
    return pl.pallas_call(
        _merge_out_kernel,
        out_shape=jax.ShapeDtypeStruct((b, s, d), F32),
        grid=(b * spt, d // tn),
        in_specs=[o_spec, o_spec, o_spec, l_spec, l_spec, l_spec,
                  pl.BlockSpec((1, tm, tn), cmap),
                  pl.BlockSpec((1, 1, tn), lambda i, j: (i // spt, 0, j)),
                  pl.BlockSpec((d, tn), lambda i, j: (0, j)),
                  pl.BlockSpec((LANES, d), lambda i, j: (0, 0))],
        out_specs=pl.BlockSpec((1, tm, tn), cmap),
        scratch_shapes=[pltpu.VMEM((2, tm, LANES), F32),
                        pltpu.VMEM((2, d // LANES, tm, LANES), F32),
                        pltpu.VMEM((tm, d), BF16)],
        compiler_params=_params(("arbitrary", "arbitrary")),
        name="merge_out",
    )(*outs, *lses, x, gate, w_out, expand)


def _out_kernel(o_ref, x_ref, g_ref, w_ref, out_ref):
    out_ref[0] = x_ref[0] + g_ref[0] * _dot(o_ref[0], w_ref[...])


def _out_proj(o, x, gate, w_out, tm=512):
    b, s, d = x.shape
    spt = s // tm
    xmap = lambda i: (i // spt, i % spt, 0)
    return pl.pallas_call(
        _out_kernel,
        out_shape=jax.ShapeDtypeStruct((b, s, d), F32),
        grid=(b * spt,),
        in_specs=[pl.BlockSpec((1, tm, d), xmap),
                  pl.BlockSpec((1, tm, d), xmap),
                  pl.BlockSpec((1, 1, d), lambda i: (i // spt, 0, 0)),
                  pl.BlockSpec((d, d), lambda i: (0, 0))],
        out_specs=pl.BlockSpec((1, tm, d), xmap),
        compiler_params=_params(("arbitrary",)),
        name="out_proj",
    )(o, x, gate, w_out)


def _cum_kernel(f_ref, fb_ref, tri_ref, o_ref, carry_ref):
    @pl.when(pl.program_id(1) == 0)
    def _():
        carry_ref[...] = jnp.zeros_like(carry_ref)

    z = f_ref[0] + fb_ref[...]
    logf = jnp.minimum(z, 0.0) - jnp.log(1.0 + jnp.exp(-jnp.abs(z)))
    hi, mid, lo = _split3(logf)
    tri = tri_ref[...]
    cum = _dot(tri, hi) + _dot(tri, mid) + _dot(tri, lo) + carry_ref[...]
    o_ref[0] = cum
    carry_ref[...] = cum[-1:, :]


def _cum_forget(fg, f_bias, tc=512):
    b, s, w = fg.shape
    fb = jnp.pad(f_bias, (0, w - f_bias.shape[0])).reshape(1, w)
    tri = (jnp.arange(tc)[:, None] >= jnp.arange(tc)[None, :]).astype(BF16)
    return pl.pallas_call(
        _cum_kernel,
        out_shape=jax.ShapeDtypeStruct((b, s, w), F32),
        grid=(b, s // tc),
        in_specs=[pl.BlockSpec((1, tc, w), lambda i, j: (i, j, 0)),
                  pl.BlockSpec((1, w), lambda i, j: (0, 0)),
                  pl.BlockSpec((tc, tc), lambda i, j: (0, 0))],
        out_specs=pl.BlockSpec((1, tc, w), lambda i, j: (i, j, 0)),
        scratch_shapes=[pltpu.VMEM((1, w), F32)],
        compiler_params=_params(("arbitrary", "arbitrary")),
        name="cum_forget",
    )(fg, fb, tri)


ONES_ROWS = 16
FOX_HEADS_PER_STEP = 8
GATE_ROWS = HEAD_DIM // FOX_HEADS_PER_STEP
assert GATE_ROWS >= 6


def _fox_kernel(q_ref, k_ref, v_ref, cq_ref, ck_ref, gq_ref, gk_ref, o_ref, ka_ref, vt_ref,
                qat_ref, cqt_ref, ckt_ref, s_ref, p_ref, acc_ref, *, tq, tk):
    hg = pl.program_id(1)
    qi = pl.program_id(2)
    n_kt = vt_ref.shape[1]
    nh = ka_ref.shape[0]
    heads = range(nh)

    @pl.when(qi == 0)
    def _():
        gk = gk_ref[...]
        ckt_ref[...] = (ck_ref[0] * LOG2E).T
        hi, mid, lo = (-p.astype(F32) for p in _split3(ckt_ref[pl.ds(nh * hg, nh), :]))
        s_len = ckt_ref.shape[1]
        blocks = [jnp.zeros((HEAD_DIM, s_len), F32)]
        for hh in heads:
            blocks += [jnp.ones((3, s_len), F32), hi[hh:hh + 1], mid[hh:hh + 1], lo[hh:hh + 1],
                       jnp.zeros((GATE_ROWS - 6, s_len), F32)]
        gate_cols = jnp.concatenate(blocks, axis=0).T.astype(BF16)
        for hh in heads:
            sl = slice(hh * HEAD_DIM, (hh + 1) * HEAD_DIM)
            ka_ref[hh, :, :HEAD_DIM] = _headnorm(k_ref[0, :, sl], gk).astype(BF16)
            ka_ref[hh, :, HEAD_DIM:] = gate_cols[:, HEAD_DIM:]
        for pr in range(nh // 2):
            vt = v_ref[0, :, pr * PAIR:(pr + 1) * PAIR].astype(F32).T.astype(BF16)
            for hh in (2 * pr, 2 * pr + 1):
                rows = slice((hh % 2) * HEAD_DIM, (hh % 2 + 1) * HEAD_DIM)
                for jj in range(n_kt):
                    vt_ref[hh, jj, :HEAD_DIM] = vt[rows, jj * tk:(jj + 1) * tk]
                    vt_ref[hh, jj, HEAD_DIM:] = jnp.ones((ONES_ROWS, tk), BF16)

    even = lax.broadcasted_iota(jnp.int32, (tq, PAIR), 1) < HEAD_DIM
    gq2 = gq_ref[...] * (ATTN_SCALE * LOG2E)
    cqt_ref[...] = (cq_ref[0] * LOG2E).T
    hi, mid, lo = (p.astype(F32) for p in _split3(cqt_ref[pl.ds(nh * hg, nh), :]))
    for pr in range(nh // 2):
        q_t = _pairnorm(q_ref[0, :, pr * PAIR:(pr + 1) * PAIR], gq2, even).T
        for hh in (2 * pr, 2 * pr + 1):
            r0 = (hh % 2) * HEAD_DIM
            qat_ref[hh, :HEAD_DIM] = q_t[r0:r0 + HEAD_DIM].astype(BF16)
            pieces = [hi[hh:hh + 1], mid[hh:hh + 1], lo[hh:hh + 1], jnp.ones((3, tq), F32)]
            before, after = hh * GATE_ROWS, HEAD_DIM - (hh + 1) * GATE_ROWS + GATE_ROWS - 6
            if before:
                pieces.insert(0, jnp.zeros((before, tq), F32))
            pieces.append(jnp.zeros((after, tq), F32))
            qat_ref[hh, HEAD_DIM:] = jnp.concatenate(pieces, axis=0).astype(BF16)
    key = lax.broadcasted_iota(jnp.int32, (tk, tq), 0)
    qry = lax.broadcasted_iota(jnp.int32, (tk, tq), 1)
    ahead = key - qry

    def scores(t, hh, diagonal):
        start = pl.multiple_of(t * tk, tk)
        st = _dot(ka_ref[hh, pl.ds(start, tk), :], qat_ref[hh])
        if diagonal is None:
            st = jnp.where(ahead > jnp.where(t == qi, 0, tk), NEG_INF, st)
        elif diagonal:
            st = jnp.where(ahead > 0, NEG_INF, st)
        s_ref[hh] = st
        return jnp.max(st, axis=0, keepdims=True)

    def pv(t, hh, alpha_prev):
        acc_ref[hh] = alpha_prev * acc_ref[hh] + _dot(vt_ref[hh, t], p_ref[hh])

    def trip(i, carry, next_diagonal):
        out = []
        for hh in heads:
            m, alpha_prev, mx = carry[hh]
            pv(jnp.maximum(i - 1, 0), hh, alpha_prev)
            m_new = jnp.maximum(m, mx)
            alpha = jnp.exp2(m - m_new)
            p_ref[hh] = jnp.exp2(s_ref[hh] - m_new).astype(BF16)
            if next_diagonal is not None:
                mx = scores(i + 1, hh, next_diagonal)
            out.append((m_new, alpha, mx))
        return tuple(out)

    p_ref[...] = jnp.zeros_like(p_ref)
    acc_ref[...] = jnp.zeros_like(acc_ref)
    carry = tuple((jnp.full((1, tq), NEG_INF, F32), jnp.ones((1, tq), F32),
                   scores(0, hh, None)) for hh in heads)
    carry = lax.fori_loop(0, jnp.maximum(qi - 1, 0),
                          functools.partial(trip, next_diagonal=False), carry)
    carry = lax.fori_loop(jnp.maximum(qi - 1, 0), qi,
                          functools.partial(trip, next_diagonal=True), carry)
    carry = trip(qi, carry, None)
    for pr in range(nh // 2):
        o_t = []
        for hh in (2 * pr, 2 * pr + 1):
            pv(qi, hh, carry[hh][1])
            acc = acc_ref[hh]
            o_t.append(acc[:HEAD_DIM] / acc[HEAD_DIM:HEAD_DIM + 1])
        o_ref[0, :, pr * PAIR:(pr + 1) * PAIR] = (
            jnp.concatenate(o_t, axis=0).T.astype(BF16))


def _fox_attention(qkv, cum, gq, gk, tq=256):
    b, s, _ = qkv.shape
    tk = tq
    nq = s // tq
    nh = FOX_HEADS_PER_STEP
    groups = N_HEADS // nh
    gw = nh * HEAD_DIM
    vrows = HEAD_DIM + ONES_ROWS
    return pl.pallas_call(
        functools.partial(_fox_kernel, tq=tq, tk=tk),
        out_shape=jax.ShapeDtypeStruct((b, s, N_HEADS * HEAD_DIM), BF16),
        grid=(b, groups, nq),
        in_specs=[
            pl.BlockSpec((1, tq, gw), lambda bi, hg, qi: (bi, qi, hg)),
            pl.BlockSpec((1, s, gw), lambda bi, hg, qi: (bi, 0, groups + hg)),
            pl.BlockSpec((1, s, gw), lambda bi, hg, qi: (bi, 0, 2 * groups + hg)),
            pl.BlockSpec((1, tq, LANES), lambda bi, hg, qi: (bi, qi, 0)),
            pl.BlockSpec((1, s, LANES), lambda bi, hg, qi: (bi, 0, 0)),
            pl.BlockSpec((1, PAIR), lambda bi, hg, qi: (0, 0)),
            pl.BlockSpec((1, HEAD_DIM), lambda bi, hg, qi: (0, 0)),
        ],
        out_specs=pl.BlockSpec((1, tq, gw), lambda bi, hg, qi: (bi, qi, hg)),
        scratch_shapes=[pltpu.VMEM((nh, s, 2 * HEAD_DIM), BF16),
                        pltpu.VMEM((nh, s // tk, vrows, tk), BF16),
                        pltpu.VMEM((nh, 2 * HEAD_DIM, tq), BF16),
                        pltpu.VMEM((LANES, tq), F32),
                        pltpu.VMEM((LANES, s), F32),
                        pltpu.VMEM((nh, tk, tq), F32),
                        pltpu.VMEM((nh, tk, tq), BF16),
                        pltpu.VMEM((nh, vrows, tq), F32)],
        compiler_params=_params(("arbitrary", "arbitrary", "arbitrary")),
        name="fox_attn",
    )(qkv, qkv, qkv, cum, cum, jnp.concatenate([gq, gq]).reshape(1, PAIR),
      gk.reshape(1, HEAD_DIM))


ROUTER_LANES = LANES


def _route(logits):
    lane = lax.broadcasted_iota(jnp.int32, logits.shape, 1)
    lane_f = lane.astype(F32)
    big = float(ROUTER_LANES)
    is_g = lane < N_GROUPS
    gl = jnp.where(is_g, logits, -jnp.inf)
    gmax = jnp.max(gl, axis=-1, keepdims=True)
    gsum = jnp.sum(jnp.where(is_g, jnp.exp(logits - gmax), 0.0), axis=-1, keepdims=True)
    g_w = 1.0 / gsum
    gidx = jnp.min(jnp.where(gl == gmax, lane_f, big), axis=-1, keepdims=True)
    lo = N_GROUPS + EXPERTS_PER_GROUP * gidx
    in_grp = (lane_f >= lo) & (lane_f < lo + EXPERTS_PER_GROUP)
    el = jnp.where(in_grp, logits, -jnp.inf)
    t1 = jnp.max(el, axis=-1, keepdims=True)
    i1 = jnp.min(jnp.where(el == t1, lane_f, big), axis=-1, keepdims=True)
    el2 = jnp.where(lane_f == i1, -jnp.inf, el)
    t2 = jnp.max(el2, axis=-1, keepdims=True)
    i2 = jnp.min(jnp.where(el2 == t2, lane_f, big), axis=-1, keepdims=True)
    e2 = jnp.exp(t2 - t1)
    w1 = g_w / (1.0 + e2)
    w2 = w1 * e2
    gates = jnp.where(lane_f == i1, w1, jnp.where(lane_f == i2, w2, 0.0))
    return jnp.where(lane == 0, gidx, gates)


MOE_TILE = 256
SUBLANES = 8
ROUTED_COLS = D_MODEL + LANES


def _moe_route_kernel(x_ref, gam_ref, sc_ref, sh_ref, wr_ref, br_ref, tri_ref, o_ref, meta_ref,
                      cnt_ref):
    h = _modnorm(x_ref[0], gam_ref[...], sc_ref[0], sh_ref[0])
    h_hi = h.astype(BF16)
    h_lo = (h - h_hi.astype(F32)).astype(BF16)
    wr = wr_ref[...]
    w_hi = wr.astype(BF16)
    w_lo = (wr - w_hi.astype(F32)).astype(BF16)
    logits = _dot(h_hi, w_hi) + _dot(h_hi, w_lo) + _dot(h_lo, w_hi) + br_ref[...]
    route = _route(logits)
    tm = route.shape[0]

    @pl.when(pl.program_id(0) == 0)
    def _():
        cnt_ref[...] = jnp.zeros_like(cnt_ref)

    lane = lax.broadcasted_iota(jnp.int32, route.shape, 1)
    member = jnp.where((lane.astype(F32) == route[:, 0:1]) & (lane < N_GROUPS), 1.0, 0.0)
    incl = _dot(tri_ref[...], member.astype(BF16))
    rank = jnp.sum(member * (incl - member + cnt_ref[...]), axis=-1, keepdims=True)
    cnt_ref[...] = cnt_ref[...] + incl[tm - 1:tm, :]
    meta = jnp.where(lane == 0, route[:, 0:1], jnp.where(lane == 1, rank, 0.0))
    meta_ref[...] = meta.T[:SUBLANES, :]

    o_ref[:, :D_MODEL] = h
    o_ref[:, D_MODEL:] = route


def _row_copy(src, src_row, dst, dst_row, sem):
    return pltpu.make_async_copy(src.at[pl.ds(src_row, 1), :], dst.at[pl.ds(dst_row, 1), :], sem)


D_ROWS = D_MODEL // LANES


def _tile_copy(src, src_tok, dst, dst_tok, sem):
    s0 = pl.multiple_of(src_tok * D_ROWS, D_ROWS)
    d0 = pl.multiple_of(dst_tok * D_ROWS, D_ROWS)
    return pltpu.make_async_copy(src.at[pl.ds(s0, D_ROWS), :], dst.at[pl.ds(d0, D_ROWS), :], sem)


def _moe_expert_kernel(tg_ref, nt_ref, nv_ref, dest_ref, hx_hbm, wg_ref, wu_ref, wd_ref, e_ref,
                       y_hbm, buf, ybuf, wg_bf, wu_bf, wd_bf, src_ref, gsem, ssem):
    t = pl.program_id(0)
    n_used = nt_ref[0]
    slot = t % 2
    rows = buf.shape[1]
    unroll = 8

    @pl.when(t == 0)
    def _():
        def clear(p, c):
            src_ref[p] = 0
            return c
        lax.fori_loop(0, src_ref.shape[0], clear, 0, unroll=unroll)

        def invert(tok, c):
            src_ref[dest_ref[tok]] = tok
            return c
        lax.fori_loop(0, dest_ref.shape[0], invert, 0, unroll=unroll)

    def gather_start(tile, sl):
        def body(r, c):
            _row_copy(hx_hbm, src_ref[tile * rows + r], buf.at[sl], r, gsem.at[sl]).start()
            return c
        lax.fori_loop(0, rows, body, 0, unroll=unroll)

    def gather_wait(sl):
        def body(r, c):
            _row_copy(hx_hbm, 0, buf.at[sl], r, gsem.at[sl]).wait()
            return c
        lax.fori_loop(0, rows, body, 0, unroll=unroll)

    def valid_rows_loop(tile, body):
        n_valid = nv_ref[tile]
        chunks = n_valid // unroll

        def chunk(k, c):
            for i in range(unroll):
                body(k * unroll + i)
            return c
        lax.fori_loop(0, chunks, chunk, 0)

        def single(r, c):
            body(r)
            return c
        lax.fori_loop(chunks * unroll, n_valid, single, 0)

    def scatter_start(tile, sl):
        valid_rows_loop(tile, lambda r: _tile_copy(
            ybuf.at[sl], r, y_hbm, src_ref[tile * rows + r], ssem.at[sl]).start(priority=1))

    def scatter_wait(tile, sl):
        valid_rows_loop(tile, lambda r: _tile_copy(ybuf.at[sl], r, y_hbm, 0, ssem.at[sl]).wait())

    @pl.when(t == 0)
    def _():
        gather_start(0, 0)

    @pl.when(t + 1 < n_used)
    def _():
        gather_start(t + 1, 1 - slot)

    @pl.when((t == 0) | (tg_ref[t] != tg_ref[jnp.maximum(t - 1, 0)]))
    def _():
        wg_bf[...] = wg_ref[...].astype(BF16)
        wu_bf[...] = wu_ref[...].astype(BF16)
        wd_bf[...] = wd_ref[...].astype(BF16)

    @pl.when(t < n_used)
    def _():
        gather_wait(slot)
        h = buf[slot, :, :D_MODEL].astype(BF16)
        gates = buf[slot, :, D_MODEL:]
        g_hi = gates.astype(BF16)
        g_lo = (gates - g_hi.astype(F32)).astype(BF16)
        y = None
        for e in range(EXPERTS_PER_GROUP):
            a = _dot(h, wg_bf[e])
            u = _dot(h, wu_bf[e])
            spread = e_ref[0, :, e * EXPERT_FF:(e + 1) * EXPERT_FF]
            gexp = _dot(g_hi, spread) + _dot(g_lo, spread)
            hid = a * (1.0 / (1.0 + jnp.exp(-a))) * u * gexp
            part = _dot(hid.astype(BF16), wd_bf[e])
            y = part if y is None else y + part

        @pl.when(t >= 2)
        def _():
            scatter_wait(t - 2, slot)

        for c in range(D_ROWS):
            ybuf[slot, pl.ds(c, rows, stride=D_ROWS), :] = y[:, c * LANES:(c + 1) * LANES]
        scatter_start(t, slot)

    @pl.when(t == pl.num_programs(0) - 1)
    def _():
        for back in (2, 1):
            scatter_wait(n_used - back, (n_used - back) % 2)


def _moe_combine_kernel(x_ref, g_ref, y_ref, o_ref):
    tm = x_ref.shape[1]
    tm = x_ref.shape[1]
    y = jnp.concatenate([y_ref[pl.ds(c, tm, stride=D_ROWS), :] for c in range(D_ROWS)], axis=1)
    o_ref[0] = x_ref[0] + g_ref[0] * y


def _moe(x, gamma, sc, sh, gate, w_router, b_router, w_gate, w_up, w_down, layer):
    b, s, d = x.shape
    n = b * s
    tm = PERM_TILE
    spt = s // tm
    rows = MOE_TILE
    n_tiles = n // rows + N_GROUPS
    p_rows = n_tiles * rows
    bmap = lambda i: (i // spt, 0, 0)
    tri = (jnp.arange(tm)[:, None] >= jnp.arange(tm)[None, :]).astype(BF16)
    hx, meta = pl.pallas_call(
        _moe_route_kernel,
        out_shape=(jax.ShapeDtypeStruct((n, ROUTED_COLS), F32),
                   jax.ShapeDtypeStruct((SUBLANES, n), F32)),
        grid=(b * spt,),
        in_specs=[
            pl.BlockSpec((1, tm, d), lambda i: (i // spt, i % spt, 0)),
            pl.BlockSpec((1, d), lambda i: (0, 0)),
            pl.BlockSpec((1, 1, d), bmap),
            pl.BlockSpec((1, 1, d), bmap),
            pl.BlockSpec((d, ROUTER_LANES), lambda i: (0, 0)),
            pl.BlockSpec((1, ROUTER_LANES), lambda i: (0, 0)),
            pl.BlockSpec((tm, tm), lambda i: (0, 0)),
        ],
        out_specs=(pl.BlockSpec((tm, ROUTED_COLS), lambda i: (i, 0)),
                   pl.BlockSpec((SUBLANES, tm), lambda i: (0, i))),
        scratch_shapes=[pltpu.VMEM((1, LANES), F32)],
        compiler_params=_params(("arbitrary",)),
        name="moe_route",
    )(x, gamma.reshape(1, d), sc, sh, w_router, b_router, tri)

    gidx = meta[0].astype(jnp.int32)
    rank = meta[1].astype(jnp.int32)
    onehot = (gidx[:, None] == jnp.arange(N_GROUPS)[None, :]).astype(jnp.int32)
    count = jnp.sum(onehot, axis=0)
    padded = (count + rows - 1) // rows * rows
    ends = jnp.cumsum(padded)
    dest = jnp.sum(onehot * (ends - padded)[None, :], axis=1) + rank
    n_used = (ends[-1] // rows).reshape(1).astype(jnp.int32)
    tile_start = jnp.arange(n_tiles, dtype=jnp.int32) * rows
    tile_group = jnp.minimum(jnp.sum(ends[None, :] <= tile_start[:, None], axis=1),
                             N_GROUPS - 1).astype(jnp.int32)
    seg_valid_end = ends - padded + count
    n_valid = jnp.clip(jnp.sum(jnp.where(jnp.arange(N_GROUPS)[None, :] == tile_group[:, None],
                                         seg_valid_end[None, :], 0), axis=1) - tile_start,
                       0, rows).astype(jnp.int32)

    ff = EXPERTS_PER_GROUP * EXPERT_FF
    lanes = jnp.arange(ROUTER_LANES)[None, :, None]
    expert = N_GROUPS + EXPERTS_PER_GROUP * jnp.arange(N_GROUPS)[:, None, None] \
        + jnp.arange(ff)[None, None, :] // EXPERT_FF
    expand = (lanes == expert).astype(BF16)
    wmap = lambda t, tg, nt, nv, ds: (tg[t], 0, 0)
    lmap = lambda t, tg, nt, nv, ds: (layer, tg[t], 0, 0)
    y_tok = pl.pallas_call(
        _moe_expert_kernel,
        out_shape=jax.ShapeDtypeStruct((n * D_ROWS, LANES), F32),
        grid_spec=pltpu.PrefetchScalarGridSpec(
            num_scalar_prefetch=4,
            grid=(n_tiles,),
            in_specs=[
                pl.BlockSpec(memory_space=pl.ANY),
                pl.BlockSpec((None, EXPERTS_PER_GROUP, d, EXPERT_FF), lmap),
                pl.BlockSpec((None, EXPERTS_PER_GROUP, d, EXPERT_FF), lmap),
                pl.BlockSpec((None, EXPERTS_PER_GROUP, EXPERT_FF, d), lmap),
                pl.BlockSpec((1, ROUTER_LANES, ff), wmap),
            ],
            out_specs=pl.BlockSpec(memory_space=pl.ANY),
            scratch_shapes=[pltpu.VMEM((2, rows, ROUTED_COLS), F32),
                            pltpu.VMEM((2, rows * D_ROWS, LANES), F32),
                            pltpu.VMEM((EXPERTS_PER_GROUP, d, EXPERT_FF), BF16),
                            pltpu.VMEM((EXPERTS_PER_GROUP, d, EXPERT_FF), BF16),
                            pltpu.VMEM((EXPERTS_PER_GROUP, EXPERT_FF, d), BF16),
                            pltpu.SMEM((p_rows,), jnp.int32),
                            pltpu.SemaphoreType.DMA((2,)),
                            pltpu.SemaphoreType.DMA((2,))],
        ),
        compiler_params=_params(("arbitrary",)),
        name="moe_experts",
    )(tile_group, n_used, n_valid, dest, hx, w_gate, w_up, w_down, expand)

    return pl.pallas_call(
        _moe_combine_kernel,
        out_shape=jax.ShapeDtypeStruct((b, s, d), F32),
        grid=(b * spt,),
        in_specs=[pl.BlockSpec((1, tm, d), lambda i: (i // spt, i % spt, 0)),
                  pl.BlockSpec((1, 1, d), bmap),
                  pl.BlockSpec((tm * D_ROWS, LANES), lambda i: (i, 0))],
        out_specs=pl.BlockSpec((1, tm, d), lambda i: (i // spt, i % spt, 0)),
        compiler_params=_params(("arbitrary",)),
        name="moe_combine",
    )(x, gate, y_tok)


def kernel(x, c, w_ada, b_ada, norm_mix, norm_ffn, rel_bias, a_w_in, a_w_out, a_q_norm,
           a_k_norm, b_w_in, b_f_bias, b_w_out, b_q_norm, b_k_norm, router_group_w,
           router_group_b, router_expert_w, router_expert_b, w_gate, w_up, w_down):
    b, s, d = x.shape
    depth = w_ada.shape[0]
    mod = _ada(c, w_ada, b_ada).reshape(depth, b, 6, 1, d)
    pad_r = ROUTER_LANES - N_GROUPS - N_EXPERTS
    for i in range(depth):
        sh_m, sc_m, g_m, sh_f, sc_f, g_f = (mod[i, :, k] for k in range(6))
        j = i // 2
        if i % 2 == 0:
            proj = _normproj_a(x, norm_mix[i], sc_m, sh_m, a_w_in[j].astype(BF16))
            outs, lses = [], []
            for g, (window, dil) in enumerate(A_DILATED):
                tab = rel_bias[:, g * N_HEADS:(g + 1) * N_HEADS]
                o, l = _attn_a_group(proj, g, tab, a_q_norm[j], a_k_norm[j], window, dil)
                outs.append(o)
                lses.append(l)
            x = _merge_out(outs, lses, x, g_m, a_w_out[j].astype(BF16))
        else:
            n_qkv = 3 * N_HEADS * HEAD_DIM
            w_in = b_w_in[j]
            w_f = jnp.pad(w_in[:, n_qkv:], ((0, 0), (0, LANES - N_HEADS))).astype(BF16)
            qkv, fg = _normproj(x, norm_mix[i], sc_m, sh_m, w_in[:, :n_qkv].astype(BF16), w_f)
            cum = _cum_forget(fg, b_f_bias[j])
            o = _fox_attention(qkv, cum, b_q_norm[j], b_k_norm[j])
            x = _out_proj(o, x, g_m, b_w_out[j].astype(BF16))
        w_router = jnp.pad(jnp.concatenate([router_group_w[i], router_expert_w[i]], axis=1),
                           ((0, 0), (0, pad_r)))
        b_router = jnp.pad(jnp.concatenate([router_group_b[i], router_expert_b[i]]),
                           (0, pad_r)).reshape(1, ROUTER_LANES)
        x = _moe(x, norm_ffn[i], sc_f, sh_f, g_f, w_router, b_router,
                 w_gate, w_up, w_down, i)
    return x
```

```python
import functools
import math

import jax
import jax.numpy as jnp
import numpy as np
from jax import lax
from jax.experimental import pallas as pl
from jax.experimental.pallas import tpu as pltpu

F32 = jnp.float32
BF16 = jnp.bfloat16

D_MODEL = 1024
HEAD_DIM = 64
N_HEADS = 16
ATTN_SCALE = HEAD_DIM ** -0.5
A_DILATED = ((128, 1), (512, 4), (2048, 16))
Q_BLOCK = 128
REL_BUCKETS = 32
REL_MAX_DIST = 2048
N_GROUPS = 4
EXPERTS_PER_GROUP = 4
N_EXPERTS = 16
EXPERT_FF = 256
EPS = 1e-6
NEG_INF = -1e30

LANES = 128
VMEM_LIMIT = 48 * 1024 * 1024
LOG2E = 1.4426950408889634
LN2 = 0.6931471805599453
PERM_TILE = 1024
PAIR = 2 * HEAD_DIM
N_PAIRS = N_HEADS // 2


def _params(sem):
    return pltpu.CompilerParams(dimension_semantics=sem, vmem_limit_bytes=VMEM_LIMIT)


def _split3(x):
    hi = x.astype(BF16)
    r1 = x - hi.astype(F32)
    mid = r1.astype(BF16)
    lo = (r1 - mid.astype(F32)).astype(BF16)
    return hi, mid, lo


def _dot(a, b):
    return jnp.dot(a, b, preferred_element_type=F32)


def _dot_nt(a, b):
    return lax.dot_general(a, b, (((1,), (1,)), ((), ())), preferred_element_type=F32)


def _ada_kernel(c_ref, w_ref, b_ref, o_ref):
    c = c_ref[...]
    a = c * (1.0 / (1.0 + jnp.exp(-c)))
    o_ref[0] = _dot(a.astype(BF16), w_ref[0].astype(BF16)) + b_ref[0]


def _ada(c, w_ada, b_ada):
    depth, d, n6 = w_ada.shape
    b = c.shape[0]
    rows = 8
    c_pad = jnp.pad(c, ((0, rows - b), (0, 0)))
    tn = 1536
    out = pl.pallas_call(
        _ada_kernel,
        out_shape=jax.ShapeDtypeStruct((depth, rows, n6), F32),
        grid=(depth, n6 // tn),
        in_specs=[
            pl.BlockSpec((rows, d), lambda i, j: (0, 0)),
            pl.BlockSpec((1, d, tn), lambda i, j: (i, 0, j)),
            pl.BlockSpec((1, 1, tn), lambda i, j: (i, 0, j)),
        ],
        out_specs=pl.BlockSpec((1, rows, tn), lambda i, j: (i, 0, j)),
        compiler_params=_params(("arbitrary", "arbitrary")),
        name="ada_mod",
    )(c_pad, w_ada, b_ada.reshape(depth, 1, n6))
    return out[:, :b]


def _modnorm(x, gamma, sc, sh):
    ms = jnp.mean(x * x, axis=-1, keepdims=True)
    y = x * lax.rsqrt(ms + EPS) * gamma
    return y * (1.0 + sc) + sh


def _normproj_kernel(x_ref, g_ref, sc_ref, sh_ref, w_ref, o_ref, h_ref):
    @pl.when(pl.program_id(1) == 0)
    def _():
        h_ref[...] = _modnorm(x_ref[0], g_ref[...], sc_ref[0], sh_ref[0]).astype(BF16)

    o_ref[0] = _dot(h_ref[...], w_ref[...]).astype(BF16)


def _normproj_gate_kernel(x_ref, g_ref, sc_ref, sh_ref, w_ref, wf_ref, o_ref, f_ref, h_ref):
    @pl.when(pl.program_id(1) == 0)
    def _():
        h = _modnorm(x_ref[0], g_ref[...], sc_ref[0], sh_ref[0]).astype(BF16)
        h_ref[...] = h
        f_ref[0] = _dot(h, wf_ref[...])

    o_ref[0] = _dot(h_ref[...], w_ref[...]).astype(BF16)


def _normproj_gate_resid_kernel(x_ref, y_ref, gy_ref, g_ref, sc_ref, sh_ref, w_ref, wf_ref,
                                o_ref, f_ref, xn_ref, h_ref):
    @pl.when(pl.program_id(1) == 0)
    def _():
        tm = x_ref.shape[1]
        y = jnp.concatenate([y_ref[pl.ds(c, tm, stride=D_ROWS), :] for c in range(D_ROWS)],
                            axis=1)
        xn = x_ref[0] + gy_ref[0] * y
        xn_ref[0] = xn
        h = _modnorm(xn, g_ref[...], sc_ref[0], sh_ref[0]).astype(BF16)
        h_ref[...] = h
        f_ref[0] = _dot(h, wf_ref[...])

    o_ref[0] = _dot(h_ref[...], w_ref[...]).astype(BF16)


def _normproj_resid(x, y_tok, gate_y, gamma, sc, sh, w, w_f, tm=1024, tn=1536):
    b, s, d = x.shape
    ncols = w.shape[1]
    spt = s // tm
    xmap = lambda i, j: (i // spt, i % spt, 0)
    bmap = lambda i, j: (i // spt, 0, 0)
    return pl.pallas_call(
        _normproj_gate_resid_kernel,
        out_shape=(jax.ShapeDtypeStruct((b, s, ncols), BF16),
                   jax.ShapeDtypeStruct((b, s, LANES), F32),
                   jax.ShapeDtypeStruct((b, s, d), F32)),
        grid=(b * spt, ncols // tn),
        in_specs=[
            pl.BlockSpec((1, tm, d), xmap),
            pl.BlockSpec((tm * D_ROWS, LANES), lambda i, j: (i, 0)),
            pl.BlockSpec((1, 1, d), bmap),
            pl.BlockSpec((1, d), lambda i, j: (0, 0)),
            pl.BlockSpec((1, 1, d), bmap),
            pl.BlockSpec((1, 1, d), bmap),
            pl.BlockSpec((d, tn), lambda i, j: (0, j)),
            pl.BlockSpec((d, LANES), lambda i, j: (0, 0)),
        ],
        out_specs=(pl.BlockSpec((1, tm, tn), lambda i, j: (i // spt, i % spt, j)),
                   pl.BlockSpec((1, tm, LANES), xmap),
                   pl.BlockSpec((1, tm, d), xmap)),
        scratch_shapes=[pltpu.VMEM((tm, d), BF16)],
        compiler_params=_params(("arbitrary", "arbitrary")),
        name="normproj_gate_resid",
    )(x, y_tok, gate_y, gamma.reshape(1, d), sc, sh, w, w_f)


def _normproj(x, gamma, sc, sh, w, w_f=None, tm=1024, tn=1536):
    b, s, d = x.shape
    ncols = w.shape[1]
    spt = s // tm
    grid = (b * spt, ncols // tn)
    xmap = lambda i, j: (i // spt, i % spt, 0)
    bmap = lambda i, j: (i // spt, 0, 0)
    in_specs = [
        pl.BlockSpec((1, tm, d), xmap),
        pl.BlockSpec((1, d), lambda i, j: (0, 0)),
        pl.BlockSpec((1, 1, d), bmap),
        pl.BlockSpec((1, 1, d), bmap),
        pl.BlockSpec((d, tn), lambda i, j: (0, j)),
    ]
    o_spec = pl.BlockSpec((1, tm, tn), lambda i, j: (i // spt, i % spt, j))
    scratch = [pltpu.VMEM((tm, d), BF16)]
    if w_f is None:
        return pl.pallas_call(
            _normproj_kernel,
            out_shape=jax.ShapeDtypeStruct((b, s, ncols), BF16),
            grid=grid, in_specs=in_specs, out_specs=o_spec, scratch_shapes=scratch,
            compiler_params=_params(("arbitrary", "arbitrary")),
            name="normproj",
        )(x, gamma.reshape(1, d), sc, sh, w)
    in_specs.append(pl.BlockSpec((d, LANES), lambda i, j: (0, 0)))
    f_spec = pl.BlockSpec((1, tm, LANES), xmap)
    return pl.pallas_call(
        _normproj_gate_kernel,
        out_shape=(jax.ShapeDtypeStruct((b, s, ncols), BF16),
                   jax.ShapeDtypeStruct((b, s, LANES), F32)),
        grid=grid, in_specs=in_specs, out_specs=(o_spec, f_spec), scratch_shapes=scratch,
        compiler_params=_params(("arbitrary", "arbitrary")),
        name="normproj_gate",
    )(x, gamma.reshape(1, d), sc, sh, w, w_f)


def _normproj_a_kernel(x_ref, g_ref, sc_ref, sh_ref, w_ref, o_ref, hf_ref, hp_ref):
    tm = x_ref.shape[1]
    n_slab = x_ref.shape[2] // LANES

    @pl.when(pl.program_id(1) == 0)
    def _():
        h = _modnorm(x_ref[0], g_ref[...], sc_ref[0], sh_ref[0])
        hp_ref[0] = h.astype(BF16)
        for c in range(n_slab):
            hf_ref[c] = h[:, c * LANES:(c + 1) * LANES]
        for g, (_, r) in enumerate(A_DILATED):
            if r == 1:
                continue
            n = tm // r
            for jr in range(r):
                for c in range(n_slab):
                    hp_ref[g, jr * n:(jr + 1) * n, c * LANES:(c + 1) * LANES] = (
                        hf_ref[c, pl.ds(jr, n, stride=r), :].astype(BF16))

    group = (pl.program_id(1) * w_ref.shape[1]) // (3 * N_HEADS * HEAD_DIM)
    o_ref[0] = _dot(hp_ref[group], w_ref[...]).astype(BF16)


def _normproj_a(x, gamma, sc, sh, w, tn=1536):
    b, s, d = x.shape
    tm = PERM_TILE
    ncols = w.shape[1]
    spt = s // tm
    bmap = lambda i, j: (i // spt, 0, 0)
    return pl.pallas_call(
        _normproj_a_kernel,
        out_shape=jax.ShapeDtypeStruct((b, s, ncols), BF16),
        grid=(b * spt, ncols // tn),
        in_specs=[
            pl.BlockSpec((1, tm, d), lambda i, j: (i // spt, i % spt, 0)),
            pl.BlockSpec((1, d), lambda i, j: (0, 0)),
            pl.BlockSpec((1, 1, d), bmap),
            pl.BlockSpec((1, 1, d), bmap),
            pl.BlockSpec((d, tn), lambda i, j: (0, j)),
        ],
        out_specs=pl.BlockSpec((1, tm, tn), lambda i, j: (i // spt, i % spt, j)),
        scratch_shapes=[pltpu.VMEM((d // LANES, tm, LANES), F32),
                        pltpu.VMEM((len(A_DILATED), tm, d), BF16)],
        compiler_params=_params(("arbitrary", "arbitrary")),
        name="normproj_a",
    )(x, gamma.reshape(1, d), sc, sh, w)


def _headnorm(t, gain):
    t = t.astype(F32)
    ms = jnp.mean(t * t, axis=-1, keepdims=True)
    return t * lax.rsqrt(ms + EPS) * gain


def _pairnorm(t, gain2, even):
    t = t.astype(F32)
    sq = t * t
    tot = jnp.sum(sq, axis=-1, keepdims=True)
    s_even = jnp.sum(jnp.where(even, sq, 0.0), axis=-1, keepdims=True)
    inv = jnp.where(even, lax.rsqrt(s_even * (1.0 / HEAD_DIM) + EPS),
                    lax.rsqrt((tot - s_even) * (1.0 / HEAD_DIM) + EPS))
    return t * inv * gain2


def _rows(ref, cols=slice(None)):
    if len(ref.shape) == 2:
        return ref[:, cols]
    return jnp.concatenate([ref[i, :, cols] for i in range(ref.shape[0])], axis=0)


def _store_rows(ref, cols, val):
    if len(ref.shape) == 2:
        ref[:, cols] = val
    else:
        n = ref.shape[1]
        for i in range(ref.shape[0]):
            ref[i, :, cols] = val[i * n:(i + 1) * n]


def _attn_a_kernel(q_ref, k_ref, v_ref, bias_ref, gq_ref, gk_ref, o_ref, lse_ref,
                   kn_ref, vv_ref):
    ub = pl.program_id(2)
    slot = ub % 2
    prev = 1 - slot

    @pl.when(ub == 0)
    def _():
        kn_ref[1] = jnp.zeros(kn_ref.shape[1:], BF16)
        vv_ref[1] = jnp.zeros(vv_ref.shape[1:], BF16)

    first = jnp.minimum(ub, 1)
    lane = lax.broadcasted_iota(jnp.int32, (Q_BLOCK, PAIR), 1)
    even = lane < HEAD_DIM
    lane_row = lax.broadcasted_iota(jnp.int32, (1, PAIR), 1)
    keep_even = jnp.where(lane_row < HEAD_DIM, 1.0, 0.0).astype(BF16)
    keep_odd = jnp.where(lane_row < HEAD_DIM, 0.0, 1.0).astype(BF16)
    gq2 = gq_ref[...] * (ATTN_SCALE * LOG2E)
    gk2 = gk_ref[...]
    for hp in range(N_PAIRS):
        cs = slice(hp * PAIR, (hp + 1) * PAIR)
        kn_ref[slot, :, cs] = _pairnorm(_rows(k_ref, cs), gk2, even).astype(BF16)
    vv_ref[slot] = _rows(v_ref)
    lse_tile = jnp.zeros((Q_BLOCK, LANES), F32)
    for hp in range(N_PAIRS):
        cs = slice(hp * PAIR, (hp + 1) * PAIR)
        qn = _pairnorm(_rows(q_ref, cs), gq2, even)
        qq = jnp.concatenate([jnp.where(even, qn, 0.0), jnp.where(even, 0.0, qn)],
                             axis=0).astype(BF16)
        kcat = jnp.concatenate([kn_ref[prev, :, cs], kn_ref[slot, :, cs]], axis=0)
        s = _dot_nt(qq, kcat) + bias_ref[first, hp]
        m = jnp.max(s, axis=-1, keepdims=True)
        p = jnp.exp2(s - m)
        l = jnp.sum(p, axis=-1, keepdims=True)
        pb = p.astype(BF16)
        vcat = jnp.concatenate([vv_ref[prev, :, cs], vv_ref[slot, :, cs]], axis=0)
        o = (_dot(pb[:Q_BLOCK], vcat * keep_even)
             + _dot(pb[Q_BLOCK:], vcat * keep_odd))
        inv = 1.0 / l
        o = o * jnp.where(even, inv[:Q_BLOCK], inv[Q_BLOCK:])
        _store_rows(o_ref, cs, o.astype(BF16))
        lse = m * LN2 + jnp.log(l)
        lse_tile = jnp.where(lane == 2 * hp, lse[:Q_BLOCK],
                             jnp.where(lane == 2 * hp + 1, lse[Q_BLOCK:], lse_tile))
    _store_rows(lse_ref, slice(None), lse_tile)


def _t5_bucket(dist):
    max_exact = REL_BUCKETS // 2
    d = np.maximum(dist, 0)
    large = max_exact + (np.log(np.maximum(d, 1).astype(np.float32) / max_exact)
                         / math.log(REL_MAX_DIST / max_exact)
                         * (REL_BUCKETS - max_exact)).astype(np.int32)
    large = np.minimum(large, REL_BUCKETS - 1)
    return np.where(d < max_exact, d, large)


def _bias_kernel(tab_ref, onehot_ref, o_ref):
    hi, mid, lo = _split3(tab_ref[...])
    oh = onehot_ref[...]
    o_ref[...] = (_dot(hi, oh) + _dot(mid, oh) + _dot(lo, oh)) * LOG2E


def _bias_tables(rel_bias_g, window, dilation):
    qi = np.arange(Q_BLOCK)[:, None]
    kj = np.arange(2 * Q_BLOCK)[None, :]
    dist = qi + Q_BLOCK - kj
    bucket = _t5_bucket(dist * dilation).reshape(1, -1)
    onehot = jnp.asarray(bucket == np.arange(REL_BUCKETS)[:, None], BF16)
    npos = onehot.shape[1]
    chunk = 4096
    bias = pl.pallas_call(
        _bias_kernel,
        out_shape=jax.ShapeDtypeStruct((N_HEADS, npos), F32),
        grid=(npos // chunk,),
        in_specs=[pl.BlockSpec((N_HEADS, REL_BUCKETS), lambda i: (0, 0)),
                  pl.BlockSpec((REL_BUCKETS, chunk), lambda i: (0, i))],
        out_specs=pl.BlockSpec((N_HEADS, chunk), lambda i: (0, i)),
        compiler_params=_params(("arbitrary",)),
        name="rel_bias_table",
    )(rel_bias_g.T.astype(F32), onehot)
    bias = bias.reshape(N_PAIRS, 2 * Q_BLOCK, 2 * Q_BLOCK)
    band = (dist >= 0) & (dist <= window // dilation)
    first = band & (qi - dist >= 0)
    band = np.concatenate([band, band], axis=0)[None]
    first = np.concatenate([first, first], axis=0)[None]
    return jnp.stack([jnp.where(first, bias, NEG_INF), jnp.where(band, bias, NEG_INF)], axis=0)


def _attn_a_group(proj, g, rel_bias_g, gq, gk, window, dilation):
    b, s, cols = proj.shape
    r = dilation
    nt = s // PERM_TILE
    n_t = PERM_TILE // r
    nb = s // r // Q_BLOCK
    gw = N_HEADS * HEAD_DIM
    tables = _bias_tables(rel_bias_g, window, dilation)
    base = g * 3
    if n_t >= Q_BLOCK:
        bpt = n_t // Q_BLOCK
        blk = lambda w: (None, None, None, Q_BLOCK, w)
        imap = lambda col: (lambda bi, jr, ub: (bi, ub // bpt, jr, ub % bpt, col))
    else:
        blk = lambda w: (None, Q_BLOCK // n_t, None, n_t, w)
        imap = lambda col: (lambda bi, jr, ub: (bi, ub, jr, 0, col))
    pv = proj.reshape(b, nt, r, n_t, cols)
    gains = lambda t: jnp.concatenate([t, t]).reshape(1, PAIR)
    o, lse = pl.pallas_call(
        _attn_a_kernel,
        out_shape=(jax.ShapeDtypeStruct((b, nt, r, n_t, gw), BF16),
                   jax.ShapeDtypeStruct((b, nt, r, n_t, LANES), F32)),
        grid=(b, r, nb),
        in_specs=[
            pl.BlockSpec(blk(gw), imap(base)),
            pl.BlockSpec(blk(gw), imap(base + 1)),
            pl.BlockSpec(blk(gw), imap(base + 2)),
            pl.BlockSpec(tables.shape, lambda bi, jr, ub: (0, 0, 0, 0)),
            pl.BlockSpec((1, PAIR), lambda bi, jr, ub: (0, 0)),
            pl.BlockSpec((1, PAIR), lambda bi, jr, ub: (0, 0)),
        ],
        out_specs=(pl.BlockSpec(blk(gw), imap(0)), pl.BlockSpec(blk(LANES), imap(0))),
        scratch_shapes=[pltpu.VMEM((2, Q_BLOCK, gw), BF16), pltpu.VMEM((2, Q_BLOCK, gw), BF16)],
        compiler_params=_params(("arbitrary", "arbitrary", "arbitrary")),
        name=f"attn_a_g{g}",
    )(pv, pv, pv, tables, gains(gq), gains(gk))
    return o.reshape(b, s, gw), lse.reshape(b, s, LANES)


MERGE_CHUNK = 128


def _merge_out_kernel(o0_ref, o1_ref, o2_ref, l0_ref, l1_ref, l2_ref, x_ref, g_ref, w_ref,
                      e_ref, out_ref, lun_ref, oun_ref, om_ref):
    tm = o0_ref.shape[1]
    n_slab = o0_ref.shape[2] // LANES

    @pl.when(pl.program_id(1) == 0)
    def _():
        for gi, (o_ref, l_ref) in enumerate(((o1_ref, l1_ref), (o2_ref, l2_ref))):
            r = A_DILATED[gi + 1][1]
            n = tm // r
            for jr in range(r):
                rows = slice(jr * n, (jr + 1) * n)
                lun_ref[gi, pl.ds(jr, n, stride=r), :] = l_ref[0, rows, :]
                for c in range(n_slab):
                    oun_ref[gi, c, pl.ds(jr, n, stride=r), :] = (
                        o_ref[0, rows, c * LANES:(c + 1) * LANES].astype(F32))
        expand = e_ref[...]

        def widen(w):
            hi = w.astype(BF16)
            lo = (w - hi.astype(F32)).astype(BF16)
            return _dot(hi, expand) + _dot(lo, expand)

        for k in range(tm // MERGE_CHUNK):
            rs = slice(k * MERGE_CHUNK, (k + 1) * MERGE_CHUNK)
            l0, l1, l2 = l0_ref[0, rs, :], lun_ref[0, rs, :], lun_ref[1, rs, :]
            m = jnp.maximum(jnp.maximum(l0, l1), l2)
            e0, e1, e2 = jnp.exp(l0 - m), jnp.exp(l1 - m), jnp.exp(l2 - m)
            inv = 1.0 / (e0 + e1 + e2)
            o1 = jnp.concatenate([oun_ref[0, c, rs, :] for c in range(n_slab)], axis=1)
            o2 = jnp.concatenate([oun_ref[1, c, rs, :] for c in range(n_slab)], axis=1)
            o = (widen(e0 * inv) * o0_ref[0, rs, :].astype(F32)
                 + widen(e1 * inv) * o1 + widen(e2 * inv) * o2)
            om_ref[rs, :] = o.astype(BF16)

    out_ref[0] = x_ref[0] + g_ref[0] * _dot(om_ref[...], w_ref[...])


def _merge_out(outs, lses, x, gate, w_out, tn=512):
    b, s, d = x.shape
    tm = PERM_TILE
    spt = s // tm
    rmap = lambda i, j: (i // spt, i % spt, 0)
    cmap = lambda i, j: (i // spt, i % spt, j)
    expand = ((jnp.arange(d)[None, :] // HEAD_DIM == jnp.arange(LANES)[:, None])
              & (jnp.arange(LANES)[:, None] < N_HEADS)).astype(BF16)
    o_spec = pl.BlockSpec((1, tm, d), rmap)
    l_spec = pl.BlockSpec((1, tm, LANES), rmap)
    return pl.pallas_call(
        _merge_out_kernel,
        out_shape=jax.ShapeDtypeStruct((b, s, d), F32),
        grid=(b * spt, d // tn),
        in_specs=[o_spec, o_spec, o_spec, l_spec, l_spec, l_spec,
                  pl.BlockSpec((1, tm, tn), cmap),
                  pl.BlockSpec((1, 1, tn), lambda i, j: (i // spt, 0, j)),
                  pl.BlockSpec((d, tn), lambda i, j: (0, j)),
                  pl.BlockSpec((LANES, d), lambda i, j: (0, 0))],
        out_specs=pl.BlockSpec((1, tm, tn), cmap),
        scratch_shapes=[pltpu.VMEM((2, tm, LANES), F32),
                        pltpu.VMEM((2, d // LANES, tm, LANES), F32),
                        pltpu.VMEM((tm, d), BF16)],
        compiler_params=_params(("arbitrary", "arbitrary")),
        name="merge_out",
    )(*outs, *lses, x, gate, w_out, expand)


def _out_kernel(o_ref, x_ref, g_ref, w_ref, out_ref):
    out_ref[0] = x_ref[0] + g_ref[0] * _dot(o_ref[0], w_ref[...])


def _out_proj(o, x, gate, w_out, tm=512):
    b, s, d = x.shape
    spt = s // tm
    xmap = lambda i: (i // spt, i % spt, 0)
    return pl.pallas_call(
        _out_kernel,
        out_shape=jax.ShapeDtypeStruct((b, s, d), F32),
        grid=(b * spt,),
        in_specs=[pl.BlockSpec((1, tm, d), xmap),
                  pl.BlockSpec((1, tm, d), xmap),
                  pl.BlockSpec((1, 1, d), lambda i: (i // spt, 0, 0)),
                  pl.BlockSpec((d, d), lambda i: (0, 0))],
        out_specs=pl.BlockSpec((1, tm, d), xmap),
        compiler_params=_params(("arbitrary",)),
        name="out_proj",
    )(o, x, gate, w_out)


def _cum_kernel(f_ref, fb_ref, tri_ref, o_ref, carry_ref):
    @pl.when(pl.program_id(1) == 0)
    def _():
        carry_ref[...] = jnp.zeros_like(carry_ref)

    z = f_ref[0] + fb_ref[...]
    logf = jnp.minimum(z, 0.0) - jnp.log(1.0 + jnp.exp(-jnp.abs(z)))
    hi, mid, lo = _split3(logf)
    tri = tri_ref[...]
    cum = _dot(tri, hi) + _dot(tri, mid) + _dot(tri, lo) + carry_ref[...]
    o_ref[0] = cum
    carry_ref[...] = cum[-1:, :]


def _cum_forget(fg, f_bias, tc=512):
    b, s, w = fg.shape
    fb = jnp.pad(f_bias, (0, w - f_bias.shape[0])).reshape(1, w)
    tri = (jnp.arange(tc)[:, None] >= jnp.arange(tc)[None, :]).astype(BF16)
    return pl.pallas_call(
        _cum_kernel,
        out_shape=jax.ShapeDtypeStruct((b, s, w), F32),
        grid=(b, s // tc),
        in_specs=[pl.BlockSpec((1, tc, w), lambda i, j: (i, j, 0)),
                  pl.BlockSpec((1, w), lambda i, j: (0, 0)),
                  pl.BlockSpec((tc, tc), lambda i, j: (0, 0))],
        out_specs=pl.BlockSpec((1, tc, w), lambda i, j: (i, j, 0)),
        scratch_shapes=[pltpu.VMEM((1, w), F32)],
        compiler_params=_params(("arbitrary", "arbitrary")),
        name="cum_forget",
    )(fg, fb, tri)


ONES_ROWS = 16
FOX_HEADS_PER_STEP = 8
GATE_ROWS = HEAD_DIM // FOX_HEADS_PER_STEP
assert GATE_ROWS >= 6


def _fox_kernel(q_ref, k_ref, v_ref, cq_ref, ck_ref, gq_ref, gk_ref, o_ref, ka_ref, vt_ref,
                qat_ref, cqt_ref, ckt_ref, s_ref, p_ref, acc_ref, *, tq, tk):
    hg = pl.program_id(1)
    qi = pl.program_id(2)
    n_kt = vt_ref.shape[1]
    nh = ka_ref.shape[0]
    heads = range(nh)

    @pl.when(qi == 0)
    def _():
        gk = gk_ref[...]
        ckt_ref[...] = (ck_ref[0] * LOG2E).T
        hi, mid, lo = (-p.astype(F32) for p in _split3(ckt_ref[pl.ds(nh * hg, nh), :]))
        s_len = ckt_ref.shape[1]
        blocks = [jnp.zeros((HEAD_DIM, s_len), F32)]
        for hh in heads:
            blocks += [jnp.ones((3, s_len), F32), hi[hh:hh + 1], mid[hh:hh + 1], lo[hh:hh + 1],
                       jnp.zeros((GATE_ROWS - 6, s_len), F32)]
        gate_cols = jnp.concatenate(blocks, axis=0).T.astype(BF16)
        for hh in heads:
            sl = slice(hh * HEAD_DIM, (hh + 1) * HEAD_DIM)
            ka_ref[hh, :, :HEAD_DIM] = _headnorm(k_ref[0, :, sl], gk).astype(BF16)
            ka_ref[hh, :, HEAD_DIM:] = gate_cols[:, HEAD_DIM:]
        for pr in range(nh // 2):
            vt = v_ref[0, :, pr * PAIR:(pr + 1) * PAIR].astype(F32).T.astype(BF16)
            for hh in (2 * pr, 2 * pr + 1):
                rows = slice((hh % 2) * HEAD_DIM, (hh % 2 + 1) * HEAD_DIM)
                for jj in range(n_kt):
                    vt_ref[hh, jj, :HEAD_DIM] = vt[rows, jj * tk:(jj + 1) * tk]
                    vt_ref[hh, jj, HEAD_DIM:] = jnp.ones((ONES_ROWS, tk), BF16)

    even = lax.broadcasted_iota(jnp.int32, (tq, PAIR), 1) < HEAD_DIM
    gq2 = gq_ref[...] * (ATTN_SCALE * LOG2E)
    cqt_ref[...] = (cq_ref[0] * LOG2E).T
    hi, mid, lo = (p.astype(F32) for p in _split3(cqt_ref[pl.ds(nh * hg, nh), :]))
    for pr in range(nh // 2):
        q_t = _pairnorm(q_ref[0, :, pr * PAIR:(pr + 1) * PAIR], gq2, even).T
        for hh in (2 * pr, 2 * pr + 1):
            r0 = (hh % 2) * HEAD_DIM
            qat_ref[hh, :HEAD_DIM] = q_t[r0:r0 + HEAD_DIM].astype(BF16)
            pieces = [hi[hh:hh + 1], mid[hh:hh + 1], lo[hh:hh + 1], jnp.ones((3, tq), F32)]
            before, after = hh * GATE_ROWS, HEAD_DIM - (hh + 1) * GATE_ROWS + GATE_ROWS - 6
            if before:
                pieces.insert(0, jnp.zeros((before, tq), F32))
            pieces.append(jnp.zeros((after, tq), F32))
            qat_ref[hh, HEAD_DIM:] = jnp.concatenate(pieces, axis=0).astype(BF16)
    key = lax.broadcasted_iota(jnp.int32, (tk, tq), 0)
    qry = lax.broadcasted_iota(jnp.int32, (tk, tq), 1)
    ahead = key - qry

    def scores(t, hh, diagonal):
        start = pl.multiple_of(t * tk, tk)
        st = _dot(ka_ref[hh, pl.ds(start, tk), :], qat_ref[hh])
        if diagonal is None:
            st = jnp.where(ahead > jnp.where(t == qi, 0, tk), NEG_INF, st)
        elif diagonal:
            st = jnp.where(ahead > 0, NEG_INF, st)
        s_ref[hh] = st
        return jnp.max(st, axis=0, keepdims=True)

    def pv(t, hh, alpha_prev):
        acc_ref[hh] = alpha_prev * acc_ref[hh] + _dot(vt_ref[hh, t], p_ref[hh])

    def trip(i, carry, next_diagonal):
        out = []
        for hh in heads:
            m, alpha_prev, mx = carry[hh]
            pv(jnp.maximum(i - 1, 0), hh, alpha_prev)
            m_new = jnp.maximum(m, mx)
            alpha = jnp.exp2(m - m_new)
            p_ref[hh] = jnp.exp2(s_ref[hh] - m_new).astype(BF16)
            if next_diagonal is not None:
                mx = scores(i + 1, hh, next_diagonal)
            out.append((m_new, alpha, mx))
        return tuple(out)

    p_ref[...] = jnp.zeros_like(p_ref)
    acc_ref[...] = jnp.zeros_like(acc_ref)
    carry = tuple((jnp.full((1, tq), NEG_INF, F32), jnp.ones((1, tq), F32),
                   scores(0, hh, None)) for hh in heads)
    carry = lax.fori_loop(0, jnp.maximum(qi - 1, 0),
                          functools.partial(trip, next_diagonal=False), carry)
    carry = lax.fori_loop(jnp.maximum(qi - 1, 0), qi,
                          functools.partial(trip, next_diagonal=True), carry)
    carry = trip(qi, carry, None)
    for pr in range(nh // 2):
        o_t = []
        for hh in (2 * pr, 2 * pr + 1):
            pv(qi, hh, carry[hh][1])
            acc = acc_ref[hh]
            o_t.append(acc[:HEAD_DIM] / acc[HEAD_DIM:HEAD_DIM + 1])
        o_ref[0, :, pr * PAIR:(pr + 1) * PAIR] = (
            jnp.concatenate(o_t, axis=0).T.astype(BF16))


def _fox_attention(qkv, cum, gq, gk, tq=256):
    b, s, _ = qkv.shape
    tk = tq
    nq = s // tq
    nh = FOX_HEADS_PER_STEP
    groups = N_HEADS // nh
    gw = nh * HEAD_DIM
    vrows = HEAD_DIM + ONES_ROWS
    return pl.pallas_call(
        functools.partial(_fox_kernel, tq=tq, tk=tk),
        out_shape=jax.ShapeDtypeStruct((b, s, N_HEADS * HEAD_DIM), BF16),
        grid=(b, groups, nq),
        in_specs=[
            pl.BlockSpec((1, tq, gw), lambda bi, hg, qi: (bi, qi, hg)),
            pl.BlockSpec((1, s, gw), lambda bi, hg, qi: (bi, 0, groups + hg)),
            pl.BlockSpec((1, s, gw), lambda bi, hg, qi: (bi, 0, 2 * groups + hg)),
            pl.BlockSpec((1, tq, LANES), lambda bi, hg, qi: (bi, qi, 0)),
            pl.BlockSpec((1, s, LANES), lambda bi, hg, qi: (bi, 0, 0)),
            pl.BlockSpec((1, PAIR), lambda bi, hg, qi: (0, 0)),
            pl.BlockSpec((1, HEAD_DIM), lambda bi, hg, qi: (0, 0)),
        ],
        out_specs=pl.BlockSpec((1, tq, gw), lambda bi, hg, qi: (bi, qi, hg)),
        scratch_shapes=[pltpu.VMEM((nh, s, 2 * HEAD_DIM), BF16),
                        pltpu.VMEM((nh, s // tk, vrows, tk), BF16),
                        pltpu.VMEM((nh, 2 * HEAD_DIM, tq), BF16),
                        pltpu.VMEM((LANES, tq), F32),
                        pltpu.VMEM((LANES, s), F32),
                        pltpu.VMEM((nh, tk, tq), F32),
                        pltpu.VMEM((nh, tk, tq), BF16),
                        pltpu.VMEM((nh, vrows, tq), F32)],
        compiler_params=_params(("arbitrary", "arbitrary", "arbitrary")),
        name="fox_attn",
    )(qkv, qkv, qkv, cum, cum, jnp.concatenate([gq, gq]).reshape(1, PAIR),
      gk.reshape(1, HEAD_DIM))


ROUTER_LANES = LANES


def _route(logits):
    lane = lax.broadcasted_iota(jnp.int32, logits.shape, 1)
    lane_f = lane.astype(F32)
    big = float(ROUTER_LANES)
    is_g = lane < N_GROUPS
    gl = jnp.where(is_g, logits, -jnp.inf)
    gmax = jnp.max(gl, axis=-1, keepdims=True)
    gsum = jnp.sum(jnp.where(is_g, jnp.exp(logits - gmax), 0.0), axis=-1, keepdims=True)
    g_w = 1.0 / gsum
    gidx = jnp.min(jnp.where(gl == gmax, lane_f, big), axis=-1, keepdims=True)
    lo = N_GROUPS + EXPERTS_PER_GROUP * gidx
    in_grp = (lane_f >= lo) & (lane_f < lo + EXPERTS_PER_GROUP)
    el = jnp.where(in_grp, logits, -jnp.inf)
    t1 = jnp.max(el, axis=-1, keepdims=True)
    i1 = jnp.min(jnp.where(el == t1, lane_f, big), axis=-1, keepdims=True)
    el2 = jnp.where(lane_f == i1, -jnp.inf, el)
    t2 = jnp.max(el2, axis=-1, keepdims=True)
    i2 = jnp.min(jnp.where(el2 == t2, lane_f, big), axis=-1, keepdims=True)
    e2 = jnp.exp(t2 - t1)
    w1 = g_w / (1.0 + e2)
    w2 = w1 * e2
    gates = jnp.where(lane_f == i1, w1, jnp.where(lane_f == i2, w2, 0.0))
    return jnp.where(lane == 0, gidx, gates)


MOE_TILE = 512
SUBLANES = 8
ROUTED_COLS = D_MODEL + LANES


def _moe_route_kernel(x_ref, gam_ref, sc_ref, sh_ref, wr_ref, br_ref, tri_ref, o_ref, meta_ref,
                      cnt_ref):
    h = _modnorm(x_ref[0], gam_ref[...], sc_ref[0], sh_ref[0])
    h_hi = h.astype(BF16)
    h_lo = (h - h_hi.astype(F32)).astype(BF16)
    wr = wr_ref[...]
    w_hi = wr.astype(BF16)
    w_lo = (wr - w_hi.astype(F32)).astype(BF16)
    logits = _dot(h_hi, w_hi) + _dot(h_hi, w_lo) + _dot(h_lo, w_hi) + br_ref[...]
    route = _route(logits)
    tm = route.shape[0]

    @pl.when(pl.program_id(0) == 0)
    def _():
        cnt_ref[...] = jnp.zeros_like(cnt_ref)

    lane = lax.broadcasted_iota(jnp.int32, route.shape, 1)
    member = jnp.where((lane.astype(F32) == route[:, 0:1]) & (lane < N_GROUPS), 1.0, 0.0)
    incl = _dot(tri_ref[...], member.astype(BF16))
    rank = jnp.sum(member * (incl - member + cnt_ref[...]), axis=-1, keepdims=True)
    cnt_ref[...] = cnt_ref[...] + incl[tm - 1:tm, :]
    meta = jnp.where(lane == 0, route[:, 0:1], jnp.where(lane == 1, rank, 0.0))
    meta_ref[...] = meta.T[:SUBLANES, :]

    o_ref[:, :D_MODEL] = h
    o_ref[:, D_MODEL:] = route


def _row_copy(src, src_row, dst, dst_row, sem):
    return pltpu.make_async_copy(src.at[pl.ds(src_row, 1), :], dst.at[pl.ds(dst_row, 1), :], sem)


D_ROWS = D_MODEL // LANES


def _tile_copy(src, src_tok, dst, dst_tok, sem):
    s0 = pl.multiple_of(src_tok * D_ROWS, D_ROWS)
    d0 = pl.multiple_of(dst_tok * D_ROWS, D_ROWS)
    return pltpu.make_async_copy(src.at[pl.ds(s0, D_ROWS), :], dst.at[pl.ds(d0, D_ROWS), :], sem)


def _moe_expert_kernel(tg_ref, nt_ref, nv_ref, dest_ref, hx_hbm, wg_ref, wu_ref, wd_ref, e_ref,
                       y_hbm, buf, ybuf, wg_bf, wu_bf, wd_bf, src_ref, gsem, ssem):
    t = pl.program_id(0)
    n_used = nt_ref[0]
    slot = t % 2
    rows = buf.shape[1]
    unroll = 8

    @pl.when(t == 0)
    def _():
        def clear(p, c):
            src_ref[p] = 0
            return c
        lax.fori_loop(0, src_ref.shape[0], clear, 0, unroll=unroll)

        def invert(tok, c):
            src_ref[dest_ref[tok]] = tok
            return c
        lax.fori_loop(0, dest_ref.shape[0], invert, 0, unroll=unroll)

    def gather_start(tile, sl):
        def body(r, c):
            _row_copy(hx_hbm, src_ref[tile * rows + r], buf.at[sl], r, gsem.at[sl]).start()
            return c
        lax.fori_loop(0, rows, body, 0, unroll=unroll)

    def gather_wait(sl):
        def body(r, c):
            _row_copy(hx_hbm, 0, buf.at[sl], r, gsem.at[sl]).wait()
            return c
        lax.fori_loop(0, rows, body, 0, unroll=unroll)

    def valid_rows_loop(tile, body):
        n_valid = nv_ref[tile]
        chunks = n_valid // unroll

        def chunk(k, c):
            for i in range(unroll):
                body(k * unroll + i)
            return c
        lax.fori_loop(0, chunks, chunk, 0)

        def single(r, c):
            body(r)
            return c
        lax.fori_loop(chunks * unroll, n_valid, single, 0)

    def scatter_start(tile, sl):
        valid_rows_loop(tile, lambda r: _tile_copy(
            ybuf.at[sl], r, y_hbm, src_ref[tile * rows + r], ssem.at[sl]).start(priority=1))

    def scatter_wait(tile, sl):
        valid_rows_loop(tile, lambda r: _tile_copy(ybuf.at[sl], r, y_hbm, 0, ssem.at[sl]).wait())

    @pl.when(t == 0)
    def _():
        gather_start(0, 0)

    @pl.when(t + 1 < n_used)
    def _():
        gather_start(t + 1, 1 - slot)

    @pl.when((t == 0) | (tg_ref[t] != tg_ref[jnp.maximum(t - 1, 0)]))
    def _():
        wg_bf[...] = wg_ref[...].astype(BF16)
        wu_bf[...] = wu_ref[...].astype(BF16)
        wd_bf[...] = wd_ref[...].astype(BF16)

    @pl.when(t < n_used)
    def _():
        gather_wait(slot)
        h = buf[slot, :, :D_MODEL].astype(BF16)
        gates = buf[slot, :, D_MODEL:]
        g_hi = gates.astype(BF16)
        g_lo = (gates - g_hi.astype(F32)).astype(BF16)
        y = None
        for e in range(EXPERTS_PER_GROUP):
            a = _dot(h, wg_bf[e])
            u = _dot(h, wu_bf[e])
            spread = e_ref[0, :, e * EXPERT_FF:(e + 1) * EXPERT_FF]
            gexp = _dot(g_hi, spread) + _dot(g_lo, spread)
            hid = a * (1.0 / (1.0 + jnp.exp(-a))) * u * gexp
            part = _dot(hid.astype(BF16), wd_bf[e])
            y = part if y is None else y + part

        @pl.when(t >= 2)
        def _():
            scatter_wait(t - 2, slot)

        for c in range(D_ROWS):
            ybuf[slot, pl.ds(c, rows, stride=D_ROWS), :] = y[:, c * LANES:(c + 1) * LANES]
        scatter_start(t, slot)

    @pl.when(t == pl.num_programs(0) - 1)
    def _():
        for back in (2, 1):
            scatter_wait(n_used - back, (n_used - back) % 2)


def _moe_combine_kernel(x_ref, g_ref, y_ref, o_ref):
    tm = x_ref.shape[1]
    tm = x_ref.shape[1]
    y = jnp.concatenate([y_ref[pl.ds(c, tm, stride=D_ROWS), :] for c in range(D_ROWS)], axis=1)
    o_ref[0] = x_ref[0] + g_ref[0] * y


def _moe(x, gamma, sc, sh, gate, w_router, b_router, w_gate, w_up, w_down, layer, combine):
    b, s, d = x.shape
    n = b * s
    tm = PERM_TILE
    spt = s // tm
    rows = MOE_TILE
    n_tiles = n // rows + N_GROUPS
    p_rows = n_tiles * rows
    bmap = lambda i: (i // spt, 0, 0)
    tri = (jnp.arange(tm)[:, None] >= jnp.arange(tm)[None, :]).astype(BF16)
    hx, meta = pl.pallas_call(
        _moe_route_kernel,
        out_shape=(jax.ShapeDtypeStruct((n, ROUTED_COLS), F32),
                   jax.ShapeDtypeStruct((SUBLANES, n), F32)),
        grid=(b * spt,),
        in_specs=[
            pl.BlockSpec((1, tm, d), lambda i: (i // spt, i % spt, 0)),
            pl.BlockSpec((1, d), lambda i: (0, 0)),
            pl.BlockSpec((1, 1, d), bmap),
            pl.BlockSpec((1, 1, d), bmap),
            pl.BlockSpec((d, ROUTER_LANES), lambda i: (0, 0)),
            pl.BlockSpec((1, ROUTER_LANES), lambda i: (0, 0)),
            pl.BlockSpec((tm, tm), lambda i: (0, 0)),
        ],
        out_specs=(pl.BlockSpec((tm, ROUTED_COLS), lambda i: (i, 0)),
                   pl.BlockSpec((SUBLANES, tm), lambda i: (0, i))),
        scratch_shapes=[pltpu.VMEM((1, LANES), F32)],
        compiler_params=_params(("arbitrary",)),
        name="moe_route",
    )(x, gamma.reshape(1, d), sc, sh, w_router, b_router, tri)

    gidx = meta[0].astype(jnp.int32)
    rank = meta[1].astype(jnp.int32)
    onehot = (gidx[:, None] == jnp.arange(N_GROUPS)[None, :]).astype(jnp.int32)
    count = jnp.sum(onehot, axis=0)
    padded = (count + rows - 1) // rows * rows
    ends = jnp.cumsum(padded)
    dest = jnp.sum(onehot * (ends - padded)[None, :], axis=1) + rank
    n_used = (ends[-1] // rows).reshape(1).astype(jnp.int32)
    tile_start = jnp.arange(n_tiles, dtype=jnp.int32) * rows
    tile_group = jnp.minimum(jnp.sum(ends[None, :] <= tile_start[:, None], axis=1),
                             N_GROUPS - 1).astype(jnp.int32)
    seg_valid_end = ends - padded + count
    n_valid = jnp.clip(jnp.sum(jnp.where(jnp.arange(N_GROUPS)[None, :] == tile_group[:, None],
                                         seg_valid_end[None, :], 0), axis=1) - tile_start,
                       0, rows).astype(jnp.int32)

    ff = EXPERTS_PER_GROUP * EXPERT_FF
    lanes = jnp.arange(ROUTER_LANES)[None, :, None]
    expert = N_GROUPS + EXPERTS_PER_GROUP * jnp.arange(N_GROUPS)[:, None, None] \
        + jnp.arange(ff)[None, None, :] // EXPERT_FF
    expand = (lanes == expert).astype(BF16)
    wmap = lambda t, tg, nt, nv, ds: (tg[t], 0, 0)
    lmap = lambda t, tg, nt, nv, ds: (layer, tg[t], 0, 0)
    y_tok = pl.pallas_call(
        _moe_expert_kernel,
        out_shape=jax.ShapeDtypeStruct((n * D_ROWS, LANES), F32),
        grid_spec=pltpu.PrefetchScalarGridSpec(
            num_scalar_prefetch=4,
            grid=(n_tiles,),
            in_specs=[
                pl.BlockSpec(memory_space=pl.ANY),
                pl.BlockSpec((None, EXPERTS_PER_GROUP, d, EXPERT_FF), lmap),
                pl.BlockSpec((None, EXPERTS_PER_GROUP, d, EXPERT_FF), lmap),
                pl.BlockSpec((None, EXPERTS_PER_GROUP, EXPERT_FF, d), lmap),
                pl.BlockSpec((1, ROUTER_LANES, ff), wmap),
            ],
            out_specs=pl.BlockSpec(memory_space=pl.ANY),
            scratch_shapes=[pltpu.VMEM((2, rows, ROUTED_COLS), F32),
                            pltpu.VMEM((2, rows * D_ROWS, LANES), F32),
                            pltpu.VMEM((EXPERTS_PER_GROUP, d, EXPERT_FF), BF16),
                            pltpu.VMEM((EXPERTS_PER_GROUP, d, EXPERT_FF), BF16),
                            pltpu.VMEM((EXPERTS_PER_GROUP, EXPERT_FF, d), BF16),
                            pltpu.SMEM((p_rows,), jnp.int32),
                            pltpu.SemaphoreType.DMA((2,)),
                            pltpu.SemaphoreType.DMA((2,))],
        ),
        compiler_params=_params(("arbitrary",)),
        name="moe_experts",
    )(tile_group, n_used, n_valid, dest, hx, w_gate, w_up, w_down, expand)

    if not combine:
        return y_tok
    return pl.pallas_call(
        _moe_combine_kernel,
        out_shape=jax.ShapeDtypeStruct((b, s, d), F32),
        grid=(b * spt,),
        in_specs=[pl.BlockSpec((1, tm, d), lambda i: (i // spt, i % spt, 0)),
                  pl.BlockSpec((1, 1, d), bmap),
                  pl.BlockSpec((tm * D_ROWS, LANES), lambda i: (i, 0))],
        out_specs=pl.BlockSpec((1, tm, d), lambda i: (i // spt, i % spt, 0)),
        compiler_params=_params(("arbitrary",)),
        name="moe_combine",
    )(x, gate, y_tok)


def kernel(x, c, w_ada, b_ada, norm_mix, norm_ffn, rel_bias, a_w_in, a_w_out, a_q_norm,
           a_k_norm, b_w_in, b_f_bias, b_w_out, b_q_norm, b_k_norm, router_group_w,
           router_group_b, router_expert_w, router_expert_b, w_gate, w_up, w_down):
    b, s, d = x.shape
    depth = w_ada.shape[0]
    mod = _ada(c, w_ada, b_ada).reshape(depth, b, 6, 1, d)
    pad_r = ROUTER_LANES - N_GROUPS - N_EXPERTS
    pending = None
    for i in range(depth):
        sh_m, sc_m, g_m, sh_f, sc_f, g_f = (mod[i, :, k] for k in range(6))
        j = i // 2
        if i % 2 == 0:
            assert pending is None
            proj = _normproj_a(x, norm_mix[i], sc_m, sh_m, a_w_in[j].astype(BF16))
            outs, lses = [], []
            for g, (window, dil) in enumerate(A_DILATED):
                tab = rel_bias[:, g * N_HEADS:(g + 1) * N_HEADS]
                o, l = _attn_a_group(proj, g, tab, a_q_norm[j], a_k_norm[j], window, dil)
                outs.append(o)
                lses.append(l)
            x = _merge_out(outs, lses, x, g_m, a_w_out[j].astype(BF16))
        else:
            n_qkv = 3 * N_HEADS * HEAD_DIM
            w_in = b_w_in[j]
            w_f = jnp.pad(w_in[:, n_qkv:], ((0, 0), (0, LANES - N_HEADS))).astype(BF16)
            w_qkv = w_in[:, :n_qkv].astype(BF16)
            if pending is None:
                qkv, fg = _normproj(x, norm_mix[i], sc_m, sh_m, w_qkv, w_f)
            else:
                qkv, fg, x = _normproj_resid(x, *pending, norm_mix[i], sc_m, sh_m, w_qkv, w_f)
                pending = None
            cum = _cum_forget(fg, b_f_bias[j])
            o = _fox_attention(qkv, cum, b_q_norm[j], b_k_norm[j])
            x = _out_proj(o, x, g_m, b_w_out[j].astype(BF16))
        w_router = jnp.pad(jnp.concatenate([router_group_w[i], router_expert_w[i]], axis=1),
                           ((0, 0), (0, pad_r)))
        b_router = jnp.pad(jnp.concatenate([router_group_b[i], router_expert_b[i]]),
                           (0, pad_r)).reshape(1, ROUTER_LANES)
        defer = i + 1 < depth and (i + 1) % 2 == 1
        out = _moe(x, norm_ffn[i], sc_f, sh_f, g_f, w_router, b_router,
                   w_gate, w_up, w_down, i, combine=not defer)
        if defer:
            pending = (out, g_f)
        else:
            x = out
    return x
```

```python
import functools
import math

import jax
import jax.numpy as jnp
import numpy as np
from jax import lax
from jax.experimental import pallas as pl
from jax.experimental.pallas import tpu as pltpu

F32 = jnp.float32
BF16 = jnp.bfloat16

D_MODEL = 1024
HEAD_DIM = 64
N_HEADS = 16
ATTN_SCALE = HEAD_DIM ** -0.5
A_DILATED = ((128, 1), (512, 4), (2048, 16))
Q_BLOCK = 128
REL_BUCKETS = 32
REL_MAX_DIST = 2048
N_GROUPS = 4
EXPERTS_PER_GROUP = 4
N_EXPERTS = 16
EXPERT_FF = 256
EPS = 1e-6
NEG_INF = -1e30

LANES = 128
VMEM_LIMIT = 48 * 1024 * 1024
LOG2E = 1.4426950408889634
LN2 = 0.6931471805599453
PERM_TILE = 1024
PAIR = 2 * HEAD_DIM
N_PAIRS = N_HEADS // 2


def _params(sem):
    return pltpu.CompilerParams(dimension_semantics=sem, vmem_limit_bytes=VMEM_LIMIT)


def _split3(x):
    hi = x.astype(BF16)
    r1 = x - hi.astype(F32)
    mid = r1.astype(BF16)
    lo = (r1 - mid.astype(F32)).astype(BF16)
    return hi, mid, lo


def _dot(a, b):
    return jnp.dot(a, b, preferred_element_type=F32)


def _dot_nt(a, b):
    return lax.dot_general(a, b, (((1,), (1,)), ((), ())), preferred_element_type=F32)


def _ada_kernel(c_ref, w_ref, b_ref, o_ref):
    c = c_ref[...]
    a = c * (1.0 / (1.0 + jnp.exp(-c)))
    o_ref[0] = _dot(a.astype(BF16), w_ref[0].astype(BF16)) + b_ref[0]


def _ada(c, w_ada, b_ada):
    depth, d, n6 = w_ada.shape
    b = c.shape[0]
    rows = 8
    c_pad = jnp.pad(c, ((0, rows - b), (0, 0)))
    tn = 1536
    out = pl.pallas_call(
        _ada_kernel,
        out_shape=jax.ShapeDtypeStruct((depth, rows, n6), F32),
        grid=(depth, n6 // tn),
        in_specs=[
            pl.BlockSpec((rows, d), lambda i, j: (0, 0)),
            pl.BlockSpec((1, d, tn), lambda i, j: (i, 0, j)),
            pl.BlockSpec((1, 1, tn), lambda i, j: (i, 0, j)),
        ],
        out_specs=pl.BlockSpec((1, rows, tn), lambda i, j: (i, 0, j)),
        compiler_params=_params(("arbitrary", "arbitrary")),
        name="ada_mod",
    )(c_pad, w_ada, b_ada.reshape(depth, 1, n6))
    return out[:, :b]


def _modnorm(x, gamma, sc, sh):
    ms = jnp.mean(x * x, axis=-1, keepdims=True)
    y = x * lax.rsqrt(ms + EPS) * gamma
    return y * (1.0 + sc) + sh


def _normproj_kernel(x_ref, g_ref, sc_ref, sh_ref, w_ref, o_ref, h_ref):
    @pl.when(pl.program_id(1) == 0)
    def _():
        h_ref[...] = _modnorm(x_ref[0], g_ref[...], sc_ref[0], sh_ref[0]).astype(BF16)

    o_ref[0] = _dot(h_ref[...], w_ref[...]).astype(BF16)


def _normproj_gate_kernel(x_ref, g_ref, sc_ref, sh_ref, w_ref, wf_ref, o_ref, f_ref, h_ref):
    @pl.when(pl.program_id(1) == 0)
    def _():
        h = _modnorm(x_ref[0], g_ref[...], sc_ref[0], sh_ref[0]).astype(BF16)
        h_ref[...] = h
        f_ref[0] = _dot(h, wf_ref[...])

    o_ref[0] = _dot(h_ref[...], w_ref[...]).astype(BF16)


def _normproj(x, gamma, sc, sh, w, w_f=None, tm=1024, tn=1536):
    b, s, d = x.shape
    ncols = w.shape[1]
    spt = s // tm
    grid = (b * spt, ncols // tn)
    xmap = lambda i, j: (i // spt, i % spt, 0)
    bmap = lambda i, j: (i // spt, 0, 0)
    in_specs = [
        pl.BlockSpec((1, tm, d), xmap),
        pl.BlockSpec((1, d), lambda i, j: (0, 0)),
        pl.BlockSpec((1, 1, d), bmap),
        pl.BlockSpec((1, 1, d), bmap),
        pl.BlockSpec((d, tn), lambda i, j: (0, j)),
    ]
    o_spec = pl.BlockSpec((1, tm, tn), lambda i, j: (i // spt, i % spt, j))
    scratch = [pltpu.VMEM((tm, d), BF16)]
    if w_f is None:
        return pl.pallas_call(
            _normproj_kernel,
            out_shape=jax.ShapeDtypeStruct((b, s, ncols), BF16),
            grid=grid, in_specs=in_specs, out_specs=o_spec, scratch_shapes=scratch,
            compiler_params=_params(("arbitrary", "arbitrary")),
            name="normproj",
        )(x, gamma.reshape(1, d), sc, sh, w)
    in_specs.append(pl.BlockSpec((d, LANES), lambda i, j: (0, 0)))
    f_spec = pl.BlockSpec((1, tm, LANES), xmap)
    return pl.pallas_call(
        _normproj_gate_kernel,
        out_shape=(jax.ShapeDtypeStruct((b, s, ncols), BF16),
                   jax.ShapeDtypeStruct((b, s, LANES), F32)),
        grid=grid, in_specs=in_specs, out_specs=(o_spec, f_spec), scratch_shapes=scratch,
        compiler_params=_params(("arbitrary", "arbitrary")),
        name="normproj_gate",
    )(x, gamma.reshape(1, d), sc, sh, w, w_f)


def _normproj_a_kernel(x_ref, g_ref, sc_ref, sh_ref, w_ref, o_ref, hf_ref, hp_ref):
    tm = x_ref.shape[1]
    n_slab = x_ref.shape[2] // LANES

    @pl.when(pl.program_id(1) == 0)
    def _():
        h = _modnorm(x_ref[0], g_ref[...], sc_ref[0], sh_ref[0])
        hp_ref[0] = h.astype(BF16)
        for c in range(n_slab):
            hf_ref[c] = h[:, c * LANES:(c + 1) * LANES]
        for g, (_, r) in enumerate(A_DILATED):
            if r == 1:
                continue
            n = tm // r
            for jr in range(r):
                for c in range(n_slab):
                    hp_ref[g, jr * n:(jr + 1) * n, c * LANES:(c + 1) * LANES] = (
                        hf_ref[c, pl.ds(jr, n, stride=r), :].astype(BF16))

    group = (pl.program_id(1) * w_ref.shape[1]) // (3 * N_HEADS * HEAD_DIM)
    o_ref[0] = _dot(hp_ref[group], w_ref[...]).astype(BF16)


def _normproj_a(x, gamma, sc, sh, w, tn=1536):
    b, s, d = x.shape
    tm = PERM_TILE
    ncols = w.shape[1]
    spt = s // tm
    bmap = lambda i, j: (i // spt, 0, 0)
    return pl.pallas_call(
        _normproj_a_kernel,
        out_shape=jax.ShapeDtypeStruct((b, s, ncols), BF16),
        grid=(b * spt, ncols // tn),
        in_specs=[
            pl.BlockSpec((1, tm, d), lambda i, j: (i // spt, i % spt, 0)),
            pl.BlockSpec((1, d), lambda i, j: (0, 0)),
            pl.BlockSpec((1, 1, d), bmap),
            pl.BlockSpec((1, 1, d), bmap),
            pl.BlockSpec((d, tn), lambda i, j: (0, j)),
        ],
        out_specs=pl.BlockSpec((1, tm, tn), lambda i, j: (i // spt, i % spt, j)),
        scratch_shapes=[pltpu.VMEM((d // LANES, tm, LANES), F32),
                        pltpu.VMEM((len(A_DILATED), tm, d), BF16)],
        compiler_params=_params(("arbitrary", "arbitrary")),
        name="normproj_a",
    )(x, gamma.reshape(1, d), sc, sh, w)


def _headnorm(t, gain):
    t = t.astype(F32)
    ms = jnp.mean(t * t, axis=-1, keepdims=True)
    return t * lax.rsqrt(ms + EPS) * gain


def _pairnorm(t, gain2, even):
    t = t.astype(F32)
    sq = t * t
    tot = jnp.sum(sq, axis=-1, keepdims=True)
    s_even = jnp.sum(jnp.where(even, sq, 0.0), axis=-1, keepdims=True)
    inv = jnp.where(even, lax.rsqrt(s_even * (1.0 / HEAD_DIM) + EPS),
                    lax.rsqrt((tot - s_even) * (1.0 / HEAD_DIM) + EPS))
    return t * inv * gain2


def _rows(ref, cols=slice(None)):
    if len(ref.shape) == 2:
        return ref[:, cols]
    return jnp.concatenate([ref[i, :, cols] for i in range(ref.shape[0])], axis=0)


def _store_rows(ref, cols, val):
    if len(ref.shape) == 2:
        ref[:, cols] = val
    else:
        n = ref.shape[1]
        for i in range(ref.shape[0]):
            ref[i, :, cols] = val[i * n:(i + 1) * n]


def _attn_a_kernel(q_ref, k_ref, v_ref, bias_ref, gq_ref, gk_ref, o_ref, lse_ref,
                   kn_ref, vv_ref):
    ub = pl.program_id(2)
    slot = ub % 2
    prev = 1 - slot

    @pl.when(ub == 0)
    def _():
        kn_ref[1] = jnp.zeros(kn_ref.shape[1:], BF16)
        vv_ref[1] = jnp.zeros(vv_ref.shape[1:], BF16)

    first = jnp.minimum(ub, 1)
    lane = lax.broadcasted_iota(jnp.int32, (Q_BLOCK, PAIR), 1)
    even = lane < HEAD_DIM
    lane_row = lax.broadcasted_iota(jnp.int32, (1, PAIR), 1)
    keep_even = jnp.where(lane_row < HEAD_DIM, 1.0, 0.0).astype(BF16)
    keep_odd = jnp.where(lane_row < HEAD_DIM, 0.0, 1.0).astype(BF16)
    gq2 = gq_ref[...] * (ATTN_SCALE * LOG2E)
    gk2 = gk_ref[...]
    for hp in range(N_PAIRS):
        cs = slice(hp * PAIR, (hp + 1) * PAIR)
        kn_ref[slot, :, cs] = _pairnorm(_rows(k_ref, cs), gk2, even).astype(BF16)
    vv_ref[slot] = _rows(v_ref)
    lse_tile = jnp.zeros((Q_BLOCK, LANES), F32)
    for hp in range(N_PAIRS):
        cs = slice(hp * PAIR, (hp + 1) * PAIR)
        qn = _pairnorm(_rows(q_ref, cs), gq2, even)
        qq = jnp.concatenate([jnp.where(even, qn, 0.0), jnp.where(even, 0.0, qn)],
                             axis=0).astype(BF16)
        kcat = jnp.concatenate([kn_ref[prev, :, cs], kn_ref[slot, :, cs]], axis=0)
        s = _dot_nt(qq, kcat) + bias_ref[first, hp]
        m = jnp.max(s, axis=-1, keepdims=True)
        p = jnp.exp2(s - m)
        l = jnp.sum(p, axis=-1, keepdims=True)
        pb = p.astype(BF16)
        vcat = jnp.concatenate([vv_ref[prev, :, cs], vv_ref[slot, :, cs]], axis=0)
        o = (_dot(pb[:Q_BLOCK], vcat * keep_even)
             + _dot(pb[Q_BLOCK:], vcat * keep_odd))
        inv = 1.0 / l
        o = o * jnp.where(even, inv[:Q_BLOCK], inv[Q_BLOCK:])
        _store_rows(o_ref, cs, o.astype(BF16))
        lse = m * LN2 + jnp.log(l)
        lse_tile = jnp.where(lane == 2 * hp, lse[:Q_BLOCK],
                             jnp.where(lane == 2 * hp + 1, lse[Q_BLOCK:], lse_tile))
    _store_rows(lse_ref, slice(None), lse_tile)


def _t5_bucket(dist):
    max_exact = REL_BUCKETS // 2
    d = np.maximum(dist, 0)
    large = max_exact + (np.log(np.maximum(d, 1).astype(np.float32) / max_exact)
                         / math.log(REL_MAX_DIST / max_exact)
                         * (REL_BUCKETS - max_exact)).astype(np.int32)
    large = np.minimum(large, REL_BUCKETS - 1)
    return np.where(d < max_exact, d, large)


def _bias_kernel(tab_ref, onehot_ref, o_ref):
    hi, mid, lo = _split3(tab_ref[...])
    oh = onehot_ref[...]
    o_ref[...] = (_dot(hi, oh) + _dot(mid, oh) + _dot(lo, oh)) * LOG2E


def _bias_tables(rel_bias_g, window, dilation):
    qi = np.arange(Q_BLOCK)[:, None]
    kj = np.arange(2 * Q_BLOCK)[None, :]
    dist = qi + Q_BLOCK - kj
    bucket = _t5_bucket(dist * dilation).reshape(1, -1)
    onehot = jnp.asarray(bucket == np.arange(REL_BUCKETS)[:, None], BF16)
    npos = onehot.shape[1]
    chunk = 4096
    bias = pl.pallas_call(
        _bias_kernel,
        out_shape=jax.ShapeDtypeStruct((N_HEADS, npos), F32),
        grid=(npos // chunk,),
        in_specs=[pl.BlockSpec((N_HEADS, REL_BUCKETS), lambda i: (0, 0)),
                  pl.BlockSpec((REL_BUCKETS, chunk), lambda i: (0, i))],
        out_specs=pl.BlockSpec((N_HEADS, chunk), lambda i: (0, i)),
        compiler_params=_params(("arbitrary",)),
        name="rel_bias_table",
    )(rel_bias_g.T.astype(F32), onehot)
    bias = bias.reshape(N_PAIRS, 2 * Q_BLOCK, 2 * Q_BLOCK)
    band = (dist >= 0) & (dist <= window // dilation)
    first = band & (qi - dist >= 0)
    band = np.concatenate([band, band], axis=0)[None]
    first = np.concatenate([first, first], axis=0)[None]
    return jnp.stack([jnp.where(first, bias, NEG_INF), jnp.where(band, bias, NEG_INF)], axis=0)


def _attn_a_group(proj, g, rel_bias_g, gq, gk, window, dilation):
    b, s, cols = proj.shape
    r = dilation
    nt = s // PERM_TILE
    n_t = PERM_TILE // r
    nb = s // r // Q_BLOCK
    gw = N_HEADS * HEAD_DIM
    tables = _bias_tables(rel_bias_g, window, dilation)
    base = g * 3
    if n_t >= Q_BLOCK:
        bpt = n_t // Q_BLOCK
        blk = lambda w: (None, None, None, Q_BLOCK, w)
        imap = lambda col: (lambda bi, jr, ub: (bi, ub // bpt, jr, ub % bpt, col))
    else:
        blk = lambda w: (None, Q_BLOCK // n_t, None, n_t, w)
        imap = lambda col: (lambda bi, jr, ub: (bi, ub, jr, 0, col))
    pv = proj.reshape(b, nt, r, n_t, cols)
    gains = lambda t: jnp.concatenate([t, t]).reshape(1, PAIR)
    o, lse = pl.pallas_call(
        _attn_a_kernel,
        out_shape=(jax.ShapeDtypeStruct((b, nt, r, n_t, gw), BF16),
                   jax.ShapeDtypeStruct((b, nt, r, n_t, LANES), F32)),
        grid=(b, r, nb),
        in_specs=[
            pl.BlockSpec(blk(gw), imap(base)),
            pl.BlockSpec(blk(gw), imap(base + 1)),
            pl.BlockSpec(blk(gw), imap(base + 2)),
            pl.BlockSpec(tables.shape, lambda bi, jr, ub: (0, 0, 0, 0)),
            pl.BlockSpec((1, PAIR), lambda bi, jr, ub: (0, 0)),
            pl.BlockSpec((1, PAIR), lambda bi, jr, ub: (0, 0)),
        ],
        out_specs=(pl.BlockSpec(blk(gw), imap(0)), pl.BlockSpec(blk(LANES), imap(0))),
        scratch_shapes=[pltpu.VMEM((2, Q_BLOCK, gw), BF16), pltpu.VMEM((2, Q_BLOCK, gw), BF16)],
        compiler_params=_params(("arbitrary", "arbitrary", "arbitrary")),
        name=f"attn_a_g{g}",
    )(pv, pv, pv, tables, gains(gq), gains(gk))
    return o.reshape(b, s, gw), lse.reshape(b, s, LANES)


MERGE_CHUNK = 128


def _merge_out_kernel(o0_ref, o1_ref, o2_ref, l0_ref, l1_ref, l2_ref, x_ref, g_ref, w_ref,
                      e_ref, out_ref, lun_ref, oun_ref, om_ref):
    tm = o0_ref.shape[1]
    n_slab = o0_ref.shape[2] // LANES

    @pl.when(pl.program_id(1) == 0)
    def _():
        for gi, (o_ref, l_ref) in enumerate(((o1_ref, l1_ref), (o2_ref, l2_ref))):
            r = A_DILATED[gi + 1][1]
            n = tm // r
            for jr in range(r):
                rows = slice(jr * n, (jr + 1) * n)
                lun_ref[gi, pl.ds(jr, n, stride=r), :] = l_ref[0, rows, :]
                for c in range(n_slab):
                    oun_ref[gi, c, pl.ds(jr, n, stride=r), :] = (
                        o_ref[0, rows, c * LANES:(c + 1) * LANES].astype(F32))
        expand = e_ref[...]

        def widen(w):
            hi = w.astype(BF16)
            lo = (w - hi.astype(F32)).astype(BF16)
            return _dot(hi, expand) + _dot(lo, expand)

        for k in range(tm // MERGE_CHUNK):
            rs = slice(k * MERGE_CHUNK, (k + 1) * MERGE_CHUNK)
            l0, l1, l2 = l0_ref[0, rs, :], lun_ref[0, rs, :], lun_ref[1, rs, :]
            m = jnp.maximum(jnp.maximum(l0, l1), l2)
            e0, e1, e2 = jnp.exp(l0 - m), jnp.exp(l1 - m), jnp.exp(l2 - m)
            inv = 1.0 / (e0 + e1 + e2)
            o1 = jnp.concatenate([oun_ref[0, c, rs, :] for c in range(n_slab)], axis=1)
            o2 = jnp.concatenate([oun_ref[1, c, rs, :] for c in range(n_slab)], axis=1)
            o = (widen(e0 * inv) * o0_ref[0, rs, :].astype(F32)
                 + widen(e1 * inv) * o1 + widen(e2 * inv) * o2)
            om_ref[rs, :] = o.astype(BF16)

    out_ref[0] = x_ref[0] + g_ref[0] * _dot(om_ref[...], w_ref[...])


def _merge_out(outs, lses, x, gate, w_out, tn=512):
    b, s, d = x.shape
    tm = PERM_TILE
    spt = s // tm
    rmap = lambda i, j: (i // spt, i % spt, 0)
    cmap = lambda i, j: (i // spt, i % spt, j)
    expand = ((jnp.arange(d)[None, :] // HEAD_DIM == jnp.arange(LANES)[:, None])
              & (jnp.arange(LANES)[:, None] < N_HEADS)).astype(BF16)
    o_spec = pl.BlockSpec((1, tm, d), rmap)
    l_spec = pl.BlockSpec((1, tm, LANES), rmap)
    return pl.pallas_call(
        _merge_out_kernel,
        out_shape=jax.ShapeDtypeStruct((b, s, d), F32),
        grid=(b * spt, d // tn),
        in_specs=[o_spec, o_spec, o_spec, l_spec, l_spec, l_spec,
                  pl.BlockSpec((1, tm, tn), cmap),
                  pl.BlockSpec((1, 1, tn), lambda i, j: (i // spt, 0, j)),
                  pl.BlockSpec((d, tn), lambda i, j: (0, j)),
                  pl.BlockSpec((LANES, d), lambda i, j: (0, 0))],
        out_specs=pl.BlockSpec((1, tm, tn), cmap),
        scratch_shapes=[pltpu.VMEM((2, tm, LANES), F32),
                        pltpu.VMEM((2, d // LANES, tm, LANES), F32),
                        pltpu.VMEM((tm, d), BF16)],
        compiler_params=_params(("arbitrary", "arbitrary")),
        name="merge_out",
    )(*outs, *lses, x, gate, w_out, expand)


def _out_kernel(o_ref, x_ref, g_ref, w_ref, out_ref):
    out_ref[0] = x_ref[0] + g_ref[0] * _dot(o_ref[0], w_ref[...])


def _out_proj(o, x, gate, w_out, tm=512):
    b, s, d = x.shape
    spt = s // tm
    xmap = lambda i: (i // spt, i % spt, 0)
    return pl.pallas_call(
        _out_kernel,
        out_shape=jax.ShapeDtypeStruct((b, s, d), F32),
        grid=(b * spt,),
        in_specs=[pl.BlockSpec((1, tm, d), xmap),
                  pl.BlockSpec((1, tm, d), xmap),
                  pl.BlockSpec((1, 1, d), lambda i: (i // spt, 0, 0)),
                  pl.BlockSpec((d, d), lambda i: (0, 0))],
        out_specs=pl.BlockSpec((1, tm, d), xmap),
        compiler_params=_params(("arbitrary",)),
        name="out_proj",
    )(o, x, gate, w_out)


def _cum_kernel(f_ref, fb_ref, tri_ref, o_ref, carry_ref):
    @pl.when(pl.program_id(1) == 0)
    def _():
        carry_ref[...] = jnp.zeros_like(carry_ref)

    z = f_ref[0] + fb_ref[...]
    logf = jnp.minimum(z, 0.0) - jnp.log(1.0 + jnp.exp(-jnp.abs(z)))
    hi, mid, lo = _split3(logf)
    tri = tri_ref[...]
    cum = _dot(tri, hi) + _dot(tri, mid) + _dot(tri, lo) + carry_ref[...]
    o_ref[0] = cum
    carry_ref[...] = cum[-1:, :]


def _cum_forget(fg, f_bias, tc=512):
    b, s, w = fg.shape
    fb = jnp.pad(f_bias, (0, w - f_bias.shape[0])).reshape(1, w)
    tri = (jnp.arange(tc)[:, None] >= jnp.arange(tc)[None, :]).astype(BF16)
    return pl.pallas_call(
        _cum_kernel,
        out_shape=jax.ShapeDtypeStruct((b, s, w), F32),
        grid=(b, s // tc),
        in_specs=[pl.BlockSpec((1, tc, w), lambda i, j: (i, j, 0)),
                  pl.BlockSpec((1, w), lambda i, j: (0, 0)),
                  pl.BlockSpec((tc, tc), lambda i, j: (0, 0))],
        out_specs=pl.BlockSpec((1, tc, w), lambda i, j: (i, j, 0)),
        scratch_shapes=[pltpu.VMEM((1, w), F32)],
        compiler_params=_params(("arbitrary", "arbitrary")),
        name="cum_forget",
    )(fg, fb, tri)


ONES_ROWS = 16
FOX_HEADS_PER_STEP = 8
GATE_ROWS = HEAD_DIM // FOX_HEADS_PER_STEP
assert GATE_ROWS >= 6


def _fox_kernel(q_ref, k_ref, v_ref, cq_ref, ck_ref, gq_ref, gk_ref, o_ref, ka_ref, vt_ref,
                qat_ref, cqt_ref, ckt_ref, s_ref, p_ref, acc_ref, *, tq, tk):
    hg = pl.program_id(1)
    qi = pl.program_id(2)
    n_kt = vt_ref.shape[1]
    nh = ka_ref.shape[0]
    heads = range(nh)

    @pl.when(qi == 0)
    def _():
        gk = gk_ref[...]
        ckt_ref[...] = (ck_ref[0] * LOG2E).T
        hi, mid, lo = (-p.astype(F32) for p in _split3(ckt_ref[pl.ds(nh * hg, nh), :]))
        s_len = ckt_ref.shape[1]
        blocks = [jnp.zeros((HEAD_DIM, s_len), F32)]
        for hh in heads:
            blocks += [jnp.ones((3, s_len), F32), hi[hh:hh + 1], mid[hh:hh + 1], lo[hh:hh + 1],
                       jnp.zeros((GATE_ROWS - 6, s_len), F32)]
        gate_cols = jnp.concatenate(blocks, axis=0).T.astype(BF16)
        for hh in heads:
            sl = slice(hh * HEAD_DIM, (hh + 1) * HEAD_DIM)
            ka_ref[hh, :, :HEAD_DIM] = _headnorm(k_ref[0, :, sl], gk).astype(BF16)
            ka_ref[hh, :, HEAD_DIM:] = gate_cols[:, HEAD_DIM:]
        for pr in range(nh // 2):
            vt = v_ref[0, :, pr * PAIR:(pr + 1) * PAIR].astype(F32).T.astype(BF16)
            for hh in (2 * pr, 2 * pr + 1):
                rows = slice((hh % 2) * HEAD_DIM, (hh % 2 + 1) * HEAD_DIM)
                for jj in range(n_kt):
                    vt_ref[hh, jj, :HEAD_DIM] = vt[rows, jj * tk:(jj + 1) * tk]
                    vt_ref[hh, jj, HEAD_DIM:] = jnp.ones((ONES_ROWS, tk), BF16)

    even = lax.broadcasted_iota(jnp.int32, (tq, PAIR), 1) < HEAD_DIM
    gq2 = gq_ref[...] * (ATTN_SCALE * LOG2E)
    cqt_ref[...] = (cq_ref[0] * LOG2E).T
    hi, mid, lo = (p.astype(F32) for p in _split3(cqt_ref[pl.ds(nh * hg, nh), :]))
    for pr in range(nh // 2):
        q_t = _pairnorm(q_ref[0, :, pr * PAIR:(pr + 1) * PAIR], gq2, even).T
        for hh in (2 * pr, 2 * pr + 1):
            r0 = (hh % 2) * HEAD_DIM
            qat_ref[hh, :HEAD_DIM] = q_t[r0:r0 + HEAD_DIM].astype(BF16)
            pieces = [hi[hh:hh + 1], mid[hh:hh + 1], lo[hh:hh + 1], jnp.ones((3, tq), F32)]
            before, after = hh * GATE_ROWS, HEAD_DIM - (hh + 1) * GATE_ROWS + GATE_ROWS - 6
            if before:
                pieces.insert(0, jnp.zeros((before, tq), F32))
            pieces.append(jnp.zeros((after, tq), F32))
            qat_ref[hh, HEAD_DIM:] = jnp.concatenate(pieces, axis=0).astype(BF16)
    key = lax.broadcasted_iota(jnp.int32, (tk, tq), 0)
    qry = lax.broadcasted_iota(jnp.int32, (tk, tq), 1)
    ahead = key - qry

    def scores(t, hh, diagonal):
        start = pl.multiple_of(t * tk, tk)
        st = _dot(ka_ref[hh, pl.ds(start, tk), :], qat_ref[hh])
        if diagonal is None:
            st = jnp.where(ahead > jnp.where(t == qi, 0, tk), NEG_INF, st)
        elif diagonal:
            st = jnp.where(ahead > 0, NEG_INF, st)
        s_ref[hh] = st
        return jnp.max(st, axis=0, keepdims=True)

    def pv(t, hh, alpha_prev):
        acc_ref[hh] = alpha_prev * acc_ref[hh] + _dot(vt_ref[hh, t], p_ref[hh])

    def trip(i, carry, next_diagonal):
        out = []
        for hh in heads:
            m, alpha_prev, mx = carry[hh]
            pv(jnp.maximum(i - 1, 0), hh, alpha_prev)
            m_new = jnp.maximum(m, mx)
            alpha = jnp.exp2(m - m_new)
            p_ref[hh] = jnp.exp2(s_ref[hh] - m_new).astype(BF16)
            if next_diagonal is not None:
                mx = scores(i + 1, hh, next_diagonal)
            out.append((m_new, alpha, mx))
        return tuple(out)

    p_ref[...] = jnp.zeros_like(p_ref)
    acc_ref[...] = jnp.zeros_like(acc_ref)
    carry = tuple((jnp.full((1, tq), NEG_INF, F32), jnp.ones((1, tq), F32),
                   scores(0, hh, None)) for hh in heads)
    carry = lax.fori_loop(0, jnp.maximum(qi - 1, 0),
                          functools.partial(trip, next_diagonal=False), carry)
    carry = lax.fori_loop(jnp.maximum(qi - 1, 0), qi,
                          functools.partial(trip, next_diagonal=True), carry)
    carry = trip(qi, carry, None)
    for pr in range(nh // 2):
        o_t = []
        for hh in (2 * pr, 2 * pr + 1):
            pv(qi, hh, carry[hh][1])
            acc = acc_ref[hh]
            o_t.append(acc[:HEAD_DIM] / acc[HEAD_DIM:HEAD_DIM + 1])
        o_ref[0, :, pr * PAIR:(pr + 1) * PAIR] = (
            jnp.concatenate(o_t, axis=0).T.astype(BF16))


def _fox_attention(qkv, cum, gq, gk, tq=256):
    b, s, _ = qkv.shape
    tk = tq
    nq = s // tq
    nh = FOX_HEADS_PER_STEP
    groups = N_HEADS // nh
    gw = nh * HEAD_DIM
    vrows = HEAD_DIM + ONES_ROWS
    return pl.pallas_call(
        functools.partial(_fox_kernel, tq=tq, tk=tk),
        out_shape=jax.ShapeDtypeStruct((b, s, N_HEADS * HEAD_DIM), BF16),
        grid=(b, groups, nq),
        in_specs=[
            pl.BlockSpec((1, tq, gw), lambda bi, hg, qi: (bi, qi, hg)),
            pl.BlockSpec((1, s, gw), lambda bi, hg, qi: (bi, 0, groups + hg)),
            pl.BlockSpec((1, s, gw), lambda bi, hg, qi: (bi, 0, 2 * groups + hg)),
            pl.BlockSpec((1, tq, LANES), lambda bi, hg, qi: (bi, qi, 0)),
            pl.BlockSpec((1, s, LANES), lambda bi, hg, qi: (bi, 0, 0)),
            pl.BlockSpec((1, PAIR), lambda bi, hg, qi: (0, 0)),
            pl.BlockSpec((1, HEAD_DIM), lambda bi, hg, qi: (0, 0)),
        ],
        out_specs=pl.BlockSpec((1, tq, gw), lambda bi, hg, qi: (bi, qi, hg)),
        scratch_shapes=[pltpu.VMEM((nh, s, 2 * HEAD_DIM), BF16),
                        pltpu.VMEM((nh, s // tk, vrows, tk), BF16),
                        pltpu.VMEM((nh, 2 * HEAD_DIM, tq), BF16),
                        pltpu.VMEM((LANES, tq), F32),
                        pltpu.VMEM((LANES, s), F32),
                        pltpu.VMEM((nh, tk, tq), F32),
                        pltpu.VMEM((nh, tk, tq), BF16),
                        pltpu.VMEM((nh, vrows, tq), F32)],
        compiler_params=_params(("arbitrary", "arbitrary", "arbitrary")),
        name="fox_attn",
    )(qkv, qkv, qkv, cum, cum, jnp.concatenate([gq, gq]).reshape(1, PAIR),
      gk.reshape(1, HEAD_DIM))


ROUTER_LANES = LANES


def _route(logits):
    lane = lax.broadcasted_iota(jnp.int32, logits.shape, 1)
    lane_f = lane.astype(F32)
    big = float(ROUTER_LANES)
    is_g = lane < N_GROUPS
    gl = jnp.where(is_g, logits, -jnp.inf)
    gmax = jnp.max(gl, axis=-1, keepdims=True)
    gsum = jnp.sum(jnp.where(is_g, jnp.exp(logits - gmax), 0.0), axis=-1, keepdims=True)
    g_w = 1.0 / gsum
    gidx = jnp.min(jnp.where(gl == gmax, lane_f, big), axis=-1, keepdims=True)
    lo = N_GROUPS + EXPERTS_PER_GROUP * gidx
    in_grp = (lane_f >= lo) & (lane_f < lo + EXPERTS_PER_GROUP)
    el = jnp.where(in_grp, logits, -jnp.inf)
    t1 = jnp.max(el, axis=-1, keepdims=True)
    i1 = jnp.min(jnp.where(el == t1, lane_f, big), axis=-1, keepdims=True)
    el2 = jnp.where(lane_f == i1, -jnp.inf, el)
    t2 = jnp.max(el2, axis=-1, keepdims=True)
    i2 = jnp.min(jnp.where(el2 == t2, lane_f, big), axis=-1, keepdims=True)
    e2 = jnp.exp(t2 - t1)
    w1 = g_w / (1.0 + e2)
    w2 = w1 * e2
    gates = jnp.where(lane_f == i1, w1, jnp.where(lane_f == i2, w2, 0.0))
    return jnp.where(lane == 0, gidx, gates)


MOE_TILE = 512
SUBLANES = 8
ROUTED_COLS = D_MODEL + LANES


def _moe_route_kernel(x_ref, gam_ref, sc_ref, sh_ref, wr_ref, br_ref, tri_ref, o_ref, meta_ref,
                      cnt_ref):
    h = _modnorm(x_ref[0], gam_ref[...], sc_ref[0], sh_ref[0])
    h_hi = h.astype(BF16)
    h_lo = (h - h_hi.astype(F32)).astype(BF16)
    wr = wr_ref[...]
    w_hi = wr.astype(BF16)
    w_lo = (wr - w_hi.astype(F32)).astype(BF16)
    logits = _dot(h_hi, w_hi) + _dot(h_hi, w_lo) + _dot(h_lo, w_hi) + br_ref[...]
    route = _route(logits)
    tm = route.shape[0]

    @pl.when(pl.program_id(0) == 0)
    def _():
        cnt_ref[...] = jnp.zeros_like(cnt_ref)

    lane = lax.broadcasted_iota(jnp.int32, route.shape, 1)
    member = jnp.where((lane.astype(F32) == route[:, 0:1]) & (lane < N_GROUPS), 1.0, 0.0)
    incl = _dot(tri_ref[...], member.astype(BF16))
    rank = jnp.sum(member * (incl - member + cnt_ref[...]), axis=-1, keepdims=True)
    cnt_ref[...] = cnt_ref[...] + incl[tm - 1:tm, :]
    meta = jnp.where(lane == 0, route[:, 0:1], jnp.where(lane == 1, rank, 0.0))
    meta_ref[...] = meta.T[:SUBLANES, :]

    o_ref[:, :D_MODEL] = h
    o_ref[:, D_MODEL:] = route


def _row_copy(src, src_row, dst, dst_row, sem):
    return pltpu.make_async_copy(src.at[pl.ds(src_row, 1), :], dst.at[pl.ds(dst_row, 1), :], sem)


D_ROWS = D_MODEL // LANES


def _tile_copy(src, src_tok, dst, dst_tok, sem):
    s0 = pl.multiple_of(src_tok * D_ROWS, D_ROWS)
    d0 = pl.multiple_of(dst_tok * D_ROWS, D_ROWS)
    return pltpu.make_async_copy(src.at[pl.ds(s0, D_ROWS), :], dst.at[pl.ds(d0, D_ROWS), :], sem)


def _moe_expert_kernel(tg_ref, nt_ref, nv_ref, dest_ref, hx_hbm, wg_ref, wu_ref, wd_ref, e_ref,
                       y_hbm, buf, ybuf, wg_bf, wu_bf, wd_bf, src_ref, gsem, ssem):
    t = pl.program_id(0)
    n_used = nt_ref[0]
    slot = t % 2
    rows = buf.shape[1]
    unroll = 8

    @pl.when(t == 0)
    def _():
        def clear(p, c):
            src_ref[p] = 0
            return c
        lax.fori_loop(0, src_ref.shape[0], clear, 0, unroll=unroll)

        def invert(tok, c):
            src_ref[dest_ref[tok]] = tok
            return c
        lax.fori_loop(0, dest_ref.shape[0], invert, 0, unroll=unroll)

    def gather_start(tile, sl):
        def chunk(k, c):
            for i in range(unroll):
                r = k * unroll + i
                _row_copy(hx_hbm, src_ref[tile * rows + r], buf.at[sl], r,
                          gsem.at[sl]).start(priority=i % 2)
            return c
        lax.fori_loop(0, rows // unroll, chunk, 0)

    def gather_wait(sl):
        def body(r, c):
            _row_copy(hx_hbm, 0, buf.at[sl], r, gsem.at[sl]).wait()
            return c
        lax.fori_loop(0, rows, body, 0, unroll=unroll)

    def valid_rows_loop(tile, body):
        n_valid = nv_ref[tile]
        chunks = n_valid // unroll

        def chunk(k, c):
            for i in range(unroll):
                body(k * unroll + i)
            return c
        lax.fori_loop(0, chunks, chunk, 0)

        def single(r, c):
            body(r)
            return c
        lax.fori_loop(chunks * unroll, n_valid, single, 0)

    def scatter_start(tile, sl):
        valid_rows_loop(tile, lambda r: _tile_copy(
            ybuf.at[sl], r, y_hbm, src_ref[tile * rows + r], ssem.at[sl]).start(priority=1))

    def scatter_wait(tile, sl):
        valid_rows_loop(tile, lambda r: _tile_copy(ybuf.at[sl], r, y_hbm, 0, ssem.at[sl]).wait())

    @pl.when(t == 0)
    def _():
        gather_start(0, 0)

    @pl.when(t + 1 < n_used)
    def _():
        gather_start(t + 1, 1 - slot)

    @pl.when((t == 0) | (tg_ref[t] != tg_ref[jnp.maximum(t - 1, 0)]))
    def _():
        wg_bf[...] = wg_ref[...].astype(BF16)
        wu_bf[...] = wu_ref[...].astype(BF16)
        wd_bf[...] = wd_ref[...].astype(BF16)

    @pl.when(t < n_used)
    def _():
        gather_wait(slot)
        h = buf[slot, :, :D_MODEL].astype(BF16)
        gates = buf[slot, :, D_MODEL:]
        g_hi = gates.astype(BF16)
        g_lo = (gates - g_hi.astype(F32)).astype(BF16)
        y = None
        for e in range(EXPERTS_PER_GROUP):
            a = _dot(h, wg_bf[e])
            u = _dot(h, wu_bf[e])
            spread = e_ref[0, :, e * EXPERT_FF:(e + 1) * EXPERT_FF]
            gexp = _dot(g_hi, spread) + _dot(g_lo, spread)
            hid = a * (1.0 / (1.0 + jnp.exp(-a))) * u * gexp
            part = _dot(hid.astype(BF16), wd_bf[e])
            y = part if y is None else y + part

        @pl.when(t >= 2)
        def _():
            scatter_wait(t - 2, slot)

        for c in range(D_ROWS):
            ybuf[slot, pl.ds(c, rows, stride=D_ROWS), :] = y[:, c * LANES:(c + 1) * LANES]
        scatter_start(t, slot)

    @pl.when(t == pl.num_programs(0) - 1)
    def _():
        for back in (2, 1):
            scatter_wait(n_used - back, (n_used - back) % 2)


def _moe_combine_kernel(x_ref, g_ref, y_ref, o_ref):
    tm = x_ref.shape[1]
    tm = x_ref.shape[1]
    y = jnp.concatenate([y_ref[pl.ds(c, tm, stride=D_ROWS), :] for c in range(D_ROWS)], axis=1)
    o_ref[0] = x_ref[0] + g_ref[0] * y


def _moe(x, gamma, sc, sh, gate, w_router, b_router, w_gate, w_up, w_down, layer):
    b, s, d = x.shape
    n = b * s
    tm = PERM_TILE
    spt = s // tm
    rows = MOE_TILE
    n_tiles = n // rows + N_GROUPS
    p_rows = n_tiles * rows
    bmap = lambda i: (i // spt, 0, 0)
    tri = (jnp.arange(tm)[:, None] >= jnp.arange(tm)[None, :]).astype(BF16)
    hx, meta = pl.pallas_call(
        _moe_route_kernel,
        out_shape=(jax.ShapeDtypeStruct((n, ROUTED_COLS), F32),
                   jax.ShapeDtypeStruct((SUBLANES, n), F32)),
        grid=(b * spt,),
        in_specs=[
            pl.BlockSpec((1, tm, d), lambda i: (i // spt, i % spt, 0)),
            pl.BlockSpec((1, d), lambda i: (0, 0)),
            pl.BlockSpec((1, 1, d), bmap),
            pl.BlockSpec((1, 1, d), bmap),
            pl.BlockSpec((d, ROUTER_LANES), lambda i: (0, 0)),
            pl.BlockSpec((1, ROUTER_LANES), lambda i: (0, 0)),
            pl.BlockSpec((tm, tm), lambda i: (0, 0)),
        ],
        out_specs=(pl.BlockSpec((tm, ROUTED_COLS), lambda i: (i, 0)),
                   pl.BlockSpec((SUBLANES, tm), lambda i: (0, i))),
        scratch_shapes=[pltpu.VMEM((1, LANES), F32)],
        compiler_params=_params(("arbitrary",)),
        name="moe_route",
    )(x, gamma.reshape(1, d), sc, sh, w_router, b_router, tri)

    gidx = meta[0].astype(jnp.int32)
    rank = meta[1].astype(jnp.int32)
    onehot = (gidx[:, None] == jnp.arange(N_GROUPS)[None, :]).astype(jnp.int32)
    count = jnp.sum(onehot, axis=0)
    padded = (count + rows - 1) // rows * rows
    ends = jnp.cumsum(padded)
    dest = jnp.sum(onehot * (ends - padded)[None, :], axis=1) + rank
    n_used = (ends[-1] // rows).reshape(1).astype(jnp.int32)
    tile_start = jnp.arange(n_tiles, dtype=jnp.int32) * rows
    tile_group = jnp.minimum(jnp.sum(ends[None, :] <= tile_start[:, None], axis=1),
                             N_GROUPS - 1).astype(jnp.int32)
    seg_valid_end = ends - padded + count
    n_valid = jnp.clip(jnp.sum(jnp.where(jnp.arange(N_GROUPS)[None, :] == tile_group[:, None],
                                         seg_valid_end[None, :], 0), axis=1) - tile_start,
                       0, rows).astype(jnp.int32)

    ff = EXPERTS_PER_GROUP * EXPERT_FF
    lanes = jnp.arange(ROUTER_LANES)[None, :, None]
    expert = N_GROUPS + EXPERTS_PER_GROUP * jnp.arange(N_GROUPS)[:, None, None] \
        + jnp.arange(ff)[None, None, :] // EXPERT_FF
    expand = (lanes == expert).astype(BF16)
    wmap = lambda t, tg, nt, nv, ds: (tg[t], 0, 0)
    lmap = lambda t, tg, nt, nv, ds: (layer, tg[t], 0, 0)
    y_tok = pl.pallas_call(
        _moe_expert_kernel,
        out_shape=jax.ShapeDtypeStruct((n * D_ROWS, LANES), F32),
        grid_spec=pltpu.PrefetchScalarGridSpec(
            num_scalar_prefetch=4,
            grid=(n_tiles,),
            in_specs=[
                pl.BlockSpec(memory_space=pl.ANY),
                pl.BlockSpec((None, EXPERTS_PER_GROUP, d, EXPERT_FF), lmap),
                pl.BlockSpec((None, EXPERTS_PER_GROUP, d, EXPERT_FF), lmap),
                pl.BlockSpec((None, EXPERTS_PER_GROUP, EXPERT_FF, d), lmap),
                pl.BlockSpec((1, ROUTER_LANES, ff), wmap),
            ],
            out_specs=pl.BlockSpec(memory_space=pl.ANY),
            scratch_shapes=[pltpu.VMEM((2, rows, ROUTED_COLS), F32),
                            pltpu.VMEM((2, rows * D_ROWS, LANES), F32),
                            pltpu.VMEM((EXPERTS_PER_GROUP, d, EXPERT_FF), BF16),
                            pltpu.VMEM((EXPERTS_PER_GROUP, d, EXPERT_FF), BF16),
                            pltpu.VMEM((EXPERTS_PER_GROUP, EXPERT_FF, d), BF16),
                            pltpu.SMEM((p_rows,), jnp.int32),
                            pltpu.SemaphoreType.DMA((2,)),
                            pltpu.SemaphoreType.DMA((2,))],
        ),
        compiler_params=_params(("arbitrary",)),
        name="moe_experts",
    )(tile_group, n_used, n_valid, dest, hx, w_gate, w_up, w_down, expand)

    return pl.pallas_call(
        _moe_combine_kernel,
        out_shape=jax.ShapeDtypeStruct((b, s, d), F32),
        grid=(b * spt,),
        in_specs=[pl.BlockSpec((1, tm, d), lambda i: (i // spt, i % spt, 0)),
                  pl.BlockSpec((1, 1, d), bmap),
                  pl.BlockSpec((tm * D_ROWS, LANES), lambda i: (i, 0))],
        out_specs=pl.BlockSpec((1, tm, d), lambda i: (i // spt, i % spt, 0)),
        compiler_params=_params(("arbitrary",)),
        name="moe_combine",
    )(x, gate, y_tok)


def kernel(x, c, w_ada, b_ada, norm_mix, norm_ffn, rel_bias, a_w_in, a_w_out, a_q_norm,
           a_k_norm, b_w_in, b_f_bias, b_w_out, b_q_norm, b_k_norm, router_group_w,
           router_group_b, router_expert_w, router_expert_b, w_gate, w_up, w_down):
    b, s, d = x.shape
    depth = w_ada.shape[0]
    mod = _ada(c, w_ada, b_ada).reshape(depth, b, 6, 1, d)
    pad_r = ROUTER_LANES - N_GROUPS - N_EXPERTS
    for i in range(depth):
        sh_m, sc_m, g_m, sh_f, sc_f, g_f = (mod[i, :, k] for k in range(6))
        j = i // 2
        if i % 2 == 0:
            proj = _normproj_a(x, norm_mix[i], sc_m, sh_m, a_w_in[j].astype(BF16))
            outs, lses = [], []
            for g, (window, dil) in enumerate(A_DILATED):
                tab = rel_bias[:, g * N_HEADS:(g + 1) * N_HEADS]
                o, l = _attn_a_group(proj, g, tab, a_q_norm[j], a_k_norm[j], window, dil)
                outs.append(o)
                lses.append(l)
            x = _merge_out(outs, lses, x, g_m, a_w_out[j].astype(BF16))
        else:
            n_qkv = 3 * N_HEADS * HEAD_DIM
            w_in = b_w_in[j]
            w_f = jnp.pad(w_in[:, n_qkv:], ((0, 0), (0, LANES - N_HEADS))).astype(BF16)
            qkv, fg = _normproj(x, norm_mix[i], sc_m, sh_m, w_in[:, :n_qkv].astype(BF16), w_f)
            cum = _cum_forget(fg, b_f_bias[j])
            o = _fox_attention(qkv, cum, b_q_norm[j], b_k_norm[j])
            x = _out_proj(o, x, g_m, b_w_out[j].astype(BF16))
        w_router = jnp.pad(jnp.concatenate([router_group_w[i], router_expert_w[i]], axis=1),
                           ((0, 0), (0, pad_r)))
        b_router = jnp.pad(jnp.concatenate([router_group_b[i], router_expert_b[i]]),
                           (0, pad_r)).reshape(1, ROUTER_LANES)
        x = _moe(x, norm_ffn[i], sc_f, sh_f, g_f, w_router, b_router,
                 w_gate, w_up, w_down, i)
    return x
```
